```python
import jax
import jax.numpy as jnp
from jax import lax
import numpy as np

D_MODEL = 2048
BATCH = 1
SEQ = 16384
DEPTH = 2

GRID_W = 64
CTX_LEN = 256
EPS = 1e-6
ROPE_THETA = 10000.0

ML_HEADS = 4
ML_QK = 64
ML_V = 128
ML_CHUNK = 64
ML_CONV = 3
NA_HEADS = 8
NA_DIM = 64
NA_WIN_R = 8
NA_WIN_C = 16
SW_HEADS = 8
SW_KV_HEADS = 2
SW_DIM = 64
SW_WINDOW = 128
SW_BLOCK = 128
MLA_HEADS = 4
MLA_Q_RANK = 384
MLA_KV_RANK = 256
MLA_NOPE = 128
MLA_ROPE = 64
MLA_V = 128
MLA_BLOCK = 128

N_BRANCH = 4
BRANCH_W = 512
D_FF = ((8 * D_MODEL // 3 + 255) // 256) * 256

ML_COLS = 2 * ML_HEADS * ML_QK + 2 * ML_HEADS * ML_V + 4 * ML_HEADS
NA_COLS = 3 * NA_HEADS * NA_DIM
SW_COLS = (SW_HEADS + 2 * SW_KV_HEADS) * SW_DIM
MLA_COLS = MLA_Q_RANK + MLA_KV_RANK + MLA_ROPE
GATE_COLS = N_BRANCH * D_MODEL
IN_COLS = ML_COLS + NA_COLS + SW_COLS + MLA_COLS + GATE_COLS
IN_SPLITS = [ML_COLS, ML_COLS + NA_COLS, ML_COLS + NA_COLS + SW_COLS,
             ML_COLS + NA_COLS + SW_COLS + MLA_COLS]

kernel_name = "hybrid_gated_diffusion_block"


def rmsnorm(x, g):
    x32 = x.astype(jnp.float32)
    y = x32 * lax.rsqrt(jnp.mean(x32 * x32, axis=-1, keepdims=True) + EPS)
    return (y * g.astype(jnp.float32)).astype(x.dtype)


def modulate(h, shift, scale):
    return h * (1 + scale[:, None, :]) + shift[:, None, :]


def axial_rope_tables(n_tokens, dim):
    a = dim // 2
    inv = 1.0 / (ROPE_THETA ** (jnp.arange(0, a, 2, dtype=jnp.float32) / a))
    t = jnp.arange(n_tokens)
    row = (t // GRID_W).astype(jnp.float32)
    col = (t % GRID_W).astype(jnp.float32)
    ang = jnp.concatenate([row[:, None] * inv, col[:, None] * inv], axis=-1)
    return jnp.cos(ang), jnp.sin(ang)


def apply_rope(x, cos, sin):
    half = x.shape[-1] // 2
    x1, x2 = x[..., :half], x[..., half:]
    if x.ndim == 4:
        cos, sin = cos[:, None, :], sin[:, None, :]
    cos, sin = cos.astype(x.dtype), sin.astype(x.dtype)
    return jnp.concatenate([x1 * cos - x2 * sin, x2 * cos + x1 * sin], axis=-1)


def dense_softmax_attn(q, k, v, scale):
    s = jnp.einsum('bqhd,bkhd->bhqk', q, k).astype(jnp.float32) * scale
    p = jax.nn.softmax(s, axis=-1).astype(v.dtype)
    return jnp.einsum('bhqk,bkhd->bqhd', p, v)


def depthwise_conv_centred(x, w):
    K, C = w.shape
    return lax.conv_general_dilated(x, w[:, None, :].astype(x.dtype), window_strides=(1,),
                                    padding=[(K // 2, K // 2)],
                                    dimension_numbers=('NWC', 'WIO', 'NWC'),
                                    feature_group_count=C)


def mlstm_chunkwise(q, k, v, log_i, log_f, state):
    B, T, H, dk = q.shape
    dv = v.shape[-1]
    L = ML_CHUNK
    nc = T // L

    def chunks(a):
        a = a.astype(jnp.float32).reshape((B, nc, L, H) + a.shape[3:])
        return jnp.moveaxis(a, (1, 3), (0, 2))

    lower = jnp.tril(jnp.ones((L, L), dtype=bool))

    def step(carry, inp):
        C, n, m = carry
        qc, kc, vc, ic, fc = inp
        b = jnp.cumsum(fc, axis=-1)
        dmat = jnp.where(lower, b[..., :, None] - b[..., None, :] + ic[..., None, :], -jnp.inf)
        inter = b + m[..., None]
        m_t = jnp.maximum(inter, jnp.max(dmat, axis=-1))
        s = jnp.einsum('bhtd,bhsd->bhts', qc, kc) * jnp.exp(dmat - m_t[..., None])
        w_inter = jnp.exp(inter - m_t)
        num = jnp.einsum('bhts,bhsv->bhtv', s, vc) + w_inter[..., None] * jnp.einsum('bhtd,bhdv->bhtv', qc, C)
        den = jnp.sum(s, axis=-1) + w_inter * jnp.einsum('bhtd,bhd->bht', qc, n)
        h = num / jnp.maximum(jnp.abs(den), jnp.exp(-m_t))[..., None]
        b_end = b[..., -1]
        d_end = b_end[..., None] - b + ic
        m_new = jnp.maximum(b_end + m, jnp.max(d_end, axis=-1))
        w_s = jnp.exp(d_end - m_new[..., None])
        w_c = jnp.exp(b_end + m - m_new)
        C_new = w_c[..., None, None] * C + jnp.einsum('bhs,bhsd,bhsv->bhdv', w_s, kc, vc)
        n_new = w_c[..., None] * n + jnp.einsum('bhs,bhsd->bhd', w_s, kc)
        return (C_new, n_new, m_new), h

    state, h = lax.scan(step, state, (chunks(q), chunks(k), chunks(v), chunks(log_i), chunks(log_f)))
    h = jnp.moveaxis(h, (0, 2), (1, 3)).reshape(B, T, H, dv).astype(v.dtype)
    return h, state


def mlstm_branch(p_lat, p_ctx, conv_w, gate_b, norm_g, need_ctx):
    qkw = ML_HEADS * ML_QK
    vw = ML_HEADS * ML_V

    def prep(p):
        B, T, _ = p.shape
        qk, v, o, g = jnp.split(p, [2 * qkw, 2 * qkw + vw, 2 * qkw + 2 * vw], axis=-1)
        qk = jax.nn.silu(depthwise_conv_centred(qk, conv_w))
        q = qk[..., :qkw].reshape(B, T, ML_HEADS, ML_QK)
        k = qk[..., qkw:].reshape(B, T, ML_HEADS, ML_QK) * (ML_QK ** -0.5)
        v = v.reshape(B, T, ML_HEADS, ML_V)
        g = g.reshape(B, T, 4, ML_HEADS).astype(jnp.float32) + gate_b.astype(jnp.float32)
        fwd = (q, k, v, g[:, :, 0], jax.nn.log_sigmoid(g[:, :, 2]))
        bwd = (jnp.flip(q, axis=1), jnp.flip(k, axis=1), jnp.flip(v, axis=1),
               jnp.flip(g[:, :, 1], axis=1), jnp.flip(jax.nn.log_sigmoid(g[:, :, 3]), axis=1))
        return fwd, bwd, o

    B = p_lat.shape[0]
    zero = (jnp.zeros((B, ML_HEADS, ML_QK, ML_V), jnp.float32),
            jnp.zeros((B, ML_HEADS, ML_QK), jnp.float32),
            jnp.zeros((B, ML_HEADS), jnp.float32))
    c_fwd, c_bwd, o_c = prep(p_ctx)
    l_fwd, l_bwd, o_l = prep(p_lat)
    hc_f, st_f = mlstm_chunkwise(*c_fwd, zero)
    hc_b, st_b = mlstm_chunkwise(*c_bwd, zero)
    hl_f, _ = mlstm_chunkwise(*l_fwd, st_f)
    hl_b, _ = mlstm_chunkwise(*l_bwd, st_b)

    def finish(hf, hb, o):
        B_, T = o.shape[:2]
        h = rmsnorm(hf + jnp.flip(hb, axis=1), norm_g.reshape(ML_HEADS, ML_V))
        return h.reshape(B_, T, ML_HEADS * ML_V) * jax.nn.sigmoid(o)

    y_lat = finish(hl_f, hl_b, o_l)
    y_ctx = finish(hc_f, hc_b, o_c) if need_ctx else None
    return y_lat, y_ctx


def na_branch(p_lat, p_ctx, rpb, rows, need_ctx):
    def heads(p):
        B, T, _ = p.shape
        q, k, v = jnp.split(p, 3, axis=-1)
        return tuple(a.reshape(B, T, NA_HEADS, NA_DIM) for a in (q, k, v))

    q, k, v = heads(p_lat)
    qc, kc, vc = heads(p_ctx)
    B, S = q.shape[:2]
    scale = NA_DIM ** -0.5
    win_r = min(NA_WIN_R, rows)
    win_c = NA_WIN_C
    n_loc = win_r * win_c
    qg = q.reshape(B, rows, GRID_W, NA_HEADS, NA_DIM)
    kg = k.reshape(B, rows, GRID_W, NA_HEADS, NA_DIM)
    vg = v.reshape(B, rows, GRID_W, NA_HEADS, NA_DIM)
    cols = jnp.arange(GRID_W)
    col_start = jnp.clip(cols - win_c // 2, 0, GRID_W - win_c)
    col_idx = col_start[:, None] + jnp.arange(win_c)[None, :]
    col_bias_idx = col_idx - cols[:, None] + (NA_WIN_C - 1)

    def row_block(args):
        r, q_row = args
        rs = jnp.clip(r - win_r // 2, 0, rows - win_r)
        k_band = lax.dynamic_slice_in_dim(kg, rs, win_r, axis=1)
        v_band = lax.dynamic_slice_in_dim(vg, rs, win_r, axis=1)
        k_win = jnp.moveaxis(k_band[:, :, col_idx], 1, 2).reshape(B, GRID_W, n_loc, NA_HEADS, NA_DIM)
        v_win = jnp.moveaxis(v_band[:, :, col_idx], 1, 2).reshape(B, GRID_W, n_loc, NA_HEADS, NA_DIM)
        row_bias_idx = rs + jnp.arange(win_r) - r + (NA_WIN_R - 1)
        bias = rpb[:, row_bias_idx[:, None, None], col_bias_idx[None, :, :]]
        bias = jnp.transpose(bias, (0, 2, 1, 3)).reshape(NA_HEADS, GRID_W, n_loc)
        s_loc = jnp.einsum('bwhd,bwnhd->bhwn', q_row, k_win).astype(jnp.float32) * scale + bias.astype(jnp.float32)
        s_ctx = jnp.einsum('bwhd,bchd->bhwc', q_row, kc).astype(jnp.float32) * scale
        p = jax.nn.softmax(jnp.concatenate([s_loc, s_ctx], axis=-1), axis=-1).astype(v.dtype)
        return (jnp.einsum('bhwn,bwnhd->bwhd', p[..., :n_loc], v_win)
                + jnp.einsum('bhwc,bchd->bwhd', p[..., n_loc:], vc))

    out = lax.map(row_block, (jnp.arange(rows), jnp.moveaxis(qg, 1, 0)))
    y_lat = jnp.moveaxis(out, 0, 1).reshape(B, S, NA_HEADS * NA_DIM)
    y_ctx = None
    if need_ctx:
        y_ctx = dense_softmax_attn(qc, kc, vc, scale).reshape(B, qc.shape[1], NA_HEADS * NA_DIM)
    return y_lat, y_ctx


def sw_branch(p_lat, p_ctx, sink, cos, sin, need_ctx):
    qw = SW_HEADS * SW_DIM
    kw = SW_KV_HEADS * SW_DIM
    G = SW_HEADS // SW_KV_HEADS

    def heads(p):
        B, T, _ = p.shape
        q, k, v = jnp.split(p, [qw, qw + kw], axis=-1)
        return (q.reshape(B, T, SW_HEADS, SW_DIM), k.reshape(B, T, SW_KV_HEADS, SW_DIM),
                v.reshape(B, T, SW_KV_HEADS, SW_DIM))

    q, k, v = heads(p_lat)
    qc, kc, vc = heads(p_ctx)
    q, k = apply_rope(q, cos, sin), apply_rope(k, cos, sin)
    B, S = q.shape[:2]
    nb = S // SW_BLOCK
    scale = SW_DIM ** -0.5
    sink_r = sink.astype(jnp.float32).reshape(SW_KV_HEADS, G)
    qb = q.reshape(B, nb, SW_BLOCK, SW_KV_HEADS, G, SW_DIM)

    def band(a):
        ap = jnp.pad(a, ((0, 0), (SW_BLOCK, SW_BLOCK), (0, 0), (0, 0)))
        ap = ap.reshape(B, nb + 2, SW_BLOCK, SW_KV_HEADS, SW_DIM)
        return jnp.concatenate([ap[:, :-2], ap[:, 1:-1], ap[:, 2:]], axis=2)

    kb, vb = band(k), band(v)
    blk = jnp.arange(nb)[:, None, None]
    qpos = blk * SW_BLOCK + jnp.arange(SW_BLOCK)[None, :, None]
    kpos = (blk - 1) * SW_BLOCK + jnp.arange(3 * SW_BLOCK)[None, None, :]
    mask = (jnp.abs(qpos - kpos) <= SW_WINDOW) & (kpos >= 0) & (kpos < S)
    s_loc = jnp.einsum('bnqhgd,bnkhd->bnhgqk', qb, kb).astype(jnp.float32) * scale
    s_loc = jnp.where(mask[None, :, None, None], s_loc, -jnp.inf)
    s_ctx = jnp.einsum('bnqhgd,bchd->bnhgqc', qb, kc).astype(jnp.float32) * scale
    s_sink = jnp.broadcast_to(sink_r.reshape(1, 1, SW_KV_HEADS, G, 1, 1), s_ctx.shape[:-1] + (1,))
    p = jax.nn.softmax(jnp.concatenate([s_loc, s_ctx, s_sink], axis=-1), axis=-1).astype(v.dtype)
    n_loc = 3 * SW_BLOCK
    n_ctx = kc.shape[1]
    out = (jnp.einsum('bnhgqk,bnkhd->bnqhgd', p[..., :n_loc], vb)
           + jnp.einsum('bnhgqc,bchd->bnqhgd', p[..., n_loc:n_loc + n_ctx], vc))
    y_lat = out.reshape(B, S, SW_HEADS * SW_DIM)
    y_ctx = None
    if need_ctx:
        qcg = qc.reshape(B, n_ctx, SW_KV_HEADS, G, SW_DIM)
        s = jnp.einsum('bqhgd,bkhd->bhgqk', qcg, kc).astype(jnp.float32) * scale
        sk = jnp.broadcast_to(sink_r.reshape(1, SW_KV_HEADS, G, 1, 1), s.shape[:-1] + (1,))
        pc = jax.nn.softmax(jnp.concatenate([s, sk], axis=-1), axis=-1)[..., :-1].astype(vc.dtype)
        y_ctx = jnp.einsum('bhgqk,bkhd->bqhgd', pc, vc).reshape(B, n_ctx, SW_HEADS * SW_DIM)
    return y_lat, y_ctx


def mla_attend(qn, qr, kn, kr, v):
    s = (jnp.einsum('bqhd,bkhd->bhqk', qn, kn) + jnp.einsum('bqhr,bkr->bhqk', qr, kr)).astype(jnp.float32)
    p = jax.nn.softmax(s * ((MLA_NOPE + MLA_ROPE) ** -0.5), axis=-1).astype(v.dtype)
    return jnp.einsum('bhqk,bkhd->bqhd', p, v)


def mla_branch(p_lat, p_ctx, g_q, g_kv, w_uq, w_ukv, cos, sin, need_ctx):
    def prep(p, rope):
        B, T, _ = p.shape
        cq, ckv, kr = jnp.split(p, [MLA_Q_RANK, MLA_Q_RANK + MLA_KV_RANK], axis=-1)
        q = (rmsnorm(cq, g_q) @ w_uq).reshape(B, T, MLA_HEADS, MLA_NOPE + MLA_ROPE)
        kv = (rmsnorm(ckv, g_kv) @ w_ukv).reshape(B, T, MLA_HEADS, MLA_NOPE + MLA_V)
        qn, qr = q[..., :MLA_NOPE], q[..., MLA_NOPE:]
        kn, v = kv[..., :MLA_NOPE], kv[..., MLA_NOPE:]
        if rope:
            qr, kr = apply_rope(qr, cos, sin), apply_rope(kr, cos, sin)
        return qn, qr, kn, kr, v

    qn, qr, kn, kr, v = prep(p_lat, True)
    qnc, qrc, knc, krc, vc = prep(p_ctx, False)
    B, S = qn.shape[:2]
    nb = S // MLA_BLOCK
    kn_all = jnp.concatenate([kn, knc], axis=1)
    kr_all = jnp.concatenate([kr, krc], axis=1)
    v_all = jnp.concatenate([v, vc], axis=1)
    qn_b = jnp.moveaxis(qn.reshape(B, nb, MLA_BLOCK, MLA_HEADS, MLA_NOPE), 1, 0)
    qr_b = jnp.moveaxis(qr.reshape(B, nb, MLA_BLOCK, MLA_HEADS, MLA_ROPE), 1, 0)
    out = lax.map(lambda a: mla_attend(a[0], a[1], kn_all, kr_all, v_all), (qn_b, qr_b))
    y_lat = jnp.moveaxis(out, 0, 1).reshape(B, S, MLA_HEADS * MLA_V)
    y_ctx = None
    if need_ctx:
        y_ctx = mla_attend(qnc, qrc, knc, krc, vc).reshape(B, qnc.shape[1], MLA_HEADS * MLA_V)
    return y_lat, y_ctx


def gated_merge(ys, gate_pre, w_branch, w_out):
    gates = jax.nn.sigmoid(gate_pre)
    merged = gates[..., :D_MODEL] * (ys[0] @ w_branch[0])
    for i in range(1, N_BRANCH):
        merged = merged + gates[..., i * D_MODEL:(i + 1) * D_MODEL] * (ys[i] @ w_branch[i])
    return merged @ w_out


def swiglu(h, w_in, w_out):
    a, g = jnp.split(h @ w_in, 2, axis=-1)
    return (jax.nn.silu(a) * g) @ w_out


def setup_inputs(seed: int = 0) -> dict:
    key = jax.random.key(seed)
    ks = jax.random.split(key, 24)
    D, L = D_MODEL, DEPTH

    def nrm(k, shape, s):
        return jax.random.normal(k, shape, jnp.float32) * s

    gate_base = jnp.concatenate([jnp.zeros((2, ML_HEADS), jnp.float32),
                                 jnp.tile(jnp.linspace(3.0, 6.0, ML_HEADS)[None, :], (2, 1))], axis=0)
    return {
        "x": nrm(ks[0], (BATCH, SEQ, D), 1.0),
        "c": nrm(ks[1], (BATCH, D), 1.0),
        "ctx": nrm(ks[2], (BATCH, CTX_LEN, D), 1.0),
        "c_ctx": nrm(ks[3], (D,), 1.0),
        "w_ada": nrm(ks[4], (L, D, 6 * D), D ** -0.5),
        "b_ada": nrm(ks[5], (L, 6 * D), 0.02),
        "norm1_g": 1.0 + nrm(ks[6], (L, D), 0.02),
        "norm2_g": 1.0 + nrm(ks[7], (L, D), 0.02),
        "w_in": nrm(ks[8], (L, D, IN_COLS), D ** -0.5),
        "ml_conv_w": nrm(ks[9], (L, ML_CONV, 2 * ML_HEADS * ML_QK), ML_CONV ** -0.5),
        "ml_gate_b": gate_base[None] + nrm(ks[10], (L, 4, ML_HEADS), 0.1),
        "ml_norm_g": 1.0 + nrm(ks[11], (L, ML_HEADS * ML_V), 0.02),
        "na_rpb": nrm(ks[12], (L, NA_HEADS, 2 * NA_WIN_R - 1, 2 * NA_WIN_C - 1), 0.1),
        "sw_sink": nrm(ks[13], (L, SW_HEADS), 0.5),
        "mla_q_norm_g": 1.0 + nrm(ks[14], (L, MLA_Q_RANK), 0.02),
        "mla_kv_norm_g": 1.0 + nrm(ks[15], (L, MLA_KV_RANK), 0.02),
        "mla_w_uq": nrm(ks[16], (L, MLA_Q_RANK, MLA_HEADS * (MLA_NOPE + MLA_ROPE)), MLA_Q_RANK ** -0.5),
        "mla_w_ukv": nrm(ks[17], (L, MLA_KV_RANK, MLA_HEADS * (MLA_NOPE + MLA_V)), MLA_KV_RANK ** -0.5),
        "w_branch": nrm(ks[18], (L, N_BRANCH, BRANCH_W, D), BRANCH_W ** -0.5),
        "w_out": nrm(ks[19], (L, D, D), D ** -0.5),
        "w_ffn_in": nrm(ks[20], (L, D, 2 * D_FF), D ** -0.5),
        "w_ffn_out": nrm(ks[21], (L, D_FF, D), D_FF ** -0.5),
        "final_norm_g": 1.0 + nrm(ks[22], (D,), 0.02),
    }


def reference(x, c, ctx, c_ctx, w_ada, b_ada, norm1_g, norm2_g, w_in, ml_conv_w, ml_gate_b,
              ml_norm_g, na_rpb, sw_sink, mla_q_norm_g, mla_kv_norm_g, mla_w_uq, mla_w_ukv,
              w_branch, w_out, w_ffn_in, w_ffn_out, final_norm_g):
    S = x.shape[1]
    rows = S // GRID_W
    cos_sw, sin_sw = axial_rope_tables(S, SW_DIM)
    cos_mla, sin_mla = axial_rope_tables(S, MLA_ROPE)
    xc = ctx
    for l in range(DEPTH):
        need_ctx = l < DEPTH - 1
        mod = jax.nn.silu(c) @ w_ada[l] + b_ada[l]
        mod_c = jax.nn.silu(c_ctx)[None, :] @ w_ada[l] + b_ada[l]
        sh1, sc1, g1, sh2, sc2, g2 = jnp.split(mod, 6, axis=-1)
        csh1, csc1, cg1, csh2, csc2, cg2 = jnp.split(mod_c, 6, axis=-1)

        hl = modulate(rmsnorm(x, norm1_g[l]), sh1, sc1)
        hc = modulate(rmsnorm(xc, norm1_g[l]), csh1, csc1)
        p_lat = hl @ w_in[l]
        p_ctx = hc @ w_in[l]
        ml_l, na_l, sw_l, mla_l, gate_l = jnp.split(p_lat, IN_SPLITS, axis=-1)
        ml_c, na_c, sw_c, mla_c, gate_c = jnp.split(p_ctx, IN_SPLITS, axis=-1)
        ya_l, ya_c = mlstm_branch(ml_l, ml_c, ml_conv_w[l], ml_gate_b[l], ml_norm_g[l], need_ctx)
        yb_l, yb_c = na_branch(na_l, na_c, na_rpb[l], rows, need_ctx)
        yc_l, yc_c = sw_branch(sw_l, sw_c, sw_sink[l], cos_sw, sin_sw, need_ctx)
        yd_l, yd_c = mla_branch(mla_l, mla_c, mla_q_norm_g[l], mla_kv_norm_g[l], mla_w_uq[l],
                                mla_w_ukv[l], cos_mla, sin_mla, need_ctx)
        x = x + g1[:, None, :] * gated_merge([ya_l, yb_l, yc_l, yd_l], gate_l, w_branch[l], w_out[l])
        x = x + g2[:, None, :] * swiglu(modulate(rmsnorm(x, norm2_g[l]), sh2, sc2), w_ffn_in[l], w_ffn_out[l])
        if need_ctx:
            xc = xc + cg1[:, None, :] * gated_merge([ya_c, yb_c, yc_c, yd_c], gate_c, w_branch[l], w_out[l])
            xc = xc + cg2[:, None, :] * swiglu(modulate(rmsnorm(xc, norm2_g[l]), csh2, csc2),
                                               w_ffn_in[l], w_ffn_out[l])
    return rmsnorm(x, final_norm_g)
```

```python
import functools

import numpy as np
import jax
import jax.numpy as jnp
from jax import lax
from jax.experimental import pallas as pl
from jax.experimental.pallas import tpu as pltpu

F32 = jnp.float32
CD = jnp.bfloat16

GRID_W = 64
EPS = 1e-6
ROPE_THETA = 10000.0
ML_HEADS, ML_QK, ML_V, ML_CONV = 4, 64, 128, 3
NA_HEADS, NA_DIM, NA_WIN_R, NA_WIN_C = 8, 64, 8, 16
SW_HEADS, SW_KV_HEADS, SW_DIM, SW_WINDOW = 8, 2, 64, 128
MLA_HEADS, MLA_Q_RANK, MLA_KV_RANK, MLA_NOPE, MLA_ROPE, MLA_V = 4, 384, 256, 128, 64, 128
N_BRANCH, BRANCH_W = 4, 512

ML_QKW = ML_HEADS * ML_QK
ML_VW = ML_HEADS * ML_V
ML_COLS = 2 * ML_QKW + 2 * ML_VW + 4 * ML_HEADS
ML_PAD = 1664
NA_COLS = 3 * NA_HEADS * NA_DIM
SW_COLS = (SW_HEADS + 2 * SW_KV_HEADS) * SW_DIM
MLA_COLS = MLA_Q_RANK + MLA_KV_RANK + MLA_ROPE
MLA_HW = 256
LANE = 128
NEG = -1e30
VMEM_LIMIT = 56 * 1024 * 1024

NA_QROWS = 4
SW_TQ = 256
ML_CHUNK = 256


def _cparams(*sem):
    return pltpu.CompilerParams(dimension_semantics=sem, vmem_limit_bytes=VMEM_LIMIT)


def _row_tile(t, pref=512):
    tm = min(pref, t)
    assert t % tm == 0
    return tm


def _dot(a, b):
    return jnp.dot(a, b, preferred_element_type=F32)


def _dot_nt(a, b):
    return lax.dot_general(a, b, (((1,), (1,)), ((), ())), preferred_element_type=F32)


def _sigmoid(x):
    return 1.0 / (1.0 + jnp.exp(-x))


def _adaln_kernel(c_ref, w_ref, b_ref, o_ref):
    c = c_ref[...]
    a = (c * _sigmoid(c)).astype(CD)
    o_ref[0] = _dot(a, w_ref[0].astype(CD)) + b_ref[0]


def adaln(cc, w_ada, b_ada):
    nl, d, n = w_ada.shape
    tn = 1024
    return pl.pallas_call(
        _adaln_kernel,
        grid=(nl, n // tn),
        in_specs=[pl.BlockSpec((8, d), lambda l, j: (0, 0)),
                  pl.BlockSpec((1, d, tn), lambda l, j: (l, 0, j)),
                  pl.BlockSpec((1, 1, tn), lambda l, j: (l, 0, j))],
        out_specs=pl.BlockSpec((1, 8, tn), lambda l, j: (l, 0, j)),
        out_shape=jax.ShapeDtypeStruct((nl, 8, n), F32),
        compiler_params=_cparams("parallel", "parallel"),
        name="adaln",
    )(cc, w_ada, b_ada.reshape(nl, 1, n))


def _norm_mod_kernel(x_ref, g_ref, sh_ref, sc_ref, o_ref):
    x = x_ref[...]
    y = x * lax.rsqrt(jnp.mean(x * x, axis=-1, keepdims=True) + EPS)
    y = y * g_ref[...]
    o_ref[...] = (y * (1.0 + sc_ref[...]) + sh_ref[...]).astype(o_ref.dtype)


def norm_mod(x, g, shift, scale, out_dtype):
    t, d = x.shape
    tm = _row_tile(t, 256)
    vec = pl.BlockSpec((1, d), lambda i: (0, 0))
    return pl.pallas_call(
        _norm_mod_kernel,
        grid=(t // tm,),
        in_specs=[pl.BlockSpec((tm, d), lambda i: (i, 0)), vec, vec, vec],
        out_specs=pl.BlockSpec((tm, d), lambda i: (i, 0)),
        out_shape=jax.ShapeDtypeStruct((t, d), out_dtype),
        compiler_params=_cparams("parallel"),
        name="norm_mod",
    )(x, g.reshape(1, d), shift.reshape(1, d), scale.reshape(1, d))


def _mm_kernel(a_ref, b_ref, o_ref):
    o_ref[...] = _dot(a_ref[...], b_ref[...]).astype(o_ref.dtype)


def matmul(a, b, out_dtype, tn=None):
    t, k = a.shape
    n = b.shape[1]
    tn = n if tn is None else tn
    tm = _row_tile(t)
    return pl.pallas_call(
        _mm_kernel,
        grid=(n // tn, t // tm),
        in_specs=[pl.BlockSpec((tm, k), lambda j, i: (i, 0)),
                  pl.BlockSpec((k, tn), lambda j, i: (0, j))],
        out_specs=pl.BlockSpec((tm, tn), lambda j, i: (i, j)),
        out_shape=jax.ShapeDtypeStruct((t, n), out_dtype),
        compiler_params=_cparams("parallel", "parallel"),
        name="matmul",
    )(a, b)


def _mm_res_kernel(a_ref, b_ref, x_ref, g_ref, o_ref):
    o_ref[...] = x_ref[...] + g_ref[...] * _dot(a_ref[...], b_ref[...])


def matmul_residual(a, b, x, gate, tn=512):
    t, k = a.shape
    n = b.shape[1]
    tm = _row_tile(t)
    return pl.pallas_call(
        _mm_res_kernel,
        grid=(n // tn, t // tm),
        in_specs=[pl.BlockSpec((tm, k), lambda j, i: (i, 0)),
                  pl.BlockSpec((k, tn), lambda j, i: (0, j)),
                  pl.BlockSpec((tm, tn), lambda j, i: (i, j)),
                  pl.BlockSpec((1, tn), lambda j, i: (0, j))],
        out_specs=pl.BlockSpec((tm, tn), lambda j, i: (i, j)),
        out_shape=jax.ShapeDtypeStruct((t, n), F32),
        compiler_params=_cparams("parallel", "parallel"),
        name="matmul_residual",
    )(a, b, x, gate.reshape(1, n))


def _rope_mm_kernel(n_rope, a_ref, w_ref, wr_ref, cos_ref, sin_ref, o_ref):
    a = a_ref[...]
    p = _dot(a, w_ref[...])
    pr = _dot(a, wr_ref[...])
    cos = cos_ref[...]
    sin = sin_ref[...]
    for j in range(p.shape[1] // LANE):
        sl = slice(j * LANE, (j + 1) * LANE)
        if j < n_rope:
            o_ref[:, sl] = (p[:, sl] * cos + pr[:, sl] * sin).astype(o_ref.dtype)
        else:
            o_ref[:, sl] = p[:, sl].astype(o_ref.dtype)


def rope_matmul(a, w, w_rot, cos, sin, n_rope):
    t, k = a.shape
    n = w.shape[1]
    nr = n_rope * LANE
    tm = _row_tile(t)
    return pl.pallas_call(
        functools.partial(_rope_mm_kernel, n_rope),
        grid=(t // tm,),
        in_specs=[pl.BlockSpec((tm, k), lambda i: (i, 0)),
                  pl.BlockSpec((k, n), lambda i: (0, 0)),
                  pl.BlockSpec((k, nr), lambda i: (0, 0)),
                  pl.BlockSpec((tm, LANE), lambda i: (i, 0)),
                  pl.BlockSpec((tm, LANE), lambda i: (i, 0))],
        out_specs=pl.BlockSpec((tm, n), lambda i: (i, 0)),
        out_shape=jax.ShapeDtypeStruct((t, n), CD),
        compiler_params=_cparams("parallel"),
        name="rope_matmul",
    )(a, w, w_rot, cos, sin)


def _mla_proj_kernel(h_ref, w1_ref, gq_ref, gkv_ref, wq_ref, wqr_ref, wkv_ref,
                     cq_ref, sq_ref, ck_ref, sk_ref, q_out, k_out, v_out):
    p = _dot(h_ref[...], w1_ref[...])
    cq = p[:, :MLA_Q_RANK]
    ckv = p[:, MLA_Q_RANK:MLA_Q_RANK + MLA_KV_RANK]
    o = MLA_Q_RANK + MLA_KV_RANK
    kr = p[:, o:o + LANE]
    krr = p[:, o + LANE:o + 2 * LANE]
    cqn = (cq * lax.rsqrt(jnp.mean(cq * cq, axis=-1, keepdims=True) + EPS) * gq_ref[...]).astype(CD)
    ckvn = (ckv * lax.rsqrt(jnp.mean(ckv * ckv, axis=-1, keepdims=True) + EPS) * gkv_ref[...]).astype(CD)
    qa = _dot(cqn, wq_ref[...])
    qb = _dot(cqn, wqr_ref[...])
    kv = _dot(ckvn, wkv_ref[...])
    krp = (kr * ck_ref[...] + krr * sk_ref[...]).astype(CD)
    cq_t = cq_ref[...]
    sq_t = sq_ref[...]
    for h in range(MLA_HEADS):
        qs = slice(h * MLA_HW, (h + 1) * MLA_HW)
        q_out[:, qs] = (qa[:, qs] * cq_t + qb[:, qs] * sq_t).astype(CD)
        k_out[:, h * MLA_HW:h * MLA_HW + MLA_NOPE] = kv[:, h * 256:h * 256 + MLA_NOPE].astype(CD)
        k_out[:, h * MLA_HW + MLA_NOPE:(h + 1) * MLA_HW] = krp
        v_out[:, h * MLA_V:(h + 1) * MLA_V] = kv[:, h * 256 + MLA_NOPE:(h + 1) * 256].astype(CD)


def mla_proj(h, w1, gq, gkv, wq, wqr, wkv, cos_q, sin_q, cos_k, sin_k):
    t, d = h.shape
    tm = _row_tile(t)
    full = lambda a: pl.BlockSpec(a.shape, lambda i: (0,) * a.ndim)
    rows = lambda w: pl.BlockSpec((tm, w), lambda i: (i, 0))
    hw = MLA_HEADS * MLA_HW
    return pl.pallas_call(
        _mla_proj_kernel,
        grid=(t // tm,),
        in_specs=[rows(d), full(w1), full(gq), full(gkv), full(wq), full(wqr), full(wkv),
                  rows(MLA_HW), rows(MLA_HW), rows(LANE), rows(LANE)],
        out_specs=[rows(hw), rows(hw), rows(MLA_HEADS * MLA_V)],
        out_shape=[jax.ShapeDtypeStruct((t, hw), CD), jax.ShapeDtypeStruct((t, hw), CD),
                   jax.ShapeDtypeStruct((t, MLA_HEADS * MLA_V), CD)],
        compiler_params=_cparams("parallel"),
        name="mla_proj",
    )(h, w1, gq, gkv, wq, wqr, wkv, cos_q, sin_q, cos_k, sin_k)


def _flash_kernel(nk, q_ref, k_ref, v_ref, kc_ref, vc_ref, o_ref, m_sc, l_sc, acc_sc):
    ki = pl.program_id(2)

    @pl.when(ki == 0)
    def _():
        m_sc[...] = jnp.full(m_sc.shape, NEG, F32)
        l_sc[...] = jnp.zeros(l_sc.shape, F32)
        acc_sc[...] = jnp.zeros(acc_sc.shape, F32)

    q = q_ref[...]

    def step(k, v):
        s = _dot_nt(q, k)
        m_prev = m_sc[...]
        m_new = jnp.maximum(m_prev, jnp.max(s, axis=-1, keepdims=True))
        alpha = jnp.exp(m_prev - m_new)
        p = jnp.exp(s - m_new)
        l_sc[...] = alpha * l_sc[...] + jnp.sum(p, axis=-1, keepdims=True)
        acc_sc[...] = alpha * acc_sc[...] + _dot(p.astype(CD), v)
        m_sc[...] = m_new

    step(k_ref[...], v_ref[...])

    @pl.when(ki == nk - 1)
    def _():
        step(kc_ref[...], vc_ref[...])
        o_ref[...] = (acc_sc[...] / l_sc[...]).astype(o_ref.dtype)


def mla_flash(q, k, v, kc, vc):
    s = q.shape[0]
    c = kc.shape[0]
    tq = _row_tile(s, 512)
    tk = _row_tile(s, 512)
    nk = s // tk
    return pl.pallas_call(
        functools.partial(_flash_kernel, nk),
        grid=(MLA_HEADS, s // tq, nk),
        in_specs=[pl.BlockSpec((tq, MLA_HW), lambda h, i, j: (i, h)),
                  pl.BlockSpec((tk, MLA_HW), lambda h, i, j: (j, h)),
                  pl.BlockSpec((tk, MLA_V), lambda h, i, j: (j, h)),
                  pl.BlockSpec((c, MLA_HW), lambda h, i, j: (0, h)),
                  pl.BlockSpec((c, MLA_V), lambda h, i, j: (0, h))],
        out_specs=pl.BlockSpec((tq, MLA_V), lambda h, i, j: (i, h)),
        out_shape=jax.ShapeDtypeStruct((s, MLA_HEADS * MLA_V), CD),
        scratch_shapes=[pltpu.VMEM((tq, 1), F32), pltpu.VMEM((tq, 1), F32),
                        pltpu.VMEM((tq, MLA_V), F32)],
        compiler_params=_cparams("parallel", "parallel", "arbitrary"),
        name="mla_flash",
    )(q, k, v, kc, vc)


def _dense_attn_kernel(n_heads, group, dqk, dv, q_ref, k_ref, v_ref, sink_ref, o_ref):
    q = q_ref[...]
    k = k_ref[...]
    v = v_ref[...]
    for h in range(n_heads):
        g = h // group
        s = _dot_nt(q[:, h * dqk:(h + 1) * dqk], k[:, g * dqk:(g + 1) * dqk])
        m = jnp.max(s, axis=-1, keepdims=True)
        l = jnp.zeros_like(m)
        if sink_ref is not None:
            sk = sink_ref[0:1, h:h + 1]
            m = jnp.maximum(m, sk)
            l = jnp.exp(sk - m)
        p = jnp.exp(s - m)
        l = l + jnp.sum(p, axis=-1, keepdims=True)
        o = _dot(p.astype(CD), v[:, g * dv:(g + 1) * dv]) / l
        o_ref[:, h * dv:(h + 1) * dv] = o.astype(o_ref.dtype)


def dense_attn(qkv_specs, n_heads, group, dqk, dv, sink=None):
    c = qkv_specs[0][0].shape[0]
    arrays = [a for a, _, _ in qkv_specs]
    specs = [pl.BlockSpec((c, w), functools.partial(lambda b, i: (0, b), b)) for _, w, b in qkv_specs]
    if sink is None:
        kern = lambda q, k, v, o: _dense_attn_kernel(n_heads, group, dqk, dv, q, k, v, None, o)
    else:
        kern = functools.partial(_dense_attn_kernel, n_heads, group, dqk, dv)
        arrays.append(sink)
        specs.append(pl.BlockSpec(sink.shape, lambda i: (0, 0)))
    return pl.pallas_call(
        kern,
        grid=(1,),
        in_specs=specs,
        out_specs=pl.BlockSpec((c, n_heads * dv), lambda i: (0, 0)),
        out_shape=jax.ShapeDtypeStruct((c, n_heads * dv), CD),
        compiler_params=_cparams("arbitrary"),
        name="dense_attn",
    )(*arrays)


def _na_kernel(q_ref, kp_ref, ko_ref, kn_ref, vp_ref, vo_ref, vn_ref, kc_ref, vc_ref, bias_ref, o_ref):
    q = q_ref[...]
    ks = (kp_ref[...], ko_ref[...], kn_ref[...])
    vs = (vp_ref[...], vo_ref[...], vn_ref[...])
    kc = kc_ref[...]
    vc = vc_ref[...]
    tb = q.shape[0]
    for h in range(NA_HEADS):
        hs = slice(h * NA_DIM, (h + 1) * NA_DIM)
        qh = q[:, hs]
        bias = bias_ref[0, h]
        s_loc = [_dot_nt(qh, kk[:, hs]) + bias[:, j * tb:(j + 1) * tb] for j, kk in enumerate(ks)]
        s_ctx = _dot_nt(qh, kc[:, hs])
        m = jnp.max(s_ctx, axis=-1, keepdims=True)
        for s in s_loc:
            m = jnp.maximum(m, jnp.max(s, axis=-1, keepdims=True))
        p_ctx = jnp.exp(s_ctx - m)
        l = jnp.sum(p_ctx, axis=-1, keepdims=True)
        o = _dot(p_ctx.astype(CD), vc[:, hs])
        for s, vv in zip(s_loc, vs):
            p = jnp.exp(s - m)
            l = l + jnp.sum(p, axis=-1, keepdims=True)
            o = o + _dot(p.astype(CD), vv[:, hs])
        o_ref[:, hs] = (o / l).astype(o_ref.dtype)


def _na_bias(rpb, rows):
    qr = NA_QROWS
    tb = qr * GRID_W
    nb = rows // qr
    out = []
    for b in (0, 1, nb - 1):
        t = np.arange(tb)
        r = qr * b + t // GRID_W
        qc = t % GRID_W
        rs = np.clip(r - NA_WIN_R // 2, 0, rows - NA_WIN_R)
        cs = np.clip(qc - NA_WIN_C // 2, 0, GRID_W - NA_WIN_C)
        u = np.arange(3 * tb)
        kr = qr * (b - 1) + u // GRID_W
        kcol = u % GRID_W
        blk = b - 1 + u // tb
        valid = ((kr[None, :] >= rs[:, None]) & (kr[None, :] < rs[:, None] + NA_WIN_R)
                 & (kcol[None, :] >= cs[:, None]) & (kcol[None, :] < cs[:, None] + NA_WIN_C)
                 & (blk[None, :] >= 0) & (blk[None, :] < nb))
        ri = np.clip(kr[None, :] - r[:, None] + NA_WIN_R - 1, 0, 2 * NA_WIN_R - 2)
        ci = np.clip(kcol[None, :] - qc[:, None] + NA_WIN_C - 1, 0, 2 * NA_WIN_C - 2)
        bias = rpb.astype(F32)[:, ri, ci]
        out.append(jnp.where(valid[None], bias, NEG))
    return jnp.stack(out)


def na_attn(p_lat, p_ctx, rpb, rows):
    s = p_lat.shape[0]
    c = p_ctx.shape[0]
    w = NA_HEADS * NA_DIM
    tb = NA_QROWS * GRID_W
    nb = s // tb
    assert rows % NA_QROWS == 0 and nb >= 3 and rows >= 2 * NA_WIN_R
    bias = _na_bias(rpb, rows)

    def blk(col, off):
        return pl.BlockSpec((tb, w), lambda i: (jnp.clip(i + off, 0, nb - 1), col))

    return pl.pallas_call(
        _na_kernel,
        grid=(nb,),
        in_specs=[blk(0, 0), blk(1, -1), blk(1, 0), blk(1, 1), blk(2, -1), blk(2, 0), blk(2, 1),
                  pl.BlockSpec((c, w), lambda i: (0, 1)),
                  pl.BlockSpec((c, w), lambda i: (0, 2)),
                  pl.BlockSpec((1, NA_HEADS, tb, 3 * tb),
                               lambda i: (jnp.where(i == 0, 0, jnp.where(i == nb - 1, 2, 1)), 0, 0, 0))],
        out_specs=pl.BlockSpec((tb, w), lambda i: (i, 0)),
        out_shape=jax.ShapeDtypeStruct((s, w), CD),
        compiler_params=_cparams("parallel"),
        name="na_attn",
    )(p_lat, p_lat, p_lat, p_lat, p_lat, p_lat, p_lat, p_ctx, p_ctx, bias)


def _sw_kernel(seq, q_ref, kp_ref, ko_ref, kn_ref, vp_ref, vo_ref, vn_ref, kc_ref, vc_ref, sink_ref, o_ref):
    i = pl.program_id(0)
    q = q_ref[...]
    tq = q.shape[0]
    kcat = jnp.concatenate([kp_ref[...], ko_ref[...], kn_ref[...]], axis=0)
    vcat = jnp.concatenate([vp_ref[...], vo_ref[...], vn_ref[...]], axis=0)
    kc = kc_ref[...]
    vc = vc_ref[...]
    nkk = kcat.shape[0]
    r = lax.broadcasted_iota(jnp.int32, (tq, nkk), 0)
    cc = lax.broadcasted_iota(jnp.int32, (tq, nkk), 1)
    kpos = i * tq - SW_WINDOW + cc
    d = cc - r
    mask = (d >= 0) & (d <= 2 * SW_WINDOW) & (kpos >= 0) & (kpos < seq)
    grp = SW_HEADS // SW_KV_HEADS
    for h in range(SW_HEADS):
        g = h // grp
        gs = slice(g * SW_DIM, (g + 1) * SW_DIM)
        qh = q[:, h * SW_DIM:(h + 1) * SW_DIM]
        s = jnp.where(mask, _dot_nt(qh, kcat[:, gs]), NEG)
        sc = _dot_nt(qh, kc[:, gs])
        sk = sink_ref[0:1, h:h + 1]
        m = jnp.maximum(jnp.maximum(jnp.max(s, axis=-1, keepdims=True),
                                    jnp.max(sc, axis=-1, keepdims=True)), sk)
        p = jnp.exp(s - m)
        pc = jnp.exp(sc - m)
        l = jnp.sum(p, axis=-1, keepdims=True) + jnp.sum(pc, axis=-1, keepdims=True) + jnp.exp(sk - m)
        o = (_dot(p.astype(CD), vcat[:, gs]) + _dot(pc.astype(CD), vc[:, gs])) / l
        o_ref[:, h * SW_DIM:(h + 1) * SW_DIM] = o.astype(o_ref.dtype)


def sw_attn(p_lat, p_ctx, sink):
    s = p_lat.shape[0]
    c = p_ctx.shape[0]
    tq = SW_TQ
    assert s % tq == 0 and tq % SW_WINDOW == 0
    nq = s // tq
    per = tq // SW_WINDOW
    nwb = s // SW_WINDOW
    kcol = SW_HEADS * SW_DIM // LANE
    prev = lambda col: pl.BlockSpec((SW_WINDOW, LANE), lambda i: (jnp.maximum(i * per - 1, 0), col))
    own = lambda col: pl.BlockSpec((tq, LANE), lambda i: (i, col))
    nxt = lambda col: pl.BlockSpec((SW_WINDOW, LANE), lambda i: (jnp.minimum((i + 1) * per, nwb - 1), col))
    return pl.pallas_call(
        functools.partial(_sw_kernel, s),
        grid=(nq,),
        in_specs=[pl.BlockSpec((tq, SW_HEADS * SW_DIM), lambda i: (i, 0)),
                  prev(kcol), own(kcol), nxt(kcol), prev(kcol + 1), own(kcol + 1), nxt(kcol + 1),
                  pl.BlockSpec((c, LANE), lambda i: (0, kcol)),
                  pl.BlockSpec((c, LANE), lambda i: (0, kcol + 1)),
                  pl.BlockSpec(sink.shape, lambda i: (0, 0))],
        out_specs=pl.BlockSpec((tq, SW_HEADS * SW_DIM), lambda i: (i, 0)),
        out_shape=jax.ShapeDtypeStruct((s, SW_HEADS * SW_DIM), CD),
        compiler_params=_cparams("parallel"),
        name="sw_attn",
    )(p_lat, p_lat, p_lat, p_lat, p_lat, p_lat, p_lat, p_ctx, p_ctx, sink)


def _ml_prep_kernel(nt, x_ref, xp_ref, xn_ref, g_ref, w_ref, b_ref, q_out, k_out, g_out):
    i = pl.program_id(0)
    x = x_ref[...]
    tm = x.shape[0]
    row = lax.broadcasted_iota(jnp.int32, x.shape, 0)
    has_prev = jnp.where(i > 0, 1.0, 0.0)
    has_next = jnp.where(i < nt - 1, 1.0, 0.0)
    prev_row = xp_ref[7:8, :] * has_prev
    next_row = xn_ref[0:1, :] * has_next
    xm = jnp.where(row == 0, prev_row, pltpu.roll(x, 1, axis=0))
    xq = jnp.where(row == tm - 1, next_row, pltpu.roll(x, tm - 1, axis=0))
    y = w_ref[0:1, :] * xm + w_ref[1:2, :] * x + w_ref[2:3, :] * xq
    y = y * _sigmoid(y)
    q_out[...] = y[:, :ML_QKW].astype(CD)
    k_out[...] = (y[:, ML_QKW:] * (ML_QK ** -0.5)).astype(CD)
    g = g_ref[...] + b_ref[...]
    lane = lax.broadcasted_iota(jnp.int32, g.shape, 1)
    logsig = jnp.minimum(g, 0.0) - jnp.log(1.0 + jnp.exp(-jnp.abs(g)))
    g_out[...] = jnp.where(lane < 2 * ML_HEADS, g, logsig)


def ml_prep(p_ml, conv_w, gate_b):
    t = p_ml.shape[0]
    tm = _row_tile(t, 256)
    nt = t // tm
    wq = 2 * ML_QKW
    gcol = (2 * ML_QKW + 2 * ML_VW) // LANE
    w8 = jnp.zeros((8, wq), F32).at[:ML_CONV].set(conv_w.astype(F32))
    b = jnp.zeros((1, LANE), F32).at[0, :4 * ML_HEADS].set(gate_b.astype(F32).reshape(-1))
    h8 = tm // 8
    return pl.pallas_call(
        functools.partial(_ml_prep_kernel, nt),
        grid=(nt,),
        in_specs=[pl.BlockSpec((tm, wq), lambda i: (i, 0)),
                  pl.BlockSpec((8, wq), lambda i: (jnp.maximum(i * h8 - 1, 0), 0)),
                  pl.BlockSpec((8, wq), lambda i: (jnp.minimum((i + 1) * h8, t // 8 - 1), 0)),
                  pl.BlockSpec((tm, LANE), lambda i: (i, gcol)),
                  pl.BlockSpec((8, wq), lambda i: (0, 0)),
                  pl.BlockSpec((1, LANE), lambda i: (0, 0))],
        out_specs=[pl.BlockSpec((tm, ML_QKW), lambda i: (i, 0)),
                   pl.BlockSpec((tm, ML_QKW), lambda i: (i, 0)),
                   pl.BlockSpec((tm, LANE), lambda i: (i, 0))],
        out_shape=[jax.ShapeDtypeStruct((t, ML_QKW), CD), jax.ShapeDtypeStruct((t, ML_QKW), CD),
                   jax.ShapeDtypeStruct((t, LANE), F32)],
        compiler_params=_cparams("parallel"),
        name="ml_prep",
    )(p_ml, p_ml, p_ml, p_ml, w8, b)


def _ml_scan_kernel(qf_ref, ktf_ref, vf_ref, gcf_ref, grf_ref,
                    qb_ref, ktb_ref, vb_ref, gcb_ref, grb_ref,
                    c0_ref, m0_ref, hf_ref, hb_ref, c_ref, m_ref):
    @pl.when(pl.program_id(0) == 0)
    def _():
        c_ref[...] = c0_ref[...]
        m_ref[...] = m0_ref[...]

    ln = qf_ref.shape[0]
    r = lax.broadcasted_iota(jnp.int32, (ln, ln), 0)
    c = lax.broadcasted_iota(jnp.int32, (ln, ln), 1)
    lower = r >= c
    upper = c >= r
    lower_f = jnp.where(lower, 1.0, 0.0)
    upper_f = jnp.where(upper, 1.0, 0.0)
    ones_col = jnp.where(lax.broadcasted_iota(jnp.int32, (ln, LANE), 1) == 0, 1.0, 0.0).astype(CD)
    hi = lax.Precision.HIGHEST

    streams = ((qf_ref, ktf_ref, vf_ref, gcf_ref, grf_ref, hf_ref, lower, lower_f, upper_f, ln - 1),
               (qb_ref, ktb_ref, vb_ref, gcb_ref, grb_ref, hb_ref, upper, upper_f, lower_f, 0))
    for d, (q_ref, kt_ref, v_ref, gc_ref, gr_ref, h_ref, mask, tri_c, tri_r, last) in enumerate(streams):
        gcol = gc_ref[...]
        grow = gr_ref[...]
        bcol = jnp.dot(tri_c, gcol, precision=hi, preferred_element_type=F32)
        brow = jnp.dot(grow, tri_r, precision=hi, preferred_element_type=F32)
        q = q_ref[...]
        kt = kt_ref[...]
        v = v_ref[...]
        for hd in range(ML_HEADS):
            ch = d * ML_HEADS + hd
            ic = d * ML_HEADS + hd
            fc = 2 * ML_HEADS + ic
            i_c = gcol[:, ic:ic + 1]
            b_c = bcol[:, fc:fc + 1]
            i_r = grow[ic:ic + 1, :]
            b_r = brow[fc:fc + 1, :]
            m_prev = m_ref[ch, 0:1, 0:1]
            cst = c_ref[ch]
            dmat = jnp.where(mask, b_c - b_r + i_r, NEG)
            inter = b_c + m_prev
            m_t = jnp.maximum(inter, jnp.max(dmat, axis=-1, keepdims=True))
            qh = q[:, hd * ML_QK:(hd + 1) * ML_QK]
            kth = kt[hd * ML_QK:(hd + 1) * ML_QK, :]
            s = lax.dot_general(qh, kth, (((1,), (0,)), ((), ())), preferred_element_type=F32)
            s = s * jnp.exp(dmat - m_t)
            w_inter = jnp.exp(inter - m_t)
            v_aug = jnp.concatenate([v[:, hd * ML_V:(hd + 1) * ML_V].astype(CD), ones_col], axis=1)
            num = _dot(s.astype(CD), v_aug) + w_inter * _dot(qh, cst.astype(CD))
            den = num[:, ML_V:ML_V + 1]
            h = num[:, :ML_V] / jnp.maximum(jnp.abs(den), jnp.exp(-m_t))
            h_ref[:, hd * ML_V:(hd + 1) * ML_V] = h
            b_end = b_c[last:last + 1, :]
            m_new = jnp.maximum(b_end + m_prev, jnp.max(b_end - b_c + i_c, axis=0, keepdims=True))
            w_s = jnp.exp(b_end - b_r + i_r - m_new)
            w_c = jnp.exp(b_end + m_prev - m_new)
            ktw = (kth.astype(F32) * w_s).astype(CD)
            c_ref[ch] = w_c * cst + _dot(ktw, v_aug)
            m_ref[ch] = jnp.broadcast_to(m_new, m_ref.shape[1:])


def ml_scan(qc, kc, p_ml, gates, state):
    t = qc.shape[0]
    ln = min(ML_CHUNK, t)
    assert t % ln == 0
    nc = t // ln
    kt = kc.T
    gr = gates[:, :4 * ML_HEADS].T
    c0, m0 = state
    vcol = 2 * ML_QKW // ML_VW
    fwd = lambda j: j
    bwd = lambda j: nc - 1 - j

    def specs(ix):
        return [pl.BlockSpec((ln, ML_QKW), lambda j: (ix(j), 0)),
                pl.BlockSpec((ML_QKW, ln), lambda j: (0, ix(j))),
                pl.BlockSpec((ln, ML_VW), lambda j: (ix(j), vcol)),
                pl.BlockSpec((ln, LANE), lambda j: (ix(j), 0)),
                pl.BlockSpec((4 * ML_HEADS, ln), lambda j: (0, ix(j)))]

    st_specs = [pl.BlockSpec(c0.shape, lambda j: (0, 0, 0)), pl.BlockSpec(m0.shape, lambda j: (0, 0, 0))]
    return pl.pallas_call(
        _ml_scan_kernel,
        grid=(nc,),
        in_specs=specs(fwd) + specs(bwd) + st_specs,
        out_specs=[pl.BlockSpec((ln, ML_VW), lambda j: (fwd(j), 0)),
                   pl.BlockSpec((ln, ML_VW), lambda j: (bwd(j), 0))] + st_specs,
        out_shape=[jax.ShapeDtypeStruct((t, ML_VW), F32), jax.ShapeDtypeStruct((t, ML_VW), F32),
                   jax.ShapeDtypeStruct(c0.shape, F32), jax.ShapeDtypeStruct(m0.shape, F32)],
        compiler_params=_cparams("arbitrary"),
        name="ml_scan",
    )(qc, kt, p_ml, gates, gr, qc, kt, p_ml, gates, gr, c0, m0)


def _ml_finish_kernel(hf_ref, hb_ref, o_ref, g_ref, y_ref):
    h = hf_ref[...] + hb_ref[...]
    og = _sigmoid(o_ref[...])
    g = g_ref[...]
    for hd in range(ML_HEADS):
        sl = slice(hd * ML_V, (hd + 1) * ML_V)
        hh = h[:, sl]
        y = hh * lax.rsqrt(jnp.mean(hh * hh, axis=-1, keepdims=True) + EPS) * g[:, sl]
        y_ref[:, sl] = (y * og[:, sl]).astype(y_ref.dtype)


def ml_finish(hf, hb, p_ml, norm_g):
    t = hf.shape[0]
    tm = _row_tile(t, 256)
    ocol = (2 * ML_QKW + ML_VW) // ML_VW
    blk = pl.BlockSpec((tm, ML_VW), lambda i: (i, 0))
    return pl.pallas_call(
        _ml_finish_kernel,
        grid=(t // tm,),
        in_specs=[blk, blk, pl.BlockSpec((tm, ML_VW), lambda i: (i, ocol)),
                  pl.BlockSpec((1, ML_VW), lambda i: (0, 0))],
        out_specs=blk,
        out_shape=jax.ShapeDtypeStruct((t, ML_VW), CD),
        compiler_params=_cparams("parallel"),
        name="ml_finish",
    )(hf, hb, p_ml, norm_g.reshape(1, ML_VW).astype(F32))


def _merge_kernel(h_ref, ya_ref, yb_ref, yc_ref, yd_ref, g0_ref, g1_ref, g2_ref, g3_ref, wb_ref, o_ref):
    h = h_ref[...]
    acc = None
    for i, (y_ref, g_ref) in enumerate(zip((ya_ref, yb_ref, yc_ref, yd_ref), (g0_ref, g1_ref, g2_ref, g3_ref))):
        term = _sigmoid(_dot(h, g_ref[...])) * _dot(y_ref[...], wb_ref[i])
        acc = term if acc is None else acc + term
    o_ref[...] = acc.astype(o_ref.dtype)


def gated_merge(h, ys, w_gate, w_branch, tn=512):
    t, d = h.shape
    tm = _row_tile(t)
    nj = d // tn
    yspec = pl.BlockSpec((tm, BRANCH_W), lambda j, i: (i, 0))
    gspec = lambda b: pl.BlockSpec((d, tn), lambda j, i: (0, b * nj + j))
    return pl.pallas_call(
        _merge_kernel,
        grid=(nj, t // tm),
        in_specs=[pl.BlockSpec((tm, d), lambda j, i: (i, 0)), yspec, yspec, yspec, yspec,
                  gspec(0), gspec(1), gspec(2), gspec(3),
                  pl.BlockSpec((N_BRANCH, BRANCH_W, tn), lambda j, i: (0, 0, j))],
        out_specs=pl.BlockSpec((tm, tn), lambda j, i: (i, j)),
        out_shape=jax.ShapeDtypeStruct((t, d), CD),
        compiler_params=_cparams("parallel", "parallel"),
        name="gated_merge",
    )(h, *ys, w_gate, w_gate, w_gate, w_gate, w_branch)


def _ffn1_kernel(h_ref, wa_ref, wg_ref, o_ref):
    h = h_ref[...]
    a = _dot(h, wa_ref[...])
    g = _dot(h, wg_ref[...])
    o_ref[...] = (a * _sigmoid(a) * g).astype(o_ref.dtype)


def ffn_up(h, w, tn=512):
    t, d = h.shape
    dff = w.shape[1] // 2
    tm = _row_tile(t)
    nj = dff // tn
    return pl.pallas_call(
        _ffn1_kernel,
        grid=(nj, t // tm),
        in_specs=[pl.BlockSpec((tm, d), lambda j, i: (i, 0)),
                  pl.BlockSpec((d, tn), lambda j, i: (0, j)),
                  pl.BlockSpec((d, tn), lambda j, i: (0, nj + j))],
        out_specs=pl.BlockSpec((tm, tn), lambda j, i: (i, j)),
        out_shape=jax.ShapeDtypeStruct((t, dff), CD),
        compiler_params=_cparams("parallel", "parallel"),
        name="ffn_up",
    )(h, w, w)


def _rot_cols(w, dim):
    k, n = w.shape
    w4 = w.reshape(k, n // dim, 2, dim // 2)
    return jnp.stack([-w4[:, :, 1], w4[:, :, 0]], axis=2).reshape(k, n)


def _rope_tables(n_tokens, dim):
    a = dim // 2
    inv = 1.0 / (ROPE_THETA ** (jnp.arange(0, a, 2, dtype=F32) / a))
    t = jnp.arange(n_tokens)
    row = (t // GRID_W).astype(F32)
    col = (t % GRID_W).astype(F32)
    ang = jnp.concatenate([row[:, None] * inv, col[:, None] * inv], axis=-1)
    return jnp.cos(ang), jnp.sin(ang)


def _layer_weights(w_in_l, mla_w_uq_l, mla_w_ukv_l):
    d = w_in_l.shape[0]
    o_na = ML_COLS
    o_sw = o_na + NA_COLS
    o_mla = o_sw + SW_COLS
    o_gate = o_mla + MLA_COLS
    w = {}
    w["ml"] = jnp.pad(w_in_l[:, :ML_COLS], ((0, 0), (0, ML_PAD - ML_COLS))).astype(CD)
    na = w_in_l[:, o_na:o_sw]
    nq = NA_HEADS * NA_DIM
    w["na"] = jnp.concatenate([na[:, :nq] * (NA_DIM ** -0.5), na[:, nq:]], axis=1).astype(CD)
    sq = SW_HEADS * SW_DIM
    skv = SW_KV_HEADS * SW_DIM
    sw_q = w_in_l[:, o_sw:o_sw + sq] * (SW_DIM ** -0.5)
    sw_k = w_in_l[:, o_sw + sq:o_sw + sq + skv]
    sw_v = w_in_l[:, o_sw + sq + skv:o_mla]
    w["sw"] = jnp.concatenate([sw_q, sw_k, sw_v], axis=1).astype(CD)
    w["sw_rot"] = jnp.concatenate([_rot_cols(sw_q, SW_DIM), _rot_cols(sw_k, SW_DIM)], axis=1).astype(CD)
    m_cq = w_in_l[:, o_mla:o_mla + MLA_Q_RANK + MLA_KV_RANK]
    m_kr = w_in_l[:, o_mla + MLA_Q_RANK + MLA_KV_RANK:o_gate]
    z = jnp.zeros((d, LANE - MLA_ROPE), F32)
    w["mla1"] = jnp.concatenate([m_cq, m_kr, z, _rot_cols(m_kr, MLA_ROPE), z], axis=1).astype(CD)
    uq = mla_w_uq_l.reshape(MLA_Q_RANK, MLA_HEADS, MLA_NOPE + MLA_ROPE)
    zq = jnp.zeros((MLA_Q_RANK, MLA_HEADS, MLA_HW - MLA_NOPE - MLA_ROPE), F32)
    w["mla_q"] = jnp.concatenate([uq, zq], axis=2).reshape(MLA_Q_RANK, -1).astype(CD)
    uqr = _rot_cols(uq[:, :, MLA_NOPE:].reshape(MLA_Q_RANK, -1), MLA_ROPE).reshape(MLA_Q_RANK, MLA_HEADS, MLA_ROPE)
    w["mla_q_rot"] = jnp.concatenate([jnp.zeros_like(uq[:, :, :MLA_NOPE]), uqr, zq], axis=2
                                     ).reshape(MLA_Q_RANK, -1).astype(CD)
    w["mla_kv"] = mla_w_ukv_l.astype(CD)
    w["gate"] = w_in_l[:, o_gate:].astype(CD)
    return w


def _tables(s, c):
    cos_sw, sin_sw = _rope_tables(s, SW_DIM)
    cos_m, sin_m = _rope_tables(s, MLA_ROPE)
    tile2 = lambda a: jnp.concatenate([a, a, a, a], axis=1)
    scale = (MLA_NOPE + MLA_ROPE) ** -0.5
    zpad = jnp.zeros((s, MLA_HW - MLA_NOPE - MLA_ROPE), F32)
    lat = dict(
        sw_cos=tile2(cos_sw), sw_sin=tile2(sin_sw),
        q_cos=scale * jnp.concatenate([jnp.ones((s, MLA_NOPE), F32), cos_m, cos_m, zpad], axis=1),
        q_sin=scale * jnp.concatenate([jnp.zeros((s, MLA_NOPE), F32), sin_m, sin_m, zpad], axis=1),
        k_cos=jnp.concatenate([cos_m, cos_m, zpad], axis=1),
        k_sin=jnp.concatenate([sin_m, sin_m, zpad], axis=1))
    ctx = dict(
        sw_cos=jnp.ones((c, LANE), F32), sw_sin=jnp.zeros((c, LANE), F32),
        q_cos=jnp.full((c, MLA_HW), scale, F32), q_sin=jnp.zeros((c, MLA_HW), F32),
        k_cos=jnp.ones((c, LANE), F32), k_sin=jnp.zeros((c, LANE), F32))
    return lat, ctx


def _project(h, w, tab, gq, gkv):
    p_ml = matmul(h, w["ml"], F32)
    p_na = matmul(h, w["na"], CD)
    n_rope = (SW_HEADS + SW_KV_HEADS) * SW_DIM // LANE
    p_sw = rope_matmul(h, w["sw"], w["sw_rot"], tab["sw_cos"], tab["sw_sin"], n_rope)
    mq, mk, mv = mla_proj(h, w["mla1"], gq, gkv, w["mla_q"], w["mla_q_rot"], w["mla_kv"],
                          tab["q_cos"], tab["q_sin"], tab["k_cos"], tab["k_sin"])
    return p_ml, p_na, p_sw, (mq, mk, mv)


def _dense_tail(x, h, ys, w, w_branch_l, w_out_l, g1, norm2_g_l, sh2, sc2, w_ffn_in_l, w_ffn_out_l, g2):
    merged = gated_merge(h, ys, w["gate"], w_branch_l)
    x = matmul_residual(merged, w_out_l, x, g1)
    h2 = norm_mod(x, norm2_g_l, sh2, sc2, CD)
    u = ffn_up(h2, w_ffn_in_l)
    return matmul_residual(u, w_ffn_out_l, x, g2)


def kernel(x, c, ctx, c_ctx, w_ada, b_ada, norm1_g, norm2_g, w_in, ml_conv_w, ml_gate_b, ml_norm_g, na_rpb,
           sw_sink, mla_q_norm_g, mla_kv_norm_g, mla_w_uq, mla_w_ukv, w_branch, w_out, w_ffn_in, w_ffn_out,
           final_norm_g):
    bsz, s, d = x.shape
    assert bsz == 1
    depth = w_in.shape[0]
    n_ctx = ctx.shape[1]
    rows = s // GRID_W
    xl = x[0].astype(F32)
    xc = ctx[0].astype(F32)
    cc = jnp.zeros((8, d), F32).at[0].set(c[0]).at[1].set(c_ctx)
    mod = adaln(cc, w_ada, b_ada)
    tab_l, tab_c = _tables(s, n_ctx)
    zero_state = (jnp.zeros((2 * ML_HEADS, ML_QK, 2 * ML_V), F32), jnp.zeros((2 * ML_HEADS, 8, LANE), F32))

    for l in range(depth):
        need_ctx = l < depth - 1
        w = _layer_weights(w_in[l], mla_w_uq[l], mla_w_ukv[l])
        wb = w_branch[l].astype(CD)
        wo = w_out[l].astype(CD)
        wf1 = w_ffn_in[l].astype(CD)
        wf2 = w_ffn_out[l].astype(CD)
        sh1, sc1, g1, sh2, sc2, g2 = [mod[l, 0, i * d:(i + 1) * d] for i in range(6)]
        csh1, csc1, cg1, csh2, csc2, cg2 = [mod[l, 1, i * d:(i + 1) * d] for i in range(6)]
        gq = mla_q_norm_g[l].reshape(1, -1).astype(F32)
        gkv = mla_kv_norm_g[l].reshape(1, -1).astype(F32)
        sink = sw_sink[l].reshape(1, -1).astype(F32)

        hl = norm_mod(xl, norm1_g[l], sh1, sc1, CD)
        hc = norm_mod(xc, norm1_g[l], csh1, csc1, CD)
        ml_l, na_l, sw_l, (mq_l, mk_l, mv_l) = _project(hl, w, tab_l, gq, gkv)
        ml_c, na_c, sw_c, (mq_c, mk_c, mv_c) = _project(hc, w, tab_c, gq, gkv)

        qc_c, kc_c, gt_c = ml_prep(ml_c, ml_conv_w[l], ml_gate_b[l])
        qc_l, kc_l, gt_l = ml_prep(ml_l, ml_conv_w[l], ml_gate_b[l])
        hf_c, hb_c, cst, mst = ml_scan(qc_c, kc_c, ml_c, gt_c, zero_state)
        hf_l, hb_l, _, _ = ml_scan(qc_l, kc_l, ml_l, gt_l, (cst, mst))
        ya_l = ml_finish(hf_l, hb_l, ml_l, ml_norm_g[l])
        yb_l = na_attn(na_l, na_c, na_rpb[l], rows)
        yc_l = sw_attn(sw_l, sw_c, sink)
        yd_l = mla_flash(mq_l, mk_l, mv_l, mk_c, mv_c)

        xl = _dense_tail(xl, hl, (ya_l, yb_l, yc_l, yd_l), w, wb, wo, g1, norm2_g[l], sh2, sc2, wf1, wf2, g2)
        if need_ctx:
            ya_c = ml_finish(hf_c, hb_c, ml_c, ml_norm_g[l])
            nw = NA_HEADS * NA_DIM
            yb_c = dense_attn([(na_c, nw, 0), (na_c, nw, 1), (na_c, nw, 2)], NA_HEADS, 1, NA_DIM, NA_DIM)
            kcol = SW_HEADS * SW_DIM // LANE
            yc_c = dense_attn([(sw_c, SW_HEADS * SW_DIM, 0), (sw_c, LANE, kcol), (sw_c, LANE, kcol + 1)],
                              SW_HEADS, SW_HEADS // SW_KV_HEADS, SW_DIM, SW_DIM, sink=sink)
            yd_c = dense_attn([(mq_c, MLA_HEADS * MLA_HW, 0), (mk_c, MLA_HEADS * MLA_HW, 0),
                               (mv_c, MLA_HEADS * MLA_V, 0)], MLA_HEADS, 1, MLA_HW, MLA_V)
            xc = _dense_tail(xc, hc, (ya_c, yb_c, yc_c, yd_c), w, wb, wo, cg1, norm2_g[l], csh2, csc2,
                             wf1, wf2, cg2)

    zeros = jnp.zeros((d,), F32)
    out = norm_mod(xl, final_norm_g, zeros, zeros, F32)
    return out[None].astype(x.dtype)
```

```python
import functools

import numpy as np
import jax
import jax.numpy as jnp
from jax import lax
from jax.experimental import pallas as pl
from jax.experimental.pallas import tpu as pltpu

F32 = jnp.float32
CD = jnp.bfloat16

GRID_W = 64
EPS = 1e-6
ROPE_THETA = 10000.0
ML_HEADS, ML_QK, ML_V, ML_CONV = 4, 64, 128, 3
NA_HEADS, NA_DIM, NA_WIN_R, NA_WIN_C = 8, 64, 8, 16
SW_HEADS, SW_KV_HEADS, SW_DIM, SW_WINDOW = 8, 2, 64, 128
MLA_HEADS, MLA_Q_RANK, MLA_KV_RANK, MLA_NOPE, MLA_ROPE, MLA_V = 4, 384, 256, 128, 64, 128
N_BRANCH, BRANCH_W = 4, 512

ML_QKW = ML_HEADS * ML_QK
ML_VW = ML_HEADS * ML_V
ML_COLS = 2 * ML_QKW + 2 * ML_VW + 4 * ML_HEADS
ML_PAD = 1664
NA_COLS = 3 * NA_HEADS * NA_DIM
SW_COLS = (SW_HEADS + 2 * SW_KV_HEADS) * SW_DIM
MLA_COLS = MLA_Q_RANK + MLA_KV_RANK + MLA_ROPE
MLA_HW = 256
LANE = 128
NEG = -1e30
VMEM_LIMIT = 56 * 1024 * 1024

NA_QROWS = 4
SW_TQ = 256
ML_CHUNK = 256


def _cparams(*sem):
    return pltpu.CompilerParams(dimension_semantics=sem, vmem_limit_bytes=VMEM_LIMIT)


def _row_tile(t, pref=512):
    tm = min(pref, t)
    assert t % tm == 0
    return tm


def _dot(a, b):
    return jnp.dot(a, b, preferred_element_type=F32)


def _dot_nt(a, b):
    return lax.dot_general(a, b, (((1,), (1,)), ((), ())), preferred_element_type=F32)


def _sigmoid(x):
    return 1.0 / (1.0 + jnp.exp(-x))


def _adaln_kernel(c_ref, w_ref, b_ref, o_ref):
    c = c_ref[...]
    a = (c * _sigmoid(c)).astype(CD)
    o_ref[0] = _dot(a, w_ref[0].astype(CD)) + b_ref[0]


def adaln(cc, w_ada, b_ada):
    nl, d, n = w_ada.shape
    tn = 1024
    return pl.pallas_call(
        _adaln_kernel,
        grid=(nl, n // tn),
        in_specs=[pl.BlockSpec((8, d), lambda l, j: (0, 0)),
                  pl.BlockSpec((1, d, tn), lambda l, j: (l, 0, j)),
                  pl.BlockSpec((1, 1, tn), lambda l, j: (l, 0, j))],
        out_specs=pl.BlockSpec((1, 8, tn), lambda l, j: (l, 0, j)),
        out_shape=jax.ShapeDtypeStruct((nl, 8, n), F32),
        compiler_params=_cparams("parallel", "parallel"),
        name="adaln",
    )(cc, w_ada, b_ada.reshape(nl, 1, n))


def _norm_mod_kernel(x_ref, g_ref, sh_ref, sc_ref, o_ref):
    x = x_ref[...]
    y = x * lax.rsqrt(jnp.mean(x * x, axis=-1, keepdims=True) + EPS)
    y = y * g_ref[...]
    o_ref[...] = (y * (1.0 + sc_ref[...]) + sh_ref[...]).astype(o_ref.dtype)


def norm_mod(x, g, shift, scale, out_dtype):
    t, d = x.shape
    tm = _row_tile(t, 256)
    vec = pl.BlockSpec((1, d), lambda i: (0, 0))
    return pl.pallas_call(
        _norm_mod_kernel,
        grid=(t // tm,),
        in_specs=[pl.BlockSpec((tm, d), lambda i: (i, 0)), vec, vec, vec],
        out_specs=pl.BlockSpec((tm, d), lambda i: (i, 0)),
        out_shape=jax.ShapeDtypeStruct((t, d), out_dtype),
        compiler_params=_cparams("parallel"),
        name="norm_mod",
    )(x, g.reshape(1, d), shift.reshape(1, d), scale.reshape(1, d))


def _mm_kernel(a_ref, b_ref, o_ref):
    o_ref[...] = _dot(a_ref[...], b_ref[...]).astype(o_ref.dtype)


def matmul(a, b, out_dtype, tn=None):
    t, k = a.shape
    n = b.shape[1]
    tn = n if tn is None else tn
    tm = _row_tile(t)
    return pl.pallas_call(
        _mm_kernel,
        grid=(n // tn, t // tm),
        in_specs=[pl.BlockSpec((tm, k), lambda j, i: (i, 0)),
                  pl.BlockSpec((k, tn), lambda j, i: (0, j))],
        out_specs=pl.BlockSpec((tm, tn), lambda j, i: (i, j)),
        out_shape=jax.ShapeDtypeStruct((t, n), out_dtype),
        compiler_params=_cparams("parallel", "parallel"),
        name="matmul",
    )(a, b)


def _mm_res_kernel(a_ref, b_ref, x_ref, g_ref, o_ref):
    o_ref[...] = x_ref[...] + g_ref[...] * _dot(a_ref[...], b_ref[...])


def matmul_residual(a, b, x, gate, tn=512):
    t, k = a.shape
    n = b.shape[1]
    tm = _row_tile(t)
    return pl.pallas_call(
        _mm_res_kernel,
        grid=(n // tn, t // tm),
        in_specs=[pl.BlockSpec((tm, k), lambda j, i: (i, 0)),
                  pl.BlockSpec((k, tn), lambda j, i: (0, j)),
                  pl.BlockSpec((tm, tn), lambda j, i: (i, j)),
                  pl.BlockSpec((1, tn), lambda j, i: (0, j))],
        out_specs=pl.BlockSpec((tm, tn), lambda j, i: (i, j)),
        out_shape=jax.ShapeDtypeStruct((t, n), F32),
        compiler_params=_cparams("parallel", "parallel"),
        name="matmul_residual",
    )(a, b, x, gate.reshape(1, n))


def _rope_mm_kernel(n_rope, a_ref, w_ref, wr_ref, cos_ref, sin_ref, o_ref):
    a = a_ref[...]
    p = _dot(a, w_ref[...])
    pr = _dot(a, wr_ref[...])
    cos = cos_ref[...]
    sin = sin_ref[...]
    for j in range(p.shape[1] // LANE):
        sl = slice(j * LANE, (j + 1) * LANE)
        if j < n_rope:
            o_ref[:, sl] = (p[:, sl] * cos + pr[:, sl] * sin).astype(o_ref.dtype)
        else:
            o_ref[:, sl] = p[:, sl].astype(o_ref.dtype)


def rope_matmul(a, w, w_rot, cos, sin, n_rope):
    t, k = a.shape
    n = w.shape[1]
    nr = n_rope * LANE
    tm = _row_tile(t)
    return pl.pallas_call(
        functools.partial(_rope_mm_kernel, n_rope),
        grid=(t // tm,),
        in_specs=[pl.BlockSpec((tm, k), lambda i: (i, 0)),
                  pl.BlockSpec((k, n), lambda i: (0, 0)),
                  pl.BlockSpec((k, nr), lambda i: (0, 0)),
                  pl.BlockSpec((tm, LANE), lambda i: (i, 0)),
                  pl.BlockSpec((tm, LANE), lambda i: (i, 0))],
        out_specs=pl.BlockSpec((tm, n), lambda i: (i, 0)),
        out_shape=jax.ShapeDtypeStruct((t, n), CD),
        compiler_params=_cparams("parallel"),
        name="rope_matmul",
    )(a, w, w_rot, cos, sin)


def _mla_proj_kernel(h_ref, w1_ref, gq_ref, gkv_ref, wq_ref, wqr_ref, wkv_ref,
                     cq_ref, sq_ref, ck_ref, sk_ref, q_out, k_out, v_out):
    p = _dot(h_ref[...], w1_ref[...])
    cq = p[:, :MLA_Q_RANK]
    ckv = p[:, MLA_Q_RANK:MLA_Q_RANK + MLA_KV_RANK]
    o = MLA_Q_RANK + MLA_KV_RANK
    kr = p[:, o:o + LANE]
    krr = p[:, o + LANE:o + 2 * LANE]
    cqn = (cq * lax.rsqrt(jnp.mean(cq * cq, axis=-1, keepdims=True) + EPS) * gq_ref[...]).astype(CD)
    ckvn = (ckv * lax.rsqrt(jnp.mean(ckv * ckv, axis=-1, keepdims=True) + EPS) * gkv_ref[...]).astype(CD)
    qa = _dot(cqn, wq_ref[...])
    qb = _dot(cqn, wqr_ref[...])
    kv = _dot(ckvn, wkv_ref[...])
    krp = (kr * ck_ref[...] + krr * sk_ref[...]).astype(CD)
    cq_t = cq_ref[...]
    sq_t = sq_ref[...]
    for h in range(MLA_HEADS):
        qs = slice(h * MLA_HW, (h + 1) * MLA_HW)
        q_out[:, qs] = (qa[:, qs] * cq_t + qb[:, qs] * sq_t).astype(CD)
        k_out[:, h * MLA_HW:h * MLA_HW + MLA_NOPE] = kv[:, h * 256:h * 256 + MLA_NOPE].astype(CD)
        k_out[:, h * MLA_HW + MLA_NOPE:(h + 1) * MLA_HW] = krp
        v_out[:, h * MLA_V:(h + 1) * MLA_V] = kv[:, h * 256 + MLA_NOPE:(h + 1) * 256].astype(CD)


def mla_proj(h, w1, gq, gkv, wq, wqr, wkv, cos_q, sin_q, cos_k, sin_k):
    t, d = h.shape
    tm = _row_tile(t)
    full = lambda a: pl.BlockSpec(a.shape, lambda i: (0,) * a.ndim)
    rows = lambda w: pl.BlockSpec((tm, w), lambda i: (i, 0))
    hw = MLA_HEADS * MLA_HW
    return pl.pallas_call(
        _mla_proj_kernel,
        grid=(t // tm,),
        in_specs=[rows(d), full(w1), full(gq), full(gkv), full(wq), full(wqr), full(wkv),
                  rows(MLA_HW), rows(MLA_HW), rows(LANE), rows(LANE)],
        out_specs=[rows(hw), rows(hw), rows(MLA_HEADS * MLA_V)],
        out_shape=[jax.ShapeDtypeStruct((t, hw), CD), jax.ShapeDtypeStruct((t, hw), CD),
                   jax.ShapeDtypeStruct((t, MLA_HEADS * MLA_V), CD)],
        compiler_params=_cparams("parallel"),
        name="mla_proj",
    )(h, w1, gq, gkv, wq, wqr, wkv, cos_q, sin_q, cos_k, sin_k)


def _flash_kernel(tk, q_ref, k_ref, v_ref, kc_ref, vc_ref, o_ref, s_sc, m_sc, acc_sc):
    nkb = k_ref.shape[0] // tk
    q = q_ref[...]
    m_sc[...] = jnp.full(m_sc.shape, NEG, F32)
    acc_sc[...] = jnp.zeros(acc_sc.shape, F32)

    def ones_col(n):
        return jnp.where(lax.broadcasted_iota(jnp.int32, (n, LANE), 1) == 0, 1.0, 0.0).astype(CD)

    def kblk(j):
        return k_ref[pl.ds(pl.multiple_of(j * tk, tk), tk), :]

    def vblk(j):
        return v_ref[pl.ds(pl.multiple_of(j * tk, tk), tk), :]

    def softmax_pv(s, v):
        m_prev = m_sc[...]
        m_new = jnp.maximum(m_prev, jnp.max(s, axis=-1, keepdims=True))
        alpha = jnp.exp2(m_prev - m_new)
        p = jnp.exp2(s - m_new).astype(CD)
        v_aug = jnp.concatenate([v, ones_col(v.shape[0])], axis=1)
        acc_sc[...] = alpha * acc_sc[...] + _dot(p, v_aug)
        m_sc[...] = m_new

    s_sc[0] = _dot_nt(q, kblk(0))

    def body(jj, carry):
        j0 = 2 * jj
        s_sc[1] = _dot_nt(q, kblk(j0 + 1))
        softmax_pv(s_sc[0], vblk(j0))
        s_sc[0] = _dot_nt(q, kblk(jnp.minimum(j0 + 2, nkb - 1)))
        softmax_pv(s_sc[1], vblk(j0 + 1))
        return carry

    lax.fori_loop(0, nkb // 2, body, 0)
    softmax_pv(_dot_nt(q, kc_ref[...]), vc_ref[...])
    acc = acc_sc[...]
    o_ref[...] = (acc[:, :MLA_V] / acc[:, MLA_V:MLA_V + 1]).astype(o_ref.dtype)


def mla_flash(q, k, v, kc, vc):
    s = q.shape[0]
    c = kc.shape[0]
    tq = _row_tile(s, 512)
    tk = _row_tile(s, 512)
    assert (s // tk) % 2 == 0
    return pl.pallas_call(
        functools.partial(_flash_kernel, tk),
        grid=(MLA_HEADS, s // tq),
        in_specs=[pl.BlockSpec((tq, MLA_HW), lambda h, i: (i, h)),
                  pl.BlockSpec((s, MLA_HW), lambda h, i: (0, h)),
                  pl.BlockSpec((s, MLA_V), lambda h, i: (0, h)),
                  pl.BlockSpec((c, MLA_HW), lambda h, i: (0, h)),
                  pl.BlockSpec((c, MLA_V), lambda h, i: (0, h))],
        out_specs=pl.BlockSpec((tq, MLA_V), lambda h, i: (i, h)),
        out_shape=jax.ShapeDtypeStruct((s, MLA_HEADS * MLA_V), CD),
        scratch_shapes=[pltpu.VMEM((2, tq, tk), F32), pltpu.VMEM((tq, 1), F32),
                        pltpu.VMEM((tq, 2 * MLA_V), F32)],
        compiler_params=_cparams("parallel", "arbitrary"),
        name="mla_flash",
    )(q, k, v, kc, vc)


def _dense_attn_kernel(n_heads, group, dqk, dv, log2_scores, q_ref, k_ref, v_ref, sink_ref, o_ref):
    q = q_ref[...]
    k = k_ref[...]
    v = v_ref[...]
    ex = jnp.exp2 if log2_scores else jnp.exp
    for h in range(n_heads):
        g = h // group
        s = _dot_nt(q[:, h * dqk:(h + 1) * dqk], k[:, g * dqk:(g + 1) * dqk])
        m = jnp.max(s, axis=-1, keepdims=True)
        l = jnp.zeros_like(m)
        if sink_ref is not None:
            sk = sink_ref[0:1, h:h + 1]
            m = jnp.maximum(m, sk)
            l = ex(sk - m)
        p = ex(s - m)
        l = l + jnp.sum(p, axis=-1, keepdims=True)
        o = _dot(p.astype(CD), v[:, g * dv:(g + 1) * dv]) / l
        o_ref[:, h * dv:(h + 1) * dv] = o.astype(o_ref.dtype)


def dense_attn(qkv_specs, n_heads, group, dqk, dv, sink=None, log2_scores=False):
    c = qkv_specs[0][0].shape[0]
    arrays = [a for a, _, _ in qkv_specs]
    specs = [pl.BlockSpec((c, w), functools.partial(lambda b, i: (0, b), b)) for _, w, b in qkv_specs]
    if sink is None:
        kern = lambda q, k, v, o: _dense_attn_kernel(n_heads, group, dqk, dv, log2_scores, q, k, v, None, o)
    else:
        kern = functools.partial(_dense_attn_kernel, n_heads, group, dqk, dv, log2_scores)
        arrays.append(sink)
        specs.append(pl.BlockSpec(sink.shape, lambda i: (0, 0)))
    return pl.pallas_call(
        kern,
        grid=(1,),
        in_specs=specs,
        out_specs=pl.BlockSpec((c, n_heads * dv), lambda i: (0, 0)),
        out_shape=jax.ShapeDtypeStruct((c, n_heads * dv), CD),
        compiler_params=_cparams("arbitrary"),
        name="dense_attn",
    )(*arrays)


def _na_kernel(q_ref, kp_ref, ko_ref, kn_ref, vp_ref, vo_ref, vn_ref, kc_ref, vc_ref, bias_ref, o_ref):
    q = q_ref[...]
    ks = (kp_ref[...], ko_ref[...], kn_ref[...])
    vs = (vp_ref[...], vo_ref[...], vn_ref[...])
    kc = kc_ref[...]
    vc = vc_ref[...]
    tb = q.shape[0]
    for h in range(NA_HEADS):
        hs = slice(h * NA_DIM, (h + 1) * NA_DIM)
        qh = q[:, hs]
        bias = bias_ref[0, h]
        s_loc = [_dot_nt(qh, kk[:, hs]) + bias[:, j * tb:(j + 1) * tb] for j, kk in enumerate(ks)]
        s_ctx = _dot_nt(qh, kc[:, hs])
        m = jnp.max(s_ctx, axis=-1, keepdims=True)
        for s in s_loc:
            m = jnp.maximum(m, jnp.max(s, axis=-1, keepdims=True))
        p_ctx = jnp.exp(s_ctx - m)
        l = jnp.sum(p_ctx, axis=-1, keepdims=True)
        o = _dot(p_ctx.astype(CD), vc[:, hs])
        for s, vv in zip(s_loc, vs):
            p = jnp.exp(s - m)
            l = l + jnp.sum(p, axis=-1, keepdims=True)
            o = o + _dot(p.astype(CD), vv[:, hs])
        o_ref[:, hs] = (o / l).astype(o_ref.dtype)


def _na_bias(rpb, rows):
    qr = NA_QROWS
    tb = qr * GRID_W
    nb = rows // qr
    nkr = 3 * qr
    rsel = np.zeros((3, qr, nkr, 2 * NA_WIN_R - 1), np.float32)
    for v, b in enumerate((0, 1, nb - 1)):
        for a in range(qr):
            r = qr * b + a
            rs = min(max(r - NA_WIN_R // 2, 0), rows - NA_WIN_R)
            for u in range(nkr):
                kr = qr * (b - 1) + u
                if 0 <= b - 1 + u // qr < nb and rs <= kr < rs + NA_WIN_R:
                    rsel[v, a, u, kr - r + NA_WIN_R - 1] = 1.0
    csel = np.zeros((GRID_W, GRID_W, 2 * NA_WIN_C - 1), np.float32)
    for qc in range(GRID_W):
        cs = min(max(qc - NA_WIN_C // 2, 0), GRID_W - NA_WIN_C)
        for kc in range(cs, cs + NA_WIN_C):
            csel[qc, kc, kc - qc + NA_WIN_C - 1] = 1.0
    valid = np.einsum('vaud,qkj->vaquk', rsel, csel) > 0
    hi = lax.Precision.HIGHEST
    cols = jnp.einsum('hdj,qkj->hdqk', rpb.astype(F32), jnp.asarray(csel), precision=hi)
    bias = jnp.einsum('vaud,hdqk->vhaquk', jnp.asarray(rsel), cols, precision=hi)
    bias = jnp.where(valid[:, None], bias, NEG)
    return bias.reshape(3, NA_HEADS, tb, 3 * tb)


def na_attn(p_lat, p_ctx, rpb, rows):
    s = p_lat.shape[0]
    c = p_ctx.shape[0]
    w = NA_HEADS * NA_DIM
    tb = NA_QROWS * GRID_W
    nb = s // tb
    assert rows % NA_QROWS == 0 and nb >= 3 and rows >= 2 * NA_WIN_R
    bias = _na_bias(rpb, rows)

    def blk(col, off):
        return pl.BlockSpec((tb, w), lambda i: (jnp.clip(i + off, 0, nb - 1), col))

    return pl.pallas_call(
        _na_kernel,
        grid=(nb,),
        in_specs=[blk(0, 0), blk(1, -1), blk(1, 0), blk(1, 1), blk(2, -1), blk(2, 0), blk(2, 1),
                  pl.BlockSpec((c, w), lambda i: (0, 1)),
                  pl.BlockSpec((c, w), lambda i: (0, 2)),
                  pl.BlockSpec((1, NA_HEADS, tb, 3 * tb),
                               lambda i: (jnp.where(i == 0, 0, jnp.where(i == nb - 1, 2, 1)), 0, 0, 0))],
        out_specs=pl.BlockSpec((tb, w), lambda i: (i, 0)),
        out_shape=jax.ShapeDtypeStruct((s, w), CD),
        compiler_params=_cparams("parallel"),
        name="na_attn",
    )(p_lat, p_lat, p_lat, p_lat, p_lat, p_lat, p_lat, p_ctx, p_ctx, bias)


def _sw_kernel(seq, q_ref, kp_ref, ko_ref, kn_ref, vp_ref, vo_ref, vn_ref, kc_ref, vc_ref, sink_ref, o_ref):
    i = pl.program_id(0)
    q = q_ref[...]
    tq = q.shape[0]
    kcat = jnp.concatenate([kp_ref[...], ko_ref[...], kn_ref[...]], axis=0)
    vcat = jnp.concatenate([vp_ref[...], vo_ref[...], vn_ref[...]], axis=0)
    kc = kc_ref[...]
    vc = vc_ref[...]
    nkk = kcat.shape[0]
    r = lax.broadcasted_iota(jnp.int32, (tq, nkk), 0)
    cc = lax.broadcasted_iota(jnp.int32, (tq, nkk), 1)
    kpos = i * tq - SW_WINDOW + cc
    d = cc - r
    mask = (d >= 0) & (d <= 2 * SW_WINDOW) & (kpos >= 0) & (kpos < seq)
    grp = SW_HEADS // SW_KV_HEADS
    for h in range(SW_HEADS):
        g = h // grp
        gs = slice(g * SW_DIM, (g + 1) * SW_DIM)
        qh = q[:, h * SW_DIM:(h + 1) * SW_DIM]
        s = jnp.where(mask, _dot_nt(qh, kcat[:, gs]), NEG)
        sc = _dot_nt(qh, kc[:, gs])
        sk = sink_ref[0:1, h:h + 1]
        m = jnp.maximum(jnp.maximum(jnp.max(s, axis=-1, keepdims=True),
                                    jnp.max(sc, axis=-1, keepdims=True)), sk)
        p = jnp.exp(s - m)
        pc = jnp.exp(sc - m)
        l = jnp.sum(p, axis=-1, keepdims=True) + jnp.sum(pc, axis=-1, keepdims=True) + jnp.exp(sk - m)
        o = (_dot(p.astype(CD), vcat[:, gs]) + _dot(pc.astype(CD), vc[:, gs])) / l
        o_ref[:, h * SW_DIM:(h + 1) * SW_DIM] = o.astype(o_ref.dtype)


def sw_attn(p_lat, p_ctx, sink):
    s = p_lat.shape[0]
    c = p_ctx.shape[0]
    tq = SW_TQ
    assert s % tq == 0 and tq % SW_WINDOW == 0
    nq = s // tq
    per = tq // SW_WINDOW
    nwb = s // SW_WINDOW
    kcol = SW_HEADS * SW_DIM // LANE
    prev = lambda col: pl.BlockSpec((SW_WINDOW, LANE), lambda i: (jnp.maximum(i * per - 1, 0), col))
    own = lambda col: pl.BlockSpec((tq, LANE), lambda i: (i, col))
    nxt = lambda col: pl.BlockSpec((SW_WINDOW, LANE), lambda i: (jnp.minimum((i + 1) * per, nwb - 1), col))
    return pl.pallas_call(
        functools.partial(_sw_kernel, s),
        grid=(nq,),
        in_specs=[pl.BlockSpec((tq, SW_HEADS * SW_DIM), lambda i: (i, 0)),
                  prev(kcol), own(kcol), nxt(kcol), prev(kcol + 1), own(kcol + 1), nxt(kcol + 1),
                  pl.BlockSpec((c, LANE), lambda i: (0, kcol)),
                  pl.BlockSpec((c, LANE), lambda i: (0, kcol + 1)),
                  pl.BlockSpec(sink.shape, lambda i: (0, 0))],
        out_specs=pl.BlockSpec((tq, SW_HEADS * SW_DIM), lambda i: (i, 0)),
        out_shape=jax.ShapeDtypeStruct((s, SW_HEADS * SW_DIM), CD),
        compiler_params=_cparams("parallel"),
        name="sw_attn",
    )(p_lat, p_lat, p_lat, p_lat, p_lat, p_lat, p_lat, p_ctx, p_ctx, sink)


def _ml_prep_kernel(nt, x_ref, xp_ref, xn_ref, g_ref, w_ref, b_ref, q_out, k_out, g_out):
    i = pl.program_id(0)
    x = x_ref[...]
    tm = x.shape[0]
    row = lax.broadcasted_iota(jnp.int32, x.shape, 0)
    has_prev = jnp.where(i > 0, 1.0, 0.0)
    has_next = jnp.where(i < nt - 1, 1.0, 0.0)
    prev_row = xp_ref[7:8, :] * has_prev
    next_row = xn_ref[0:1, :] * has_next
    xm = jnp.where(row == 0, prev_row, pltpu.roll(x, 1, axis=0))
    xq = jnp.where(row == tm - 1, next_row, pltpu.roll(x, tm - 1, axis=0))
    y = w_ref[0:1, :] * xm + w_ref[1:2, :] * x + w_ref[2:3, :] * xq
    y = y * _sigmoid(y)
    q_out[...] = y[:, :ML_QKW].astype(CD)
    k_out[...] = (y[:, ML_QKW:] * (ML_QK ** -0.5)).astype(CD)
    g = g_ref[...] + b_ref[...]
    lane = lax.broadcasted_iota(jnp.int32, g.shape, 1)
    logsig = jnp.minimum(g, 0.0) - jnp.log(1.0 + jnp.exp(-jnp.abs(g)))
    g_out[...] = jnp.where(lane < 2 * ML_HEADS, g, logsig)


def ml_prep(p_ml, conv_w, gate_b):
    t = p_ml.shape[0]
    tm = _row_tile(t, 256)
    nt = t // tm
    wq = 2 * ML_QKW
    gcol = (2 * ML_QKW + 2 * ML_VW) // LANE
    w8 = jnp.zeros((8, wq), F32).at[:ML_CONV].set(conv_w.astype(F32))
    b = jnp.zeros((1, LANE), F32).at[0, :4 * ML_HEADS].set(gate_b.astype(F32).reshape(-1))
    h8 = tm // 8
    return pl.pallas_call(
        functools.partial(_ml_prep_kernel, nt),
        grid=(nt,),
        in_specs=[pl.BlockSpec((tm, wq), lambda i: (i, 0)),
                  pl.BlockSpec((8, wq), lambda i: (jnp.maximum(i * h8 - 1, 0), 0)),
                  pl.BlockSpec((8, wq), lambda i: (jnp.minimum((i + 1) * h8, t // 8 - 1), 0)),
                  pl.BlockSpec((tm, LANE), lambda i: (i, gcol)),
                  pl.BlockSpec((8, wq), lambda i: (0, 0)),
                  pl.BlockSpec((1, LANE), lambda i: (0, 0))],
        out_specs=[pl.BlockSpec((tm, ML_QKW), lambda i: (i, 0)),
                   pl.BlockSpec((tm, ML_QKW), lambda i: (i, 0)),
                   pl.BlockSpec((tm, LANE), lambda i: (i, 0))],
        out_shape=[jax.ShapeDtypeStruct((t, ML_QKW), CD), jax.ShapeDtypeStruct((t, ML_QKW), CD),
                   jax.ShapeDtypeStruct((t, LANE), F32)],
        compiler_params=_cparams("parallel"),
        name="ml_prep",
    )(p_ml, p_ml, p_ml, p_ml, w8, b)


def _ml_scan_kernel(qf_ref, ktf_ref, vf_ref, gcf_ref, grf_ref,
                    qb_ref, ktb_ref, vb_ref, gcb_ref, grb_ref,
                    c0_ref, m0_ref, hf_ref, hb_ref, c_ref, m_ref):
    @pl.when(pl.program_id(0) == 0)
    def _():
        c_ref[...] = c0_ref[...]
        m_ref[...] = m0_ref[...]

    ln = qf_ref.shape[0]
    r = lax.broadcasted_iota(jnp.int32, (ln, ln), 0)
    c = lax.broadcasted_iota(jnp.int32, (ln, ln), 1)
    lower = r >= c
    upper = c >= r
    lower_f = jnp.where(lower, 1.0, 0.0)
    upper_f = jnp.where(upper, 1.0, 0.0)
    ones_col = jnp.where(lax.broadcasted_iota(jnp.int32, (ln, LANE), 1) == 0, 1.0, 0.0).astype(CD)
    hi = lax.Precision.HIGHEST

    streams = ((qf_ref, ktf_ref, vf_ref, gcf_ref, grf_ref, hf_ref, lower, lower_f, upper_f, ln - 1),
               (qb_ref, ktb_ref, vb_ref, gcb_ref, grb_ref, hb_ref, upper, upper_f, lower_f, 0))
    for d, (q_ref, kt_ref, v_ref, gc_ref, gr_ref, h_ref, mask, tri_c, tri_r, last) in enumerate(streams):
        gcol = gc_ref[...]
        grow = gr_ref[...]
        bcol = jnp.dot(tri_c, gcol, precision=hi, preferred_element_type=F32)
        brow = jnp.dot(grow, tri_r, precision=hi, preferred_element_type=F32)
        q = q_ref[...]
        kt = kt_ref[...]
        v = v_ref[...]
        for hd in range(ML_HEADS):
            ch = d * ML_HEADS + hd
            ic = d * ML_HEADS + hd
            fc = 2 * ML_HEADS + ic
            i_c = gcol[:, ic:ic + 1]
            b_c = bcol[:, fc:fc + 1]
            i_r = grow[ic:ic + 1, :]
            b_r = brow[fc:fc + 1, :]
            m_prev = m_ref[ch, 0:1, 0:1]
            cst = c_ref[ch]
            dmat = jnp.where(mask, b_c - b_r + i_r, NEG)
            inter = b_c + m_prev
            m_t = jnp.maximum(inter, jnp.max(dmat, axis=-1, keepdims=True))
            qh = q[:, hd * ML_QK:(hd + 1) * ML_QK]
            kth = kt[hd * ML_QK:(hd + 1) * ML_QK, :]
            s = lax.dot_general(qh, kth, (((1,), (0,)), ((), ())), preferred_element_type=F32)
            s = s * jnp.exp(dmat - m_t)
            w_inter = jnp.exp(inter - m_t)
            v_aug = jnp.concatenate([v[:, hd * ML_V:(hd + 1) * ML_V].astype(CD), ones_col], axis=1)
            num = _dot(s.astype(CD), v_aug) + w_inter * _dot(qh, cst.astype(CD))
            den = num[:, ML_V:ML_V + 1]
            h = num[:, :ML_V] / jnp.maximum(jnp.abs(den), jnp.exp(-m_t))
            h_ref[:, hd * ML_V:(hd + 1) * ML_V] = h
            b_end = b_c[last:last + 1, :]
            m_new = jnp.maximum(b_end + m_prev, jnp.max(b_end - b_c + i_c, axis=0, keepdims=True))
            w_s = jnp.exp(b_end - b_r + i_r - m_new)
            w_c = jnp.exp(b_end + m_prev - m_new)
            ktw = (kth.astype(F32) * w_s).astype(CD)
            c_ref[ch] = w_c * cst + _dot(ktw, v_aug)
            m_ref[ch] = jnp.broadcast_to(m_new, m_ref.shape[1:])


def ml_scan(qc, kc, p_ml, gates, state):
    t = qc.shape[0]
    ln = min(ML_CHUNK, t)
    assert t % ln == 0
    nc = t // ln
    kt = kc.T
    gr = gates[:, :4 * ML_HEADS].T
    c0, m0 = state
    vcol = 2 * ML_QKW // ML_VW
    fwd = lambda j: j
    bwd = lambda j: nc - 1 - j

    def specs(ix):
        return [pl.BlockSpec((ln, ML_QKW), lambda j: (ix(j), 0)),
                pl.BlockSpec((ML_QKW, ln), lambda j: (0, ix(j))),
                pl.BlockSpec((ln, ML_VW), lambda j: (ix(j), vcol)),
                pl.BlockSpec((ln, LANE), lambda j: (ix(j), 0)),
                pl.BlockSpec((4 * ML_HEADS, ln), lambda j: (0, ix(j)))]

    st_specs = [pl.BlockSpec(c0.shape, lambda j: (0, 0, 0)), pl.BlockSpec(m0.shape, lambda j: (0, 0, 0))]
    return pl.pallas_call(
        _ml_scan_kernel,
        grid=(nc,),
        in_specs=specs(fwd) + specs(bwd) + st_specs,
        out_specs=[pl.BlockSpec((ln, ML_VW), lambda j: (fwd(j), 0)),
                   pl.BlockSpec((ln, ML_VW), lambda j: (bwd(j), 0))] + st_specs,
        out_shape=[jax.ShapeDtypeStruct((t, ML_VW), F32), jax.ShapeDtypeStruct((t, ML_VW), F32),
                   jax.ShapeDtypeStruct(c0.shape, F32), jax.ShapeDtypeStruct(m0.shape, F32)],
        compiler_params=_cparams("arbitrary"),
        name="ml_scan",
    )(qc, kt, p_ml, gates, gr, qc, kt, p_ml, gates, gr, c0, m0)


def _ml_finish_kernel(hf_ref, hb_ref, o_ref, g_ref, y_ref):
    h = hf_ref[...] + hb_ref[...]
    og = _sigmoid(o_ref[...])
    g = g_ref[...]
    for hd in range(ML_HEADS):
        sl = slice(hd * ML_V, (hd + 1) * ML_V)
        hh = h[:, sl]
        y = hh * lax.rsqrt(jnp.mean(hh * hh, axis=-1, keepdims=True) + EPS) * g[:, sl]
        y_ref[:, sl] = (y * og[:, sl]).astype(y_ref.dtype)


def ml_finish(hf, hb, p_ml, norm_g):
    t = hf.shape[0]
    tm = _row_tile(t, 256)
    ocol = (2 * ML_QKW + ML_VW) // ML_VW
    blk = pl.BlockSpec((tm, ML_VW), lambda i: (i, 0))
    return pl.pallas_call(
        _ml_finish_kernel,
        grid=(t // tm,),
        in_specs=[blk, blk, pl.BlockSpec((tm, ML_VW), lambda i: (i, ocol)),
                  pl.BlockSpec((1, ML_VW), lambda i: (0, 0))],
        out_specs=blk,
        out_shape=jax.ShapeDtypeStruct((t, ML_VW), CD),
        compiler_params=_cparams("parallel"),
        name="ml_finish",
    )(hf, hb, p_ml, norm_g.reshape(1, ML_VW).astype(F32))


def _merge_kernel(h_ref, ya_ref, yb_ref, yc_ref, yd_ref, g0_ref, g1_ref, g2_ref, g3_ref, wb_ref, o_ref):
    h = h_ref[...]
    acc = None
    for i, (y_ref, g_ref) in enumerate(zip((ya_ref, yb_ref, yc_ref, yd_ref), (g0_ref, g1_ref, g2_ref, g3_ref))):
        term = _sigmoid(_dot(h, g_ref[...])) * _dot(y_ref[...], wb_ref[i])
        acc = term if acc is None else acc + term
    o_ref[...] = acc.astype(o_ref.dtype)


def gated_merge(h, ys, w_gate, w_branch, tn=512):
    t, d = h.shape
    tm = _row_tile(t)
    nj = d // tn
    yspec = pl.BlockSpec((tm, BRANCH_W), lambda j, i: (i, 0))
    gspec = lambda b: pl.BlockSpec((d, tn), lambda j, i: (0, b * nj + j))
    return pl.pallas_call(
        _merge_kernel,
        grid=(nj, t // tm),
        in_specs=[pl.BlockSpec((tm, d), lambda j, i: (i, 0)), yspec, yspec, yspec, yspec,
                  gspec(0), gspec(1), gspec(2), gspec(3),
                  pl.BlockSpec((N_BRANCH, BRANCH_W, tn), lambda j, i: (0, 0, j))],
        out_specs=pl.BlockSpec((tm, tn), lambda j, i: (i, j)),
        out_shape=jax.ShapeDtypeStruct((t, d), CD),
        compiler_params=_cparams("parallel", "parallel"),
        name="gated_merge",
    )(h, *ys, w_gate, w_gate, w_gate, w_gate, w_branch)


def _ffn1_kernel(h_ref, wa_ref, wg_ref, o_ref):
    h = h_ref[...]
    a = _dot(h, wa_ref[...])
    g = _dot(h, wg_ref[...])
    o_ref[...] = (a * _sigmoid(a) * g).astype(o_ref.dtype)


def ffn_up(h, w, tn=512):
    t, d = h.shape
    dff = w.shape[1] // 2
    tm = _row_tile(t)
    nj = dff // tn
    return pl.pallas_call(
        _ffn1_kernel,
        grid=(nj, t // tm),
        in_specs=[pl.BlockSpec((tm, d), lambda j, i: (i, 0)),
                  pl.BlockSpec((d, tn), lambda j, i: (0, j)),
                  pl.BlockSpec((d, tn), lambda j, i: (0, nj + j))],
        out_specs=pl.BlockSpec((tm, tn), lambda j, i: (i, j)),
        out_shape=jax.ShapeDtypeStruct((t, dff), CD),
        compiler_params=_cparams("parallel", "parallel"),
        name="ffn_up",
    )(h, w, w)


def _rot_cols(w, dim):
    k, n = w.shape
    w4 = w.reshape(k, n // dim, 2, dim // 2)
    return jnp.stack([-w4[:, :, 1], w4[:, :, 0]], axis=2).reshape(k, n)


def _rope_tables(n_tokens, dim):
    a = dim // 2
    inv = 1.0 / (ROPE_THETA ** (jnp.arange(0, a, 2, dtype=F32) / a))
    t = jnp.arange(n_tokens)
    row = (t // GRID_W).astype(F32)
    col = (t % GRID_W).astype(F32)
    ang = jnp.concatenate([row[:, None] * inv, col[:, None] * inv], axis=-1)
    return jnp.cos(ang), jnp.sin(ang)


def _layer_weights(w_in_l, mla_w_uq_l, mla_w_ukv_l):
    d = w_in_l.shape[0]
    o_na = ML_COLS
    o_sw = o_na + NA_COLS
    o_mla = o_sw + SW_COLS
    o_gate = o_mla + MLA_COLS
    w = {}
    w["ml"] = jnp.pad(w_in_l[:, :ML_COLS], ((0, 0), (0, ML_PAD - ML_COLS))).astype(CD)
    na = w_in_l[:, o_na:o_sw]
    nq = NA_HEADS * NA_DIM
    w["na"] = jnp.concatenate([na[:, :nq] * (NA_DIM ** -0.5), na[:, nq:]], axis=1).astype(CD)
    sq = SW_HEADS * SW_DIM
    skv = SW_KV_HEADS * SW_DIM
    sw_q = w_in_l[:, o_sw:o_sw + sq] * (SW_DIM ** -0.5)
    sw_k = w_in_l[:, o_sw + sq:o_sw + sq + skv]
    sw_v = w_in_l[:, o_sw + sq + skv:o_mla]
    w["sw"] = jnp.concatenate([sw_q, sw_k, sw_v], axis=1).astype(CD)
    w["sw_rot"] = jnp.concatenate([_rot_cols(sw_q, SW_DIM), _rot_cols(sw_k, SW_DIM)], axis=1).astype(CD)
    m_cq = w_in_l[:, o_mla:o_mla + MLA_Q_RANK + MLA_KV_RANK]
    m_kr = w_in_l[:, o_mla + MLA_Q_RANK + MLA_KV_RANK:o_gate]
    z = jnp.zeros((d, LANE - MLA_ROPE), F32)
    w["mla1"] = jnp.concatenate([m_cq, m_kr, z, _rot_cols(m_kr, MLA_ROPE), z], axis=1).astype(CD)
    uq = mla_w_uq_l.reshape(MLA_Q_RANK, MLA_HEADS, MLA_NOPE + MLA_ROPE)
    zq = jnp.zeros((MLA_Q_RANK, MLA_HEADS, MLA_HW - MLA_NOPE - MLA_ROPE), F32)
    w["mla_q"] = jnp.concatenate([uq, zq], axis=2).reshape(MLA_Q_RANK, -1).astype(CD)
    uqr = _rot_cols(uq[:, :, MLA_NOPE:].reshape(MLA_Q_RANK, -1), MLA_ROPE).reshape(MLA_Q_RANK, MLA_HEADS, MLA_ROPE)
    w["mla_q_rot"] = jnp.concatenate([jnp.zeros_like(uq[:, :, :MLA_NOPE]), uqr, zq], axis=2
                                     ).reshape(MLA_Q_RANK, -1).astype(CD)
    w["mla_kv"] = mla_w_ukv_l.astype(CD)
    w["gate"] = w_in_l[:, o_gate:].astype(CD)
    return w


def _tables(s, c):
    cos_sw, sin_sw = _rope_tables(s, SW_DIM)
    cos_m, sin_m = _rope_tables(s, MLA_ROPE)
    tile2 = lambda a: jnp.concatenate([a, a, a, a], axis=1)
    scale = (MLA_NOPE + MLA_ROPE) ** -0.5 * float(np.log2(np.e))
    zpad = jnp.zeros((s, MLA_HW - MLA_NOPE - MLA_ROPE), F32)
    lat = dict(
        sw_cos=tile2(cos_sw), sw_sin=tile2(sin_sw),
        q_cos=scale * jnp.concatenate([jnp.ones((s, MLA_NOPE), F32), cos_m, cos_m, zpad], axis=1),
        q_sin=scale * jnp.concatenate([jnp.zeros((s, MLA_NOPE), F32), sin_m, sin_m, zpad], axis=1),
        k_cos=jnp.concatenate([cos_m, cos_m, zpad], axis=1),
        k_sin=jnp.concatenate([sin_m, sin_m, zpad], axis=1))
    ctx = dict(
        sw_cos=jnp.ones((c, LANE), F32), sw_sin=jnp.zeros((c, LANE), F32),
        q_cos=jnp.full((c, MLA_HW), scale, F32), q_sin=jnp.zeros((c, MLA_HW), F32),
        k_cos=jnp.ones((c, LANE), F32), k_sin=jnp.zeros((c, LANE), F32))
    return lat, ctx


def _project(h, w, tab, gq, gkv):
    p_ml = matmul(h, w["ml"], F32)
    p_na = matmul(h, w["na"], CD)
    n_rope = (SW_HEADS + SW_KV_HEADS) * SW_DIM // LANE
    p_sw = rope_matmul(h, w["sw"], w["sw_rot"], tab["sw_cos"], tab["sw_sin"], n_rope)
    mq, mk, mv = mla_proj(h, w["mla1"], gq, gkv, w["mla_q"], w["mla_q_rot"], w["mla_kv"],
                          tab["q_cos"], tab["q_sin"], tab["k_cos"], tab["k_sin"])
    return p_ml, p_na, p_sw, (mq, mk, mv)


def _dense_tail(x, h, ys, w, w_branch_l, w_out_l, g1, norm2_g_l, sh2, sc2, w_ffn_in_l, w_ffn_out_l, g2):
    merged = gated_merge(h, ys, w["gate"], w_branch_l)
    x = matmul_residual(merged, w_out_l, x, g1)
    h2 = norm_mod(x, norm2_g_l, sh2, sc2, CD)
    u = ffn_up(h2, w_ffn_in_l)
    return matmul_residual(u, w_ffn_out_l, x, g2)


def kernel(x, c, ctx, c_ctx, w_ada, b_ada, norm1_g, norm2_g, w_in, ml_conv_w, ml_gate_b, ml_norm_g, na_rpb,
           sw_sink, mla_q_norm_g, mla_kv_norm_g, mla_w_uq, mla_w_ukv, w_branch, w_out, w_ffn_in, w_ffn_out,
           final_norm_g):
    bsz, s, d = x.shape
    assert bsz == 1
    depth = w_in.shape[0]
    n_ctx = ctx.shape[1]
    rows = s // GRID_W
    xl = x[0].astype(F32)
    xc = ctx[0].astype(F32)
    cc = jnp.zeros((8, d), F32).at[0].set(c[0]).at[1].set(c_ctx)
    mod = adaln(cc, w_ada, b_ada)
    tab_l, tab_c = _tables(s, n_ctx)
    zero_state = (jnp.zeros((2 * ML_HEADS, ML_QK, 2 * ML_V), F32), jnp.zeros((2 * ML_HEADS, 8, LANE), F32))

    for l in range(depth):
        need_ctx = l < depth - 1
        w = _layer_weights(w_in[l], mla_w_uq[l], mla_w_ukv[l])
        wb = w_branch[l].astype(CD)
        wo = w_out[l].astype(CD)
        wf1 = w_ffn_in[l].astype(CD)
        wf2 = w_ffn_out[l].astype(CD)
        sh1, sc1, g1, sh2, sc2, g2 = [mod[l, 0, i * d:(i + 1) * d] for i in range(6)]
        csh1, csc1, cg1, csh2, csc2, cg2 = [mod[l, 1, i * d:(i + 1) * d] for i in range(6)]
        gq = mla_q_norm_g[l].reshape(1, -1).astype(F32)
        gkv = mla_kv_norm_g[l].reshape(1, -1).astype(F32)
        sink = sw_sink[l].reshape(1, -1).astype(F32)

        hl = norm_mod(xl, norm1_g[l], sh1, sc1, CD)
        hc = norm_mod(xc, norm1_g[l], csh1, csc1, CD)
        ml_l, na_l, sw_l, (mq_l, mk_l, mv_l) = _project(hl, w, tab_l, gq, gkv)
        ml_c, na_c, sw_c, (mq_c, mk_c, mv_c) = _project(hc, w, tab_c, gq, gkv)

        qc_c, kc_c, gt_c = ml_prep(ml_c, ml_conv_w[l], ml_gate_b[l])
        qc_l, kc_l, gt_l = ml_prep(ml_l, ml_conv_w[l], ml_gate_b[l])
        hf_c, hb_c, cst, mst = ml_scan(qc_c, kc_c, ml_c, gt_c, zero_state)
        hf_l, hb_l, _, _ = ml_scan(qc_l, kc_l, ml_l, gt_l, (cst, mst))
        ya_l = ml_finish(hf_l, hb_l, ml_l, ml_norm_g[l])
        yb_l = na_attn(na_l, na_c, na_rpb[l], rows)
        yc_l = sw_attn(sw_l, sw_c, sink)
        yd_l = mla_flash(mq_l, mk_l, mv_l, mk_c, mv_c)

        xl = _dense_tail(xl, hl, (ya_l, yb_l, yc_l, yd_l), w, wb, wo, g1, norm2_g[l], sh2, sc2, wf1, wf2, g2)
        if need_ctx:
            ya_c = ml_finish(hf_c, hb_c, ml_c, ml_norm_g[l])
            nw = NA_HEADS * NA_DIM
            yb_c = dense_attn([(na_c, nw, 0), (na_c, nw, 1), (na_c, nw, 2)], NA_HEADS, 1, NA_DIM, NA_DIM)
            kcol = SW_HEADS * SW_DIM // LANE
            yc_c = dense_attn([(sw_c, SW_HEADS * SW_DIM, 0), (sw_c, LANE, kcol), (sw_c, LANE, kcol + 1)],
                              SW_HEADS, SW_HEADS // SW_KV_HEADS, SW_DIM, SW_DIM, sink=sink)
            yd_c = dense_attn([(mq_c, MLA_HEADS * MLA_HW, 0), (mk_c, MLA_HEADS * MLA_HW, 0),
                               (mv_c, MLA_HEADS * MLA_V, 0)], MLA_HEADS, 1, MLA_HW, MLA_V, log2_scores=True)
            xc = _dense_tail(xc, hc, (ya_c, yb_c, yc_c, yd_c), w, wb, wo, cg1, norm2_g[l], csh2, csc2,
                             wf1, wf2, cg2)

    zeros = jnp.zeros((d,), F32)
    out = norm_mod(xl, final_norm_g, zeros, zeros, F32)
    return out[None].astype(x.dtype)
```

```python
import functools

import numpy as np
import jax
import jax.numpy as jnp
from jax import lax
from jax.experimental import pallas as pl
from jax.experimental.pallas import tpu as pltpu

F32 = jnp.float32
CD = jnp.bfloat16

GRID_W = 64
EPS = 1e-6
ROPE_THETA = 10000.0
ML_HEADS, ML_QK, ML_V, ML_CONV = 4, 64, 128, 3
NA_HEADS, NA_DIM, NA_WIN_R, NA_WIN_C = 8, 64, 8, 16
SW_HEADS, SW_KV_HEADS, SW_DIM, SW_WINDOW = 8, 2, 64, 128
MLA_HEADS, MLA_Q_RANK, MLA_KV_RANK, MLA_NOPE, MLA_ROPE, MLA_V = 4, 384, 256, 128, 64, 128
N_BRANCH, BRANCH_W = 4, 512

ML_QKW = ML_HEADS * ML_QK
ML_VW = ML_HEADS * ML_V
ML_COLS = 2 * ML_QKW + 2 * ML_VW + 4 * ML_HEADS
ML_PAD = 1664
NA_COLS = 3 * NA_HEADS * NA_DIM
SW_COLS = (SW_HEADS + 2 * SW_KV_HEADS) * SW_DIM
MLA_COLS = MLA_Q_RANK + MLA_KV_RANK + MLA_ROPE
MLA_HW = 256
LANE = 128
NEG = -1e30
LOG2E = float(np.log2(np.e))
VMEM_LIMIT = 56 * 1024 * 1024

NA_QROWS = 4
SW_TQ = 256
ML_CHUNK = 256
MLA_TK = 1280


def _cparams(*sem):
    return pltpu.CompilerParams(dimension_semantics=sem, vmem_limit_bytes=VMEM_LIMIT)


def _row_tile(t, pref=512):
    tm = min(pref, t)
    assert t % tm == 0
    return tm


def _dot(a, b):
    return jnp.dot(a, b, preferred_element_type=F32)


def _dot_nt(a, b):
    return lax.dot_general(a, b, (((1,), (1,)), ((), ())), preferred_element_type=F32)


def _sigmoid(x):
    return 1.0 / (1.0 + jnp.exp(-x))


def _ones_col(n, w):
    return jnp.where(lax.broadcasted_iota(jnp.int32, (n, w), 1) == 0, 1.0, 0.0).astype(CD)


def _adaln_kernel(c_ref, w_ref, b_ref, o_ref):
    c = c_ref[...]
    a = (c * _sigmoid(c)).astype(CD)
    o_ref[0] = _dot(a, w_ref[0].astype(CD)) + b_ref[0]


def adaln(cc, w_ada, b_ada):
    nl, d, n = w_ada.shape
    tn = 1024
    return pl.pallas_call(
        _adaln_kernel,
        grid=(nl, n // tn),
        in_specs=[pl.BlockSpec((8, d), lambda l, j: (0, 0)),
                  pl.BlockSpec((1, d, tn), lambda l, j: (l, 0, j)),
                  pl.BlockSpec((1, 1, tn), lambda l, j: (l, 0, j))],
        out_specs=pl.BlockSpec((1, 8, tn), lambda l, j: (l, 0, j)),
        out_shape=jax.ShapeDtypeStruct((nl, 8, n), F32),
        compiler_params=_cparams("parallel", "parallel"),
        name="adaln",
    )(cc, w_ada, b_ada.reshape(nl, 1, n))


def _norm_mod_kernel(x_ref, g_ref, sh_ref, sc_ref, o_ref):
    x = x_ref[...]
    y = x * lax.rsqrt(jnp.mean(x * x, axis=-1, keepdims=True) + EPS)
    y = y * g_ref[...]
    o_ref[...] = (y * (1.0 + sc_ref[...]) + sh_ref[...]).astype(o_ref.dtype)


def norm_mod(x, g, shift, scale, out_dtype):
    t, d = x.shape
    tm = _row_tile(t, 256)
    vec = pl.BlockSpec((1, d), lambda i: (0, 0))
    return pl.pallas_call(
        _norm_mod_kernel,
        grid=(t // tm,),
        in_specs=[pl.BlockSpec((tm, d), lambda i: (i, 0)), vec, vec, vec],
        out_specs=pl.BlockSpec((tm, d), lambda i: (i, 0)),
        out_shape=jax.ShapeDtypeStruct((t, d), out_dtype),
        compiler_params=_cparams("parallel"),
        name="norm_mod",
    )(x, g.reshape(1, d), shift.reshape(1, d), scale.reshape(1, d))


def _mm_kernel(a_ref, b_ref, o_ref):
    o_ref[...] = _dot(a_ref[...], b_ref[...]).astype(o_ref.dtype)


def matmul(a, b, out_dtype, tn=None):
    t, k = a.shape
    n = b.shape[1]
    tn = n if tn is None else tn
    tm = _row_tile(t)
    return pl.pallas_call(
        _mm_kernel,
        grid=(n // tn, t // tm),
        in_specs=[pl.BlockSpec((tm, k), lambda j, i: (i, 0)),
                  pl.BlockSpec((k, tn), lambda j, i: (0, j))],
        out_specs=pl.BlockSpec((tm, tn), lambda j, i: (i, j)),
        out_shape=jax.ShapeDtypeStruct((t, n), out_dtype),
        compiler_params=_cparams("parallel", "parallel"),
        name="matmul",
    )(a, b)


def _mm_res_kernel(a_ref, b_ref, x_ref, g_ref, o_ref):
    o_ref[...] = x_ref[...] + g_ref[...] * _dot(a_ref[...], b_ref[...])


def matmul_residual(a, b, x, gate, tn=512):
    t, k = a.shape
    n = b.shape[1]
    tm = _row_tile(t)
    return pl.pallas_call(
        _mm_res_kernel,
        grid=(n // tn, t // tm),
        in_specs=[pl.BlockSpec((tm, k), lambda j, i: (i, 0)),
                  pl.BlockSpec((k, tn), lambda j, i: (0, j)),
                  pl.BlockSpec((tm, tn), lambda j, i: (i, j)),
                  pl.BlockSpec((1, tn), lambda j, i: (0, j))],
        out_specs=pl.BlockSpec((tm, tn), lambda j, i: (i, j)),
        out_shape=jax.ShapeDtypeStruct((t, n), F32),
        compiler_params=_cparams("parallel", "parallel"),
        name="matmul_residual",
    )(a, b, x, gate.reshape(1, n))


def _rope_mm_kernel(n_rope, a_ref, w_ref, wr_ref, cos_ref, sin_ref, o_ref):
    a = a_ref[...]
    p = _dot(a, w_ref[...])
    pr = _dot(a, wr_ref[...])
    cos = cos_ref[...]
    sin = sin_ref[...]
    for j in range(p.shape[1] // LANE):
        sl = slice(j * LANE, (j + 1) * LANE)
        if j < n_rope:
            o_ref[:, sl] = (p[:, sl] * cos + pr[:, sl] * sin).astype(o_ref.dtype)
        else:
            o_ref[:, sl] = p[:, sl].astype(o_ref.dtype)


def rope_matmul(a, w, w_rot, cos, sin, n_rope):
    t, k = a.shape
    n = w.shape[1]
    nr = n_rope * LANE
    tm = _row_tile(t)
    return pl.pallas_call(
        functools.partial(_rope_mm_kernel, n_rope),
        grid=(t // tm,),
        in_specs=[pl.BlockSpec((tm, k), lambda i: (i, 0)),
                  pl.BlockSpec((k, n), lambda i: (0, 0)),
                  pl.BlockSpec((k, nr), lambda i: (0, 0)),
                  pl.BlockSpec((tm, LANE), lambda i: (i, 0)),
                  pl.BlockSpec((tm, LANE), lambda i: (i, 0))],
        out_specs=pl.BlockSpec((tm, n), lambda i: (i, 0)),
        out_shape=jax.ShapeDtypeStruct((t, n), CD),
        compiler_params=_cparams("parallel"),
        name="rope_matmul",
    )(a, w, w_rot, cos, sin)


def _mla_proj_kernel(h_ref, w1_ref, gq_ref, gkv_ref, wq_ref, wqr_ref, wkv_ref,
                     cq_ref, sq_ref, ck_ref, sk_ref, q_out, k_out, v_out):
    p = _dot(h_ref[...], w1_ref[...])
    cq = p[:, :MLA_Q_RANK]
    ckv = p[:, MLA_Q_RANK:MLA_Q_RANK + MLA_KV_RANK]
    o = MLA_Q_RANK + MLA_KV_RANK
    kr = p[:, o:o + LANE]
    krr = p[:, o + LANE:o + 2 * LANE]
    cqn = (cq * lax.rsqrt(jnp.mean(cq * cq, axis=-1, keepdims=True) + EPS) * gq_ref[...]).astype(CD)
    ckvn = (ckv * lax.rsqrt(jnp.mean(ckv * ckv, axis=-1, keepdims=True) + EPS) * gkv_ref[...]).astype(CD)
    qa = _dot(cqn, wq_ref[...])
    qb = _dot(cqn, wqr_ref[...])
    kv = _dot(ckvn, wkv_ref[...])
    krp = (kr * ck_ref[...] + krr * sk_ref[...]).astype(CD)
    cq_t = cq_ref[...]
    sq_t = sq_ref[...]
    for h in range(MLA_HEADS):
        qs = slice(h * MLA_HW, (h + 1) * MLA_HW)
        q_out[:, qs] = (qa[:, qs] * cq_t + qb[:, qs] * sq_t).astype(CD)
        k_out[:, h * MLA_HW:h * MLA_HW + MLA_NOPE] = kv[:, h * 256:h * 256 + MLA_NOPE].astype(CD)
        k_out[:, h * MLA_HW + MLA_NOPE:(h + 1) * MLA_HW] = krp
        v_out[:, h * MLA_V:(h + 1) * MLA_V] = kv[:, h * 256 + MLA_NOPE:(h + 1) * 256].astype(CD)


def mla_proj(h, w1, gq, gkv, wq, wqr, wkv, cos_q, sin_q, cos_k, sin_k):
    t, d = h.shape
    tm = _row_tile(t)
    full = lambda a: pl.BlockSpec(a.shape, lambda i: (0,) * a.ndim)
    rows = lambda w: pl.BlockSpec((tm, w), lambda i: (i, 0))
    hw = MLA_HEADS * MLA_HW
    return pl.pallas_call(
        _mla_proj_kernel,
        grid=(t // tm,),
        in_specs=[rows(d), full(w1), full(gq), full(gkv), full(wq), full(wqr), full(wkv),
                  rows(MLA_HW), rows(MLA_HW), rows(LANE), rows(LANE)],
        out_specs=[rows(hw), rows(hw), rows(MLA_HEADS * MLA_V)],
        out_shape=[jax.ShapeDtypeStruct((t, hw), CD), jax.ShapeDtypeStruct((t, hw), CD),
                   jax.ShapeDtypeStruct((t, MLA_HEADS * MLA_V), CD)],
        compiler_params=_cparams("parallel"),
        name="mla_proj",
    )(h, w1, gq, gkv, wq, wqr, wkv, cos_q, sin_q, cos_k, sin_k)


def _flash_kernel(tq, tk, unroll, q_ref, k_ref, v_ref, o_ref, s_sc, m_sc, acc_sc):
    nkb = k_ref.shape[0] // tk
    nsub = q_ref.shape[0] // tq
    total = nsub * nkb
    m_sc[...] = jnp.full(m_sc.shape, NEG, F32)
    acc_sc[...] = jnp.zeros(acc_sc.shape, F32)
    ones_col = jnp.where(lax.broadcasted_iota(jnp.int32, (tk, LANE), 1) == 0, 1.0, 0.0).astype(CD)

    def split(b):
        sub = b // nkb
        return sub, b - sub * nkb

    def scores(b):
        sub, j = split(b)
        q = q_ref[pl.ds(pl.multiple_of(sub * tq, tq), tq), :]
        k = k_ref[pl.ds(pl.multiple_of(j * tk, tk), tk), :]
        return _dot_nt(q, k)

    def softmax_pv(s, b):
        sub, j = split(b)
        v = v_ref[pl.ds(pl.multiple_of(j * tk, tk), tk), :]
        m_prev = m_sc[sub]
        m_new = jnp.maximum(m_prev, jnp.max(s, axis=-1, keepdims=True))
        alpha = jnp.exp2(m_prev - m_new)
        p = jnp.exp2(s - m_new).astype(CD)
        v_aug = jnp.concatenate([v, ones_col], axis=1)
        acc_sc[sub] = alpha * acc_sc[sub] + _dot(p, v_aug)
        m_sc[sub] = m_new

    s_sc[0] = scores(0)

    def body(g, carry):
        b0 = g * unroll
        for u in range(unroll):
            s_sc[(u + 1) % 2] = scores(jnp.minimum(b0 + u + 1, total - 1))
            softmax_pv(s_sc[u % 2], b0 + u)
        return carry

    lax.fori_loop(0, total // unroll, body, 0)
    for sub in range(nsub):
        acc = acc_sc[sub]
        o_ref[sub * tq:(sub + 1) * tq, :] = (acc[:, :MLA_V] / acc[:, MLA_V:MLA_V + 1]).astype(o_ref.dtype)


def mla_flash(q, k, v):
    s = q.shape[0]
    nk = k.shape[0]
    tq = _row_tile(s, 512)
    tstep = _row_tile(s, 4 * tq)
    tk = MLA_TK
    total = (tstep // tq) * (nk // tk)
    assert nk % tk == 0 and total % 2 == 0
    unroll = 4 if total % 4 == 0 else 2
    return pl.pallas_call(
        functools.partial(_flash_kernel, tq, tk, unroll),
        grid=(MLA_HEADS, s // tstep),
        in_specs=[pl.BlockSpec((tstep, MLA_HW), lambda h, i: (i, h)),
                  pl.BlockSpec((nk, MLA_HW), lambda h, i: (0, h)),
                  pl.BlockSpec((nk, MLA_V), lambda h, i: (0, h))],
        out_specs=pl.BlockSpec((tstep, MLA_V), lambda h, i: (i, h)),
        out_shape=jax.ShapeDtypeStruct((s, MLA_HEADS * MLA_V), CD),
        scratch_shapes=[pltpu.VMEM((2, tq, tk), F32), pltpu.VMEM((tstep // tq, tq, 1), F32),
                        pltpu.VMEM((tstep // tq, tq, 2 * MLA_V), F32)],
        compiler_params=_cparams("parallel", "arbitrary"),
        name="mla_flash",
    )(q, k, v)


def _dense_attn_kernel(n_heads, group, dqk, dv, log2_scores, q_ref, k_ref, v_ref, sink_ref, o_ref):
    q = q_ref[...]
    k = k_ref[...]
    v = v_ref[...]
    ex = jnp.exp2 if log2_scores else jnp.exp
    for h in range(n_heads):
        g = h // group
        s = _dot_nt(q[:, h * dqk:(h + 1) * dqk], k[:, g * dqk:(g + 1) * dqk])
        m = jnp.max(s, axis=-1, keepdims=True)
        l = jnp.zeros_like(m)
        if sink_ref is not None:
            sk = sink_ref[0:1, h:h + 1] * (LOG2E if log2_scores else 1.0)
            m = jnp.maximum(m, sk)
            l = ex(sk - m)
        p = ex(s - m)
        l = l + jnp.sum(p, axis=-1, keepdims=True)
        o = _dot(p.astype(CD), v[:, g * dv:(g + 1) * dv]) / l
        o_ref[:, h * dv:(h + 1) * dv] = o.astype(o_ref.dtype)


def dense_attn(qkv_specs, n_heads, group, dqk, dv, sink=None, log2_scores=False):
    c = qkv_specs[0][0].shape[0]
    arrays = [a for a, _, _ in qkv_specs]
    specs = [pl.BlockSpec((c, w), functools.partial(lambda b, i: (0, b), b)) for _, w, b in qkv_specs]
    if sink is None:
        kern = lambda q, k, v, o: _dense_attn_kernel(n_heads, group, dqk, dv, log2_scores, q, k, v, None, o)
    else:
        kern = functools.partial(_dense_attn_kernel, n_heads, group, dqk, dv, log2_scores)
        arrays.append(sink)
        specs.append(pl.BlockSpec(sink.shape, lambda i: (0, 0)))
    return pl.pallas_call(
        kern,
        grid=(1,),
        in_specs=specs,
        out_specs=pl.BlockSpec((c, n_heads * dv), lambda i: (0, 0)),
        out_shape=jax.ShapeDtypeStruct((c, n_heads * dv), CD),
        compiler_params=_cparams("arbitrary"),
        name="dense_attn",
    )(*arrays)


def _na_kernel(q_ref, kp_ref, ko_ref, kn_ref, vp_ref, vo_ref, vn_ref, kc_ref, vc_ref, bias_ref, o_ref):
    q = q_ref[...]
    kcat = jnp.concatenate([kp_ref[...], ko_ref[...], kn_ref[...], kc_ref[...]], axis=0)
    vcat = jnp.concatenate([vp_ref[...], vo_ref[...], vn_ref[...], vc_ref[...]], axis=0)
    nloc = 3 * q.shape[0]
    ones = _ones_col(kcat.shape[0], NA_DIM)
    for h in range(NA_HEADS):
        hs = slice(h * NA_DIM, (h + 1) * NA_DIM)
        s = _dot_nt(q[:, hs], kcat[:, hs])
        s_loc = s[:, :nloc] + bias_ref[0, h]
        s_ctx = s[:, nloc:]
        m = jnp.maximum(jnp.max(s_loc, axis=-1, keepdims=True), jnp.max(s_ctx, axis=-1, keepdims=True))
        p = jnp.concatenate([jnp.exp2(s_loc - m), jnp.exp2(s_ctx - m)], axis=1).astype(CD)
        acc = _dot(p, jnp.concatenate([vcat[:, hs], ones], axis=1))
        o_ref[:, hs] = (acc[:, :NA_DIM] / acc[:, NA_DIM:NA_DIM + 1]).astype(o_ref.dtype)


def _na_bias(rpb, rows):
    qr = NA_QROWS
    tb = qr * GRID_W
    nb = rows // qr
    nkr = 3 * qr
    rsel = np.zeros((3, qr, nkr, 2 * NA_WIN_R - 1), np.float32)
    for v, b in enumerate((0, 1, nb - 1)):
        for a in range(qr):
            r = qr * b + a
            rs = min(max(r - NA_WIN_R // 2, 0), rows - NA_WIN_R)
            for u in range(nkr):
                kr = qr * (b - 1) + u
                if 0 <= b - 1 + u // qr < nb and rs <= kr < rs + NA_WIN_R:
                    rsel[v, a, u, kr - r + NA_WIN_R - 1] = 1.0
    csel = np.zeros((GRID_W, GRID_W, 2 * NA_WIN_C - 1), np.float32)
    for qc in range(GRID_W):
        cs = min(max(qc - NA_WIN_C // 2, 0), GRID_W - NA_WIN_C)
        for kc in range(cs, cs + NA_WIN_C):
            csel[qc, kc, kc - qc + NA_WIN_C - 1] = 1.0
    valid = np.einsum('vaud,qkj->vaquk', rsel, csel) > 0
    hi = lax.Precision.HIGHEST
    cols = jnp.einsum('hdj,qkj->hdqk', rpb.astype(F32), jnp.asarray(csel), precision=hi)
    bias = jnp.einsum('vaud,hdqk->vhaquk', jnp.asarray(rsel), cols, precision=hi)
    bias = jnp.where(valid[:, None], bias * LOG2E, NEG)
    return bias.reshape(3, NA_HEADS, tb, 3 * tb)


def na_attn(p_lat, p_ctx, rpb, rows):
    s = p_lat.shape[0]
    c = p_ctx.shape[0]
    w = NA_HEADS * NA_DIM
    tb = NA_QROWS * GRID_W
    nb = s // tb
    assert rows % NA_QROWS == 0 and nb >= 3 and rows >= 2 * NA_WIN_R
    bias = _na_bias(rpb, rows)

    def blk(col, off):
        return pl.BlockSpec((tb, w), lambda i: (jnp.clip(i + off, 0, nb - 1), col))

    return pl.pallas_call(
        _na_kernel,
        grid=(nb,),
        in_specs=[blk(0, 0), blk(1, -1), blk(1, 0), blk(1, 1), blk(2, -1), blk(2, 0), blk(2, 1),
                  pl.BlockSpec((c, w), lambda i: (0, 1)),
                  pl.BlockSpec((c, w), lambda i: (0, 2)),
                  pl.BlockSpec((1, NA_HEADS, tb, 3 * tb),
                               lambda i: (jnp.where(i == 0, 0, jnp.where(i == nb - 1, 2, 1)), 0, 0, 0))],
        out_specs=pl.BlockSpec((tb, w), lambda i: (i, 0)),
        out_shape=jax.ShapeDtypeStruct((s, w), CD),
        compiler_params=_cparams("parallel"),
        name="na_attn",
    )(p_lat, p_lat, p_lat, p_lat, p_lat, p_lat, p_lat, p_ctx, p_ctx, bias)


def _sw_kernel(seq, q_ref, kp_ref, ko_ref, kn_ref, vp_ref, vo_ref, vn_ref, kc_ref, vc_ref, sink_ref, o_ref):
    i = pl.program_id(0)
    q = q_ref[...]
    tq = q.shape[0]
    kcat = jnp.concatenate([kp_ref[...], ko_ref[...], kn_ref[...], kc_ref[...]], axis=0)
    vcat = jnp.concatenate([vp_ref[...], vo_ref[...], vn_ref[...], vc_ref[...]], axis=0)
    nkk = tq + 2 * SW_WINDOW
    r = lax.broadcasted_iota(jnp.int32, (tq, nkk), 0)
    cc = lax.broadcasted_iota(jnp.int32, (tq, nkk), 1)
    kpos = i * tq - SW_WINDOW + cc
    d = cc - r
    mask = (d >= 0) & (d <= 2 * SW_WINDOW) & (kpos >= 0) & (kpos < seq)
    ones = _ones_col(kcat.shape[0], SW_DIM)
    grp = SW_HEADS // SW_KV_HEADS
    for g in range(SW_KV_HEADS):
        gs = slice(g * SW_DIM, (g + 1) * SW_DIM)
        kg = kcat[:, gs]
        v_aug = jnp.concatenate([vcat[:, gs], ones], axis=1)
        for h in range(g * grp, (g + 1) * grp):
            s = _dot_nt(q[:, h * SW_DIM:(h + 1) * SW_DIM], kg)
            s_loc = jnp.where(mask, s[:, :nkk], NEG)
            s_ctx = s[:, nkk:]
            sk = sink_ref[0:1, h:h + 1] * LOG2E
            m = jnp.maximum(jnp.maximum(jnp.max(s_loc, axis=-1, keepdims=True),
                                        jnp.max(s_ctx, axis=-1, keepdims=True)), sk)
            p = jnp.concatenate([jnp.exp2(s_loc - m), jnp.exp2(s_ctx - m)], axis=1).astype(CD)
            acc = _dot(p, v_aug)
            l = acc[:, SW_DIM:SW_DIM + 1] + jnp.exp2(sk - m)
            o_ref[:, h * SW_DIM:(h + 1) * SW_DIM] = (acc[:, :SW_DIM] / l).astype(o_ref.dtype)


def sw_attn(p_lat, p_ctx, sink):
    s = p_lat.shape[0]
    c = p_ctx.shape[0]
    tq = SW_TQ
    assert s % tq == 0 and tq % SW_WINDOW == 0
    nq = s // tq
    per = tq // SW_WINDOW
    nwb = s // SW_WINDOW
    kcol = SW_HEADS * SW_DIM // LANE
    prev = lambda col: pl.BlockSpec((SW_WINDOW, LANE), lambda i: (jnp.maximum(i * per - 1, 0), col))
    own = lambda col: pl.BlockSpec((tq, LANE), lambda i: (i, col))
    nxt = lambda col: pl.BlockSpec((SW_WINDOW, LANE), lambda i: (jnp.minimum((i + 1) * per, nwb - 1), col))
    return pl.pallas_call(
        functools.partial(_sw_kernel, s),
        grid=(nq,),
        in_specs=[pl.BlockSpec((tq, SW_HEADS * SW_DIM), lambda i: (i, 0)),
                  prev(kcol), own(kcol), nxt(kcol), prev(kcol + 1), own(kcol + 1), nxt(kcol + 1),
                  pl.BlockSpec((c, LANE), lambda i: (0, kcol)),
                  pl.BlockSpec((c, LANE), lambda i: (0, kcol + 1)),
                  pl.BlockSpec(sink.shape, lambda i: (0, 0))],
        out_specs=pl.BlockSpec((tq, SW_HEADS * SW_DIM), lambda i: (i, 0)),
        out_shape=jax.ShapeDtypeStruct((s, SW_HEADS * SW_DIM), CD),
        compiler_params=_cparams("parallel"),
        name="sw_attn",
    )(p_lat, p_lat, p_lat, p_lat, p_lat, p_lat, p_lat, p_ctx, p_ctx, sink)


def _ml_prep_kernel(nt, x_ref, xp_ref, xn_ref, g_ref, w_ref, b_ref, q_out, k_out, g_out):
    i = pl.program_id(0)
    x = x_ref[...]
    tm = x.shape[0]
    row = lax.broadcasted_iota(jnp.int32, x.shape, 0)
    has_prev = jnp.where(i > 0, 1.0, 0.0)
    has_next = jnp.where(i < nt - 1, 1.0, 0.0)
    prev_row = xp_ref[7:8, :] * has_prev
    next_row = xn_ref[0:1, :] * has_next
    xm = jnp.where(row == 0, prev_row, pltpu.roll(x, 1, axis=0))
    xq = jnp.where(row == tm - 1, next_row, pltpu.roll(x, tm - 1, axis=0))
    y = w_ref[0:1, :] * xm + w_ref[1:2, :] * x + w_ref[2:3, :] * xq
    y = y * _sigmoid(y)
    q_out[...] = y[:, :ML_QKW].astype(CD)
    k_out[...] = (y[:, ML_QKW:] * (ML_QK ** -0.5)).astype(CD)
    g = g_ref[...] + b_ref[...]
    lane = lax.broadcasted_iota(jnp.int32, g.shape, 1)
    logsig = jnp.minimum(g, 0.0) - jnp.log(1.0 + jnp.exp(-jnp.abs(g)))
    g_out[...] = jnp.where(lane < 2 * ML_HEADS, g, logsig)


def ml_prep(p_ml, conv_w, gate_b):
    t = p_ml.shape[0]
    tm = _row_tile(t, 256)
    nt = t // tm
    wq = 2 * ML_QKW
    gcol = (2 * ML_QKW + 2 * ML_VW) // LANE
    w8 = jnp.zeros((8, wq), F32).at[:ML_CONV].set(conv_w.astype(F32))
    b = jnp.zeros((1, LANE), F32).at[0, :4 * ML_HEADS].set(gate_b.astype(F32).reshape(-1))
    h8 = tm // 8
    return pl.pallas_call(
        functools.partial(_ml_prep_kernel, nt),
        grid=(nt,),
        in_specs=[pl.BlockSpec((tm, wq), lambda i: (i, 0)),
                  pl.BlockSpec((8, wq), lambda i: (jnp.maximum(i * h8 - 1, 0), 0)),
                  pl.BlockSpec((8, wq), lambda i: (jnp.minimum((i + 1) * h8, t // 8 - 1), 0)),
                  pl.BlockSpec((tm, LANE), lambda i: (i, gcol)),
                  pl.BlockSpec((8, wq), lambda i: (0, 0)),
                  pl.BlockSpec((1, LANE), lambda i: (0, 0))],
        out_specs=[pl.BlockSpec((tm, ML_QKW), lambda i: (i, 0)),
                   pl.BlockSpec((tm, ML_QKW), lambda i: (i, 0)),
                   pl.BlockSpec((tm, LANE), lambda i: (i, 0))],
        out_shape=[jax.ShapeDtypeStruct((t, ML_QKW), CD), jax.ShapeDtypeStruct((t, ML_QKW), CD),
                   jax.ShapeDtypeStruct((t, LANE), F32)],
        compiler_params=_cparams("parallel"),
        name="ml_prep",
    )(p_ml, p_ml, p_ml, p_ml, w8, b)


def _ml_scan_kernel(qf_ref, ktf_ref, vf_ref, gcf_ref, grf_ref,
                    qb_ref, ktb_ref, vb_ref, gcb_ref, grb_ref,
                    c0_ref, m0_ref, hf_ref, hb_ref, c_ref, m_ref):
    @pl.when(pl.program_id(0) == 0)
    def _():
        c_ref[...] = c0_ref[...]
        m_ref[...] = m0_ref[...]

    ln = qf_ref.shape[0]
    r = lax.broadcasted_iota(jnp.int32, (ln, ln), 0)
    c = lax.broadcasted_iota(jnp.int32, (ln, ln), 1)
    lower = r >= c
    upper = c >= r
    lower_f = jnp.where(lower, 1.0, 0.0)
    upper_f = jnp.where(upper, 1.0, 0.0)
    ones_col = jnp.where(lax.broadcasted_iota(jnp.int32, (ln, LANE), 1) == 0, 1.0, 0.0).astype(CD)
    hi = lax.Precision.HIGHEST

    streams = ((qf_ref, ktf_ref, vf_ref, gcf_ref, grf_ref, hf_ref, lower, lower_f, upper_f, ln - 1),
               (qb_ref, ktb_ref, vb_ref, gcb_ref, grb_ref, hb_ref, upper, upper_f, lower_f, 0))
    for d, (q_ref, kt_ref, v_ref, gc_ref, gr_ref, h_ref, mask, tri_c, tri_r, last) in enumerate(streams):
        gcol = gc_ref[...]
        grow = gr_ref[...]
        bcol = jnp.dot(tri_c, gcol, precision=hi, preferred_element_type=F32)
        brow = jnp.dot(grow, tri_r, precision=hi, preferred_element_type=F32)
        q = q_ref[...]
        kt = kt_ref[...]
        v = v_ref[...]
        for hd in range(ML_HEADS):
            ch = d * ML_HEADS + hd
            ic = d * ML_HEADS + hd
            fc = 2 * ML_HEADS + ic
            i_c = gcol[:, ic:ic + 1]
            b_c = bcol[:, fc:fc + 1]
            i_r = grow[ic:ic + 1, :]
            b_r = brow[fc:fc + 1, :]
            m_prev = m_ref[ch, 0:1, 0:1]
            cst = c_ref[ch]
            dmat = jnp.where(mask, b_c - b_r + i_r, NEG)
            inter = b_c + m_prev
            m_t = jnp.maximum(inter, jnp.max(dmat, axis=-1, keepdims=True))
            qh = q[:, hd * ML_QK:(hd + 1) * ML_QK]
            kth = kt[hd * ML_QK:(hd + 1) * ML_QK, :]
            s = lax.dot_general(qh, kth, (((1,), (0,)), ((), ())), preferred_element_type=F32)
            s = s * jnp.exp(dmat - m_t)
            w_inter = jnp.exp(inter - m_t)
            v_aug = jnp.concatenate([v[:, hd * ML_V:(hd + 1) * ML_V].astype(CD), ones_col], axis=1)
            num = _dot(s.astype(CD), v_aug) + w_inter * _dot(qh, cst.astype(CD))
            den = num[:, ML_V:ML_V + 1]
            h = num[:, :ML_V] / jnp.maximum(jnp.abs(den), jnp.exp(-m_t))
            h_ref[:, hd * ML_V:(hd + 1) * ML_V] = h
            b_end = b_c[last:last + 1, :]
            m_new = jnp.maximum(b_end + m_prev, jnp.max(b_end - b_c + i_c, axis=0, keepdims=True))
            w_s = jnp.exp(b_end - b_r + i_r - m_new)
            w_c = jnp.exp(b_end + m_prev - m_new)
            ktw = (kth.astype(F32) * w_s).astype(CD)
            c_ref[ch] = w_c * cst + _dot(ktw, v_aug)
            m_ref[ch] = jnp.broadcast_to(m_new, m_ref.shape[1:])


def ml_scan(qc, kc, p_ml, gates, state):
    t = qc.shape[0]
    ln = min(ML_CHUNK, t)
    assert t % ln == 0
    nc = t // ln
    kt = kc.T
    gr = gates[:, :4 * ML_HEADS].T
    c0, m0 = state
    vcol = 2 * ML_QKW // ML_VW
    fwd = lambda j: j
    bwd = lambda j: nc - 1 - j

    def specs(ix):
        return [pl.BlockSpec((ln, ML_QKW), lambda j: (ix(j), 0)),
                pl.BlockSpec((ML_QKW, ln), lambda j: (0, ix(j))),
                pl.BlockSpec((ln, ML_VW), lambda j: (ix(j), vcol)),
                pl.BlockSpec((ln, LANE), lambda j: (ix(j), 0)),
                pl.BlockSpec((4 * ML_HEADS, ln), lambda j: (0, ix(j)))]

    st_specs = [pl.BlockSpec(c0.shape, lambda j: (0, 0, 0)), pl.BlockSpec(m0.shape, lambda j: (0, 0, 0))]
    return pl.pallas_call(
        _ml_scan_kernel,
        grid=(nc,),
        in_specs=specs(fwd) + specs(bwd) + st_specs,
        out_specs=[pl.BlockSpec((ln, ML_VW), lambda j: (fwd(j), 0)),
                   pl.BlockSpec((ln, ML_VW), lambda j: (bwd(j), 0))] + st_specs,
        out_shape=[jax.ShapeDtypeStruct((t, ML_VW), F32), jax.ShapeDtypeStruct((t, ML_VW), F32),
                   jax.ShapeDtypeStruct(c0.shape, F32), jax.ShapeDtypeStruct(m0.shape, F32)],
        compiler_params=_cparams("arbitrary"),
        name="ml_scan",
    )(qc, kt, p_ml, gates, gr, qc, kt, p_ml, gates, gr, c0, m0)


def _ml_finish_kernel(hf_ref, hb_ref, o_ref, g_ref, y_ref):
    h = hf_ref[...] + hb_ref[...]
    og = _sigmoid(o_ref[...])
    g = g_ref[...]
    for hd in range(ML_HEADS):
        sl = slice(hd * ML_V, (hd + 1) * ML_V)
        hh = h[:, sl]
        y = hh * lax.rsqrt(jnp.mean(hh * hh, axis=-1, keepdims=True) + EPS) * g[:, sl]
        y_ref[:, sl] = (y * og[:, sl]).astype(y_ref.dtype)


def ml_finish(hf, hb, p_ml, norm_g):
    t = hf.shape[0]
    tm = _row_tile(t, 256)
    ocol = (2 * ML_QKW + ML_VW) // ML_VW
    blk = pl.BlockSpec((tm, ML_VW), lambda i: (i, 0))
    return pl.pallas_call(
        _ml_finish_kernel,
        grid=(t // tm,),
        in_specs=[blk, blk, pl.BlockSpec((tm, ML_VW), lambda i: (i, ocol)),
                  pl.BlockSpec((1, ML_VW), lambda i: (0, 0))],
        out_specs=blk,
        out_shape=jax.ShapeDtypeStruct((t, ML_VW), CD),
        compiler_params=_cparams("parallel"),
        name="ml_finish",
    )(hf, hb, p_ml, norm_g.reshape(1, ML_VW).astype(F32))


def _merge_kernel(h_ref, ya_ref, yb_ref, yc_ref, yd_ref, g0_ref, g1_ref, g2_ref, g3_ref, wb_ref, o_ref):
    h = h_ref[...]
    acc = None
    for i, (y_ref, g_ref) in enumerate(zip((ya_ref, yb_ref, yc_ref, yd_ref), (g0_ref, g1_ref, g2_ref, g3_ref))):
        term = _sigmoid(_dot(h, g_ref[...])) * _dot(y_ref[...], wb_ref[i])
        acc = term if acc is None else acc + term
    o_ref[...] = acc.astype(o_ref.dtype)


def gated_merge(h, ys, w_gate, w_branch, tn=512):
    t, d = h.shape
    tm = _row_tile(t)
    nj = d // tn
    yspec = pl.BlockSpec((tm, BRANCH_W), lambda j, i: (i, 0))
    gspec = lambda b: pl.BlockSpec((d, tn), lambda j, i: (0, b * nj + j))
    return pl.pallas_call(
        _merge_kernel,
        grid=(nj, t // tm),
        in_specs=[pl.BlockSpec((tm, d), lambda j, i: (i, 0)), yspec, yspec, yspec, yspec,
                  gspec(0), gspec(1), gspec(2), gspec(3),
                  pl.BlockSpec((N_BRANCH, BRANCH_W, tn), lambda j, i: (0, 0, j))],
        out_specs=pl.BlockSpec((tm, tn), lambda j, i: (i, j)),
        out_shape=jax.ShapeDtypeStruct((t, d), CD),
        compiler_params=_cparams("parallel", "parallel"),
        name="gated_merge",
    )(h, *ys, w_gate, w_gate, w_gate, w_gate, w_branch)


def _ffn1_kernel(h_ref, wa_ref, wg_ref, o_ref):
    h = h_ref[...]
    a = _dot(h, wa_ref[...])
    g = _dot(h, wg_ref[...])
    o_ref[...] = (a * _sigmoid(a) * g).astype(o_ref.dtype)


def ffn_up(h, w, tn=512):
    t, d = h.shape
    dff = w.shape[1] // 2
    tm = _row_tile(t)
    nj = dff // tn
    return pl.pallas_call(
        _ffn1_kernel,
        grid=(nj, t // tm),
        in_specs=[pl.BlockSpec((tm, d), lambda j, i: (i, 0)),
                  pl.BlockSpec((d, tn), lambda j, i: (0, j)),
                  pl.BlockSpec((d, tn), lambda j, i: (0, nj + j))],
        out_specs=pl.BlockSpec((tm, tn), lambda j, i: (i, j)),
        out_shape=jax.ShapeDtypeStruct((t, dff), CD),
        compiler_params=_cparams("parallel", "parallel"),
        name="ffn_up",
    )(h, w, w)


def _rot_cols(w, dim):
    k, n = w.shape
    w4 = w.reshape(k, n // dim, 2, dim // 2)
    return jnp.stack([-w4[:, :, 1], w4[:, :, 0]], axis=2).reshape(k, n)


def _rope_tables(n_tokens, dim):
    a = dim // 2
    inv = 1.0 / (ROPE_THETA ** (jnp.arange(0, a, 2, dtype=F32) / a))
    t = jnp.arange(n_tokens)
    row = (t // GRID_W).astype(F32)
    col = (t % GRID_W).astype(F32)
    ang = jnp.concatenate([row[:, None] * inv, col[:, None] * inv], axis=-1)
    return jnp.cos(ang), jnp.sin(ang)


def _layer_weights(w_in_l, mla_w_uq_l, mla_w_ukv_l):
    d = w_in_l.shape[0]
    o_na = ML_COLS
    o_sw = o_na + NA_COLS
    o_mla = o_sw + SW_COLS
    o_gate = o_mla + MLA_COLS
    w = {}
    w["ml"] = jnp.pad(w_in_l[:, :ML_COLS], ((0, 0), (0, ML_PAD - ML_COLS))).astype(CD)
    na = w_in_l[:, o_na:o_sw]
    nq = NA_HEADS * NA_DIM
    w["na"] = jnp.concatenate([na[:, :nq] * (NA_DIM ** -0.5 * LOG2E), na[:, nq:]], axis=1).astype(CD)
    sq = SW_HEADS * SW_DIM
    skv = SW_KV_HEADS * SW_DIM
    sw_q = w_in_l[:, o_sw:o_sw + sq] * (SW_DIM ** -0.5 * LOG2E)
    sw_k = w_in_l[:, o_sw + sq:o_sw + sq + skv]
    sw_v = w_in_l[:, o_sw + sq + skv:o_mla]
    w["sw"] = jnp.concatenate([sw_q, sw_k, sw_v], axis=1).astype(CD)
    w["sw_rot"] = jnp.concatenate([_rot_cols(sw_q, SW_DIM), _rot_cols(sw_k, SW_DIM)], axis=1).astype(CD)
    m_cq = w_in_l[:, o_mla:o_mla + MLA_Q_RANK + MLA_KV_RANK]
    m_kr = w_in_l[:, o_mla + MLA_Q_RANK + MLA_KV_RANK:o_gate]
    z = jnp.zeros((d, LANE - MLA_ROPE), F32)
    w["mla1"] = jnp.concatenate([m_cq, m_kr, z, _rot_cols(m_kr, MLA_ROPE), z], axis=1).astype(CD)
    uq = mla_w_uq_l.reshape(MLA_Q_RANK, MLA_HEADS, MLA_NOPE + MLA_ROPE)
    zq = jnp.zeros((MLA_Q_RANK, MLA_HEADS, MLA_HW - MLA_NOPE - MLA_ROPE), F32)
    w["mla_q"] = jnp.concatenate([uq, zq], axis=2).reshape(MLA_Q_RANK, -1).astype(CD)
    uqr = _rot_cols(uq[:, :, MLA_NOPE:].reshape(MLA_Q_RANK, -1), MLA_ROPE).reshape(MLA_Q_RANK, MLA_HEADS, MLA_ROPE)
    w["mla_q_rot"] = jnp.concatenate([jnp.zeros_like(uq[:, :, :MLA_NOPE]), uqr, zq], axis=2
                                     ).reshape(MLA_Q_RANK, -1).astype(CD)
    w["mla_kv"] = mla_w_ukv_l.astype(CD)
    w["gate"] = w_in_l[:, o_gate:].astype(CD)
    return w


def _tables(s, c):
    cos_sw, sin_sw = _rope_tables(s, SW_DIM)
    cos_m, sin_m = _rope_tables(s, MLA_ROPE)
    tile2 = lambda a: jnp.concatenate([a, a, a, a], axis=1)
    scale = (MLA_NOPE + MLA_ROPE) ** -0.5 * LOG2E
    zpad = jnp.zeros((s, MLA_HW - MLA_NOPE - MLA_ROPE), F32)
    lat = dict(
        sw_cos=tile2(cos_sw), sw_sin=tile2(sin_sw),
        q_cos=scale * jnp.concatenate([jnp.ones((s, MLA_NOPE), F32), cos_m, cos_m, zpad], axis=1),
        q_sin=scale * jnp.concatenate([jnp.zeros((s, MLA_NOPE), F32), sin_m, sin_m, zpad], axis=1),
        k_cos=jnp.concatenate([cos_m, cos_m, zpad], axis=1),
        k_sin=jnp.concatenate([sin_m, sin_m, zpad], axis=1))
    ctx = dict(
        sw_cos=jnp.ones((c, LANE), F32), sw_sin=jnp.zeros((c, LANE), F32),
        q_cos=jnp.full((c, MLA_HW), scale, F32), q_sin=jnp.zeros((c, MLA_HW), F32),
        k_cos=jnp.ones((c, LANE), F32), k_sin=jnp.zeros((c, LANE), F32))
    return lat, ctx


def _project(h, w, tab, gq, gkv):
    p_ml = matmul(h, w["ml"], F32)
    p_na = matmul(h, w["na"], CD)
    n_rope = (SW_HEADS + SW_KV_HEADS) * SW_DIM // LANE
    p_sw = rope_matmul(h, w["sw"], w["sw_rot"], tab["sw_cos"], tab["sw_sin"], n_rope)
    mq, mk, mv = mla_proj(h, w["mla1"], gq, gkv, w["mla_q"], w["mla_q_rot"], w["mla_kv"],
                          tab["q_cos"], tab["q_sin"], tab["k_cos"], tab["k_sin"])
    return p_ml, p_na, p_sw, (mq, mk, mv)


def _dense_tail(x, h, ys, w, w_branch_l, w_out_l, g1, norm2_g_l, sh2, sc2, w_ffn_in_l, w_ffn_out_l, g2):
    merged = gated_merge(h, ys, w["gate"], w_branch_l)
    x = matmul_residual(merged, w_out_l, x, g1)
    h2 = norm_mod(x, norm2_g_l, sh2, sc2, CD)
    u = ffn_up(h2, w_ffn_in_l)
    return matmul_residual(u, w_ffn_out_l, x, g2)


def kernel(x, c, ctx, c_ctx, w_ada, b_ada, norm1_g, norm2_g, w_in, ml_conv_w, ml_gate_b, ml_norm_g, na_rpb,
           sw_sink, mla_q_norm_g, mla_kv_norm_g, mla_w_uq, mla_w_ukv, w_branch, w_out, w_ffn_in, w_ffn_out,
           final_norm_g):
    bsz, s, d = x.shape
    assert bsz == 1
    depth = w_in.shape[0]
    n_ctx = ctx.shape[1]
    rows = s // GRID_W
    xl = x[0].astype(F32)
    xc = ctx[0].astype(F32)
    cc = jnp.zeros((8, d), F32).at[0].set(c[0]).at[1].set(c_ctx)
    mod = adaln(cc, w_ada, b_ada)
    tab_l, tab_c = _tables(s, n_ctx)
    zero_state = (jnp.zeros((2 * ML_HEADS, ML_QK, 2 * ML_V), F32), jnp.zeros((2 * ML_HEADS, 8, LANE), F32))

    for l in range(depth):
        need_ctx = l < depth - 1
        w = _layer_weights(w_in[l], mla_w_uq[l], mla_w_ukv[l])
        wb = w_branch[l].astype(CD)
        wo = w_out[l].astype(CD)
        wf1 = w_ffn_in[l].astype(CD)
        wf2 = w_ffn_out[l].astype(CD)
        sh1, sc1, g1, sh2, sc2, g2 = [mod[l, 0, i * d:(i + 1) * d] for i in range(6)]
        csh1, csc1, cg1, csh2, csc2, cg2 = [mod[l, 1, i * d:(i + 1) * d] for i in range(6)]
        gq = mla_q_norm_g[l].reshape(1, -1).astype(F32)
        gkv = mla_kv_norm_g[l].reshape(1, -1).astype(F32)
        sink = sw_sink[l].reshape(1, -1).astype(F32)

        hl = norm_mod(xl, norm1_g[l], sh1, sc1, CD)
        hc = norm_mod(xc, norm1_g[l], csh1, csc1, CD)
        ml_l, na_l, sw_l, (mq_l, mk_l, mv_l) = _project(hl, w, tab_l, gq, gkv)
        ml_c, na_c, sw_c, (mq_c, mk_c, mv_c) = _project(hc, w, tab_c, gq, gkv)

        qc_c, kc_c, gt_c = ml_prep(ml_c, ml_conv_w[l], ml_gate_b[l])
        qc_l, kc_l, gt_l = ml_prep(ml_l, ml_conv_w[l], ml_gate_b[l])
        hf_c, hb_c, cst, mst = ml_scan(qc_c, kc_c, ml_c, gt_c, zero_state)
        hf_l, hb_l, _, _ = ml_scan(qc_l, kc_l, ml_l, gt_l, (cst, mst))
        ya_l = ml_finish(hf_l, hb_l, ml_l, ml_norm_g[l])
        yb_l = na_attn(na_l, na_c, na_rpb[l], rows)
        yc_l = sw_attn(sw_l, sw_c, sink)
        yd_l = mla_flash(mq_l, jnp.concatenate([mk_l, mk_c]), jnp.concatenate([mv_l, mv_c]))

        xl = _dense_tail(xl, hl, (ya_l, yb_l, yc_l, yd_l), w, wb, wo, g1, norm2_g[l], sh2, sc2, wf1, wf2, g2)
        if need_ctx:
            ya_c = ml_finish(hf_c, hb_c, ml_c, ml_norm_g[l])
            nw = NA_HEADS * NA_DIM
            yb_c = dense_attn([(na_c, nw, 0), (na_c, nw, 1), (na_c, nw, 2)], NA_HEADS, 1, NA_DIM, NA_DIM,
                              log2_scores=True)
            kcol = SW_HEADS * SW_DIM // LANE
            yc_c = dense_attn([(sw_c, SW_HEADS * SW_DIM, 0), (sw_c, LANE, kcol), (sw_c, LANE, kcol + 1)],
                              SW_HEADS, SW_HEADS // SW_KV_HEADS, SW_DIM, SW_DIM, sink=sink, log2_scores=True)
            yd_c = dense_attn([(mq_c, MLA_HEADS * MLA_HW, 0), (mk_c, MLA_HEADS * MLA_HW, 0),
                               (mv_c, MLA_HEADS * MLA_V, 0)], MLA_HEADS, 1, MLA_HW, MLA_V, log2_scores=True)
            xc = _dense_tail(xc, hc, (ya_c, yb_c, yc_c, yd_c), w, wb, wo, cg1, norm2_g[l], csh2, csc2,
                             wf1, wf2, cg2)

    zeros = jnp.zeros((d,), F32)
    out = norm_mod(xl, final_norm_g, zeros, zeros, F32)
    return out[None].astype(x.dtype)
```

```python
import functools

import numpy as np
import jax
import jax.numpy as jnp
from jax import lax
from jax.experimental import pallas as pl
from jax.experimental.pallas import tpu as pltpu

F32 = jnp.float32
CD = jnp.bfloat16

GRID_W = 64
EPS = 1e-6
ROPE_THETA = 10000.0
ML_HEADS, ML_QK, ML_V, ML_CONV = 4, 64, 128, 3
NA_HEADS, NA_DIM, NA_WIN_R, NA_WIN_C = 8, 64, 8, 16
SW_HEADS, SW_KV_HEADS, SW_DIM, SW_WINDOW = 8, 2, 64, 128
MLA_HEADS, MLA_Q_RANK, MLA_KV_RANK, MLA_NOPE, MLA_ROPE, MLA_V = 4, 384, 256, 128, 64, 128
N_BRANCH, BRANCH_W = 4, 512

ML_QKW = ML_HEADS * ML_QK
ML_VW = ML_HEADS * ML_V
ML_COLS = 2 * ML_QKW + 2 * ML_VW + 4 * ML_HEADS
ML_PAD = 1664
NA_COLS = 3 * NA_HEADS * NA_DIM
SW_COLS = (SW_HEADS + 2 * SW_KV_HEADS) * SW_DIM
MLA_COLS = MLA_Q_RANK + MLA_KV_RANK + MLA_ROPE
MLA_HW = 256
LANE = 128
NEG = -1e30
LOG2E = float(np.log2(np.e))
VMEM_LIMIT = 56 * 1024 * 1024

NA_QROWS = 4
SW_TQ = 256
ML_CHUNK = 256
MLA_TK = 1280


def _cparams(*sem):
    return pltpu.CompilerParams(dimension_semantics=sem, vmem_limit_bytes=VMEM_LIMIT)


def _row_tile(t, pref=512):
    tm = min(pref, t)
    assert t % tm == 0
    return tm


def _dot(a, b):
    return jnp.dot(a, b, preferred_element_type=F32)


def _dot_nt(a, b):
    return lax.dot_general(a, b, (((1,), (1,)), ((), ())), preferred_element_type=F32)


def _sigmoid(x):
    return 1.0 / (1.0 + jnp.exp(-x))


def _ones_col(n, w):
    return jnp.where(lax.broadcasted_iota(jnp.int32, (n, w), 1) == 0, 1.0, 0.0).astype(CD)


def _adaln_kernel(c_ref, w_ref, b_ref, o_ref):
    w = w_ref[0]
    for r in range(2):
        c = c_ref[:, r:r + 1]
        o_ref[0, r:r + 1, :] = jnp.sum((c * _sigmoid(c)) * w, axis=0, keepdims=True) + b_ref[0]


def adaln(cc, w_ada, b_ada):
    nl, d, n = w_ada.shape
    tn = 1024
    return pl.pallas_call(
        _adaln_kernel,
        grid=(nl, n // tn),
        in_specs=[pl.BlockSpec((d, 2), lambda l, j: (0, 0)),
                  pl.BlockSpec((1, d, tn), lambda l, j: (l, 0, j)),
                  pl.BlockSpec((1, 1, tn), lambda l, j: (l, 0, j))],
        out_specs=pl.BlockSpec((1, 2, tn), lambda l, j: (l, 0, j)),
        out_shape=jax.ShapeDtypeStruct((nl, 2, n), F32),
        compiler_params=_cparams("parallel", "parallel"),
        name="adaln",
    )(cc, w_ada, b_ada.reshape(nl, 1, n))


def _vec_spec(d, idx):
    return pl.BlockSpec((1, 1, d), lambda *_: (idx, 0, 0))


def _norm_mod(x, g, shift, scale):
    y = x * lax.rsqrt(jnp.mean(x * x, axis=-1, keepdims=True) + EPS)
    return (y * g) * (1.0 + scale) + shift


def _norm_mod_kernel(x_ref, g_ref, sh_ref, sc_ref, o_ref):
    o_ref[...] = _norm_mod(x_ref[...], g_ref[0], sh_ref[0], sc_ref[0]).astype(o_ref.dtype)


def norm_mod(x, vt, norm_idx, out_dtype):
    t, d = x.shape
    tm = _row_tile(t, 256)
    return pl.pallas_call(
        _norm_mod_kernel,
        grid=(t // tm,),
        in_specs=[pl.BlockSpec((tm, d), lambda i: (i, 0))] + [_vec_spec(d, ix) for ix in norm_idx],
        out_specs=pl.BlockSpec((tm, d), lambda i: (i, 0)),
        out_shape=jax.ShapeDtypeStruct((t, d), out_dtype),
        compiler_params=_cparams("parallel"),
        name="norm_mod",
    )(x, vt, vt, vt)


def _mm_kernel(a_ref, b_ref, o_ref):
    o_ref[...] = _dot(a_ref[...], b_ref[...]).astype(o_ref.dtype)


def matmul(a, b, out_dtype, tn=None):
    t, k = a.shape
    n = b.shape[1]
    tn = n if tn is None else tn
    tm = _row_tile(t)
    return pl.pallas_call(
        _mm_kernel,
        grid=(n // tn, t // tm),
        in_specs=[pl.BlockSpec((tm, k), lambda j, i: (i, 0)),
                  pl.BlockSpec((k, tn), lambda j, i: (0, j))],
        out_specs=pl.BlockSpec((tm, tn), lambda j, i: (i, j)),
        out_shape=jax.ShapeDtypeStruct((t, n), out_dtype),
        compiler_params=_cparams("parallel", "parallel"),
        name="matmul",
    )(a, b)


def _mm_res_norm_kernel(nk, want_x, a_ref, b_ref, x_ref, gate_ref, g_ref, sh_ref, sc_ref, *rest):
    if want_x:
        xo_ref, ho_ref = rest[:2]
    else:
        xo_ref, ho_ref = None, rest[0]
    acc_ref = rest[-1] if nk > 1 else None
    k = pl.program_id(1)
    part = _dot(a_ref[...], b_ref[...])

    def finish(acc):
        x = x_ref[...] + gate_ref[0] * acc
        if want_x:
            xo_ref[...] = x
        ho_ref[...] = _norm_mod(x, g_ref[0], sh_ref[0], sc_ref[0]).astype(ho_ref.dtype)

    if nk == 1:
        finish(part)
        return

    @pl.when(k == 0)
    def _():
        acc_ref[...] = part

    @pl.when((k > 0) & (k < nk - 1))
    def _():
        acc_ref[...] += part

    @pl.when(k == nk - 1)
    def _():
        finish(acc_ref[...] + part)


def matmul_residual_norm(a, b, x, vt, gate_idx, norm_idx, want_x, h_dtype, tk=None):
    t, kdim = a.shape
    n = b.shape[1]
    tk = kdim if tk is None else tk
    nk = kdim // tk
    assert kdim % tk == 0
    tm = _row_tile(t)
    row = pl.BlockSpec((tm, n), lambda i, k: (i, 0))
    out_shape = [jax.ShapeDtypeStruct((t, n), h_dtype)]
    if want_x:
        out_shape = [jax.ShapeDtypeStruct((t, n), F32)] + out_shape
    out = pl.pallas_call(
        functools.partial(_mm_res_norm_kernel, nk, want_x),
        grid=(t // tm, nk),
        in_specs=[pl.BlockSpec((tm, tk), lambda i, k: (i, k)),
                  pl.BlockSpec((tk, n), lambda i, k: (k, 0)),
                  row, _vec_spec(n, gate_idx)] + [_vec_spec(n, ix) for ix in norm_idx],
        out_specs=[row] * len(out_shape),
        out_shape=out_shape,
        scratch_shapes=[pltpu.VMEM((tm, n), F32)] if nk > 1 else [],
        compiler_params=_cparams("parallel", "arbitrary"),
        name="matmul_residual_norm",
    )(a, b, x, vt, vt, vt, vt)
    return out if want_x else out[0]


def _rope_mm_kernel(n_rope, a_ref, w_ref, wr_ref, cos_ref, sin_ref, o_ref):
    a = a_ref[...]
    p = _dot(a, w_ref[...])
    pr = _dot(a, wr_ref[...])
    cos = cos_ref[...]
    sin = sin_ref[...]
    for j in range(p.shape[1] // LANE):
        sl = slice(j * LANE, (j + 1) * LANE)
        if j < n_rope:
            o_ref[:, sl] = (p[:, sl] * cos + pr[:, sl] * sin).astype(o_ref.dtype)
        else:
            o_ref[:, sl] = p[:, sl].astype(o_ref.dtype)


def rope_matmul(a, w, w_rot, cos, sin, n_rope):
    t, k = a.shape
    n = w.shape[1]
    nr = n_rope * LANE
    tm = _row_tile(t)
    return pl.pallas_call(
        functools.partial(_rope_mm_kernel, n_rope),
        grid=(t // tm,),
        in_specs=[pl.BlockSpec((tm, k), lambda i: (i, 0)),
                  pl.BlockSpec((k, n), lambda i: (0, 0)),
                  pl.BlockSpec((k, nr), lambda i: (0, 0)),
                  pl.BlockSpec((tm, LANE), lambda i: (i, 0)),
                  pl.BlockSpec((tm, LANE), lambda i: (i, 0))],
        out_specs=pl.BlockSpec((tm, n), lambda i: (i, 0)),
        out_shape=jax.ShapeDtypeStruct((t, n), CD),
        compiler_params=_cparams("parallel"),
        name="rope_matmul",
    )(a, w, w_rot, cos, sin)


def _mla_proj_kernel(h_ref, w1_ref, gq_ref, gkv_ref, wq_ref, wqr_ref, wkv_ref,
                     cq_ref, sq_ref, ck_ref, sk_ref, q_out, k_out, v_out):
    p = _dot(h_ref[...], w1_ref[...])
    cq = p[:, :MLA_Q_RANK]
    ckv = p[:, MLA_Q_RANK:MLA_Q_RANK + MLA_KV_RANK]
    o = MLA_Q_RANK + MLA_KV_RANK
    kr = p[:, o:o + LANE]
    krr = p[:, o + LANE:o + 2 * LANE]
    cqn = (cq * lax.rsqrt(jnp.mean(cq * cq, axis=-1, keepdims=True) + EPS) * gq_ref[...]).astype(CD)
    ckvn = (ckv * lax.rsqrt(jnp.mean(ckv * ckv, axis=-1, keepdims=True) + EPS) * gkv_ref[...]).astype(CD)
    qa = _dot(cqn, wq_ref[...])
    qb = _dot(cqn, wqr_ref[...])
    kv = _dot(ckvn, wkv_ref[...])
    krp = (kr * ck_ref[...] + krr * sk_ref[...]).astype(CD)
    cq_t = cq_ref[...]
    sq_t = sq_ref[...]
    for h in range(MLA_HEADS):
        qs = slice(h * MLA_HW, (h + 1) * MLA_HW)
        q_out[:, qs] = (qa[:, qs] * cq_t + qb[:, qs] * sq_t).astype(CD)
        k_out[:, h * MLA_HW:h * MLA_HW + MLA_NOPE] = kv[:, h * 256:h * 256 + MLA_NOPE].astype(CD)
        k_out[:, h * MLA_HW + MLA_NOPE:(h + 1) * MLA_HW] = krp
        v_out[:, h * MLA_V:(h + 1) * MLA_V] = kv[:, h * 256 + MLA_NOPE:(h + 1) * 256].astype(CD)


def mla_proj(h, w1, gq, gkv, wq, wqr, wkv, cos_q, sin_q, cos_k, sin_k):
    t, d = h.shape
    tm = _row_tile(t)
    full = lambda a: pl.BlockSpec(a.shape, lambda i: (0,) * a.ndim)
    rows = lambda w: pl.BlockSpec((tm, w), lambda i: (i, 0))
    hw = MLA_HEADS * MLA_HW
    return pl.pallas_call(
        _mla_proj_kernel,
        grid=(t // tm,),
        in_specs=[rows(d), full(w1), full(gq), full(gkv), full(wq), full(wqr), full(wkv),
                  rows(MLA_HW), rows(MLA_HW), rows(LANE), rows(LANE)],
        out_specs=[rows(hw), rows(hw), rows(MLA_HEADS * MLA_V)],
        out_shape=[jax.ShapeDtypeStruct((t, hw), CD), jax.ShapeDtypeStruct((t, hw), CD),
                   jax.ShapeDtypeStruct((t, MLA_HEADS * MLA_V), CD)],
        compiler_params=_cparams("parallel"),
        name="mla_proj",
    )(h, w1, gq, gkv, wq, wqr, wkv, cos_q, sin_q, cos_k, sin_k)


def _flash_kernel(tq, tk, unroll, q_ref, k_ref, v_ref, o_ref, s_sc, m_sc, acc_sc):
    nkb = k_ref.shape[0] // tk
    nsub = q_ref.shape[0] // tq
    total = nsub * nkb
    m_sc[...] = jnp.full(m_sc.shape, NEG, F32)
    acc_sc[...] = jnp.zeros(acc_sc.shape, F32)
    ones_col = jnp.where(lax.broadcasted_iota(jnp.int32, (tk, LANE), 1) == 0, 1.0, 0.0).astype(CD)

    def split(b):
        sub = b // nkb
        return sub, b - sub * nkb

    def scores(b):
        sub, j = split(b)
        q = q_ref[pl.ds(pl.multiple_of(sub * tq, tq), tq), :]
        k = k_ref[pl.ds(pl.multiple_of(j * tk, tk), tk), :]
        return _dot_nt(q, k)

    def softmax_pv(s, b):
        sub, j = split(b)
        v = v_ref[pl.ds(pl.multiple_of(j * tk, tk), tk), :]
        m_prev = m_sc[sub]
        m_new = jnp.maximum(m_prev, jnp.max(s, axis=-1, keepdims=True))
        alpha = jnp.exp2(m_prev - m_new)
        p = jnp.exp2(s - m_new).astype(CD)
        v_aug = jnp.concatenate([v, ones_col], axis=1)
        acc_sc[sub] = alpha * acc_sc[sub] + _dot(p, v_aug)
        m_sc[sub] = m_new

    s_sc[0] = scores(0)

    def body(g, carry):
        b0 = g * unroll
        for u in range(unroll):
            s_sc[(u + 1) % 2] = scores(jnp.minimum(b0 + u + 1, total - 1))
            softmax_pv(s_sc[u % 2], b0 + u)
        return carry

    lax.fori_loop(0, total // unroll, body, 0)
    for sub in range(nsub):
        acc = acc_sc[sub]
        o_ref[sub * tq:(sub + 1) * tq, :] = (acc[:, :MLA_V] / acc[:, MLA_V:MLA_V + 1]).astype(o_ref.dtype)


def mla_flash(q, k, v):
    s = q.shape[0]
    nk = k.shape[0]
    tq = _row_tile(s, 512)
    tstep = _row_tile(s, 4 * tq)
    tk = MLA_TK
    total = (tstep // tq) * (nk // tk)
    assert nk % tk == 0 and total % 2 == 0
    unroll = 4 if total % 4 == 0 else 2
    return pl.pallas_call(
        functools.partial(_flash_kernel, tq, tk, unroll),
        grid=(MLA_HEADS, s // tstep),
        in_specs=[pl.BlockSpec((tstep, MLA_HW), lambda h, i: (i, h)),
                  pl.BlockSpec((nk, MLA_HW), lambda h, i: (0, h)),
                  pl.BlockSpec((nk, MLA_V), lambda h, i: (0, h))],
        out_specs=pl.BlockSpec((tstep, MLA_V), lambda h, i: (i, h)),
        out_shape=jax.ShapeDtypeStruct((s, MLA_HEADS * MLA_V), CD),
        scratch_shapes=[pltpu.VMEM((2, tq, tk), F32), pltpu.VMEM((tstep // tq, tq, 1), F32),
                        pltpu.VMEM((tstep // tq, tq, 2 * MLA_V), F32)],
        compiler_params=_cparams("parallel", "arbitrary"),
        name="mla_flash",
    )(q, k, v)


def _dense_attn_kernel(n_heads, group, dqk, dv, log2_scores, q_ref, k_ref, v_ref, sink_ref, o_ref):
    q = q_ref[...]
    k = k_ref[...]
    v = v_ref[...]
    ex = jnp.exp2 if log2_scores else jnp.exp
    for h in range(n_heads):
        g = h // group
        s = _dot_nt(q[:, h * dqk:(h + 1) * dqk], k[:, g * dqk:(g + 1) * dqk])
        m = jnp.max(s, axis=-1, keepdims=True)
        l = jnp.zeros_like(m)
        if sink_ref is not None:
            sk = sink_ref[0:1, h:h + 1] * (LOG2E if log2_scores else 1.0)
            m = jnp.maximum(m, sk)
            l = ex(sk - m)
        p = ex(s - m)
        l = l + jnp.sum(p, axis=-1, keepdims=True)
        o = _dot(p.astype(CD), v[:, g * dv:(g + 1) * dv]) / l
        o_ref[:, h * dv:(h + 1) * dv] = o.astype(o_ref.dtype)


def dense_attn(qkv_specs, n_heads, group, dqk, dv, sink=None, log2_scores=False):
    c = qkv_specs[0][0].shape[0]
    arrays = [a for a, _, _ in qkv_specs]
    specs = [pl.BlockSpec((c, w), functools.partial(lambda b, i: (0, b), b)) for _, w, b in qkv_specs]
    if sink is None:
        kern = lambda q, k, v, o: _dense_attn_kernel(n_heads, group, dqk, dv, log2_scores, q, k, v, None, o)
    else:
        kern = functools.partial(_dense_attn_kernel, n_heads, group, dqk, dv, log2_scores)
        arrays.append(sink)
        specs.append(pl.BlockSpec(sink.shape, lambda i: (0, 0)))
    return pl.pallas_call(
        kern,
        grid=(1,),
        in_specs=specs,
        out_specs=pl.BlockSpec((c, n_heads * dv), lambda i: (0, 0)),
        out_shape=jax.ShapeDtypeStruct((c, n_heads * dv), CD),
        compiler_params=_cparams("arbitrary"),
        name="dense_attn",
    )(*arrays)


def _na_kernel(q_ref, kp_ref, ko_ref, kn_ref, vp_ref, vo_ref, vn_ref, kc_ref, vc_ref, bias_ref, o_ref):
    q = q_ref[...]
    kcat = jnp.concatenate([kp_ref[...], ko_ref[...], kn_ref[...], kc_ref[...]], axis=0)
    vcat = jnp.concatenate([vp_ref[...], vo_ref[...], vn_ref[...], vc_ref[...]], axis=0)
    nloc = 3 * q.shape[0]
    ones = _ones_col(kcat.shape[0], NA_DIM)
    for h in range(NA_HEADS):
        hs = slice(h * NA_DIM, (h + 1) * NA_DIM)
        s = _dot_nt(q[:, hs], kcat[:, hs])
        s_loc = s[:, :nloc] + bias_ref[0, h]
        s_ctx = s[:, nloc:]
        m = jnp.maximum(jnp.max(s_loc, axis=-1, keepdims=True), jnp.max(s_ctx, axis=-1, keepdims=True))
        p = jnp.concatenate([jnp.exp2(s_loc - m), jnp.exp2(s_ctx - m)], axis=1).astype(CD)
        acc = _dot(p, jnp.concatenate([vcat[:, hs], ones], axis=1))
        o_ref[:, hs] = (acc[:, :NA_DIM] / acc[:, NA_DIM:NA_DIM + 1]).astype(o_ref.dtype)


def _na_bias(rpb, rows):
    qr = NA_QROWS
    tb = qr * GRID_W
    nb = rows // qr
    nkr = 3 * qr
    rsel = np.zeros((3, qr, nkr, 2 * NA_WIN_R - 1), np.float32)
    for v, b in enumerate((0, 1, nb - 1)):
        for a in range(qr):
            r = qr * b + a
            rs = min(max(r - NA_WIN_R // 2, 0), rows - NA_WIN_R)
            for u in range(nkr):
                kr = qr * (b - 1) + u
                if 0 <= b - 1 + u // qr < nb and rs <= kr < rs + NA_WIN_R:
                    rsel[v, a, u, kr - r + NA_WIN_R - 1] = 1.0
    csel = np.zeros((GRID_W, GRID_W, 2 * NA_WIN_C - 1), np.float32)
    for qc in range(GRID_W):
        cs = min(max(qc - NA_WIN_C // 2, 0), GRID_W - NA_WIN_C)
        for kc in range(cs, cs + NA_WIN_C):
            csel[qc, kc, kc - qc + NA_WIN_C - 1] = 1.0
    valid = (np.einsum('vaud,qkj->vaquk', rsel, csel) > 0).reshape(3, tb, 3 * tb)
    assert 2 * qr == NA_WIN_R
    nd = 2 * NA_WIN_R - 1
    rp = rpb.astype(F32) * LOG2E
    gap = jnp.zeros((NA_HEADS, nd, GRID_W + 1 - (2 * NA_WIN_C - 1)), F32)
    ring = jnp.concatenate([rp[..., NA_WIN_C - 1:], gap, rp[..., :NA_WIN_C - 1]], axis=-1)
    cols = jnp.tile(ring, (1, 1, GRID_W))[..., :GRID_W * GRID_W].reshape(NA_HEADS, nd, GRID_W, GRID_W)
    off = NA_WIN_R - 1 - qr
    slabs = [cols[:, off - a:off - a + nkr].transpose(0, 2, 1, 3).reshape(NA_HEADS, GRID_W, 3 * tb)
             for a in range(qr)]
    band = jnp.stack(slabs, axis=1).reshape(NA_HEADS, tb, 3 * tb)
    return jnp.where(valid[:, None], band[None], NEG)


def na_attn(p_lat, p_ctx, rpb, rows):
    s = p_lat.shape[0]
    c = p_ctx.shape[0]
    w = NA_HEADS * NA_DIM
    tb = NA_QROWS * GRID_W
    nb = s // tb
    assert rows % NA_QROWS == 0 and nb >= 3 and rows >= 2 * NA_WIN_R
    bias = _na_bias(rpb, rows)

    def blk(col, off):
        return pl.BlockSpec((tb, w), lambda i: (jnp.clip(i + off, 0, nb - 1), col))

    return pl.pallas_call(
        _na_kernel,
        grid=(nb,),
        in_specs=[blk(0, 0), blk(1, -1), blk(1, 0), blk(1, 1), blk(2, -1), blk(2, 0), blk(2, 1),
                  pl.BlockSpec((c, w), lambda i: (0, 1)),
                  pl.BlockSpec((c, w), lambda i: (0, 2)),
                  pl.BlockSpec((1, NA_HEADS, tb, 3 * tb),
                               lambda i: (jnp.where(i == 0, 0, jnp.where(i == nb - 1, 2, 1)), 0, 0, 0))],
        out_specs=pl.BlockSpec((tb, w), lambda i: (i, 0)),
        out_shape=jax.ShapeDtypeStruct((s, w), CD),
        compiler_params=_cparams("parallel"),
        name="na_attn",
    )(p_lat, p_lat, p_lat, p_lat, p_lat, p_lat, p_lat, p_ctx, p_ctx, bias)


def _sw_kernel(seq, q_ref, kp_ref, ko_ref, kn_ref, vp_ref, vo_ref, vn_ref, kc_ref, vc_ref, sink_ref, o_ref):
    i = pl.program_id(0)
    q = q_ref[...]
    tq = q.shape[0]
    kcat = jnp.concatenate([kp_ref[...], ko_ref[...], kn_ref[...], kc_ref[...]], axis=0)
    vcat = jnp.concatenate([vp_ref[...], vo_ref[...], vn_ref[...], vc_ref[...]], axis=0)
    nkk = tq + 2 * SW_WINDOW
    r = lax.broadcasted_iota(jnp.int32, (tq, nkk), 0)
    cc = lax.broadcasted_iota(jnp.int32, (tq, nkk), 1)
    kpos = i * tq - SW_WINDOW + cc
    d = cc - r
    mask = (d >= 0) & (d <= 2 * SW_WINDOW) & (kpos >= 0) & (kpos < seq)
    ones = _ones_col(kcat.shape[0], SW_DIM)
    grp = SW_HEADS // SW_KV_HEADS
    for g in range(SW_KV_HEADS):
        gs = slice(g * SW_DIM, (g + 1) * SW_DIM)
        kg = kcat[:, gs]
        v_aug = jnp.concatenate([vcat[:, gs], ones], axis=1)
        for h in range(g * grp, (g + 1) * grp):
            s = _dot_nt(q[:, h * SW_DIM:(h + 1) * SW_DIM], kg)
            s_loc = jnp.where(mask, s[:, :nkk], NEG)
            s_ctx = s[:, nkk:]
            sk = sink_ref[0:1, h:h + 1] * LOG2E
            m = jnp.maximum(jnp.maximum(jnp.max(s_loc, axis=-1, keepdims=True),
                                        jnp.max(s_ctx, axis=-1, keepdims=True)), sk)
            p = jnp.concatenate([jnp.exp2(s_loc - m), jnp.exp2(s_ctx - m)], axis=1).astype(CD)
            acc = _dot(p, v_aug)
            l = acc[:, SW_DIM:SW_DIM + 1] + jnp.exp2(sk - m)
            o_ref[:, h * SW_DIM:(h + 1) * SW_DIM] = (acc[:, :SW_DIM] / l).astype(o_ref.dtype)


def sw_attn(p_lat, p_ctx, sink):
    s = p_lat.shape[0]
    c = p_ctx.shape[0]
    tq = SW_TQ
    assert s % tq == 0 and tq % SW_WINDOW == 0
    nq = s // tq
    per = tq // SW_WINDOW
    nwb = s // SW_WINDOW
    kcol = SW_HEADS * SW_DIM // LANE
    prev = lambda col: pl.BlockSpec((SW_WINDOW, LANE), lambda i: (jnp.maximum(i * per - 1, 0), col))
    own = lambda col: pl.BlockSpec((tq, LANE), lambda i: (i, col))
    nxt = lambda col: pl.BlockSpec((SW_WINDOW, LANE), lambda i: (jnp.minimum((i + 1) * per, nwb - 1), col))
    return pl.pallas_call(
        functools.partial(_sw_kernel, s),
        grid=(nq,),
        in_specs=[pl.BlockSpec((tq, SW_HEADS * SW_DIM), lambda i: (i, 0)),
                  prev(kcol), own(kcol), nxt(kcol), prev(kcol + 1), own(kcol + 1), nxt(kcol + 1),
                  pl.BlockSpec((c, LANE), lambda i: (0, kcol)),
                  pl.BlockSpec((c, LANE), lambda i: (0, kcol + 1)),
                  pl.BlockSpec(sink.shape, lambda i: (0, 0))],
        out_specs=pl.BlockSpec((tq, SW_HEADS * SW_DIM), lambda i: (i, 0)),
        out_shape=jax.ShapeDtypeStruct((s, SW_HEADS * SW_DIM), CD),
        compiler_params=_cparams("parallel"),
        name="sw_attn",
    )(p_lat, p_lat, p_lat, p_lat, p_lat, p_lat, p_lat, p_ctx, p_ctx, sink)


def _ml_prep_kernel(nt, x_ref, xp_ref, xn_ref, g_ref, w_ref, b_ref, q_out, k_out, g_out):
    i = pl.program_id(0)
    x = x_ref[...]
    tm = x.shape[0]
    row = lax.broadcasted_iota(jnp.int32, x.shape, 0)
    has_prev = jnp.where(i > 0, 1.0, 0.0)
    has_next = jnp.where(i < nt - 1, 1.0, 0.0)
    prev_row = xp_ref[7:8, :] * has_prev
    next_row = xn_ref[0:1, :] * has_next
    xm = jnp.where(row == 0, prev_row, pltpu.roll(x, 1, axis=0))
    xq = jnp.where(row == tm - 1, next_row, pltpu.roll(x, tm - 1, axis=0))
    y = w_ref[0:1, :] * xm + w_ref[1:2, :] * x + w_ref[2:3, :] * xq
    y = y * _sigmoid(y)
    q_out[...] = y[:, :ML_QKW].astype(CD)
    k_out[...] = (y[:, ML_QKW:] * (ML_QK ** -0.5)).astype(CD)
    g = g_ref[...] + b_ref[...]
    lane = lax.broadcasted_iota(jnp.int32, g.shape, 1)
    logsig = jnp.minimum(g, 0.0) - jnp.log(1.0 + jnp.exp(-jnp.abs(g)))
    g_out[...] = jnp.where(lane < 2 * ML_HEADS, g, logsig)


def ml_prep(p_ml, conv_w, gate_b):
    t = p_ml.shape[0]
    tm = _row_tile(t, 256)
    nt = t // tm
    wq = 2 * ML_QKW
    gcol = (2 * ML_QKW + 2 * ML_VW) // LANE
    w8 = jnp.zeros((8, wq), F32).at[:ML_CONV].set(conv_w.astype(F32))
    b = jnp.zeros((1, LANE), F32).at[0, :4 * ML_HEADS].set(gate_b.astype(F32).reshape(-1))
    h8 = tm // 8
    return pl.pallas_call(
        functools.partial(_ml_prep_kernel, nt),
        grid=(nt,),
        in_specs=[pl.BlockSpec((tm, wq), lambda i: (i, 0)),
                  pl.BlockSpec((8, wq), lambda i: (jnp.maximum(i * h8 - 1, 0), 0)),
                  pl.BlockSpec((8, wq), lambda i: (jnp.minimum((i + 1) * h8, t // 8 - 1), 0)),
                  pl.BlockSpec((tm, LANE), lambda i: (i, gcol)),
                  pl.BlockSpec((8, wq), lambda i: (0, 0)),
                  pl.BlockSpec((1, LANE), lambda i: (0, 0))],
        out_specs=[pl.BlockSpec((tm, ML_QKW), lambda i: (i, 0)),
                   pl.BlockSpec((tm, ML_QKW), lambda i: (i, 0)),
                   pl.BlockSpec((tm, LANE), lambda i: (i, 0))],
        out_shape=[jax.ShapeDtypeStruct((t, ML_QKW), CD), jax.ShapeDtypeStruct((t, ML_QKW), CD),
                   jax.ShapeDtypeStruct((t, LANE), F32)],
        compiler_params=_cparams("parallel"),
        name="ml_prep",
    )(p_ml, p_ml, p_ml, p_ml, w8, b)


def _ml_scan_kernel(qf_ref, ktf_ref, vf_ref, gcf_ref, grf_ref,
                    qb_ref, ktb_ref, vb_ref, gcb_ref, grb_ref,
                    c0_ref, m0_ref, hf_ref, hb_ref, c_ref, m_ref):
    @pl.when(pl.program_id(0) == 0)
    def _():
        c_ref[...] = c0_ref[...]
        m_ref[...] = m0_ref[...]

    ln = qf_ref.shape[0]
    r = lax.broadcasted_iota(jnp.int32, (ln, ln), 0)
    c = lax.broadcasted_iota(jnp.int32, (ln, ln), 1)
    lower = r >= c
    upper = c >= r
    lower_f = jnp.where(lower, 1.0, 0.0)
    upper_f = jnp.where(upper, 1.0, 0.0)
    ones_col = jnp.where(lax.broadcasted_iota(jnp.int32, (ln, LANE), 1) == 0, 1.0, 0.0).astype(CD)
    hi = lax.Precision.HIGHEST

    streams = ((qf_ref, ktf_ref, vf_ref, gcf_ref, grf_ref, hf_ref, lower, lower_f, upper_f, ln - 1),
               (qb_ref, ktb_ref, vb_ref, gcb_ref, grb_ref, hb_ref, upper, upper_f, lower_f, 0))
    for d, (q_ref, kt_ref, v_ref, gc_ref, gr_ref, h_ref, mask, tri_c, tri_r, last) in enumerate(streams):
        gcol = gc_ref[...]
        grow = gr_ref[...]
        bcol = jnp.dot(tri_c, gcol, precision=hi, preferred_element_type=F32)
        brow = jnp.dot(grow, tri_r, precision=hi, preferred_element_type=F32)
        q = q_ref[...]
        kt = kt_ref[...]
        v = v_ref[...]
        for hd in range(ML_HEADS):
            ch = d * ML_HEADS + hd
            ic = d * ML_HEADS + hd
            fc = 2 * ML_HEADS + ic
            i_c = gcol[:, ic:ic + 1]
            b_c = bcol[:, fc:fc + 1]
            i_r = grow[ic:ic + 1, :]
            b_r = brow[fc:fc + 1, :]
            m_prev = m_ref[ch, 0:1, 0:1]
            cst = c_ref[ch]
            dmat = jnp.where(mask, b_c - b_r + i_r, NEG)
            inter = b_c + m_prev
            m_t = jnp.maximum(inter, jnp.max(dmat, axis=-1, keepdims=True))
            qh = q[:, hd * ML_QK:(hd + 1) * ML_QK]
            kth = kt[hd * ML_QK:(hd + 1) * ML_QK, :]
            s = lax.dot_general(qh, kth, (((1,), (0,)), ((), ())), preferred_element_type=F32)
            s = s * jnp.exp(dmat - m_t)
            w_inter = jnp.exp(inter - m_t)
            v_aug = jnp.concatenate([v[:, hd * ML_V:(hd + 1) * ML_V].astype(CD), ones_col], axis=1)
            num = _dot(s.astype(CD), v_aug) + w_inter * _dot(qh, cst.astype(CD))
            den = num[:, ML_V:ML_V + 1]
            h = num[:, :ML_V] / jnp.maximum(jnp.abs(den), jnp.exp(-m_t))
            h_ref[:, hd * ML_V:(hd + 1) * ML_V] = h
            b_end = b_c[last:last + 1, :]
            m_new = jnp.maximum(b_end + m_prev, jnp.max(b_end - b_c + i_c, axis=0, keepdims=True))
            w_s = jnp.exp(b_end - b_r + i_r - m_new)
            w_c = jnp.exp(b_end + m_prev - m_new)
            ktw = (kth.astype(F32) * w_s).astype(CD)
            c_ref[ch] = w_c * cst + _dot(ktw, v_aug)
            m_ref[ch] = jnp.broadcast_to(m_new, m_ref.shape[1:])


def ml_scan(qc, kc, p_ml, gates, state):
    t = qc.shape[0]
    ln = min(ML_CHUNK, t)
    assert t % ln == 0
    nc = t // ln
    kt = kc.T
    gr = gates[:, :4 * ML_HEADS].T
    c0, m0 = state
    vcol = 2 * ML_QKW // ML_VW
    fwd = lambda j: j
    bwd = lambda j: nc - 1 - j

    def specs(ix):
        return [pl.BlockSpec((ln, ML_QKW), lambda j: (ix(j), 0)),
                pl.BlockSpec((ML_QKW, ln), lambda j: (0, ix(j))),
                pl.BlockSpec((ln, ML_VW), lambda j: (ix(j), vcol)),
                pl.BlockSpec((ln, LANE), lambda j: (ix(j), 0)),
                pl.BlockSpec((4 * ML_HEADS, ln), lambda j: (0, ix(j)))]

    st_specs = [pl.BlockSpec(c0.shape, lambda j: (0, 0, 0)), pl.BlockSpec(m0.shape, lambda j: (0, 0, 0))]
    return pl.pallas_call(
        _ml_scan_kernel,
        grid=(nc,),
        in_specs=specs(fwd) + specs(bwd) + st_specs,
        out_specs=[pl.BlockSpec((ln, ML_VW), lambda j: (fwd(j), 0)),
                   pl.BlockSpec((ln, ML_VW), lambda j: (bwd(j), 0))] + st_specs,
        out_shape=[jax.ShapeDtypeStruct((t, ML_VW), F32), jax.ShapeDtypeStruct((t, ML_VW), F32),
                   jax.ShapeDtypeStruct(c0.shape, F32), jax.ShapeDtypeStruct(m0.shape, F32)],
        compiler_params=_cparams("arbitrary"),
        name="ml_scan",
    )(qc, kt, p_ml, gates, gr, qc, kt, p_ml, gates, gr, c0, m0)


def _ml_finish_kernel(hf_ref, hb_ref, o_ref, g_ref, y_ref):
    h = hf_ref[...] + hb_ref[...]
    og = _sigmoid(o_ref[...])
    g = g_ref[...]
    for hd in range(ML_HEADS):
        sl = slice(hd * ML_V, (hd + 1) * ML_V)
        hh = h[:, sl]
        y = hh * lax.rsqrt(jnp.mean(hh * hh, axis=-1, keepdims=True) + EPS) * g[:, sl]
        y_ref[:, sl] = (y * og[:, sl]).astype(y_ref.dtype)


def ml_finish(hf, hb, p_ml, norm_g):
    t = hf.shape[0]
    tm = _row_tile(t, 256)
    ocol = (2 * ML_QKW + ML_VW) // ML_VW
    blk = pl.BlockSpec((tm, ML_VW), lambda i: (i, 0))
    return pl.pallas_call(
        _ml_finish_kernel,
        grid=(t // tm,),
        in_specs=[blk, blk, pl.BlockSpec((tm, ML_VW), lambda i: (i, ocol)),
                  pl.BlockSpec((1, ML_VW), lambda i: (0, 0))],
        out_specs=blk,
        out_shape=jax.ShapeDtypeStruct((t, ML_VW), CD),
        compiler_params=_cparams("parallel"),
        name="ml_finish",
    )(hf, hb, p_ml, norm_g.reshape(1, ML_VW).astype(F32))


def _merge_kernel(h_ref, ya_ref, yb_ref, yc_ref, yd_ref, g0_ref, g1_ref, g2_ref, g3_ref, wb_ref, o_ref):
    h = h_ref[...]
    acc = None
    for i, (y_ref, g_ref) in enumerate(zip((ya_ref, yb_ref, yc_ref, yd_ref), (g0_ref, g1_ref, g2_ref, g3_ref))):
        term = _sigmoid(_dot(h, g_ref[...])) * _dot(y_ref[...], wb_ref[i])
        acc = term if acc is None else acc + term
    o_ref[...] = acc.astype(o_ref.dtype)


def gated_merge(h, ys, w_gate, w_branch, tn=512):
    t, d = h.shape
    tm = _row_tile(t)
    nj = d // tn
    yspec = pl.BlockSpec((tm, BRANCH_W), lambda j, i: (i, 0))
    gspec = lambda b: pl.BlockSpec((d, tn), lambda j, i: (0, b * nj + j))
    return pl.pallas_call(
        _merge_kernel,
        grid=(nj, t // tm),
        in_specs=[pl.BlockSpec((tm, d), lambda j, i: (i, 0)), yspec, yspec, yspec, yspec,
                  gspec(0), gspec(1), gspec(2), gspec(3),
                  pl.BlockSpec((N_BRANCH, BRANCH_W, tn), lambda j, i: (0, 0, j))],
        out_specs=pl.BlockSpec((tm, tn), lambda j, i: (i, j)),
        out_shape=jax.ShapeDtypeStruct((t, d), CD),
        compiler_params=_cparams("parallel", "parallel"),
        name="gated_merge",
    )(h, *ys, w_gate, w_gate, w_gate, w_gate, w_branch)


def _ffn1_kernel(h_ref, wa_ref, wg_ref, o_ref, wa_sc, wg_sc):
    @pl.when(pl.program_id(1) == 0)
    def _():
        wa_sc[...] = wa_ref[...].astype(CD)
        wg_sc[...] = wg_ref[...].astype(CD)

    h = h_ref[...]
    a = _dot(h, wa_sc[...])
    g = _dot(h, wg_sc[...])
    o_ref[...] = (a * _sigmoid(a) * g).astype(o_ref.dtype)


def ffn_up(h, w, tn=512):
    t, d = h.shape
    dff = w.shape[1] // 2
    tm = _row_tile(t)
    nj = dff // tn
    return pl.pallas_call(
        _ffn1_kernel,
        grid=(nj, t // tm),
        in_specs=[pl.BlockSpec((tm, d), lambda j, i: (i, 0)),
                  pl.BlockSpec((d, tn), lambda j, i: (0, j)),
                  pl.BlockSpec((d, tn), lambda j, i: (0, nj + j))],
        out_specs=pl.BlockSpec((tm, tn), lambda j, i: (i, j)),
        out_shape=jax.ShapeDtypeStruct((t, dff), CD),
        scratch_shapes=[pltpu.VMEM((d, tn), CD), pltpu.VMEM((d, tn), CD)],
        compiler_params=_cparams("parallel", "arbitrary"),
        name="ffn_up",
    )(h, w, w)


def _rot_cols(w, dim):
    k, n = w.shape
    w4 = w.reshape(k, n // dim, 2, dim // 2)
    return jnp.stack([-w4[:, :, 1], w4[:, :, 0]], axis=2).reshape(k, n)


def _rope_tables(n_tokens, dim):
    a = dim // 2
    inv = 1.0 / (ROPE_THETA ** (jnp.arange(0, a, 2, dtype=F32) / a))
    t = jnp.arange(n_tokens)
    row = (t // GRID_W).astype(F32)
    col = (t % GRID_W).astype(F32)
    ang = jnp.concatenate([row[:, None] * inv, col[:, None] * inv], axis=-1)
    return jnp.cos(ang), jnp.sin(ang)


def _layer_weights(w_in_l, mla_w_uq_l, mla_w_ukv_l):
    d = w_in_l.shape[0]
    o_na = ML_COLS
    o_sw = o_na + NA_COLS
    o_mla = o_sw + SW_COLS
    o_gate = o_mla + MLA_COLS
    w = {}
    w["ml"] = jnp.pad(w_in_l[:, :ML_COLS], ((0, 0), (0, ML_PAD - ML_COLS))).astype(CD)
    na = w_in_l[:, o_na:o_sw]
    nq = NA_HEADS * NA_DIM
    w["na"] = jnp.concatenate([na[:, :nq] * (NA_DIM ** -0.5 * LOG2E), na[:, nq:]], axis=1).astype(CD)
    sq = SW_HEADS * SW_DIM
    skv = SW_KV_HEADS * SW_DIM
    sw_q = w_in_l[:, o_sw:o_sw + sq] * (SW_DIM ** -0.5 * LOG2E)
    sw_k = w_in_l[:, o_sw + sq:o_sw + sq + skv]
    sw_v = w_in_l[:, o_sw + sq + skv:o_mla]
    w["sw"] = jnp.concatenate([sw_q, sw_k, sw_v], axis=1).astype(CD)
    w["sw_rot"] = jnp.concatenate([_rot_cols(sw_q, SW_DIM), _rot_cols(sw_k, SW_DIM)], axis=1).astype(CD)
    m_cq = w_in_l[:, o_mla:o_mla + MLA_Q_RANK + MLA_KV_RANK]
    m_kr = w_in_l[:, o_mla + MLA_Q_RANK + MLA_KV_RANK:o_gate]
    z = jnp.zeros((d, LANE - MLA_ROPE), F32)
    w["mla1"] = jnp.concatenate([m_cq, m_kr, z, _rot_cols(m_kr, MLA_ROPE), z], axis=1).astype(CD)
    uq = mla_w_uq_l.reshape(MLA_Q_RANK, MLA_HEADS, MLA_NOPE + MLA_ROPE)
    zq = jnp.zeros((MLA_Q_RANK, MLA_HEADS, MLA_HW - MLA_NOPE - MLA_ROPE), F32)
    w["mla_q"] = jnp.concatenate([uq, zq], axis=2).reshape(MLA_Q_RANK, -1).astype(CD)
    uqr = _rot_cols(uq[:, :, MLA_NOPE:].reshape(MLA_Q_RANK, -1), MLA_ROPE).reshape(MLA_Q_RANK, MLA_HEADS, MLA_ROPE)
    w["mla_q_rot"] = jnp.concatenate([jnp.zeros_like(uq[:, :, :MLA_NOPE]), uqr, zq], axis=2
                                     ).reshape(MLA_Q_RANK, -1).astype(CD)
    w["mla_kv"] = mla_w_ukv_l.astype(CD)
    w["gate"] = w_in_l[:, o_gate:].astype(CD)
    return w


def _tables(s, c):
    cos_sw, sin_sw = _rope_tables(s, SW_DIM)
    cos_m, sin_m = _rope_tables(s, MLA_ROPE)
    tile2 = lambda a: jnp.concatenate([a, a, a, a], axis=1)
    scale = (MLA_NOPE + MLA_ROPE) ** -0.5 * LOG2E
    zpad = jnp.zeros((s, MLA_HW - MLA_NOPE - MLA_ROPE), F32)
    lat = dict(
        sw_cos=tile2(cos_sw), sw_sin=tile2(sin_sw),
        q_cos=scale * jnp.concatenate([jnp.ones((s, MLA_NOPE), F32), cos_m, cos_m, zpad], axis=1),
        q_sin=scale * jnp.concatenate([jnp.zeros((s, MLA_NOPE), F32), sin_m, sin_m, zpad], axis=1),
        k_cos=jnp.concatenate([cos_m, cos_m, zpad], axis=1),
        k_sin=jnp.concatenate([sin_m, sin_m, zpad], axis=1))
    ctx = dict(
        sw_cos=jnp.ones((c, LANE), F32), sw_sin=jnp.zeros((c, LANE), F32),
        q_cos=jnp.full((c, MLA_HW), scale, F32), q_sin=jnp.zeros((c, MLA_HW), F32),
        k_cos=jnp.ones((c, LANE), F32), k_sin=jnp.zeros((c, LANE), F32))
    return lat, ctx


def _project(h, w, tab, gq, gkv):
    p_ml = matmul(h, w["ml"], F32)
    p_na = matmul(h, w["na"], CD)
    n_rope = (SW_HEADS + SW_KV_HEADS) * SW_DIM // LANE
    p_sw = rope_matmul(h, w["sw"], w["sw_rot"], tab["sw_cos"], tab["sw_sin"], n_rope)
    mq, mk, mv = mla_proj(h, w["mla1"], gq, gkv, w["mla_q"], w["mla_q_rot"], w["mla_kv"],
                          tab["q_cos"], tab["q_sin"], tab["k_cos"], tab["k_sin"])
    return p_ml, p_na, p_sw, (mq, mk, mv)


def _dense_tail(x, h, ys, w, w_branch_l, w_out_l, w_ffn_in_l, w_ffn_out_l, vt, g1, norm2, g2, norm_next,
                last):
    merged = gated_merge(h, ys, w["gate"], w_branch_l)
    x, h2 = matmul_residual_norm(merged, w_out_l, x, vt, g1, norm2, True, CD)
    u = ffn_up(h2, w_ffn_in_l)
    tk = w_ffn_out_l.shape[0] // 4
    if last:
        return None, matmul_residual_norm(u, w_ffn_out_l, x, vt, g2, norm_next, False, F32, tk=tk)
    return matmul_residual_norm(u, w_ffn_out_l, x, vt, g2, norm_next, True, CD, tk=tk)


def kernel(x, c, ctx, c_ctx, w_ada, b_ada, norm1_g, norm2_g, w_in, ml_conv_w, ml_gate_b, ml_norm_g, na_rpb,
           sw_sink, mla_q_norm_g, mla_kv_norm_g, mla_w_uq, mla_w_ukv, w_branch, w_out, w_ffn_in, w_ffn_out,
           final_norm_g):
    bsz, s, d = x.shape
    assert bsz == 1
    depth = w_in.shape[0]
    n_ctx = ctx.shape[1]
    rows = s // GRID_W
    xl = x[0].astype(F32)
    xc = ctx[0].astype(F32)
    mod = adaln(jnp.stack([c[0], c_ctx], axis=1).astype(F32), w_ada, b_ada)
    vt = jnp.concatenate([mod.reshape(depth * 12, d), norm1_g.astype(F32), norm2_g.astype(F32),
                          final_norm_g.reshape(1, d).astype(F32), jnp.zeros((1, d), F32)]).reshape(-1, 1, d)
    mod_row = lambda l, stream, part: (l * 2 + stream) * 6 + part
    n1_row = lambda l: depth * 12 + l
    n2_row = lambda l: depth * 13 + l
    fin_row, zero_row = depth * 14, depth * 14 + 1
    norm1 = lambda l, st: (n1_row(l), mod_row(l, st, 0), mod_row(l, st, 1))
    norm2 = lambda l, st: (n2_row(l), mod_row(l, st, 3), mod_row(l, st, 4))
    tab_l, tab_c = _tables(s, n_ctx)
    zero_state = (jnp.zeros((2 * ML_HEADS, ML_QK, 2 * ML_V), F32), jnp.zeros((2 * ML_HEADS, 8, LANE), F32))

    hl = norm_mod(xl, vt, norm1(0, 0), CD)
    hc = norm_mod(xc, vt, norm1(0, 1), CD)
    for l in range(depth):
        last = l == depth - 1
        w = _layer_weights(w_in[l], mla_w_uq[l], mla_w_ukv[l])
        wb = w_branch[l].astype(CD)
        wo = w_out[l].astype(CD)
        wf2 = w_ffn_out[l].astype(CD)
        gq = mla_q_norm_g[l].reshape(1, -1).astype(F32)
        gkv = mla_kv_norm_g[l].reshape(1, -1).astype(F32)
        sink = sw_sink[l].reshape(1, -1).astype(F32)

        ml_l, na_l, sw_l, (mq_l, mk_l, mv_l) = _project(hl, w, tab_l, gq, gkv)
        ml_c, na_c, sw_c, (mq_c, mk_c, mv_c) = _project(hc, w, tab_c, gq, gkv)

        qc_c, kc_c, gt_c = ml_prep(ml_c, ml_conv_w[l], ml_gate_b[l])
        qc_l, kc_l, gt_l = ml_prep(ml_l, ml_conv_w[l], ml_gate_b[l])
        hf_c, hb_c, cst, mst = ml_scan(qc_c, kc_c, ml_c, gt_c, zero_state)
        hf_l, hb_l, _, _ = ml_scan(qc_l, kc_l, ml_l, gt_l, (cst, mst))
        ya_l = ml_finish(hf_l, hb_l, ml_l, ml_norm_g[l])
        yb_l = na_attn(na_l, na_c, na_rpb[l], rows)
        yc_l = sw_attn(sw_l, sw_c, sink)
        yd_l = mla_flash(mq_l, jnp.concatenate([mk_l, mk_c]), jnp.concatenate([mv_l, mv_c]))

        nxt_l = (fin_row, zero_row, zero_row) if last else norm1(l + 1, 0)
        xl_new, hl_new = _dense_tail(xl, hl, (ya_l, yb_l, yc_l, yd_l), w, wb, wo, w_ffn_in[l], wf2, vt,
                                     mod_row(l, 0, 2), norm2(l, 0), mod_row(l, 0, 5), nxt_l, last)
        if not last:
            ya_c = ml_finish(hf_c, hb_c, ml_c, ml_norm_g[l])
            nw = NA_HEADS * NA_DIM
            yb_c = dense_attn([(na_c, nw, 0), (na_c, nw, 1), (na_c, nw, 2)], NA_HEADS, 1, NA_DIM, NA_DIM,
                              log2_scores=True)
            kcol = SW_HEADS * SW_DIM // LANE
            yc_c = dense_attn([(sw_c, SW_HEADS * SW_DIM, 0), (sw_c, LANE, kcol), (sw_c, LANE, kcol + 1)],
                              SW_HEADS, SW_HEADS // SW_KV_HEADS, SW_DIM, SW_DIM, sink=sink, log2_scores=True)
            yd_c = dense_attn([(mq_c, MLA_HEADS * MLA_HW, 0), (mk_c, MLA_HEADS * MLA_HW, 0),
                               (mv_c, MLA_HEADS * MLA_V, 0)], MLA_HEADS, 1, MLA_HW, MLA_V, log2_scores=True)
            xc, hc = _dense_tail(xc, hc, (ya_c, yb_c, yc_c, yd_c), w, wb, wo, w_ffn_in[l], wf2, vt,
                                 mod_row(l, 1, 2), norm2(l, 1), mod_row(l, 1, 5), norm1(l + 1, 1), False)
        xl, hl = xl_new, hl_new

    return hl[None].astype(x.dtype)
```

```python
import functools

import numpy as np
import jax
import jax.numpy as jnp
from jax import lax
from jax.experimental import pallas as pl
from jax.experimental.pallas import tpu as pltpu

F32 = jnp.float32
CD = jnp.bfloat16

GRID_W = 64
EPS = 1e-6
ROPE_THETA = 10000.0
ML_HEADS, ML_QK, ML_V, ML_CONV = 4, 64, 128, 3
NA_HEADS, NA_DIM, NA_WIN_R, NA_WIN_C = 8, 64, 8, 16
SW_HEADS, SW_KV_HEADS, SW_DIM, SW_WINDOW = 8, 2, 64, 128
MLA_HEADS, MLA_Q_RANK, MLA_KV_RANK, MLA_NOPE, MLA_ROPE, MLA_V = 4, 384, 256, 128, 64, 128
N_BRANCH, BRANCH_W = 4, 512

ML_QKW = ML_HEADS * ML_QK
ML_VW = ML_HEADS * ML_V
ML_COLS = 2 * ML_QKW + 2 * ML_VW + 4 * ML_HEADS
ML_PAD = 1664
NA_COLS = 3 * NA_HEADS * NA_DIM
SW_COLS = (SW_HEADS + 2 * SW_KV_HEADS) * SW_DIM
MLA_COLS = MLA_Q_RANK + MLA_KV_RANK + MLA_ROPE
MLA_HW = 256
LANE = 128
NEG = -1e30
LOG2E = float(np.log2(np.e))
VMEM_LIMIT = 56 * 1024 * 1024

NA_QROWS = 4
SW_TQ = 256
ML_CHUNK = 256
MLA_TK = 1280


def _cparams(*sem):
    return pltpu.CompilerParams(dimension_semantics=sem, vmem_limit_bytes=VMEM_LIMIT)


def _row_tile(t, pref=512):
    tm = min(pref, t)
    assert t % tm == 0
    return tm


def _dot(a, b):
    return jnp.dot(a, b, preferred_element_type=F32)


def _dot_nt(a, b):
    return lax.dot_general(a, b, (((1,), (1,)), ((), ())), preferred_element_type=F32)


def _sigmoid(x):
    return 1.0 / (1.0 + jnp.exp(-x))


def _ones_col(n, w):
    return jnp.where(lax.broadcasted_iota(jnp.int32, (n, w), 1) == 0, 1.0, 0.0).astype(CD)


def _adaln_kernel(c_ref, w_ref, b_ref, o_ref):
    w = w_ref[0]
    for r in range(2):
        c = c_ref[:, r:r + 1]
        o_ref[0, r:r + 1, :] = jnp.sum((c * _sigmoid(c)) * w, axis=0, keepdims=True) + b_ref[0]


def adaln(cc, w_ada, b_ada):
    nl, d, n = w_ada.shape
    tn = 1024
    return pl.pallas_call(
        _adaln_kernel,
        grid=(nl, n // tn),
        in_specs=[pl.BlockSpec((d, 2), lambda l, j: (0, 0)),
                  pl.BlockSpec((1, d, tn), lambda l, j: (l, 0, j)),
                  pl.BlockSpec((1, 1, tn), lambda l, j: (l, 0, j))],
        out_specs=pl.BlockSpec((1, 2, tn), lambda l, j: (l, 0, j)),
        out_shape=jax.ShapeDtypeStruct((nl, 2, n), F32),
        compiler_params=_cparams("parallel", "parallel"),
        name="adaln",
    )(cc, w_ada, b_ada.reshape(nl, 1, n))


def _vec_spec(d, idx):
    return pl.BlockSpec((1, 1, d), lambda *_: (idx, 0, 0))


def _norm_mod(x, g, shift, scale):
    y = x * lax.rsqrt(jnp.mean(x * x, axis=-1, keepdims=True) + EPS)
    return (y * g) * (1.0 + scale) + shift


def _norm_mod_kernel(x_ref, g_ref, sh_ref, sc_ref, o_ref):
    o_ref[...] = _norm_mod(x_ref[...], g_ref[0], sh_ref[0], sc_ref[0]).astype(o_ref.dtype)


def norm_mod(x, vt, norm_idx, out_dtype):
    t, d = x.shape
    tm = _row_tile(t, 256)
    return pl.pallas_call(
        _norm_mod_kernel,
        grid=(t // tm,),
        in_specs=[pl.BlockSpec((tm, d), lambda i: (i, 0))] + [_vec_spec(d, ix) for ix in norm_idx],
        out_specs=pl.BlockSpec((tm, d), lambda i: (i, 0)),
        out_shape=jax.ShapeDtypeStruct((t, d), out_dtype),
        compiler_params=_cparams("parallel"),
        name="norm_mod",
    )(x, vt, vt, vt)


def _mm_kernel(a_ref, b_ref, o_ref):
    o_ref[...] = _dot(a_ref[...], b_ref[...]).astype(o_ref.dtype)


def matmul(a, b, out_dtype, tn=None):
    t, k = a.shape
    n = b.shape[1]
    tn = n if tn is None else tn
    tm = _row_tile(t)
    return pl.pallas_call(
        _mm_kernel,
        grid=(n // tn, t // tm),
        in_specs=[pl.BlockSpec((tm, k), lambda j, i: (i, 0)),
                  pl.BlockSpec((k, tn), lambda j, i: (0, j))],
        out_specs=pl.BlockSpec((tm, tn), lambda j, i: (i, j)),
        out_shape=jax.ShapeDtypeStruct((t, n), out_dtype),
        compiler_params=_cparams("parallel", "parallel"),
        name="matmul",
    )(a, b)


def _mm_res_norm_kernel(nk, want_x, a_ref, b_ref, x_ref, gate_ref, g_ref, sh_ref, sc_ref, *rest):
    if want_x:
        xo_ref, ho_ref = rest[:2]
    else:
        xo_ref, ho_ref = None, rest[0]
    acc_ref = rest[-1] if nk > 1 else None
    k = pl.program_id(1)
    part = _dot(a_ref[...], b_ref[0])

    def finish(acc):
        x = x_ref[...] + gate_ref[0] * acc
        if want_x:
            xo_ref[...] = x
        ho_ref[...] = _norm_mod(x, g_ref[0], sh_ref[0], sc_ref[0]).astype(ho_ref.dtype)

    if nk == 1:
        finish(part)
        return

    @pl.when(k == 0)
    def _():
        acc_ref[...] = part

    @pl.when((k > 0) & (k < nk - 1))
    def _():
        acc_ref[...] += part

    @pl.when(k == nk - 1)
    def _():
        finish(acc_ref[...] + part)


def matmul_residual_norm(a, b, l, x, vt, gate_idx, norm_idx, want_x, h_dtype, tk=None):
    t, kdim = a.shape
    n = b.shape[2]
    tk = kdim if tk is None else tk
    nk = kdim // tk
    assert kdim % tk == 0
    tm = _row_tile(t)
    row = pl.BlockSpec((tm, n), lambda i, k: (i, 0))
    out_shape = [jax.ShapeDtypeStruct((t, n), h_dtype)]
    if want_x:
        out_shape = [jax.ShapeDtypeStruct((t, n), F32)] + out_shape
    out = pl.pallas_call(
        functools.partial(_mm_res_norm_kernel, nk, want_x),
        grid=(t // tm, nk),
        in_specs=[pl.BlockSpec((tm, tk), lambda i, k: (i, k)),
                  pl.BlockSpec((1, tk, n), lambda i, k: (l, k, 0)),
                  row, _vec_spec(n, gate_idx)] + [_vec_spec(n, ix) for ix in norm_idx],
        out_specs=[row] * len(out_shape),
        out_shape=out_shape,
        scratch_shapes=[pltpu.VMEM((tm, n), F32)] if nk > 1 else [],
        compiler_params=_cparams("parallel", "arbitrary"),
        name="matmul_residual_norm",
    )(a, b, x, vt, vt, vt, vt)
    return out if want_x else out[0]


def _rope_mm_kernel(n_rope, a_ref, w_ref, wr_ref, cos_ref, sin_ref, o_ref):
    a = a_ref[...]
    p = _dot(a, w_ref[...])
    pr = _dot(a, wr_ref[...])
    cos = cos_ref[...]
    sin = sin_ref[...]
    for j in range(p.shape[1] // LANE):
        sl = slice(j * LANE, (j + 1) * LANE)
        if j < n_rope:
            o_ref[:, sl] = (p[:, sl] * cos + pr[:, sl] * sin).astype(o_ref.dtype)
        else:
            o_ref[:, sl] = p[:, sl].astype(o_ref.dtype)


def rope_matmul(a, w, w_rot, cos, sin, n_rope):
    t, k = a.shape
    n = w.shape[1]
    nr = n_rope * LANE
    tm = _row_tile(t)
    return pl.pallas_call(
        functools.partial(_rope_mm_kernel, n_rope),
        grid=(t // tm,),
        in_specs=[pl.BlockSpec((tm, k), lambda i: (i, 0)),
                  pl.BlockSpec((k, n), lambda i: (0, 0)),
                  pl.BlockSpec((k, nr), lambda i: (0, 0)),
                  pl.BlockSpec((tm, LANE), lambda i: (i, 0)),
                  pl.BlockSpec((tm, LANE), lambda i: (i, 0))],
        out_specs=pl.BlockSpec((tm, n), lambda i: (i, 0)),
        out_shape=jax.ShapeDtypeStruct((t, n), CD),
        compiler_params=_cparams("parallel"),
        name="rope_matmul",
    )(a, w, w_rot, cos, sin)


def _mla_proj_kernel(h_ref, w1_ref, gq_ref, gkv_ref, wq_ref, wqr_ref, wkv_ref,
                     cq_ref, sq_ref, ck_ref, sk_ref, q_out, k_out, v_out):
    p = _dot(h_ref[...], w1_ref[...])
    cq = p[:, :MLA_Q_RANK]
    ckv = p[:, MLA_Q_RANK:MLA_Q_RANK + MLA_KV_RANK]
    o = MLA_Q_RANK + MLA_KV_RANK
    kr = p[:, o:o + LANE]
    krr = p[:, o + LANE:o + 2 * LANE]
    cqn = (cq * lax.rsqrt(jnp.mean(cq * cq, axis=-1, keepdims=True) + EPS) * gq_ref[...]).astype(CD)
    ckvn = (ckv * lax.rsqrt(jnp.mean(ckv * ckv, axis=-1, keepdims=True) + EPS) * gkv_ref[...]).astype(CD)
    qa = _dot(cqn, wq_ref[...])
    qb = _dot(cqn, wqr_ref[...])
    kv = _dot(ckvn, wkv_ref[...])
    krp = (kr * ck_ref[...] + krr * sk_ref[...]).astype(CD)
    cq_t = cq_ref[...]
    sq_t = sq_ref[...]
    for h in range(MLA_HEADS):
        qs = slice(h * MLA_HW, (h + 1) * MLA_HW)
        q_out[:, qs] = (qa[:, qs] * cq_t + qb[:, qs] * sq_t).astype(CD)
        k_out[:, h * MLA_HW:h * MLA_HW + MLA_NOPE] = kv[:, h * 256:h * 256 + MLA_NOPE].astype(CD)
        k_out[:, h * MLA_HW + MLA_NOPE:(h + 1) * MLA_HW] = krp
        v_out[:, h * MLA_V:(h + 1) * MLA_V] = kv[:, h * 256 + MLA_NOPE:(h + 1) * 256].astype(CD)


def mla_proj(h, w1, gq, gkv, wq, wqr, wkv, cos_q, sin_q, cos_k, sin_k):
    t, d = h.shape
    tm = _row_tile(t)
    full = lambda a: pl.BlockSpec(a.shape, lambda i: (0,) * a.ndim)
    rows = lambda w: pl.BlockSpec((tm, w), lambda i: (i, 0))
    hw = MLA_HEADS * MLA_HW
    return pl.pallas_call(
        _mla_proj_kernel,
        grid=(t // tm,),
        in_specs=[rows(d), full(w1), full(gq), full(gkv), full(wq), full(wqr), full(wkv),
                  rows(MLA_HW), rows(MLA_HW), rows(LANE), rows(LANE)],
        out_specs=[rows(hw), rows(hw), rows(MLA_HEADS * MLA_V)],
        out_shape=[jax.ShapeDtypeStruct((t, hw), CD), jax.ShapeDtypeStruct((t, hw), CD),
                   jax.ShapeDtypeStruct((t, MLA_HEADS * MLA_V), CD)],
        compiler_params=_cparams("parallel"),
        name="mla_proj",
    )(h, w1, gq, gkv, wq, wqr, wkv, cos_q, sin_q, cos_k, sin_k)


def _flash_kernel(tq, tk, unroll, q_ref, k_ref, v_ref, o_ref, s_sc, m_sc, acc_sc):
    nkb = k_ref.shape[0] // tk
    nsub = q_ref.shape[0] // tq
    total = nsub * nkb
    m_sc[...] = jnp.full(m_sc.shape, NEG, F32)
    acc_sc[...] = jnp.zeros(acc_sc.shape, F32)
    ones_col = jnp.where(lax.broadcasted_iota(jnp.int32, (tk, LANE), 1) == 0, 1.0, 0.0).astype(CD)

    def split(b):
        sub = b // nkb
        return sub, b - sub * nkb

    def scores(b):
        sub, j = split(b)
        q = q_ref[pl.ds(pl.multiple_of(sub * tq, tq), tq), :]
        k = k_ref[pl.ds(pl.multiple_of(j * tk, tk), tk), :]
        return _dot_nt(q, k)

    def softmax_pv(s, b):
        sub, j = split(b)
        v = v_ref[pl.ds(pl.multiple_of(j * tk, tk), tk), :]
        m_prev = m_sc[sub]
        m_new = jnp.maximum(m_prev, jnp.max(s, axis=-1, keepdims=True))
        alpha = jnp.exp2(m_prev - m_new)
        p = jnp.exp2(s - m_new).astype(CD)
        v_aug = jnp.concatenate([v, ones_col], axis=1)
        acc_sc[sub] = alpha * acc_sc[sub] + _dot(p, v_aug)
        m_sc[sub] = m_new

    s_sc[0] = scores(0)

    def body(g, carry):
        b0 = g * unroll
        for u in range(unroll):
            s_sc[(u + 1) % 2] = scores(jnp.minimum(b0 + u + 1, total - 1))
            softmax_pv(s_sc[u % 2], b0 + u)
        return carry

    lax.fori_loop(0, total // unroll, body, 0)
    for sub in range(nsub):
        acc = acc_sc[sub]
        o_ref[sub * tq:(sub + 1) * tq, :] = (acc[:, :MLA_V] / acc[:, MLA_V:MLA_V + 1]).astype(o_ref.dtype)


def mla_flash(q, k, v):
    s = q.shape[0]
    nk = k.shape[0]
    tq = _row_tile(s, 512)
    tstep = _row_tile(s, 4 * tq)
    tk = MLA_TK
    total = (tstep // tq) * (nk // tk)
    assert nk % tk == 0 and total % 2 == 0
    unroll = 4 if total % 4 == 0 else 2
    return pl.pallas_call(
        functools.partial(_flash_kernel, tq, tk, unroll),
        grid=(MLA_HEADS, s // tstep),
        in_specs=[pl.BlockSpec((tstep, MLA_HW), lambda h, i: (i, h)),
                  pl.BlockSpec((nk, MLA_HW), lambda h, i: (0, h)),
                  pl.BlockSpec((nk, MLA_V), lambda h, i: (0, h))],
        out_specs=pl.BlockSpec((tstep, MLA_V), lambda h, i: (i, h)),
        out_shape=jax.ShapeDtypeStruct((s, MLA_HEADS * MLA_V), CD),
        scratch_shapes=[pltpu.VMEM((2, tq, tk), F32), pltpu.VMEM((tstep // tq, tq, 1), F32),
                        pltpu.VMEM((tstep // tq, tq, 2 * MLA_V), F32)],
        compiler_params=_cparams("parallel", "arbitrary"),
        name="mla_flash",
    )(q, k, v)


def _dense_attn_kernel(n_heads, group, dqk, dv, log2_scores, q_ref, k_ref, v_ref, sink_ref, o_ref):
    q = q_ref[...]
    k = k_ref[...]
    v = v_ref[...]
    ex = jnp.exp2 if log2_scores else jnp.exp
    for h in range(n_heads):
        g = h // group
        s = _dot_nt(q[:, h * dqk:(h + 1) * dqk], k[:, g * dqk:(g + 1) * dqk])
        m = jnp.max(s, axis=-1, keepdims=True)
        l = jnp.zeros_like(m)
        if sink_ref is not None:
            sk = sink_ref[0:1, h:h + 1] * (LOG2E if log2_scores else 1.0)
            m = jnp.maximum(m, sk)
            l = ex(sk - m)
        p = ex(s - m)
        l = l + jnp.sum(p, axis=-1, keepdims=True)
        o = _dot(p.astype(CD), v[:, g * dv:(g + 1) * dv]) / l
        o_ref[:, h * dv:(h + 1) * dv] = o.astype(o_ref.dtype)


def dense_attn(qkv_specs, n_heads, group, dqk, dv, sink=None, log2_scores=False):
    c = qkv_specs[0][0].shape[0]
    arrays = [a for a, _, _ in qkv_specs]
    specs = [pl.BlockSpec((c, w), functools.partial(lambda b, i: (0, b), b)) for _, w, b in qkv_specs]
    if sink is None:
        kern = lambda q, k, v, o: _dense_attn_kernel(n_heads, group, dqk, dv, log2_scores, q, k, v, None, o)
    else:
        kern = functools.partial(_dense_attn_kernel, n_heads, group, dqk, dv, log2_scores)
        arrays.append(sink)
        specs.append(pl.BlockSpec(sink.shape, lambda i: (0, 0)))
    return pl.pallas_call(
        kern,
        grid=(1,),
        in_specs=specs,
        out_specs=pl.BlockSpec((c, n_heads * dv), lambda i: (0, 0)),
        out_shape=jax.ShapeDtypeStruct((c, n_heads * dv), CD),
        compiler_params=_cparams("arbitrary"),
        name="dense_attn",
    )(*arrays)


def _na_kernel(q_ref, kp_ref, ko_ref, kn_ref, vp_ref, vo_ref, vn_ref, kc_ref, vc_ref, bias_ref, o_ref):
    q = q_ref[...]
    kcat = jnp.concatenate([kp_ref[...], ko_ref[...], kn_ref[...], kc_ref[...]], axis=0)
    vcat = jnp.concatenate([vp_ref[...], vo_ref[...], vn_ref[...], vc_ref[...]], axis=0)
    nloc = 3 * q.shape[0]
    ones = _ones_col(kcat.shape[0], NA_DIM)
    for h in range(NA_HEADS):
        hs = slice(h * NA_DIM, (h + 1) * NA_DIM)
        s = _dot_nt(q[:, hs], kcat[:, hs])
        s_loc = s[:, :nloc] + bias_ref[0, h]
        s_ctx = s[:, nloc:]
        m = jnp.maximum(jnp.max(s_loc, axis=-1, keepdims=True), jnp.max(s_ctx, axis=-1, keepdims=True))
        p = jnp.concatenate([jnp.exp2(s_loc - m), jnp.exp2(s_ctx - m)], axis=1).astype(CD)
        acc = _dot(p, jnp.concatenate([vcat[:, hs], ones], axis=1))
        o_ref[:, hs] = (acc[:, :NA_DIM] / acc[:, NA_DIM:NA_DIM + 1]).astype(o_ref.dtype)


def _na_bias(rpb, rows):
    qr = NA_QROWS
    tb = qr * GRID_W
    nb = rows // qr
    nkr = 3 * qr
    rsel = np.zeros((3, qr, nkr, 2 * NA_WIN_R - 1), np.float32)
    for v, b in enumerate((0, 1, nb - 1)):
        for a in range(qr):
            r = qr * b + a
            rs = min(max(r - NA_WIN_R // 2, 0), rows - NA_WIN_R)
            for u in range(nkr):
                kr = qr * (b - 1) + u
                if 0 <= b - 1 + u // qr < nb and rs <= kr < rs + NA_WIN_R:
                    rsel[v, a, u, kr - r + NA_WIN_R - 1] = 1.0
    csel = np.zeros((GRID_W, GRID_W, 2 * NA_WIN_C - 1), np.float32)
    for qc in range(GRID_W):
        cs = min(max(qc - NA_WIN_C // 2, 0), GRID_W - NA_WIN_C)
        for kc in range(cs, cs + NA_WIN_C):
            csel[qc, kc, kc - qc + NA_WIN_C - 1] = 1.0
    valid = (np.einsum('vaud,qkj->vaquk', rsel, csel) > 0).reshape(3, tb, 3 * tb)
    assert 2 * qr == NA_WIN_R
    nd = 2 * NA_WIN_R - 1
    rp = rpb.astype(F32) * LOG2E
    gap = jnp.zeros((NA_HEADS, nd, GRID_W + 1 - (2 * NA_WIN_C - 1)), F32)
    ring = jnp.concatenate([rp[..., NA_WIN_C - 1:], gap, rp[..., :NA_WIN_C - 1]], axis=-1)
    cols = jnp.tile(ring, (1, 1, GRID_W))[..., :GRID_W * GRID_W].reshape(NA_HEADS, nd, GRID_W, GRID_W)
    off = NA_WIN_R - 1 - qr
    slabs = [cols[:, off - a:off - a + nkr].transpose(0, 2, 1, 3).reshape(NA_HEADS, GRID_W, 3 * tb)
             for a in range(qr)]
    band = jnp.stack(slabs, axis=1).reshape(NA_HEADS, tb, 3 * tb)
    return jnp.where(valid[:, None], band[None], NEG)


def na_attn(p_lat, p_ctx, rpb, rows):
    s = p_lat.shape[0]
    c = p_ctx.shape[0]
    w = NA_HEADS * NA_DIM
    tb = NA_QROWS * GRID_W
    nb = s // tb
    assert rows % NA_QROWS == 0 and nb >= 3 and rows >= 2 * NA_WIN_R
    bias = _na_bias(rpb, rows)

    def blk(col, off):
        return pl.BlockSpec((tb, w), lambda i: (jnp.clip(i + off, 0, nb - 1), col))

    return pl.pallas_call(
        _na_kernel,
        grid=(nb,),
        in_specs=[blk(0, 0), blk(1, -1), blk(1, 0), blk(1, 1), blk(2, -1), blk(2, 0), blk(2, 1),
                  pl.BlockSpec((c, w), lambda i: (0, 1)),
                  pl.BlockSpec((c, w), lambda i: (0, 2)),
                  pl.BlockSpec((1, NA_HEADS, tb, 3 * tb),
                               lambda i: (jnp.where(i == 0, 0, jnp.where(i == nb - 1, 2, 1)), 0, 0, 0))],
        out_specs=pl.BlockSpec((tb, w), lambda i: (i, 0)),
        out_shape=jax.ShapeDtypeStruct((s, w), CD),
        compiler_params=_cparams("parallel"),
        name="na_attn",
    )(p_lat, p_lat, p_lat, p_lat, p_lat, p_lat, p_lat, p_ctx, p_ctx, bias)


def _sw_kernel(seq, q_ref, kp_ref, ko_ref, kn_ref, vp_ref, vo_ref, vn_ref, kc_ref, vc_ref, sink_ref, o_ref):
    i = pl.program_id(0)
    q = q_ref[...]
    tq = q.shape[0]
    kcat = jnp.concatenate([kp_ref[...], ko_ref[...], kn_ref[...], kc_ref[...]], axis=0)
    vcat = jnp.concatenate([vp_ref[...], vo_ref[...], vn_ref[...], vc_ref[...]], axis=0)
    nkk = tq + 2 * SW_WINDOW
    r = lax.broadcasted_iota(jnp.int32, (tq, nkk), 0)
    cc = lax.broadcasted_iota(jnp.int32, (tq, nkk), 1)
    kpos = i * tq - SW_WINDOW + cc
    d = cc - r
    mask = (d >= 0) & (d <= 2 * SW_WINDOW) & (kpos >= 0) & (kpos < seq)
    ones = _ones_col(kcat.shape[0], SW_DIM)
    grp = SW_HEADS // SW_KV_HEADS
    for g in range(SW_KV_HEADS):
        gs = slice(g * SW_DIM, (g + 1) * SW_DIM)
        kg = kcat[:, gs]
        v_aug = jnp.concatenate([vcat[:, gs], ones], axis=1)
        for h in range(g * grp, (g + 1) * grp):
            s = _dot_nt(q[:, h * SW_DIM:(h + 1) * SW_DIM], kg)
            s_loc = jnp.where(mask, s[:, :nkk], NEG)
            s_ctx = s[:, nkk:]
            sk = sink_ref[0:1, h:h + 1] * LOG2E
            m = jnp.maximum(jnp.maximum(jnp.max(s_loc, axis=-1, keepdims=True),
                                        jnp.max(s_ctx, axis=-1, keepdims=True)), sk)
            p = jnp.concatenate([jnp.exp2(s_loc - m), jnp.exp2(s_ctx - m)], axis=1).astype(CD)
            acc = _dot(p, v_aug)
            l = acc[:, SW_DIM:SW_DIM + 1] + jnp.exp2(sk - m)
            o_ref[:, h * SW_DIM:(h + 1) * SW_DIM] = (acc[:, :SW_DIM] / l).astype(o_ref.dtype)


def sw_attn(p_lat, p_ctx, sink):
    s = p_lat.shape[0]
    c = p_ctx.shape[0]
    tq = SW_TQ
    assert s % tq == 0 and tq % SW_WINDOW == 0
    nq = s // tq
    per = tq // SW_WINDOW
    nwb = s // SW_WINDOW
    kcol = SW_HEADS * SW_DIM // LANE
    prev = lambda col: pl.BlockSpec((SW_WINDOW, LANE), lambda i: (jnp.maximum(i * per - 1, 0), col))
    own = lambda col: pl.BlockSpec((tq, LANE), lambda i: (i, col))
    nxt = lambda col: pl.BlockSpec((SW_WINDOW, LANE), lambda i: (jnp.minimum((i + 1) * per, nwb - 1), col))
    return pl.pallas_call(
        functools.partial(_sw_kernel, s),
        grid=(nq,),
        in_specs=[pl.BlockSpec((tq, SW_HEADS * SW_DIM), lambda i: (i, 0)),
                  prev(kcol), own(kcol), nxt(kcol), prev(kcol + 1), own(kcol + 1), nxt(kcol + 1),
                  pl.BlockSpec((c, LANE), lambda i: (0, kcol)),
                  pl.BlockSpec((c, LANE), lambda i: (0, kcol + 1)),
                  pl.BlockSpec(sink.shape, lambda i: (0, 0))],
        out_specs=pl.BlockSpec((tq, SW_HEADS * SW_DIM), lambda i: (i, 0)),
        out_shape=jax.ShapeDtypeStruct((s, SW_HEADS * SW_DIM), CD),
        compiler_params=_cparams("parallel"),
        name="sw_attn",
    )(p_lat, p_lat, p_lat, p_lat, p_lat, p_lat, p_lat, p_ctx, p_ctx, sink)


def _ml_prep_kernel(nt, x_ref, xp_ref, xn_ref, g_ref, w_ref, b_ref, q_out, k_out, g_out):
    i = pl.program_id(0)
    x = x_ref[...]
    tm = x.shape[0]
    row = lax.broadcasted_iota(jnp.int32, x.shape, 0)
    has_prev = jnp.where(i > 0, 1.0, 0.0)
    has_next = jnp.where(i < nt - 1, 1.0, 0.0)
    prev_row = xp_ref[7:8, :] * has_prev
    next_row = xn_ref[0:1, :] * has_next
    xm = jnp.where(row == 0, prev_row, pltpu.roll(x, 1, axis=0))
    xq = jnp.where(row == tm - 1, next_row, pltpu.roll(x, tm - 1, axis=0))
    y = w_ref[0:1, :] * xm + w_ref[1:2, :] * x + w_ref[2:3, :] * xq
    y = y * _sigmoid(y)
    q_out[...] = y[:, :ML_QKW].astype(CD)
    k_out[...] = (y[:, ML_QKW:] * (ML_QK ** -0.5)).astype(CD)
    g = g_ref[...] + b_ref[...]
    lane = lax.broadcasted_iota(jnp.int32, g.shape, 1)
    logsig = jnp.minimum(g, 0.0) - jnp.log(1.0 + jnp.exp(-jnp.abs(g)))
    g_out[...] = jnp.where(lane < 2 * ML_HEADS, g, logsig)


def ml_prep(p_ml, conv_w, gate_b):
    t = p_ml.shape[0]
    tm = _row_tile(t, 256)
    nt = t // tm
    wq = 2 * ML_QKW
    gcol = (2 * ML_QKW + 2 * ML_VW) // LANE
    w8 = jnp.zeros((8, wq), F32).at[:ML_CONV].set(conv_w.astype(F32))
    b = jnp.zeros((1, LANE), F32).at[0, :4 * ML_HEADS].set(gate_b.astype(F32).reshape(-1))
    h8 = tm // 8
    return pl.pallas_call(
        functools.partial(_ml_prep_kernel, nt),
        grid=(nt,),
        in_specs=[pl.BlockSpec((tm, wq), lambda i: (i, 0)),
                  pl.BlockSpec((8, wq), lambda i: (jnp.maximum(i * h8 - 1, 0), 0)),
                  pl.BlockSpec((8, wq), lambda i: (jnp.minimum((i + 1) * h8, t // 8 - 1), 0)),
                  pl.BlockSpec((tm, LANE), lambda i: (i, gcol)),
                  pl.BlockSpec((8, wq), lambda i: (0, 0)),
                  pl.BlockSpec((1, LANE), lambda i: (0, 0))],
        out_specs=[pl.BlockSpec((tm, ML_QKW), lambda i: (i, 0)),
                   pl.BlockSpec((tm, ML_QKW), lambda i: (i, 0)),
                   pl.BlockSpec((tm, LANE), lambda i: (i, 0))],
        out_shape=[jax.ShapeDtypeStruct((t, ML_QKW), CD), jax.ShapeDtypeStruct((t, ML_QKW), CD),
                   jax.ShapeDtypeStruct((t, LANE), F32)],
        compiler_params=_cparams("parallel"),
        name="ml_prep",
    )(p_ml, p_ml, p_ml, p_ml, w8, b)


def _ml_scan_kernel(qf_ref, ktf_ref, vf_ref, gcf_ref, grf_ref,
                    qb_ref, ktb_ref, vb_ref, gcb_ref, grb_ref,
                    c0_ref, m0_ref, hf_ref, hb_ref, c_ref, m_ref):
    @pl.when(pl.program_id(0) == 0)
    def _():
        c_ref[...] = c0_ref[...]
        m_ref[...] = m0_ref[...]

    ln = qf_ref.shape[0]
    r = lax.broadcasted_iota(jnp.int32, (ln, ln), 0)
    c = lax.broadcasted_iota(jnp.int32, (ln, ln), 1)
    lower = r >= c
    upper = c >= r
    lower_f = jnp.where(lower, 1.0, 0.0)
    upper_f = jnp.where(upper, 1.0, 0.0)
    ones_col = jnp.where(lax.broadcasted_iota(jnp.int32, (ln, LANE), 1) == 0, 1.0, 0.0).astype(CD)
    hi = lax.Precision.HIGHEST

    streams = ((qf_ref, ktf_ref, vf_ref, gcf_ref, grf_ref, hf_ref, lower, lower_f, upper_f, ln - 1),
               (qb_ref, ktb_ref, vb_ref, gcb_ref, grb_ref, hb_ref, upper, upper_f, lower_f, 0))
    for d, (q_ref, kt_ref, v_ref, gc_ref, gr_ref, h_ref, mask, tri_c, tri_r, last) in enumerate(streams):
        gcol = gc_ref[...]
        grow = gr_ref[...]
        bcol = jnp.dot(tri_c, gcol, precision=hi, preferred_element_type=F32)
        brow = jnp.dot(grow, tri_r, precision=hi, preferred_element_type=F32)
        q = q_ref[...]
        kt = kt_ref[...]
        v = v_ref[...]
        for hd in range(ML_HEADS):
            ch = d * ML_HEADS + hd
            ic = d * ML_HEADS + hd
            fc = 2 * ML_HEADS + ic
            i_c = gcol[:, ic:ic + 1]
            b_c = bcol[:, fc:fc + 1]
            i_r = grow[ic:ic + 1, :]
            b_r = brow[fc:fc + 1, :]
            m_prev = m_ref[ch, 0:1, 0:1]
            cst = c_ref[ch]
            dmat = jnp.where(mask, b_c - b_r + i_r, NEG)
            inter = b_c + m_prev
            m_t = jnp.maximum(inter, jnp.max(dmat, axis=-1, keepdims=True))
            qh = q[:, hd * ML_QK:(hd + 1) * ML_QK]
            kth = kt[hd * ML_QK:(hd + 1) * ML_QK, :]
            s = lax.dot_general(qh, kth, (((1,), (0,)), ((), ())), preferred_element_type=F32)
            s = s * jnp.exp(dmat - m_t)
            w_inter = jnp.exp(inter - m_t)
            v_aug = jnp.concatenate([v[:, hd * ML_V:(hd + 1) * ML_V].astype(CD), ones_col], axis=1)
            num = _dot(s.astype(CD), v_aug) + w_inter * _dot(qh, cst.astype(CD))
            den = num[:, ML_V:ML_V + 1]
            h = num[:, :ML_V] / jnp.maximum(jnp.abs(den), jnp.exp(-m_t))
            h_ref[:, hd * ML_V:(hd + 1) * ML_V] = h
            b_end = b_c[last:last + 1, :]
            m_new = jnp.maximum(b_end + m_prev, jnp.max(b_end - b_c + i_c, axis=0, keepdims=True))
            w_s = jnp.exp(b_end - b_r + i_r - m_new)
            w_c = jnp.exp(b_end + m_prev - m_new)
            ktw = (kth.astype(F32) * w_s).astype(CD)
            c_ref[ch] = w_c * cst + _dot(ktw, v_aug)
            m_ref[ch] = jnp.broadcast_to(m_new, m_ref.shape[1:])


def ml_scan(qc, kc, p_ml, gates, state):
    t = qc.shape[0]
    ln = min(ML_CHUNK, t)
    assert t % ln == 0
    nc = t // ln
    kt = kc.T
    gr = gates[:, :4 * ML_HEADS].T
    c0, m0 = state
    vcol = 2 * ML_QKW // ML_VW
    fwd = lambda j: j
    bwd = lambda j: nc - 1 - j

    def specs(ix):
        return [pl.BlockSpec((ln, ML_QKW), lambda j: (ix(j), 0)),
                pl.BlockSpec((ML_QKW, ln), lambda j: (0, ix(j))),
                pl.BlockSpec((ln, ML_VW), lambda j: (ix(j), vcol)),
                pl.BlockSpec((ln, LANE), lambda j: (ix(j), 0)),
                pl.BlockSpec((4 * ML_HEADS, ln), lambda j: (0, ix(j)))]

    st_specs = [pl.BlockSpec(c0.shape, lambda j: (0, 0, 0)), pl.BlockSpec(m0.shape, lambda j: (0, 0, 0))]
    return pl.pallas_call(
        _ml_scan_kernel,
        grid=(nc,),
        in_specs=specs(fwd) + specs(bwd) + st_specs,
        out_specs=[pl.BlockSpec((ln, ML_VW), lambda j: (fwd(j), 0)),
                   pl.BlockSpec((ln, ML_VW), lambda j: (bwd(j), 0))] + st_specs,
        out_shape=[jax.ShapeDtypeStruct((t, ML_VW), F32), jax.ShapeDtypeStruct((t, ML_VW), F32),
                   jax.ShapeDtypeStruct(c0.shape, F32), jax.ShapeDtypeStruct(m0.shape, F32)],
        compiler_params=_cparams("arbitrary"),
        name="ml_scan",
    )(qc, kt, p_ml, gates, gr, qc, kt, p_ml, gates, gr, c0, m0)


def _ml_finish_kernel(hf_ref, hb_ref, o_ref, g_ref, y_ref):
    h = hf_ref[...] + hb_ref[...]
    og = _sigmoid(o_ref[...])
    g = g_ref[...]
    for hd in range(ML_HEADS):
        sl = slice(hd * ML_V, (hd + 1) * ML_V)
        hh = h[:, sl]
        y = hh * lax.rsqrt(jnp.mean(hh * hh, axis=-1, keepdims=True) + EPS) * g[:, sl]
        y_ref[:, sl] = (y * og[:, sl]).astype(y_ref.dtype)


def ml_finish(hf, hb, p_ml, norm_g):
    t = hf.shape[0]
    tm = _row_tile(t, 256)
    ocol = (2 * ML_QKW + ML_VW) // ML_VW
    blk = pl.BlockSpec((tm, ML_VW), lambda i: (i, 0))
    return pl.pallas_call(
        _ml_finish_kernel,
        grid=(t // tm,),
        in_specs=[blk, blk, pl.BlockSpec((tm, ML_VW), lambda i: (i, ocol)),
                  pl.BlockSpec((1, ML_VW), lambda i: (0, 0))],
        out_specs=blk,
        out_shape=jax.ShapeDtypeStruct((t, ML_VW), CD),
        compiler_params=_cparams("parallel"),
        name="ml_finish",
    )(hf, hb, p_ml, norm_g.reshape(1, ML_VW).astype(F32))


def _merge_kernel(h_ref, ya_ref, yb_ref, yc_ref, yd_ref, g0_ref, g1_ref, g2_ref, g3_ref, wb_ref, o_ref):
    h = h_ref[...]
    acc = None
    for i, (y_ref, g_ref) in enumerate(zip((ya_ref, yb_ref, yc_ref, yd_ref), (g0_ref, g1_ref, g2_ref, g3_ref))):
        term = _sigmoid(_dot(h, g_ref[...])) * _dot(y_ref[...], wb_ref[0, i])
        acc = term if acc is None else acc + term
    o_ref[...] = acc.astype(o_ref.dtype)


def gated_merge(h, ys, w_gate, w_branch, l, tn=512):
    t, d = h.shape
    tm = _row_tile(t)
    nj = d // tn
    yspec = pl.BlockSpec((tm, BRANCH_W), lambda j, i: (i, 0))
    gspec = lambda b: pl.BlockSpec((d, tn), lambda j, i: (0, b * nj + j))
    return pl.pallas_call(
        _merge_kernel,
        grid=(nj, t // tm),
        in_specs=[pl.BlockSpec((tm, d), lambda j, i: (i, 0)), yspec, yspec, yspec, yspec,
                  gspec(0), gspec(1), gspec(2), gspec(3),
                  pl.BlockSpec((1, N_BRANCH, BRANCH_W, tn), lambda j, i: (l, 0, 0, j))],
        out_specs=pl.BlockSpec((tm, tn), lambda j, i: (i, j)),
        out_shape=jax.ShapeDtypeStruct((t, d), CD),
        compiler_params=_cparams("parallel", "parallel"),
        name="gated_merge",
    )(h, *ys, w_gate, w_gate, w_gate, w_gate, w_branch)


def _ffn1_kernel(h_ref, wa_ref, wg_ref, o_ref, wa_sc, wg_sc):
    @pl.when(pl.program_id(1) == 0)
    def _():
        wa_sc[...] = wa_ref[0].astype(CD)
        wg_sc[...] = wg_ref[0].astype(CD)

    h = h_ref[...]
    a = _dot(h, wa_sc[...])
    g = _dot(h, wg_sc[...])
    o_ref[...] = (a * _sigmoid(a) * g).astype(o_ref.dtype)


def ffn_up(h, w, l, tn=512):
    t, d = h.shape
    dff = w.shape[2] // 2
    tm = _row_tile(t)
    nj = dff // tn
    return pl.pallas_call(
        _ffn1_kernel,
        grid=(nj, t // tm),
        in_specs=[pl.BlockSpec((tm, d), lambda j, i: (i, 0)),
                  pl.BlockSpec((1, d, tn), lambda j, i: (l, 0, j)),
                  pl.BlockSpec((1, d, tn), lambda j, i: (l, 0, nj + j))],
        out_specs=pl.BlockSpec((tm, tn), lambda j, i: (i, j)),
        out_shape=jax.ShapeDtypeStruct((t, dff), CD),
        scratch_shapes=[pltpu.VMEM((d, tn), CD), pltpu.VMEM((d, tn), CD)],
        compiler_params=_cparams("parallel", "arbitrary"),
        name="ffn_up",
    )(h, w, w)


def _rot_cols(w, dim):
    k, n = w.shape
    w4 = w.reshape(k, n // dim, 2, dim // 2)
    return jnp.stack([-w4[:, :, 1], w4[:, :, 0]], axis=2).reshape(k, n)


def _rope_tables(n_tokens, dim):
    a = dim // 2
    inv = 1.0 / (ROPE_THETA ** (jnp.arange(0, a, 2, dtype=F32) / a))
    rows = n_tokens // GRID_W
    assert rows * GRID_W == n_tokens
    ang_r = jnp.arange(rows, dtype=F32)[:, None] * inv
    ang_c = jnp.arange(GRID_W, dtype=F32)[:, None] * inv

    def expand(fr, fc):
        shape = (rows, GRID_W, a // 2)
        return jnp.concatenate([jnp.broadcast_to(fr[:, None, :], shape), jnp.broadcast_to(fc[None], shape)],
                               axis=-1).reshape(n_tokens, a)

    return expand(jnp.cos(ang_r), jnp.cos(ang_c)), expand(jnp.sin(ang_r), jnp.sin(ang_c))


def _layer_weights(w_in, w_in_c, l, mla_w_uq_l, mla_w_ukv_l):
    d = w_in.shape[1]
    o_na = ML_COLS
    o_sw = o_na + NA_COLS
    o_mla = o_sw + SW_COLS
    o_gate = o_mla + MLA_COLS
    f32c = lambda a, b: w_in[l, :, a:b]
    cdc = lambda a, b: w_in_c[l, :, a:b]
    w = {}
    w["ml"] = jnp.pad(cdc(0, ML_COLS), ((0, 0), (0, ML_PAD - ML_COLS)))
    nq = NA_HEADS * NA_DIM
    w["na"] = jnp.concatenate([(f32c(o_na, o_na + nq) * (NA_DIM ** -0.5 * LOG2E)).astype(CD),
                               cdc(o_na + nq, o_sw)], axis=1)
    sq = SW_HEADS * SW_DIM
    skv = SW_KV_HEADS * SW_DIM
    sw_q = f32c(o_sw, o_sw + sq) * (SW_DIM ** -0.5 * LOG2E)
    sw_k = f32c(o_sw + sq, o_sw + sq + skv)
    w["sw"] = jnp.concatenate([sw_q.astype(CD), cdc(o_sw + sq, o_mla)], axis=1)
    w["sw_rot"] = jnp.concatenate([_rot_cols(sw_q, SW_DIM), _rot_cols(sw_k, SW_DIM)], axis=1).astype(CD)
    o_kr = o_mla + MLA_Q_RANK + MLA_KV_RANK
    z = jnp.zeros((d, LANE - MLA_ROPE), CD)
    w["mla1"] = jnp.concatenate([cdc(o_mla, o_gate), z, _rot_cols(f32c(o_kr, o_gate), MLA_ROPE).astype(CD), z],
                                axis=1)
    uq = mla_w_uq_l.reshape(MLA_Q_RANK, MLA_HEADS, MLA_NOPE + MLA_ROPE)
    zq = jnp.zeros((MLA_Q_RANK, MLA_HEADS, MLA_HW - MLA_NOPE - MLA_ROPE), F32)
    w["mla_q"] = jnp.concatenate([uq, zq], axis=2).reshape(MLA_Q_RANK, -1).astype(CD)
    uqr = _rot_cols(uq[:, :, MLA_NOPE:].reshape(MLA_Q_RANK, -1), MLA_ROPE).reshape(MLA_Q_RANK, MLA_HEADS, MLA_ROPE)
    w["mla_q_rot"] = jnp.concatenate([jnp.zeros_like(uq[:, :, :MLA_NOPE]), uqr, zq], axis=2
                                     ).reshape(MLA_Q_RANK, -1).astype(CD)
    w["mla_kv"] = mla_w_ukv_l.astype(CD)
    w["gate"] = cdc(o_gate, w_in.shape[2])
    return w


def _tables(s, c):
    cos_sw, sin_sw = _rope_tables(s, SW_DIM)
    cos_m, sin_m = _rope_tables(s, MLA_ROPE)
    tile2 = lambda a: jnp.concatenate([a, a, a, a], axis=1)
    scale = (MLA_NOPE + MLA_ROPE) ** -0.5 * LOG2E
    zpad = jnp.zeros((s, MLA_HW - MLA_NOPE - MLA_ROPE), F32)
    lat = dict(
        sw_cos=tile2(cos_sw), sw_sin=tile2(sin_sw),
        q_cos=scale * jnp.concatenate([jnp.ones((s, MLA_NOPE), F32), cos_m, cos_m, zpad], axis=1),
        q_sin=scale * jnp.concatenate([jnp.zeros((s, MLA_NOPE), F32), sin_m, sin_m, zpad], axis=1),
        k_cos=jnp.concatenate([cos_m, cos_m, zpad], axis=1),
        k_sin=jnp.concatenate([sin_m, sin_m, zpad], axis=1))
    ctx = dict(
        sw_cos=jnp.ones((c, LANE), F32), sw_sin=jnp.zeros((c, LANE), F32),
        q_cos=jnp.full((c, MLA_HW), scale, F32), q_sin=jnp.zeros((c, MLA_HW), F32),
        k_cos=jnp.ones((c, LANE), F32), k_sin=jnp.zeros((c, LANE), F32))
    return lat, ctx


def _project(h, w, tab, gq, gkv):
    p_ml = matmul(h, w["ml"], F32)
    p_na = matmul(h, w["na"], CD)
    n_rope = (SW_HEADS + SW_KV_HEADS) * SW_DIM // LANE
    p_sw = rope_matmul(h, w["sw"], w["sw_rot"], tab["sw_cos"], tab["sw_sin"], n_rope)
    mq, mk, mv = mla_proj(h, w["mla1"], gq, gkv, w["mla_q"], w["mla_q_rot"], w["mla_kv"],
                          tab["q_cos"], tab["q_sin"], tab["k_cos"], tab["k_sin"])
    return p_ml, p_na, p_sw, (mq, mk, mv)


def _dense_tail(x, h, ys, w, l, w_branch, w_out, w_ffn_in, w_ffn_out, vt, g1, norm2, g2, norm_next, last):
    merged = gated_merge(h, ys, w["gate"], w_branch, l)
    x, h2 = matmul_residual_norm(merged, w_out, l, x, vt, g1, norm2, True, CD)
    u = ffn_up(h2, w_ffn_in, l)
    tk = w_ffn_out.shape[1] // 4
    if last:
        return None, matmul_residual_norm(u, w_ffn_out, l, x, vt, g2, norm_next, False, F32, tk=tk)
    return matmul_residual_norm(u, w_ffn_out, l, x, vt, g2, norm_next, True, CD, tk=tk)


def kernel(x, c, ctx, c_ctx, w_ada, b_ada, norm1_g, norm2_g, w_in, ml_conv_w, ml_gate_b, ml_norm_g, na_rpb,
           sw_sink, mla_q_norm_g, mla_kv_norm_g, mla_w_uq, mla_w_ukv, w_branch, w_out, w_ffn_in, w_ffn_out,
           final_norm_g):
    bsz, s, d = x.shape
    assert bsz == 1
    depth = w_in.shape[0]
    n_ctx = ctx.shape[1]
    rows = s // GRID_W
    xl = x[0].astype(F32)
    xc = ctx[0].astype(F32)
    mod = adaln(jnp.stack([c[0], c_ctx], axis=1).astype(F32), w_ada, b_ada)
    vt = jnp.concatenate([mod.reshape(depth * 12, d), norm1_g.astype(F32), norm2_g.astype(F32),
                          final_norm_g.reshape(1, d).astype(F32), jnp.zeros((1, d), F32)]).reshape(-1, 1, d)
    mod_row = lambda l, stream, part: (l * 2 + stream) * 6 + part
    n1_row = lambda l: depth * 12 + l
    n2_row = lambda l: depth * 13 + l
    fin_row, zero_row = depth * 14, depth * 14 + 1
    norm1 = lambda l, st: (n1_row(l), mod_row(l, st, 0), mod_row(l, st, 1))
    norm2 = lambda l, st: (n2_row(l), mod_row(l, st, 3), mod_row(l, st, 4))
    tab_l, tab_c = _tables(s, n_ctx)
    zero_state = (jnp.zeros((2 * ML_HEADS, ML_QK, 2 * ML_V), F32), jnp.zeros((2 * ML_HEADS, 8, LANE), F32))

    w_in_c = w_in.astype(CD)
    wb = w_branch.astype(CD)
    wo = w_out.astype(CD)
    wf2 = w_ffn_out.astype(CD)

    hl = norm_mod(xl, vt, norm1(0, 0), CD)
    hc = norm_mod(xc, vt, norm1(0, 1), CD)
    for l in range(depth):
        last = l == depth - 1
        w = _layer_weights(w_in, w_in_c, l, mla_w_uq[l], mla_w_ukv[l])
        gq = mla_q_norm_g[l].reshape(1, -1).astype(F32)
        gkv = mla_kv_norm_g[l].reshape(1, -1).astype(F32)
        sink = sw_sink[l].reshape(1, -1).astype(F32)

        ml_l, na_l, sw_l, (mq_l, mk_l, mv_l) = _project(hl, w, tab_l, gq, gkv)
        ml_c, na_c, sw_c, (mq_c, mk_c, mv_c) = _project(hc, w, tab_c, gq, gkv)

        qc_c, kc_c, gt_c = ml_prep(ml_c, ml_conv_w[l], ml_gate_b[l])
        qc_l, kc_l, gt_l = ml_prep(ml_l, ml_conv_w[l], ml_gate_b[l])
        hf_c, hb_c, cst, mst = ml_scan(qc_c, kc_c, ml_c, gt_c, zero_state)
        hf_l, hb_l, _, _ = ml_scan(qc_l, kc_l, ml_l, gt_l, (cst, mst))
        ya_l = ml_finish(hf_l, hb_l, ml_l, ml_norm_g[l])
        yb_l = na_attn(na_l, na_c, na_rpb[l], rows)
        yc_l = sw_attn(sw_l, sw_c, sink)
        yd_l = mla_flash(mq_l, jnp.concatenate([mk_l, mk_c]), jnp.concatenate([mv_l, mv_c]))

        nxt_l = (fin_row, zero_row, zero_row) if last else norm1(l + 1, 0)
        xl_new, hl_new = _dense_tail(xl, hl, (ya_l, yb_l, yc_l, yd_l), w, l, wb, wo, w_ffn_in, wf2, vt,
                                     mod_row(l, 0, 2), norm2(l, 0), mod_row(l, 0, 5), nxt_l, last)
        if not last:
            ya_c = ml_finish(hf_c, hb_c, ml_c, ml_norm_g[l])
            nw = NA_HEADS * NA_DIM
            yb_c = dense_attn([(na_c, nw, 0), (na_c, nw, 1), (na_c, nw, 2)], NA_HEADS, 1, NA_DIM, NA_DIM,
                              log2_scores=True)
            kcol = SW_HEADS * SW_DIM // LANE
            yc_c = dense_attn([(sw_c, SW_HEADS * SW_DIM, 0), (sw_c, LANE, kcol), (sw_c, LANE, kcol + 1)],
                              SW_HEADS, SW_HEADS // SW_KV_HEADS, SW_DIM, SW_DIM, sink=sink, log2_scores=True)
            yd_c = dense_attn([(mq_c, MLA_HEADS * MLA_HW, 0), (mk_c, MLA_HEADS * MLA_HW, 0),
                               (mv_c, MLA_HEADS * MLA_V, 0)], MLA_HEADS, 1, MLA_HW, MLA_V, log2_scores=True)
            xc, hc = _dense_tail(xc, hc, (ya_c, yb_c, yc_c, yd_c), w, l, wb, wo, w_ffn_in, wf2, vt,
                                 mod_row(l, 1, 2), norm2(l, 1), mod_row(l, 1, 5), norm1(l + 1, 1), False)
        xl, hl = xl_new, hl_new

    return hl[None].astype(x.dtype)
```

```python
import functools

import numpy as np
import jax
import jax.numpy as jnp
from jax import lax
from jax.experimental import pallas as pl
from jax.experimental.pallas import tpu as pltpu

F32 = jnp.float32
CD = jnp.bfloat16

GRID_W = 64
EPS = 1e-6
ROPE_THETA = 10000.0
ML_HEADS, ML_QK, ML_V, ML_CONV = 4, 64, 128, 3
NA_HEADS, NA_DIM, NA_WIN_R, NA_WIN_C = 8, 64, 8, 16
SW_HEADS, SW_KV_HEADS, SW_DIM, SW_WINDOW = 8, 2, 64, 128
MLA_HEADS, MLA_Q_RANK, MLA_KV_RANK, MLA_NOPE, MLA_ROPE, MLA_V = 4, 384, 256, 128, 64, 128
N_BRANCH, BRANCH_W = 4, 512

ML_QKW = ML_HEADS * ML_QK
ML_VW = ML_HEADS * ML_V
ML_COLS = 2 * ML_QKW + 2 * ML_VW + 4 * ML_HEADS
ML_PAD = 1664
NA_COLS = 3 * NA_HEADS * NA_DIM
SW_COLS = (SW_HEADS + 2 * SW_KV_HEADS) * SW_DIM
MLA_COLS = MLA_Q_RANK + MLA_KV_RANK + MLA_ROPE
MLA_HW = 256
LANE = 128
NEG = -1e30
LOG2E = float(np.log2(np.e))
VMEM_LIMIT = 56 * 1024 * 1024

NA_QROWS = 4
SW_TQ = 256
ML_CHUNK = 256
MLA_TK = 1280


def _cparams(*sem):
    return pltpu.CompilerParams(dimension_semantics=sem, vmem_limit_bytes=VMEM_LIMIT)


def _row_tile(t, pref=512):
    tm = min(pref, t)
    assert t % tm == 0
    return tm


def _dot(a, b):
    return jnp.dot(a, b, preferred_element_type=F32)


def _dot_nt(a, b):
    return lax.dot_general(a, b, (((1,), (1,)), ((), ())), preferred_element_type=F32)


def _sigmoid(x):
    return 1.0 / (1.0 + jnp.exp(-x))


def _ones_col(n, w):
    return jnp.where(lax.broadcasted_iota(jnp.int32, (n, w), 1) == 0, 1.0, 0.0).astype(CD)


def _adaln_kernel(c_ref, w_ref, b_ref, o_ref):
    w = w_ref[0]
    for r in range(2):
        c = c_ref[:, r:r + 1]
        o_ref[0, r:r + 1, :] = jnp.sum((c * _sigmoid(c)) * w, axis=0, keepdims=True) + b_ref[0]


def adaln(cc, w_ada, b_ada):
    nl, d, n = w_ada.shape
    tn = 1024
    return pl.pallas_call(
        _adaln_kernel,
        grid=(nl, n // tn),
        in_specs=[pl.BlockSpec((d, 2), lambda l, j: (0, 0)),
                  pl.BlockSpec((1, d, tn), lambda l, j: (l, 0, j)),
                  pl.BlockSpec((1, 1, tn), lambda l, j: (l, 0, j))],
        out_specs=pl.BlockSpec((1, 2, tn), lambda l, j: (l, 0, j)),
        out_shape=jax.ShapeDtypeStruct((nl, 2, n), F32),
        compiler_params=_cparams("parallel", "parallel"),
        name="adaln",
    )(cc, w_ada, b_ada.reshape(nl, 1, n))


def _vec_spec(d, idx):
    return pl.BlockSpec((1, 1, d), lambda *_: (idx, 0, 0))


def _norm_mod(x, g, shift, scale):
    y = x * lax.rsqrt(jnp.mean(x * x, axis=-1, keepdims=True) + EPS)
    return (y * g) * (1.0 + scale) + shift


def _norm_mod_kernel(x_ref, g_ref, sh_ref, sc_ref, o_ref):
    o_ref[...] = _norm_mod(x_ref[...], g_ref[0], sh_ref[0], sc_ref[0]).astype(o_ref.dtype)


def norm_mod(x, vt, norm_idx, out_dtype):
    t, d = x.shape
    tm = _row_tile(t, 256)
    return pl.pallas_call(
        _norm_mod_kernel,
        grid=(t // tm,),
        in_specs=[pl.BlockSpec((tm, d), lambda i: (i, 0))] + [_vec_spec(d, ix) for ix in norm_idx],
        out_specs=pl.BlockSpec((tm, d), lambda i: (i, 0)),
        out_shape=jax.ShapeDtypeStruct((t, d), out_dtype),
        compiler_params=_cparams("parallel"),
        name="norm_mod",
    )(x, vt, vt, vt)


def _mm_kernel(a_ref, b_ref, o_ref):
    o_ref[...] = _dot(a_ref[...], b_ref[...]).astype(o_ref.dtype)


def matmul(a, b, out_dtype, tn=None):
    t, k = a.shape
    n = b.shape[1]
    tn = n if tn is None else tn
    tm = _row_tile(t)
    return pl.pallas_call(
        _mm_kernel,
        grid=(n // tn, t // tm),
        in_specs=[pl.BlockSpec((tm, k), lambda j, i: (i, 0)),
                  pl.BlockSpec((k, tn), lambda j, i: (0, j))],
        out_specs=pl.BlockSpec((tm, tn), lambda j, i: (i, j)),
        out_shape=jax.ShapeDtypeStruct((t, n), out_dtype),
        compiler_params=_cparams("parallel", "parallel"),
        name="matmul",
    )(a, b)


def _mm_res_norm_kernel(nj, tn, want_x, a_ref, b_ref, x_ref, gate_ref, g_ref, sh_ref, sc_ref, *rest):
    xbuf, ho_ref = (rest[0], rest[1]) if want_x else (rest[1], rest[0])
    j = pl.program_id(1)
    acc = _dot(a_ref[...], b_ref[0])
    for jj in range(nj):
        @pl.when(j == jj)
        def _(jj=jj):
            sl = slice(jj * tn, (jj + 1) * tn)
            xbuf[:, sl] = x_ref[:, sl] + gate_ref[0][:, sl] * acc

    @pl.when(j == nj - 1)
    def _():
        ho_ref[...] = _norm_mod(xbuf[...], g_ref[0], sh_ref[0], sc_ref[0]).astype(ho_ref.dtype)


def matmul_residual_norm(a, b, l, x, vt, gate_idx, norm_idx, want_x, h_dtype, tn=None):
    t, kdim = a.shape
    n = b.shape[2]
    tn = n if tn is None else tn
    nj = n // tn
    tm = _row_tile(t)
    row = pl.BlockSpec((tm, n), lambda i, j: (i, 0))
    out_shape = [jax.ShapeDtypeStruct((t, n), h_dtype)]
    if want_x:
        out_shape = [jax.ShapeDtypeStruct((t, n), F32)] + out_shape
    out = pl.pallas_call(
        functools.partial(_mm_res_norm_kernel, nj, tn, want_x),
        grid=(t // tm, nj),
        in_specs=[pl.BlockSpec((tm, kdim), lambda i, j: (i, 0)),
                  pl.BlockSpec((1, kdim, tn), lambda i, j: (l, 0, j)),
                  row, _vec_spec(n, gate_idx)] + [_vec_spec(n, ix) for ix in norm_idx],
        out_specs=[row] * len(out_shape),
        out_shape=out_shape,
        scratch_shapes=[] if want_x else [pltpu.VMEM((tm, n), F32)],
        compiler_params=_cparams("parallel", "arbitrary"),
        name="matmul_residual_norm",
    )(a, b, x, vt, vt, vt, vt)
    return out if want_x else out[0]


def _rope_mm_kernel(n_rope, a_ref, w_ref, wr_ref, cos_ref, sin_ref, o_ref):
    a = a_ref[...]
    p = _dot(a, w_ref[...])
    pr = _dot(a, wr_ref[...])
    cos = cos_ref[...]
    sin = sin_ref[...]
    for j in range(p.shape[1] // LANE):
        sl = slice(j * LANE, (j + 1) * LANE)
        if j < n_rope:
            o_ref[:, sl] = (p[:, sl] * cos + pr[:, sl] * sin).astype(o_ref.dtype)
        else:
            o_ref[:, sl] = p[:, sl].astype(o_ref.dtype)


def rope_matmul(a, w, w_rot, cos, sin, n_rope):
    t, k = a.shape
    n = w.shape[1]
    nr = n_rope * LANE
    tm = _row_tile(t)
    return pl.pallas_call(
        functools.partial(_rope_mm_kernel, n_rope),
        grid=(t // tm,),
        in_specs=[pl.BlockSpec((tm, k), lambda i: (i, 0)),
                  pl.BlockSpec((k, n), lambda i: (0, 0)),
                  pl.BlockSpec((k, nr), lambda i: (0, 0)),
                  pl.BlockSpec((tm, LANE), lambda i: (i, 0)),
                  pl.BlockSpec((tm, LANE), lambda i: (i, 0))],
        out_specs=pl.BlockSpec((tm, n), lambda i: (i, 0)),
        out_shape=jax.ShapeDtypeStruct((t, n), CD),
        compiler_params=_cparams("parallel"),
        name="rope_matmul",
    )(a, w, w_rot, cos, sin)


def _mla_proj_kernel(h_ref, w1_ref, gq_ref, gkv_ref, wq_ref, wqr_ref, wkv_ref,
                     cq_ref, sq_ref, ck_ref, sk_ref, q_out, k_out, v_out):
    p = _dot(h_ref[...], w1_ref[...])
    cq = p[:, :MLA_Q_RANK]
    ckv = p[:, MLA_Q_RANK:MLA_Q_RANK + MLA_KV_RANK]
    o = MLA_Q_RANK + MLA_KV_RANK
    kr = p[:, o:o + LANE]
    krr = p[:, o + LANE:o + 2 * LANE]
    cqn = (cq * lax.rsqrt(jnp.mean(cq * cq, axis=-1, keepdims=True) + EPS) * gq_ref[...]).astype(CD)
    ckvn = (ckv * lax.rsqrt(jnp.mean(ckv * ckv, axis=-1, keepdims=True) + EPS) * gkv_ref[...]).astype(CD)
    qa = _dot(cqn, wq_ref[...])
    qb = _dot(cqn, wqr_ref[...])
    kv = _dot(ckvn, wkv_ref[...])
    krp = (kr * ck_ref[...] + krr * sk_ref[...]).astype(CD)
    cq_t = cq_ref[...]
    sq_t = sq_ref[...]
    for h in range(MLA_HEADS):
        qs = slice(h * MLA_HW, (h + 1) * MLA_HW)
        q_out[:, qs] = (qa[:, qs] * cq_t + qb[:, qs] * sq_t).astype(CD)
        k_out[:, h * MLA_HW:h * MLA_HW + MLA_NOPE] = kv[:, h * 256:h * 256 + MLA_NOPE].astype(CD)
        k_out[:, h * MLA_HW + MLA_NOPE:(h + 1) * MLA_HW] = krp
        v_out[:, h * MLA_V:(h + 1) * MLA_V] = kv[:, h * 256 + MLA_NOPE:(h + 1) * 256].astype(CD)


def mla_proj(h, w1, gq, gkv, wq, wqr, wkv, cos_q, sin_q, cos_k, sin_k):
    t, d = h.shape
    tm = _row_tile(t)
    full = lambda a: pl.BlockSpec(a.shape, lambda i: (0,) * a.ndim)
    rows = lambda w: pl.BlockSpec((tm, w), lambda i: (i, 0))
    hw = MLA_HEADS * MLA_HW
    return pl.pallas_call(
        _mla_proj_kernel,
        grid=(t // tm,),
        in_specs=[rows(d), full(w1), full(gq), full(gkv), full(wq), full(wqr), full(wkv),
                  rows(MLA_HW), rows(MLA_HW), rows(LANE), rows(LANE)],
        out_specs=[rows(hw), rows(hw), rows(MLA_HEADS * MLA_V)],
        out_shape=[jax.ShapeDtypeStruct((t, hw), CD), jax.ShapeDtypeStruct((t, hw), CD),
                   jax.ShapeDtypeStruct((t, MLA_HEADS * MLA_V), CD)],
        compiler_params=_cparams("parallel"),
        name="mla_proj",
    )(h, w1, gq, gkv, wq, wqr, wkv, cos_q, sin_q, cos_k, sin_k)


def _flash_kernel(tq, tk, unroll, q_ref, k_ref, v_ref, o_ref, s_sc, m_sc, acc_sc):
    nkb = k_ref.shape[0] // tk
    nsub = q_ref.shape[0] // tq
    total = nsub * nkb
    m_sc[...] = jnp.full(m_sc.shape, NEG, F32)
    acc_sc[...] = jnp.zeros(acc_sc.shape, F32)
    ones_col = jnp.where(lax.broadcasted_iota(jnp.int32, (tk, LANE), 1) == 0, 1.0, 0.0).astype(CD)

    def split(b):
        sub = b // nkb
        return sub, b - sub * nkb

    def scores(b):
        sub, j = split(b)
        q = q_ref[pl.ds(pl.multiple_of(sub * tq, tq), tq), :]
        k = k_ref[pl.ds(pl.multiple_of(j * tk, tk), tk), :]
        return _dot_nt(q, k)

    def softmax_pv(s, b):
        sub, j = split(b)
        v = v_ref[pl.ds(pl.multiple_of(j * tk, tk), tk), :]
        m_prev = m_sc[sub]
        m_new = jnp.maximum(m_prev, jnp.max(s, axis=-1, keepdims=True))
        alpha = jnp.exp2(m_prev - m_new)
        p = jnp.exp2(s - m_new).astype(CD)
        v_aug = jnp.concatenate([v, ones_col], axis=1)
        acc_sc[sub] = alpha * acc_sc[sub] + _dot(p, v_aug)
        m_sc[sub] = m_new

    s_sc[0] = scores(0)

    def body(g, carry):
        b0 = g * unroll
        for u in range(unroll):
            s_sc[(u + 1) % 2] = scores(jnp.minimum(b0 + u + 1, total - 1))
            softmax_pv(s_sc[u % 2], b0 + u)
        return carry

    lax.fori_loop(0, total // unroll, body, 0)
    for sub in range(nsub):
        acc = acc_sc[sub]
        o_ref[sub * tq:(sub + 1) * tq, :] = (acc[:, :MLA_V] / acc[:, MLA_V:MLA_V + 1]).astype(o_ref.dtype)


def mla_flash(q, k, v):
    s = q.shape[0]
    nk = k.shape[0]
    tq = _row_tile(s, 512)
    tstep = _row_tile(s, 4 * tq)
    tk = MLA_TK
    total = (tstep // tq) * (nk // tk)
    assert nk % tk == 0 and total % 2 == 0
    unroll = 4 if total % 4 == 0 else 2
    return pl.pallas_call(
        functools.partial(_flash_kernel, tq, tk, unroll),
        grid=(MLA_HEADS, s // tstep),
        in_specs=[pl.BlockSpec((tstep, MLA_HW), lambda h, i: (i, h)),
                  pl.BlockSpec((nk, MLA_HW), lambda h, i: (0, h)),
                  pl.BlockSpec((nk, MLA_V), lambda h, i: (0, h))],
        out_specs=pl.BlockSpec((tstep, MLA_V), lambda h, i: (i, h)),
        out_shape=jax.ShapeDtypeStruct((s, MLA_HEADS * MLA_V), CD),
        scratch_shapes=[pltpu.VMEM((2, tq, tk), F32), pltpu.VMEM((tstep // tq, tq, 1), F32),
                        pltpu.VMEM((tstep // tq, tq, 2 * MLA_V), F32)],
        compiler_params=_cparams("parallel", "arbitrary"),
        name="mla_flash",
    )(q, k, v)


def _dense_attn_kernel(n_heads, group, dqk, dv, log2_scores, q_ref, k_ref, v_ref, sink_ref, o_ref):
    q = q_ref[...]
    k = k_ref[...]
    v = v_ref[...]
    ex = jnp.exp2 if log2_scores else jnp.exp
    for h in range(n_heads):
        g = h // group
        s = _dot_nt(q[:, h * dqk:(h + 1) * dqk], k[:, g * dqk:(g + 1) * dqk])
        m = jnp.max(s, axis=-1, keepdims=True)
        l = jnp.zeros_like(m)
        if sink_ref is not None:
            sk = sink_ref[0:1, h:h + 1] * (LOG2E if log2_scores else 1.0)
            m = jnp.maximum(m, sk)
            l = ex(sk - m)
        p = ex(s - m)
        l = l + jnp.sum(p, axis=-1, keepdims=True)
        o = _dot(p.astype(CD), v[:, g * dv:(g + 1) * dv]) / l
        o_ref[:, h * dv:(h + 1) * dv] = o.astype(o_ref.dtype)


def dense_attn(qkv_specs, n_heads, group, dqk, dv, sink=None, log2_scores=False):
    c = qkv_specs[0][0].shape[0]
    arrays = [a for a, _, _ in qkv_specs]
    specs = [pl.BlockSpec((c, w), functools.partial(lambda b, i: (0, b), b)) for _, w, b in qkv_specs]
    if sink is None:
        kern = lambda q, k, v, o: _dense_attn_kernel(n_heads, group, dqk, dv, log2_scores, q, k, v, None, o)
    else:
        kern = functools.partial(_dense_attn_kernel, n_heads, group, dqk, dv, log2_scores)
        arrays.append(sink)
        specs.append(pl.BlockSpec(sink.shape, lambda i: (0, 0)))
    return pl.pallas_call(
        kern,
        grid=(1,),
        in_specs=specs,
        out_specs=pl.BlockSpec((c, n_heads * dv), lambda i: (0, 0)),
        out_shape=jax.ShapeDtypeStruct((c, n_heads * dv), CD),
        compiler_params=_cparams("arbitrary"),
        name="dense_attn",
    )(*arrays)


def _na_kernel(q_ref, kp_ref, ko_ref, kn_ref, vp_ref, vo_ref, vn_ref, kc_ref, vc_ref, bias_ref, o_ref):
    q = q_ref[...]
    kcat = jnp.concatenate([kp_ref[...], ko_ref[...], kn_ref[...], kc_ref[...]], axis=0)
    vcat = jnp.concatenate([vp_ref[...], vo_ref[...], vn_ref[...], vc_ref[...]], axis=0)
    nloc = 3 * q.shape[0]
    ones = _ones_col(kcat.shape[0], NA_DIM)
    for h in range(NA_HEADS):
        hs = slice(h * NA_DIM, (h + 1) * NA_DIM)
        s = _dot_nt(q[:, hs], kcat[:, hs])
        s_loc = s[:, :nloc] + bias_ref[0, h]
        s_ctx = s[:, nloc:]
        m = jnp.maximum(jnp.max(s_loc, axis=-1, keepdims=True), jnp.max(s_ctx, axis=-1, keepdims=True))
        p = jnp.concatenate([jnp.exp2(s_loc - m), jnp.exp2(s_ctx - m)], axis=1).astype(CD)
        acc = _dot(p, jnp.concatenate([vcat[:, hs], ones], axis=1))
        o_ref[:, hs] = (acc[:, :NA_DIM] / acc[:, NA_DIM:NA_DIM + 1]).astype(o_ref.dtype)


def _na_bias(rpb, rows):
    qr = NA_QROWS
    tb = qr * GRID_W
    nb = rows // qr
    nkr = 3 * qr
    rsel = np.zeros((3, qr, nkr, 2 * NA_WIN_R - 1), np.float32)
    for v, b in enumerate((0, 1, nb - 1)):
        for a in range(qr):
            r = qr * b + a
            rs = min(max(r - NA_WIN_R // 2, 0), rows - NA_WIN_R)
            for u in range(nkr):
                kr = qr * (b - 1) + u
                if 0 <= b - 1 + u // qr < nb and rs <= kr < rs + NA_WIN_R:
                    rsel[v, a, u, kr - r + NA_WIN_R - 1] = 1.0
    csel = np.zeros((GRID_W, GRID_W, 2 * NA_WIN_C - 1), np.float32)
    for qc in range(GRID_W):
        cs = min(max(qc - NA_WIN_C // 2, 0), GRID_W - NA_WIN_C)
        for kc in range(cs, cs + NA_WIN_C):
            csel[qc, kc, kc - qc + NA_WIN_C - 1] = 1.0
    valid = (np.einsum('vaud,qkj->vaquk', rsel, csel) > 0).reshape(3, tb, 3 * tb)
    assert 2 * qr == NA_WIN_R
    nd = 2 * NA_WIN_R - 1
    rp = rpb.astype(F32) * LOG2E
    gap = jnp.zeros((NA_HEADS, nd, GRID_W + 1 - (2 * NA_WIN_C - 1)), F32)
    ring = jnp.concatenate([rp[..., NA_WIN_C - 1:], gap, rp[..., :NA_WIN_C - 1]], axis=-1)
    cols = jnp.tile(ring, (1, 1, GRID_W))[..., :GRID_W * GRID_W].reshape(NA_HEADS, nd, GRID_W, GRID_W)
    off = NA_WIN_R - 1 - qr
    slabs = [cols[:, off - a:off - a + nkr].transpose(0, 2, 1, 3).reshape(NA_HEADS, GRID_W, 3 * tb)
             for a in range(qr)]
    band = jnp.stack(slabs, axis=1).reshape(NA_HEADS, tb, 3 * tb)
    return jnp.where(valid[:, None], band[None], NEG)


def na_attn(p_lat, p_ctx, rpb, rows):
    s = p_lat.shape[0]
    c = p_ctx.shape[0]
    w = NA_HEADS * NA_DIM
    tb = NA_QROWS * GRID_W
    nb = s // tb
    assert rows % NA_QROWS == 0 and nb >= 3 and rows >= 2 * NA_WIN_R
    bias = _na_bias(rpb, rows)

    def blk(col, off):
        return pl.BlockSpec((tb, w), lambda i: (jnp.clip(i + off, 0, nb - 1), col))

    return pl.pallas_call(
        _na_kernel,
        grid=(nb,),
        in_specs=[blk(0, 0), blk(1, -1), blk(1, 0), blk(1, 1), blk(2, -1), blk(2, 0), blk(2, 1),
                  pl.BlockSpec((c, w), lambda i: (0, 1)),
                  pl.BlockSpec((c, w), lambda i: (0, 2)),
                  pl.BlockSpec((1, NA_HEADS, tb, 3 * tb),
                               lambda i: (jnp.where(i == 0, 0, jnp.where(i == nb - 1, 2, 1)), 0, 0, 0))],
        out_specs=pl.BlockSpec((tb, w), lambda i: (i, 0)),
        out_shape=jax.ShapeDtypeStruct((s, w), CD),
        compiler_params=_cparams("parallel"),
        name="na_attn",
    )(p_lat, p_lat, p_lat, p_lat, p_lat, p_lat, p_lat, p_ctx, p_ctx, bias)


def _sw_kernel(seq, q_ref, kp_ref, ko_ref, kn_ref, vp_ref, vo_ref, vn_ref, kc_ref, vc_ref, sink_ref, o_ref):
    i = pl.program_id(0)
    q = q_ref[...]
    tq = q.shape[0]
    kcat = jnp.concatenate([kp_ref[...], ko_ref[...], kn_ref[...], kc_ref[...]], axis=0)
    vcat = jnp.concatenate([vp_ref[...], vo_ref[...], vn_ref[...], vc_ref[...]], axis=0)
    nkk = tq + 2 * SW_WINDOW
    r = lax.broadcasted_iota(jnp.int32, (tq, nkk), 0)
    cc = lax.broadcasted_iota(jnp.int32, (tq, nkk), 1)
    kpos = i * tq - SW_WINDOW + cc
    d = cc - r
    mask = (d >= 0) & (d <= 2 * SW_WINDOW) & (kpos >= 0) & (kpos < seq)
    ones = _ones_col(kcat.shape[0], SW_DIM)
    grp = SW_HEADS // SW_KV_HEADS
    for g in range(SW_KV_HEADS):
        gs = slice(g * SW_DIM, (g + 1) * SW_DIM)
        kg = kcat[:, gs]
        v_aug = jnp.concatenate([vcat[:, gs], ones], axis=1)
        for h in range(g * grp, (g + 1) * grp):
            s = _dot_nt(q[:, h * SW_DIM:(h + 1) * SW_DIM], kg)
            s_loc = jnp.where(mask, s[:, :nkk], NEG)
            s_ctx = s[:, nkk:]
            sk = sink_ref[0:1, h:h + 1] * LOG2E
            m = jnp.maximum(jnp.maximum(jnp.max(s_loc, axis=-1, keepdims=True),
                                        jnp.max(s_ctx, axis=-1, keepdims=True)), sk)
            p = jnp.concatenate([jnp.exp2(s_loc - m), jnp.exp2(s_ctx - m)], axis=1).astype(CD)
            acc = _dot(p, v_aug)
            l = acc[:, SW_DIM:SW_DIM + 1] + jnp.exp2(sk - m)
            o_ref[:, h * SW_DIM:(h + 1) * SW_DIM] = (acc[:, :SW_DIM] / l).astype(o_ref.dtype)


def sw_attn(p_lat, p_ctx, sink):
    s = p_lat.shape[0]
    c = p_ctx.shape[0]
    tq = SW_TQ
    assert s % tq == 0 and tq % SW_WINDOW == 0
    nq = s // tq
    per = tq // SW_WINDOW
    nwb = s // SW_WINDOW
    kcol = SW_HEADS * SW_DIM // LANE
    prev = lambda col: pl.BlockSpec((SW_WINDOW, LANE), lambda i: (jnp.maximum(i * per - 1, 0), col))
    own = lambda col: pl.BlockSpec((tq, LANE), lambda i: (i, col))
    nxt = lambda col: pl.BlockSpec((SW_WINDOW, LANE), lambda i: (jnp.minimum((i + 1) * per, nwb - 1), col))
    return pl.pallas_call(
        functools.partial(_sw_kernel, s),
        grid=(nq,),
        in_specs=[pl.BlockSpec((tq, SW_HEADS * SW_DIM), lambda i: (i, 0)),
                  prev(kcol), own(kcol), nxt(kcol), prev(kcol + 1), own(kcol + 1), nxt(kcol + 1),
                  pl.BlockSpec((c, LANE), lambda i: (0, kcol)),
                  pl.BlockSpec((c, LANE), lambda i: (0, kcol + 1)),
                  pl.BlockSpec(sink.shape, lambda i: (0, 0))],
        out_specs=pl.BlockSpec((tq, SW_HEADS * SW_DIM), lambda i: (i, 0)),
        out_shape=jax.ShapeDtypeStruct((s, SW_HEADS * SW_DIM), CD),
        compiler_params=_cparams("parallel"),
        name="sw_attn",
    )(p_lat, p_lat, p_lat, p_lat, p_lat, p_lat, p_lat, p_ctx, p_ctx, sink)


def _ml_prep_kernel(nt, x_ref, xp_ref, xn_ref, g_ref, w_ref, b_ref, q_out, k_out, g_out):
    i = pl.program_id(0)
    x = x_ref[...]
    tm = x.shape[0]
    row = lax.broadcasted_iota(jnp.int32, x.shape, 0)
    has_prev = jnp.where(i > 0, 1.0, 0.0)
    has_next = jnp.where(i < nt - 1, 1.0, 0.0)
    prev_row = xp_ref[7:8, :] * has_prev
    next_row = xn_ref[0:1, :] * has_next
    xm = jnp.where(row == 0, prev_row, pltpu.roll(x, 1, axis=0))
    xq = jnp.where(row == tm - 1, next_row, pltpu.roll(x, tm - 1, axis=0))
    y = w_ref[0:1, :] * xm + w_ref[1:2, :] * x + w_ref[2:3, :] * xq
    y = y * _sigmoid(y)
    q_out[...] = y[:, :ML_QKW].astype(CD)
    k_out[...] = (y[:, ML_QKW:] * (ML_QK ** -0.5)).astype(CD)
    g = g_ref[...] + b_ref[...]
    lane = lax.broadcasted_iota(jnp.int32, g.shape, 1)
    logsig = jnp.minimum(g, 0.0) - jnp.log(1.0 + jnp.exp(-jnp.abs(g)))
    g_out[...] = jnp.where(lane < 2 * ML_HEADS, g, logsig)


def ml_prep(p_ml, conv_w, gate_b):
    t = p_ml.shape[0]
    tm = _row_tile(t, 256)
    nt = t // tm
    wq = 2 * ML_QKW
    gcol = (2 * ML_QKW + 2 * ML_VW) // LANE
    w8 = jnp.zeros((8, wq), F32).at[:ML_CONV].set(conv_w.astype(F32))
    b = jnp.zeros((1, LANE), F32).at[0, :4 * ML_HEADS].set(gate_b.astype(F32).reshape(-1))
    h8 = tm // 8
    return pl.pallas_call(
        functools.partial(_ml_prep_kernel, nt),
        grid=(nt,),
        in_specs=[pl.BlockSpec((tm, wq), lambda i: (i, 0)),
                  pl.BlockSpec((8, wq), lambda i: (jnp.maximum(i * h8 - 1, 0), 0)),
                  pl.BlockSpec((8, wq), lambda i: (jnp.minimum((i + 1) * h8, t // 8 - 1), 0)),
                  pl.BlockSpec((tm, LANE), lambda i: (i, gcol)),
                  pl.BlockSpec((8, wq), lambda i: (0, 0)),
                  pl.BlockSpec((1, LANE), lambda i: (0, 0))],
        out_specs=[pl.BlockSpec((tm, ML_QKW), lambda i: (i, 0)),
                   pl.BlockSpec((tm, ML_QKW), lambda i: (i, 0)),
                   pl.BlockSpec((tm, LANE), lambda i: (i, 0))],
        out_shape=[jax.ShapeDtypeStruct((t, ML_QKW), CD), jax.ShapeDtypeStruct((t, ML_QKW), CD),
                   jax.ShapeDtypeStruct((t, LANE), F32)],
        compiler_params=_cparams("parallel"),
        name="ml_prep",
    )(p_ml, p_ml, p_ml, p_ml, w8, b)


def _ml_scan_kernel(qf_ref, ktf_ref, vf_ref, gcf_ref, grf_ref,
                    qb_ref, ktb_ref, vb_ref, gcb_ref, grb_ref,
                    c0_ref, m0_ref, hf_ref, hb_ref, c_ref, m_ref):
    @pl.when(pl.program_id(0) == 0)
    def _():
        c_ref[...] = c0_ref[...]
        m_ref[...] = m0_ref[...]

    ln = qf_ref.shape[0]
    r = lax.broadcasted_iota(jnp.int32, (ln, ln), 0)
    c = lax.broadcasted_iota(jnp.int32, (ln, ln), 1)
    lower = r >= c
    upper = c >= r
    lower_f = jnp.where(lower, 1.0, 0.0)
    upper_f = jnp.where(upper, 1.0, 0.0)
    ones_col = jnp.where(lax.broadcasted_iota(jnp.int32, (ln, LANE), 1) == 0, 1.0, 0.0).astype(CD)
    hi = lax.Precision.HIGHEST

    streams = ((qf_ref, ktf_ref, vf_ref, gcf_ref, grf_ref, hf_ref, lower, lower_f, upper_f, ln - 1),
               (qb_ref, ktb_ref, vb_ref, gcb_ref, grb_ref, hb_ref, upper, upper_f, lower_f, 0))
    for d, (q_ref, kt_ref, v_ref, gc_ref, gr_ref, h_ref, mask, tri_c, tri_r, last) in enumerate(streams):
        gcol = gc_ref[...]
        grow = gr_ref[...]
        bcol = jnp.dot(tri_c, gcol, precision=hi, preferred_element_type=F32)
        brow = jnp.dot(grow, tri_r, precision=hi, preferred_element_type=F32)
        q = q_ref[...]
        kt = kt_ref[...]
        v = v_ref[...]
        for hd in range(ML_HEADS):
            ch = d * ML_HEADS + hd
            ic = d * ML_HEADS + hd
            fc = 2 * ML_HEADS + ic
            i_c = gcol[:, ic:ic + 1]
            b_c = bcol[:, fc:fc + 1]
            i_r = grow[ic:ic + 1, :]
            b_r = brow[fc:fc + 1, :]
            m_prev = m_ref[ch, 0:1, 0:1]
            cst = c_ref[ch]
            dmat = jnp.where(mask, b_c - b_r + i_r, NEG)
            inter = b_c + m_prev
            m_t = jnp.maximum(inter, jnp.max(dmat, axis=-1, keepdims=True))
            qh = q[:, hd * ML_QK:(hd + 1) * ML_QK]
            kth = kt[hd * ML_QK:(hd + 1) * ML_QK, :]
            s = lax.dot_general(qh, kth, (((1,), (0,)), ((), ())), preferred_element_type=F32)
            s = s * jnp.exp(dmat - m_t)
            w_inter = jnp.exp(inter - m_t)
            v_aug = jnp.concatenate([v[:, hd * ML_V:(hd + 1) * ML_V].astype(CD), ones_col], axis=1)
            num = _dot(s.astype(CD), v_aug) + w_inter * _dot(qh, cst.astype(CD))
            den = num[:, ML_V:ML_V + 1]
            h = num[:, :ML_V] / jnp.maximum(jnp.abs(den), jnp.exp(-m_t))
            h_ref[:, hd * ML_V:(hd + 1) * ML_V] = h
            b_end = b_c[last:last + 1, :]
            m_new = jnp.maximum(b_end + m_prev, jnp.max(b_end - b_c + i_c, axis=0, keepdims=True))
            w_s = jnp.exp(b_end - b_r + i_r - m_new)
            w_c = jnp.exp(b_end + m_prev - m_new)
            ktw = (kth.astype(F32) * w_s).astype(CD)
            c_ref[ch] = w_c * cst + _dot(ktw, v_aug)
            m_ref[ch] = jnp.broadcast_to(m_new, m_ref.shape[1:])


def ml_scan(qc, kc, p_ml, gates, state):
    t = qc.shape[0]
    ln = min(ML_CHUNK, t)
    assert t % ln == 0
    nc = t // ln
    kt = kc.T
    gr = gates[:, :4 * ML_HEADS].T
    c0, m0 = state
    vcol = 2 * ML_QKW // ML_VW
    fwd = lambda j: j
    bwd = lambda j: nc - 1 - j

    def specs(ix):
        return [pl.BlockSpec((ln, ML_QKW), lambda j: (ix(j), 0)),
                pl.BlockSpec((ML_QKW, ln), lambda j: (0, ix(j))),
                pl.BlockSpec((ln, ML_VW), lambda j: (ix(j), vcol)),
                pl.BlockSpec((ln, LANE), lambda j: (ix(j), 0)),
                pl.BlockSpec((4 * ML_HEADS, ln), lambda j: (0, ix(j)))]

    st_specs = [pl.BlockSpec(c0.shape, lambda j: (0, 0, 0)), pl.BlockSpec(m0.shape, lambda j: (0, 0, 0))]
    return pl.pallas_call(
        _ml_scan_kernel,
        grid=(nc,),
        in_specs=specs(fwd) + specs(bwd) + st_specs,
        out_specs=[pl.BlockSpec((ln, ML_VW), lambda j: (fwd(j), 0)),
                   pl.BlockSpec((ln, ML_VW), lambda j: (bwd(j), 0))] + st_specs,
        out_shape=[jax.ShapeDtypeStruct((t, ML_VW), F32), jax.ShapeDtypeStruct((t, ML_VW), F32),
                   jax.ShapeDtypeStruct(c0.shape, F32), jax.ShapeDtypeStruct(m0.shape, F32)],
        compiler_params=_cparams("arbitrary"),
        name="ml_scan",
    )(qc, kt, p_ml, gates, gr, qc, kt, p_ml, gates, gr, c0, m0)


def _ml_finish_kernel(hf_ref, hb_ref, o_ref, g_ref, y_ref):
    h = hf_ref[...] + hb_ref[...]
    og = _sigmoid(o_ref[...])
    g = g_ref[...]
    for hd in range(ML_HEADS):
        sl = slice(hd * ML_V, (hd + 1) * ML_V)
        hh = h[:, sl]
        y = hh * lax.rsqrt(jnp.mean(hh * hh, axis=-1, keepdims=True) + EPS) * g[:, sl]
        y_ref[:, sl] = (y * og[:, sl]).astype(y_ref.dtype)


def ml_finish(hf, hb, p_ml, norm_g):
    t = hf.shape[0]
    tm = _row_tile(t, 256)
    ocol = (2 * ML_QKW + ML_VW) // ML_VW
    blk = pl.BlockSpec((tm, ML_VW), lambda i: (i, 0))
    return pl.pallas_call(
        _ml_finish_kernel,
        grid=(t // tm,),
        in_specs=[blk, blk, pl.BlockSpec((tm, ML_VW), lambda i: (i, ocol)),
                  pl.BlockSpec((1, ML_VW), lambda i: (0, 0))],
        out_specs=blk,
        out_shape=jax.ShapeDtypeStruct((t, ML_VW), CD),
        compiler_params=_cparams("parallel"),
        name="ml_finish",
    )(hf, hb, p_ml, norm_g.reshape(1, ML_VW).astype(F32))


def _merge_kernel(h_ref, ya_ref, yb_ref, yc_ref, yd_ref, g0_ref, g1_ref, g2_ref, g3_ref, wb_ref, o_ref):
    h = h_ref[...]
    acc = None
    for i, (y_ref, g_ref) in enumerate(zip((ya_ref, yb_ref, yc_ref, yd_ref), (g0_ref, g1_ref, g2_ref, g3_ref))):
        term = _sigmoid(_dot(h, g_ref[...])) * _dot(y_ref[...], wb_ref[0, i])
        acc = term if acc is None else acc + term
    o_ref[...] = acc.astype(o_ref.dtype)


def gated_merge(h, ys, w_gate, w_branch, l, tn=512):
    t, d = h.shape
    tm = _row_tile(t)
    nj = d // tn
    yspec = pl.BlockSpec((tm, BRANCH_W), lambda j, i: (i, 0))
    gspec = lambda b: pl.BlockSpec((d, tn), lambda j, i: (0, b * nj + j))
    return pl.pallas_call(
        _merge_kernel,
        grid=(nj, t // tm),
        in_specs=[pl.BlockSpec((tm, d), lambda j, i: (i, 0)), yspec, yspec, yspec, yspec,
                  gspec(0), gspec(1), gspec(2), gspec(3),
                  pl.BlockSpec((1, N_BRANCH, BRANCH_W, tn), lambda j, i: (l, 0, 0, j))],
        out_specs=pl.BlockSpec((tm, tn), lambda j, i: (i, j)),
        out_shape=jax.ShapeDtypeStruct((t, d), CD),
        compiler_params=_cparams("parallel", "parallel"),
        name="gated_merge",
    )(h, *ys, w_gate, w_gate, w_gate, w_gate, w_branch)


def _ffn1_kernel(h_ref, wa_ref, wg_ref, o_ref, wa_sc, wg_sc):
    @pl.when(pl.program_id(1) == 0)
    def _():
        wa_sc[...] = wa_ref[0].astype(CD)
        wg_sc[...] = wg_ref[0].astype(CD)

    h = h_ref[...]
    a = _dot(h, wa_sc[...])
    g = _dot(h, wg_sc[...])
    o_ref[...] = (a * _sigmoid(a) * g).astype(o_ref.dtype)


def ffn_up(h, w, l, tn=512):
    t, d = h.shape
    dff = w.shape[2] // 2
    tm = _row_tile(t)
    nj = dff // tn
    return pl.pallas_call(
        _ffn1_kernel,
        grid=(nj, t // tm),
        in_specs=[pl.BlockSpec((tm, d), lambda j, i: (i, 0)),
                  pl.BlockSpec((1, d, tn), lambda j, i: (l, 0, j)),
                  pl.BlockSpec((1, d, tn), lambda j, i: (l, 0, nj + j))],
        out_specs=pl.BlockSpec((tm, tn), lambda j, i: (i, j)),
        out_shape=jax.ShapeDtypeStruct((t, dff), CD),
        scratch_shapes=[pltpu.VMEM((d, tn), CD), pltpu.VMEM((d, tn), CD)],
        compiler_params=_cparams("parallel", "arbitrary"),
        name="ffn_up",
    )(h, w, w)


def _rot_cols(w, dim):
    k, n = w.shape
    w4 = w.reshape(k, n // dim, 2, dim // 2)
    return jnp.stack([-w4[:, :, 1], w4[:, :, 0]], axis=2).reshape(k, n)


def _rope_tables(n_tokens, dim):
    a = dim // 2
    inv = 1.0 / (ROPE_THETA ** (jnp.arange(0, a, 2, dtype=F32) / a))
    rows = n_tokens // GRID_W
    assert rows * GRID_W == n_tokens
    ang_r = jnp.arange(rows, dtype=F32)[:, None] * inv
    ang_c = jnp.arange(GRID_W, dtype=F32)[:, None] * inv

    def expand(fr, fc):
        shape = (rows, GRID_W, a // 2)
        return jnp.concatenate([jnp.broadcast_to(fr[:, None, :], shape), jnp.broadcast_to(fc[None], shape)],
                               axis=-1).reshape(n_tokens, a)

    return expand(jnp.cos(ang_r), jnp.cos(ang_c)), expand(jnp.sin(ang_r), jnp.sin(ang_c))


def _layer_weights(w_in, w_in_c, l, mla_w_uq_l, mla_w_ukv_l):
    d = w_in.shape[1]
    o_na = ML_COLS
    o_sw = o_na + NA_COLS
    o_mla = o_sw + SW_COLS
    o_gate = o_mla + MLA_COLS
    f32c = lambda a, b: w_in[l, :, a:b]
    cdc = lambda a, b: w_in_c[l, :, a:b]
    w = {}
    w["ml"] = jnp.pad(cdc(0, ML_COLS), ((0, 0), (0, ML_PAD - ML_COLS)))
    nq = NA_HEADS * NA_DIM
    w["na"] = jnp.concatenate([(f32c(o_na, o_na + nq) * (NA_DIM ** -0.5 * LOG2E)).astype(CD),
                               cdc(o_na + nq, o_sw)], axis=1)
    sq = SW_HEADS * SW_DIM
    skv = SW_KV_HEADS * SW_DIM
    sw_q = f32c(o_sw, o_sw + sq) * (SW_DIM ** -0.5 * LOG2E)
    sw_k = f32c(o_sw + sq, o_sw + sq + skv)
    w["sw"] = jnp.concatenate([sw_q.astype(CD), cdc(o_sw + sq, o_mla)], axis=1)
    w["sw_rot"] = jnp.concatenate([_rot_cols(sw_q, SW_DIM), _rot_cols(sw_k, SW_DIM)], axis=1).astype(CD)
    o_kr = o_mla + MLA_Q_RANK + MLA_KV_RANK
    z = jnp.zeros((d, LANE - MLA_ROPE), CD)
    w["mla1"] = jnp.concatenate([cdc(o_mla, o_gate), z, _rot_cols(f32c(o_kr, o_gate), MLA_ROPE).astype(CD), z],
                                axis=1)
    uq = mla_w_uq_l.reshape(MLA_Q_RANK, MLA_HEADS, MLA_NOPE + MLA_ROPE)
    zq = jnp.zeros((MLA_Q_RANK, MLA_HEADS, MLA_HW - MLA_NOPE - MLA_ROPE), F32)
    w["mla_q"] = jnp.concatenate([uq, zq], axis=2).reshape(MLA_Q_RANK, -1).astype(CD)
    uqr = _rot_cols(uq[:, :, MLA_NOPE:].reshape(MLA_Q_RANK, -1), MLA_ROPE).reshape(MLA_Q_RANK, MLA_HEADS, MLA_ROPE)
    w["mla_q_rot"] = jnp.concatenate([jnp.zeros_like(uq[:, :, :MLA_NOPE]), uqr, zq], axis=2
                                     ).reshape(MLA_Q_RANK, -1).astype(CD)
    w["mla_kv"] = mla_w_ukv_l.astype(CD)
    w["gate"] = cdc(o_gate, w_in.shape[2])
    return w


def _tables(s, c):
    cos_sw, sin_sw = _rope_tables(s, SW_DIM)
    cos_m, sin_m = _rope_tables(s, MLA_ROPE)
    tile2 = lambda a: jnp.concatenate([a, a, a, a], axis=1)
    scale = (MLA_NOPE + MLA_ROPE) ** -0.5 * LOG2E
    zpad = jnp.zeros((s, MLA_HW - MLA_NOPE - MLA_ROPE), F32)
    lat = dict(
        sw_cos=tile2(cos_sw), sw_sin=tile2(sin_sw),
        q_cos=scale * jnp.concatenate([jnp.ones((s, MLA_NOPE), F32), cos_m, cos_m, zpad], axis=1),
        q_sin=scale * jnp.concatenate([jnp.zeros((s, MLA_NOPE), F32), sin_m, sin_m, zpad], axis=1),
        k_cos=jnp.concatenate([cos_m, cos_m, zpad], axis=1),
        k_sin=jnp.concatenate([sin_m, sin_m, zpad], axis=1))
    ctx = dict(
        sw_cos=jnp.ones((c, LANE), F32), sw_sin=jnp.zeros((c, LANE), F32),
        q_cos=jnp.full((c, MLA_HW), scale, F32), q_sin=jnp.zeros((c, MLA_HW), F32),
        k_cos=jnp.ones((c, LANE), F32), k_sin=jnp.zeros((c, LANE), F32))
    return lat, ctx


def _project(h, w, tab, gq, gkv):
    p_ml = matmul(h, w["ml"], F32)
    p_na = matmul(h, w["na"], CD)
    n_rope = (SW_HEADS + SW_KV_HEADS) * SW_DIM // LANE
    p_sw = rope_matmul(h, w["sw"], w["sw_rot"], tab["sw_cos"], tab["sw_sin"], n_rope)
    mq, mk, mv = mla_proj(h, w["mla1"], gq, gkv, w["mla_q"], w["mla_q_rot"], w["mla_kv"],
                          tab["q_cos"], tab["q_sin"], tab["k_cos"], tab["k_sin"])
    return p_ml, p_na, p_sw, (mq, mk, mv)


def _dense_tail(x, h, ys, w, l, w_branch, w_out, w_ffn_in, w_ffn_out, vt, g1, norm2, g2, norm_next, last):
    merged = gated_merge(h, ys, w["gate"], w_branch, l)
    x, h2 = matmul_residual_norm(merged, w_out, l, x, vt, g1, norm2, True, CD)
    u = ffn_up(h2, w_ffn_in, l)
    if last:
        return None, matmul_residual_norm(u, w_ffn_out, l, x, vt, g2, norm_next, False, F32, tn=512)
    return matmul_residual_norm(u, w_ffn_out, l, x, vt, g2, norm_next, True, CD, tn=512)


def kernel(x, c, ctx, c_ctx, w_ada, b_ada, norm1_g, norm2_g, w_in, ml_conv_w, ml_gate_b, ml_norm_g, na_rpb,
           sw_sink, mla_q_norm_g, mla_kv_norm_g, mla_w_uq, mla_w_ukv, w_branch, w_out, w_ffn_in, w_ffn_out,
           final_norm_g):
    bsz, s, d = x.shape
    assert bsz == 1
    depth = w_in.shape[0]
    n_ctx = ctx.shape[1]
    rows = s // GRID_W
    xl = x[0].astype(F32)
    xc = ctx[0].astype(F32)
    mod = adaln(jnp.stack([c[0], c_ctx], axis=1).astype(F32), w_ada, b_ada)
    vt = jnp.concatenate([mod.reshape(depth * 12, d), norm1_g.astype(F32), norm2_g.astype(F32),
                          final_norm_g.reshape(1, d).astype(F32), jnp.zeros((1, d), F32)]).reshape(-1, 1, d)
    mod_row = lambda l, stream, part: (l * 2 + stream) * 6 + part
    n1_row = lambda l: depth * 12 + l
    n2_row = lambda l: depth * 13 + l
    fin_row, zero_row = depth * 14, depth * 14 + 1
    norm1 = lambda l, st: (n1_row(l), mod_row(l, st, 0), mod_row(l, st, 1))
    norm2 = lambda l, st: (n2_row(l), mod_row(l, st, 3), mod_row(l, st, 4))
    tab_l, tab_c = _tables(s, n_ctx)
    zero_state = (jnp.zeros((2 * ML_HEADS, ML_QK, 2 * ML_V), F32), jnp.zeros((2 * ML_HEADS, 8, LANE), F32))

    w_in_c = w_in.astype(CD)
    wb = w_branch.astype(CD)
    wo = w_out.astype(CD)
    wf2 = w_ffn_out.astype(CD)

    hl = norm_mod(xl, vt, norm1(0, 0), CD)
    hc = norm_mod(xc, vt, norm1(0, 1), CD)
    for l in range(depth):
        last = l == depth - 1
        w = _layer_weights(w_in, w_in_c, l, mla_w_uq[l], mla_w_ukv[l])
        gq = mla_q_norm_g[l].reshape(1, -1).astype(F32)
        gkv = mla_kv_norm_g[l].reshape(1, -1).astype(F32)
        sink = sw_sink[l].reshape(1, -1).astype(F32)

        ml_l, na_l, sw_l, (mq_l, mk_l, mv_l) = _project(hl, w, tab_l, gq, gkv)
        ml_c, na_c, sw_c, (mq_c, mk_c, mv_c) = _project(hc, w, tab_c, gq, gkv)

        qc_c, kc_c, gt_c = ml_prep(ml_c, ml_conv_w[l], ml_gate_b[l])
        qc_l, kc_l, gt_l = ml_prep(ml_l, ml_conv_w[l], ml_gate_b[l])
        hf_c, hb_c, cst, mst = ml_scan(qc_c, kc_c, ml_c, gt_c, zero_state)
        hf_l, hb_l, _, _ = ml_scan(qc_l, kc_l, ml_l, gt_l, (cst, mst))
        ya_l = ml_finish(hf_l, hb_l, ml_l, ml_norm_g[l])
        yb_l = na_attn(na_l, na_c, na_rpb[l], rows)
        yc_l = sw_attn(sw_l, sw_c, sink)
        yd_l = mla_flash(mq_l, jnp.concatenate([mk_l, mk_c]), jnp.concatenate([mv_l, mv_c]))

        nxt_l = (fin_row, zero_row, zero_row) if last else norm1(l + 1, 0)
        xl_new, hl_new = _dense_tail(xl, hl, (ya_l, yb_l, yc_l, yd_l), w, l, wb, wo, w_ffn_in, wf2, vt,
                                     mod_row(l, 0, 2), norm2(l, 0), mod_row(l, 0, 5), nxt_l, last)
        if not last:
            ya_c = ml_finish(hf_c, hb_c, ml_c, ml_norm_g[l])
            nw = NA_HEADS * NA_DIM
            yb_c = dense_attn([(na_c, nw, 0), (na_c, nw, 1), (na_c, nw, 2)], NA_HEADS, 1, NA_DIM, NA_DIM,
                              log2_scores=True)
            kcol = SW_HEADS * SW_DIM // LANE
            yc_c = dense_attn([(sw_c, SW_HEADS * SW_DIM, 0), (sw_c, LANE, kcol), (sw_c, LANE, kcol + 1)],
                              SW_HEADS, SW_HEADS // SW_KV_HEADS, SW_DIM, SW_DIM, sink=sink, log2_scores=True)
            yd_c = dense_attn([(mq_c, MLA_HEADS * MLA_HW, 0), (mk_c, MLA_HEADS * MLA_HW, 0),
                               (mv_c, MLA_HEADS * MLA_V, 0)], MLA_HEADS, 1, MLA_HW, MLA_V, log2_scores=True)
            xc, hc = _dense_tail(xc, hc, (ya_c, yb_c, yc_c, yd_c), w, l, wb, wo, w_ffn_in, wf2, vt,
                                 mod_row(l, 1, 2), norm2(l, 1), mod_row(l, 1, 5), norm1(l + 1, 1), False)
        xl, hl = xl_new, hl_new

    return hl[None].astype(x.dtype)
```

```python
import functools

import numpy as np
import jax
import jax.numpy as jnp
from jax import lax
from jax.experimental import pallas as pl
from jax.experimental.pallas import tpu as pltpu

F32 = jnp.float32
CD = jnp.bfloat16

GRID_W = 64
EPS = 1e-6
ROPE_THETA = 10000.0
ML_HEADS, ML_QK, ML_V, ML_CONV = 4, 64, 128, 3
NA_HEADS, NA_DIM, NA_WIN_R, NA_WIN_C = 8, 64, 8, 16
SW_HEADS, SW_KV_HEADS, SW_DIM, SW_WINDOW = 8, 2, 64, 128
MLA_HEADS, MLA_Q_RANK, MLA_KV_RANK, MLA_NOPE, MLA_ROPE, MLA_V = 4, 384, 256, 128, 64, 128
N_BRANCH, BRANCH_W = 4, 512

ML_QKW = ML_HEADS * ML_QK
ML_VW = ML_HEADS * ML_V
ML_COLS = 2 * ML_QKW + 2 * ML_VW + 4 * ML_HEADS
ML_PAD = 1664
NA_COLS = 3 * NA_HEADS * NA_DIM
SW_COLS = (SW_HEADS + 2 * SW_KV_HEADS) * SW_DIM
MLA_COLS = MLA_Q_RANK + MLA_KV_RANK + MLA_ROPE
MLA_HW = 256
LANE = 128
NEG = -1e30
LOG2E = float(np.log2(np.e))
VMEM_LIMIT = 56 * 1024 * 1024

NA_QROWS = 4
SW_TQ = 256
ML_CHUNK = 256
MLA_TK = 1280


def _cparams(*sem):
    return pltpu.CompilerParams(dimension_semantics=sem, vmem_limit_bytes=VMEM_LIMIT)


def _row_tile(t, pref=512):
    tm = min(pref, t)
    assert t % tm == 0
    return tm


def _dot(a, b):
    return jnp.dot(a, b, preferred_element_type=F32)


def _dot_nt(a, b):
    return lax.dot_general(a, b, (((1,), (1,)), ((), ())), preferred_element_type=F32)


def _sigmoid(x):
    return 1.0 / (1.0 + jnp.exp(-x))


def _ones_col(n, w):
    return jnp.where(lax.broadcasted_iota(jnp.int32, (n, w), 1) == 0, 1.0, 0.0).astype(CD)


def _adaln_kernel(c_ref, w_ref, b_ref, o_ref):
    w = w_ref[0]
    for r in range(2):
        c = c_ref[:, r:r + 1]
        o_ref[0, r:r + 1, :] = jnp.sum((c * _sigmoid(c)) * w, axis=0, keepdims=True) + b_ref[0]


def adaln(cc, w_ada, b_ada):
    nl, d, n = w_ada.shape
    tn = 1024
    return pl.pallas_call(
        _adaln_kernel,
        grid=(nl, n // tn),
        in_specs=[pl.BlockSpec((d, 2), lambda l, j: (0, 0)),
                  pl.BlockSpec((1, d, tn), lambda l, j: (l, 0, j)),
                  pl.BlockSpec((1, 1, tn), lambda l, j: (l, 0, j))],
        out_specs=pl.BlockSpec((1, 2, tn), lambda l, j: (l, 0, j)),
        out_shape=jax.ShapeDtypeStruct((nl, 2, n), F32),
        compiler_params=_cparams("parallel", "parallel"),
        name="adaln",
    )(cc, w_ada, b_ada.reshape(nl, 1, n))


def _vec_spec(d, idx):
    return pl.BlockSpec((1, 1, d), lambda *_: (idx, 0, 0))


def _norm_mod(x, g, shift, scale):
    y = x * lax.rsqrt(jnp.mean(x * x, axis=-1, keepdims=True) + EPS)
    return (y * g) * (1.0 + scale) + shift


def _norm_mod_kernel(x_ref, g_ref, sh_ref, sc_ref, o_ref):
    o_ref[...] = _norm_mod(x_ref[...], g_ref[0], sh_ref[0], sc_ref[0]).astype(o_ref.dtype)


def norm_mod(x, vt, norm_idx, out_dtype):
    t, d = x.shape
    tm = _row_tile(t, 256)
    return pl.pallas_call(
        _norm_mod_kernel,
        grid=(t // tm,),
        in_specs=[pl.BlockSpec((tm, d), lambda i: (i, 0))] + [_vec_spec(d, ix) for ix in norm_idx],
        out_specs=pl.BlockSpec((tm, d), lambda i: (i, 0)),
        out_shape=jax.ShapeDtypeStruct((t, d), out_dtype),
        compiler_params=_cparams("parallel"),
        name="norm_mod",
    )(x, vt, vt, vt)


def _mm_kernel(a_ref, b_ref, o_ref):
    o_ref[...] = _dot(a_ref[...], b_ref[...]).astype(o_ref.dtype)


def matmul(a, b, out_dtype, tn=None):
    t, k = a.shape
    n = b.shape[1]
    tn = n if tn is None else tn
    tm = _row_tile(t)
    return pl.pallas_call(
        _mm_kernel,
        grid=(n // tn, t // tm),
        in_specs=[pl.BlockSpec((tm, k), lambda j, i: (i, 0)),
                  pl.BlockSpec((k, tn), lambda j, i: (0, j))],
        out_specs=pl.BlockSpec((tm, tn), lambda j, i: (i, j)),
        out_shape=jax.ShapeDtypeStruct((t, n), out_dtype),
        compiler_params=_cparams("parallel", "parallel"),
        name="matmul",
    )(a, b)


def _mm_res_norm_kernel(nj, tn, want_x, a_ref, b_ref, x_ref, gate_ref, g_ref, sh_ref, sc_ref, *rest):
    xbuf, ho_ref = (rest[0], rest[1]) if want_x else (rest[1], rest[0])
    j = pl.program_id(1)
    acc = _dot(a_ref[...], b_ref[0])
    for jj in range(nj):
        @pl.when(j == jj)
        def _(jj=jj):
            sl = slice(jj * tn, (jj + 1) * tn)
            xbuf[:, sl] = x_ref[:, sl] + gate_ref[0][:, sl] * acc

    @pl.when(j == nj - 1)
    def _():
        ho_ref[...] = _norm_mod(xbuf[...], g_ref[0], sh_ref[0], sc_ref[0]).astype(ho_ref.dtype)


def matmul_residual_norm(a, b, l, x, vt, gate_idx, norm_idx, want_x, h_dtype, tn=None):
    t, kdim = a.shape
    n = b.shape[2]
    tn = n if tn is None else tn
    nj = n // tn
    tm = _row_tile(t)
    row = pl.BlockSpec((tm, n), lambda i, j: (i, 0))
    out_shape = [jax.ShapeDtypeStruct((t, n), h_dtype)]
    if want_x:
        out_shape = [jax.ShapeDtypeStruct((t, n), F32)] + out_shape
    out = pl.pallas_call(
        functools.partial(_mm_res_norm_kernel, nj, tn, want_x),
        grid=(t // tm, nj),
        in_specs=[pl.BlockSpec((tm, kdim), lambda i, j: (i, 0)),
                  pl.BlockSpec((1, kdim, tn), lambda i, j: (l, 0, j)),
                  row, _vec_spec(n, gate_idx)] + [_vec_spec(n, ix) for ix in norm_idx],
        out_specs=[row] * len(out_shape),
        out_shape=out_shape,
        scratch_shapes=[] if want_x else [pltpu.VMEM((tm, n), F32)],
        compiler_params=_cparams("parallel", "arbitrary"),
        name="matmul_residual_norm",
    )(a, b, x, vt, vt, vt, vt)
    return out if want_x else out[0]


def _rope_mm_kernel(n_rope, a_ref, w_ref, wr_ref, cos_ref, sin_ref, o_ref):
    a = a_ref[...]
    p = _dot(a, w_ref[...])
    pr = _dot(a, wr_ref[...])
    cos = cos_ref[...]
    sin = sin_ref[...]
    for j in range(p.shape[1] // LANE):
        sl = slice(j * LANE, (j + 1) * LANE)
        if j < n_rope:
            o_ref[:, sl] = (p[:, sl] * cos + pr[:, sl] * sin).astype(o_ref.dtype)
        else:
            o_ref[:, sl] = p[:, sl].astype(o_ref.dtype)


def rope_matmul(a, w, w_rot, cos, sin, n_rope):
    t, k = a.shape
    n = w.shape[1]
    nr = n_rope * LANE
    tm = _row_tile(t)
    return pl.pallas_call(
        functools.partial(_rope_mm_kernel, n_rope),
        grid=(t // tm,),
        in_specs=[pl.BlockSpec((tm, k), lambda i: (i, 0)),
                  pl.BlockSpec((k, n), lambda i: (0, 0)),
                  pl.BlockSpec((k, nr), lambda i: (0, 0)),
                  pl.BlockSpec((tm, LANE), lambda i: (i, 0)),
                  pl.BlockSpec((tm, LANE), lambda i: (i, 0))],
        out_specs=pl.BlockSpec((tm, n), lambda i: (i, 0)),
        out_shape=jax.ShapeDtypeStruct((t, n), CD),
        compiler_params=_cparams("parallel"),
        name="rope_matmul",
    )(a, w, w_rot, cos, sin)


def _mla_proj_kernel(h_ref, w1_ref, gq_ref, gkv_ref, wq_ref, wqr_ref, wkv_ref,
                     cq_ref, sq_ref, ck_ref, sk_ref, q_out, k_out, v_out):
    p = _dot(h_ref[...], w1_ref[...])
    cq = p[:, :MLA_Q_RANK]
    ckv = p[:, MLA_Q_RANK:MLA_Q_RANK + MLA_KV_RANK]
    o = MLA_Q_RANK + MLA_KV_RANK
    kr = p[:, o:o + LANE]
    krr = p[:, o + LANE:o + 2 * LANE]
    cqn = (cq * lax.rsqrt(jnp.mean(cq * cq, axis=-1, keepdims=True) + EPS) * gq_ref[...]).astype(CD)
    ckvn = (ckv * lax.rsqrt(jnp.mean(ckv * ckv, axis=-1, keepdims=True) + EPS) * gkv_ref[...]).astype(CD)
    qa = _dot(cqn, wq_ref[...])
    qb = _dot(cqn, wqr_ref[...])
    kv = _dot(ckvn, wkv_ref[...])
    krp = (kr * ck_ref[...] + krr * sk_ref[...]).astype(CD)
    cq_t = cq_ref[...]
    sq_t = sq_ref[...]
    for h in range(MLA_HEADS):
        qs = slice(h * MLA_HW, (h + 1) * MLA_HW)
        q_out[:, qs] = (qa[:, qs] * cq_t + qb[:, qs] * sq_t).astype(CD)
        k_out[:, h * MLA_HW:h * MLA_HW + MLA_NOPE] = kv[:, h * 256:h * 256 + MLA_NOPE].astype(CD)
        k_out[:, h * MLA_HW + MLA_NOPE:(h + 1) * MLA_HW] = krp
        v_out[:, h * MLA_V:(h + 1) * MLA_V] = kv[:, h * 256 + MLA_NOPE:(h + 1) * 256].astype(CD)


def mla_proj(h, w1, gq, gkv, wq, wqr, wkv, cos_q, sin_q, cos_k, sin_k, total_rows, into=None):
    t, d = h.shape
    tm = _row_tile(t)
    full = lambda a: pl.BlockSpec(a.shape, lambda i: (0,) * a.ndim)
    rows = lambda w: pl.BlockSpec((tm, w), lambda i: (i, 0))
    hw = MLA_HEADS * MLA_HW
    vw = MLA_HEADS * MLA_V
    args = [h, w1, gq, gkv, wq, wqr, wkv, cos_q, sin_q, cos_k, sin_k]
    in_specs = [rows(d), full(w1), full(gq), full(gkv), full(wq), full(wqr), full(wkv),
                rows(MLA_HW), rows(MLA_HW), rows(LANE), rows(LANE)]
    if into is None:
        off, aliases, kern = 0, {}, _mla_proj_kernel
    else:
        assert (total_rows - t) % tm == 0
        off = (total_rows - t) // tm
        aliases = {len(args) + i: i for i in range(3)}
        in_specs = in_specs + [pl.BlockSpec(memory_space=pl.ANY)] * 3
        args = args + list(into)
        kern = lambda *refs: _mla_proj_kernel(*refs[:11], *refs[14:])
    orow = lambda w: pl.BlockSpec((tm, w), lambda i: (off + i, 0))
    return pl.pallas_call(
        kern,
        grid=(t // tm,),
        in_specs=in_specs,
        out_specs=[orow(hw), orow(hw), orow(vw)],
        out_shape=[jax.ShapeDtypeStruct((total_rows, hw), CD), jax.ShapeDtypeStruct((total_rows, hw), CD),
                   jax.ShapeDtypeStruct((total_rows, vw), CD)],
        input_output_aliases=aliases,
        compiler_params=_cparams("parallel"),
        name="mla_proj",
    )(*args)


def _flash_kernel(tq, tk, unroll, q_ref, k_ref, v_ref, o_ref, s_sc, m_sc, acc_sc):
    nkb = k_ref.shape[0] // tk
    nsub = q_ref.shape[0] // tq
    total = nsub * nkb
    m_sc[...] = jnp.full(m_sc.shape, NEG, F32)
    acc_sc[...] = jnp.zeros(acc_sc.shape, F32)
    ones_col = jnp.where(lax.broadcasted_iota(jnp.int32, (tk, LANE), 1) == 0, 1.0, 0.0).astype(CD)

    def split(b):
        sub = b // nkb
        return sub, b - sub * nkb

    def scores(b):
        sub, j = split(b)
        q = q_ref[pl.ds(pl.multiple_of(sub * tq, tq), tq), :]
        k = k_ref[pl.ds(pl.multiple_of(j * tk, tk), tk), :]
        return _dot_nt(q, k)

    def softmax_pv(s, b):
        sub, j = split(b)
        v = v_ref[pl.ds(pl.multiple_of(j * tk, tk), tk), :]
        m_prev = m_sc[sub]
        m_new = jnp.maximum(m_prev, jnp.max(s, axis=-1, keepdims=True))
        alpha = jnp.exp2(m_prev - m_new)
        p = jnp.exp2(s - m_new).astype(CD)
        v_aug = jnp.concatenate([v, ones_col], axis=1)
        acc_sc[sub] = alpha * acc_sc[sub] + _dot(p, v_aug)
        m_sc[sub] = m_new

    s_sc[0] = scores(0)

    def body(g, carry):
        b0 = g * unroll
        for u in range(unroll):
            s_sc[(u + 1) % 2] = scores(jnp.minimum(b0 + u + 1, total - 1))
            softmax_pv(s_sc[u % 2], b0 + u)
        return carry

    lax.fori_loop(0, total // unroll, body, 0)
    for sub in range(nsub):
        acc = acc_sc[sub]
        o_ref[sub * tq:(sub + 1) * tq, :] = (acc[:, :MLA_V] / acc[:, MLA_V:MLA_V + 1]).astype(o_ref.dtype)


def mla_flash(q, k, v, s):
    nk = k.shape[0]
    tq = _row_tile(s, 512)
    tstep = _row_tile(s, 4 * tq)
    tk = MLA_TK
    total = (tstep // tq) * (nk // tk)
    assert nk % tk == 0 and total % 2 == 0
    unroll = 4 if total % 4 == 0 else 2
    return pl.pallas_call(
        functools.partial(_flash_kernel, tq, tk, unroll),
        grid=(MLA_HEADS, s // tstep),
        in_specs=[pl.BlockSpec((tstep, MLA_HW), lambda h, i: (i, h)),
                  pl.BlockSpec((nk, MLA_HW), lambda h, i: (0, h)),
                  pl.BlockSpec((nk, MLA_V), lambda h, i: (0, h))],
        out_specs=pl.BlockSpec((tstep, MLA_V), lambda h, i: (i, h)),
        out_shape=jax.ShapeDtypeStruct((s, MLA_HEADS * MLA_V), CD),
        scratch_shapes=[pltpu.VMEM((2, tq, tk), F32), pltpu.VMEM((tstep // tq, tq, 1), F32),
                        pltpu.VMEM((tstep // tq, tq, 2 * MLA_V), F32)],
        compiler_params=_cparams("parallel", "arbitrary"),
        name="mla_flash",
    )(q, k, v)


def _dense_attn_kernel(n_heads, group, dqk, dv, log2_scores, q_ref, k_ref, v_ref, sink_ref, o_ref):
    q = q_ref[...]
    k = k_ref[...]
    v = v_ref[...]
    ex = jnp.exp2 if log2_scores else jnp.exp
    for h in range(n_heads):
        g = h // group
        s = _dot_nt(q[:, h * dqk:(h + 1) * dqk], k[:, g * dqk:(g + 1) * dqk])
        m = jnp.max(s, axis=-1, keepdims=True)
        l = jnp.zeros_like(m)
        if sink_ref is not None:
            sk = sink_ref[0:1, h:h + 1] * (LOG2E if log2_scores else 1.0)
            m = jnp.maximum(m, sk)
            l = ex(sk - m)
        p = ex(s - m)
        l = l + jnp.sum(p, axis=-1, keepdims=True)
        o = _dot(p.astype(CD), v[:, g * dv:(g + 1) * dv]) / l
        o_ref[:, h * dv:(h + 1) * dv] = o.astype(o_ref.dtype)


def dense_attn(qkv_specs, n_heads, group, dqk, dv, c, sink=None, log2_scores=False, row_block=0):
    arrays = [a for a, _, _ in qkv_specs]
    specs = [pl.BlockSpec((c, w), functools.partial(lambda b, i: (row_block, b), b)) for _, w, b in qkv_specs]
    if sink is None:
        kern = lambda q, k, v, o: _dense_attn_kernel(n_heads, group, dqk, dv, log2_scores, q, k, v, None, o)
    else:
        kern = functools.partial(_dense_attn_kernel, n_heads, group, dqk, dv, log2_scores)
        arrays.append(sink)
        specs.append(pl.BlockSpec(sink.shape, lambda i: (0, 0)))
    return pl.pallas_call(
        kern,
        grid=(1,),
        in_specs=specs,
        out_specs=pl.BlockSpec((c, n_heads * dv), lambda i: (0, 0)),
        out_shape=jax.ShapeDtypeStruct((c, n_heads * dv), CD),
        compiler_params=_cparams("arbitrary"),
        name="dense_attn",
    )(*arrays)


def _na_kernel(q_ref, kp_ref, ko_ref, kn_ref, vp_ref, vo_ref, vn_ref, kc_ref, vc_ref, bias_ref, o_ref):
    q = q_ref[...]
    kcat = jnp.concatenate([kp_ref[...], ko_ref[...], kn_ref[...], kc_ref[...]], axis=0)
    vcat = jnp.concatenate([vp_ref[...], vo_ref[...], vn_ref[...], vc_ref[...]], axis=0)
    nloc = 3 * q.shape[0]
    ones = _ones_col(kcat.shape[0], NA_DIM)
    for h in range(NA_HEADS):
        hs = slice(h * NA_DIM, (h + 1) * NA_DIM)
        s = _dot_nt(q[:, hs], kcat[:, hs])
        s_loc = s[:, :nloc] + bias_ref[0, h]
        s_ctx = s[:, nloc:]
        m = jnp.maximum(jnp.max(s_loc, axis=-1, keepdims=True), jnp.max(s_ctx, axis=-1, keepdims=True))
        p = jnp.concatenate([jnp.exp2(s_loc - m), jnp.exp2(s_ctx - m)], axis=1).astype(CD)
        acc = _dot(p, jnp.concatenate([vcat[:, hs], ones], axis=1))
        o_ref[:, hs] = (acc[:, :NA_DIM] / acc[:, NA_DIM:NA_DIM + 1]).astype(o_ref.dtype)


def _na_bias(rpb, rows):
    qr = NA_QROWS
    tb = qr * GRID_W
    nb = rows // qr
    nkr = 3 * qr
    rsel = np.zeros((3, qr, nkr, 2 * NA_WIN_R - 1), np.float32)
    for v, b in enumerate((0, 1, nb - 1)):
        for a in range(qr):
            r = qr * b + a
            rs = min(max(r - NA_WIN_R // 2, 0), rows - NA_WIN_R)
            for u in range(nkr):
                kr = qr * (b - 1) + u
                if 0 <= b - 1 + u // qr < nb and rs <= kr < rs + NA_WIN_R:
                    rsel[v, a, u, kr - r + NA_WIN_R - 1] = 1.0
    csel = np.zeros((GRID_W, GRID_W, 2 * NA_WIN_C - 1), np.float32)
    for qc in range(GRID_W):
        cs = min(max(qc - NA_WIN_C // 2, 0), GRID_W - NA_WIN_C)
        for kc in range(cs, cs + NA_WIN_C):
            csel[qc, kc, kc - qc + NA_WIN_C - 1] = 1.0
    valid = (np.einsum('vaud,qkj->vaquk', rsel, csel) > 0).reshape(3, tb, 3 * tb)
    assert 2 * qr == NA_WIN_R
    nd = 2 * NA_WIN_R - 1
    rp = rpb.astype(F32) * LOG2E
    gap = jnp.zeros((NA_HEADS, nd, GRID_W + 1 - (2 * NA_WIN_C - 1)), F32)
    ring = jnp.concatenate([rp[..., NA_WIN_C - 1:], gap, rp[..., :NA_WIN_C - 1]], axis=-1)
    cols = jnp.tile(ring, (1, 1, GRID_W))[..., :GRID_W * GRID_W].reshape(NA_HEADS, nd, GRID_W, GRID_W)
    off = NA_WIN_R - 1 - qr
    slabs = [cols[:, off - a:off - a + nkr].transpose(0, 2, 1, 3).reshape(NA_HEADS, GRID_W, 3 * tb)
             for a in range(qr)]
    band = jnp.stack(slabs, axis=1).reshape(NA_HEADS, tb, 3 * tb)
    return jnp.where(valid[:, None], band[None], NEG)


def na_attn(p_lat, p_ctx, rpb, rows):
    s = p_lat.shape[0]
    c = p_ctx.shape[0]
    w = NA_HEADS * NA_DIM
    tb = NA_QROWS * GRID_W
    nb = s // tb
    assert rows % NA_QROWS == 0 and nb >= 3 and rows >= 2 * NA_WIN_R
    bias = _na_bias(rpb, rows)

    def blk(col, off):
        return pl.BlockSpec((tb, w), lambda i: (jnp.clip(i + off, 0, nb - 1), col))

    return pl.pallas_call(
        _na_kernel,
        grid=(nb,),
        in_specs=[blk(0, 0), blk(1, -1), blk(1, 0), blk(1, 1), blk(2, -1), blk(2, 0), blk(2, 1),
                  pl.BlockSpec((c, w), lambda i: (0, 1)),
                  pl.BlockSpec((c, w), lambda i: (0, 2)),
                  pl.BlockSpec((1, NA_HEADS, tb, 3 * tb),
                               lambda i: (jnp.where(i == 0, 0, jnp.where(i == nb - 1, 2, 1)), 0, 0, 0))],
        out_specs=pl.BlockSpec((tb, w), lambda i: (i, 0)),
        out_shape=jax.ShapeDtypeStruct((s, w), CD),
        compiler_params=_cparams("parallel"),
        name="na_attn",
    )(p_lat, p_lat, p_lat, p_lat, p_lat, p_lat, p_lat, p_ctx, p_ctx, bias)


def _sw_kernel(seq, q_ref, kp_ref, ko_ref, kn_ref, vp_ref, vo_ref, vn_ref, kc_ref, vc_ref, sink_ref, o_ref):
    i = pl.program_id(0)
    q = q_ref[...]
    tq = q.shape[0]
    kcat = jnp.concatenate([kp_ref[...], ko_ref[...], kn_ref[...], kc_ref[...]], axis=0)
    vcat = jnp.concatenate([vp_ref[...], vo_ref[...], vn_ref[...], vc_ref[...]], axis=0)
    nkk = tq + 2 * SW_WINDOW
    r = lax.broadcasted_iota(jnp.int32, (tq, nkk), 0)
    cc = lax.broadcasted_iota(jnp.int32, (tq, nkk), 1)
    kpos = i * tq - SW_WINDOW + cc
    d = cc - r
    mask = (d >= 0) & (d <= 2 * SW_WINDOW) & (kpos >= 0) & (kpos < seq)
    ones = _ones_col(kcat.shape[0], SW_DIM)
    grp = SW_HEADS // SW_KV_HEADS
    for g in range(SW_KV_HEADS):
        gs = slice(g * SW_DIM, (g + 1) * SW_DIM)
        kg = kcat[:, gs]
        v_aug = jnp.concatenate([vcat[:, gs], ones], axis=1)
        for h in range(g * grp, (g + 1) * grp):
            s = _dot_nt(q[:, h * SW_DIM:(h + 1) * SW_DIM], kg)
            s_loc = jnp.where(mask, s[:, :nkk], NEG)
            s_ctx = s[:, nkk:]
            sk = sink_ref[0:1, h:h + 1] * LOG2E
            m = jnp.maximum(jnp.maximum(jnp.max(s_loc, axis=-1, keepdims=True),
                                        jnp.max(s_ctx, axis=-1, keepdims=True)), sk)
            p = jnp.concatenate([jnp.exp2(s_loc - m), jnp.exp2(s_ctx - m)], axis=1).astype(CD)
            acc = _dot(p, v_aug)
            l = acc[:, SW_DIM:SW_DIM + 1] + jnp.exp2(sk - m)
            o_ref[:, h * SW_DIM:(h + 1) * SW_DIM] = (acc[:, :SW_DIM] / l).astype(o_ref.dtype)


def sw_attn(p_lat, p_ctx, sink):
    s = p_lat.shape[0]
    c = p_ctx.shape[0]
    tq = SW_TQ
    assert s % tq == 0 and tq % SW_WINDOW == 0
    nq = s // tq
    per = tq // SW_WINDOW
    nwb = s // SW_WINDOW
    kcol = SW_HEADS * SW_DIM // LANE
    prev = lambda col: pl.BlockSpec((SW_WINDOW, LANE), lambda i: (jnp.maximum(i * per - 1, 0), col))
    own = lambda col: pl.BlockSpec((tq, LANE), lambda i: (i, col))
    nxt = lambda col: pl.BlockSpec((SW_WINDOW, LANE), lambda i: (jnp.minimum((i + 1) * per, nwb - 1), col))
    return pl.pallas_call(
        functools.partial(_sw_kernel, s),
        grid=(nq,),
        in_specs=[pl.BlockSpec((tq, SW_HEADS * SW_DIM), lambda i: (i, 0)),
                  prev(kcol), own(kcol), nxt(kcol), prev(kcol + 1), own(kcol + 1), nxt(kcol + 1),
                  pl.BlockSpec((c, LANE), lambda i: (0, kcol)),
                  pl.BlockSpec((c, LANE), lambda i: (0, kcol + 1)),
                  pl.BlockSpec(sink.shape, lambda i: (0, 0))],
        out_specs=pl.BlockSpec((tq, SW_HEADS * SW_DIM), lambda i: (i, 0)),
        out_shape=jax.ShapeDtypeStruct((s, SW_HEADS * SW_DIM), CD),
        compiler_params=_cparams("parallel"),
        name="sw_attn",
    )(p_lat, p_lat, p_lat, p_lat, p_lat, p_lat, p_lat, p_ctx, p_ctx, sink)


def _ml_prep_kernel(nt, x_ref, xp_ref, xn_ref, g_ref, w_ref, b_ref, q_out, k_out, g_out):
    i = pl.program_id(0)
    x = x_ref[...]
    tm = x.shape[0]
    row = lax.broadcasted_iota(jnp.int32, x.shape, 0)
    has_prev = jnp.where(i > 0, 1.0, 0.0)
    has_next = jnp.where(i < nt - 1, 1.0, 0.0)
    prev_row = xp_ref[7:8, :] * has_prev
    next_row = xn_ref[0:1, :] * has_next
    xm = jnp.where(row == 0, prev_row, pltpu.roll(x, 1, axis=0))
    xq = jnp.where(row == tm - 1, next_row, pltpu.roll(x, tm - 1, axis=0))
    y = w_ref[0:1, :] * xm + w_ref[1:2, :] * x + w_ref[2:3, :] * xq
    y = y * _sigmoid(y)
    q_out[...] = y[:, :ML_QKW].astype(CD)
    k_out[...] = (y[:, ML_QKW:] * (ML_QK ** -0.5)).astype(CD)
    g = g_ref[...] + b_ref[...]
    lane = lax.broadcasted_iota(jnp.int32, g.shape, 1)
    logsig = jnp.minimum(g, 0.0) - jnp.log(1.0 + jnp.exp(-jnp.abs(g)))
    g_out[...] = jnp.where(lane < 2 * ML_HEADS, g, logsig)


def ml_prep(p_ml, conv_w, gate_b):
    t = p_ml.shape[0]
    tm = _row_tile(t, 256)
    nt = t // tm
    wq = 2 * ML_QKW
    gcol = (2 * ML_QKW + 2 * ML_VW) // LANE
    w8 = jnp.zeros((8, wq), F32).at[:ML_CONV].set(conv_w.astype(F32))
    b = jnp.zeros((1, LANE), F32).at[0, :4 * ML_HEADS].set(gate_b.astype(F32).reshape(-1))
    h8 = tm // 8
    return pl.pallas_call(
        functools.partial(_ml_prep_kernel, nt),
        grid=(nt,),
        in_specs=[pl.BlockSpec((tm, wq), lambda i: (i, 0)),
                  pl.BlockSpec((8, wq), lambda i: (jnp.maximum(i * h8 - 1, 0), 0)),
                  pl.BlockSpec((8, wq), lambda i: (jnp.minimum((i + 1) * h8, t // 8 - 1), 0)),
                  pl.BlockSpec((tm, LANE), lambda i: (i, gcol)),
                  pl.BlockSpec((8, wq), lambda i: (0, 0)),
                  pl.BlockSpec((1, LANE), lambda i: (0, 0))],
        out_specs=[pl.BlockSpec((tm, ML_QKW), lambda i: (i, 0)),
                   pl.BlockSpec((tm, ML_QKW), lambda i: (i, 0)),
                   pl.BlockSpec((tm, LANE), lambda i: (i, 0))],
        out_shape=[jax.ShapeDtypeStruct((t, ML_QKW), CD), jax.ShapeDtypeStruct((t, ML_QKW), CD),
                   jax.ShapeDtypeStruct((t, LANE), F32)],
        compiler_params=_cparams("parallel"),
        name="ml_prep",
    )(p_ml, p_ml, p_ml, p_ml, w8, b)


def _ml_scan_kernel(qf_ref, ktf_ref, vf_ref, gcf_ref, grf_ref,
                    qb_ref, ktb_ref, vb_ref, gcb_ref, grb_ref,
                    c0_ref, m0_ref, hf_ref, hb_ref, c_ref, m_ref):
    @pl.when(pl.program_id(0) == 0)
    def _():
        c_ref[...] = c0_ref[...]
        m_ref[...] = m0_ref[...]

    ln = qf_ref.shape[0]
    r = lax.broadcasted_iota(jnp.int32, (ln, ln), 0)
    c = lax.broadcasted_iota(jnp.int32, (ln, ln), 1)
    lower = r >= c
    upper = c >= r
    lower_f = jnp.where(lower, 1.0, 0.0)
    upper_f = jnp.where(upper, 1.0, 0.0)
    ones_col = jnp.where(lax.broadcasted_iota(jnp.int32, (ln, LANE), 1) == 0, 1.0, 0.0).astype(CD)
    hi = lax.Precision.HIGHEST

    streams = ((qf_ref, ktf_ref, vf_ref, gcf_ref, grf_ref, hf_ref, lower, lower_f, upper_f, ln - 1),
               (qb_ref, ktb_ref, vb_ref, gcb_ref, grb_ref, hb_ref, upper, upper_f, lower_f, 0))
    for d, (q_ref, kt_ref, v_ref, gc_ref, gr_ref, h_ref, mask, tri_c, tri_r, last) in enumerate(streams):
        gcol = gc_ref[...]
        grow = gr_ref[...]
        bcol = jnp.dot(tri_c, gcol, precision=hi, preferred_element_type=F32)
        brow = jnp.dot(grow, tri_r, precision=hi, preferred_element_type=F32)
        q = q_ref[...]
        kt = kt_ref[...]
        v = v_ref[...]
        for hd in range(ML_HEADS):
            ch = d * ML_HEADS + hd
            ic = d * ML_HEADS + hd
            fc = 2 * ML_HEADS + ic
            i_c = gcol[:, ic:ic + 1]
            b_c = bcol[:, fc:fc + 1]
            i_r = grow[ic:ic + 1, :]
            b_r = brow[fc:fc + 1, :]
            m_prev = m_ref[ch, 0:1, 0:1]
            cst = c_ref[ch]
            dmat = jnp.where(mask, b_c - b_r + i_r, NEG)
            inter = b_c + m_prev
            m_t = jnp.maximum(inter, jnp.max(dmat, axis=-1, keepdims=True))
            qh = q[:, hd * ML_QK:(hd + 1) * ML_QK]
            kth = kt[hd * ML_QK:(hd + 1) * ML_QK, :]
            s = lax.dot_general(qh, kth, (((1,), (0,)), ((), ())), preferred_element_type=F32)
            s = s * jnp.exp(dmat - m_t)
            w_inter = jnp.exp(inter - m_t)
            v_aug = jnp.concatenate([v[:, hd * ML_V:(hd + 1) * ML_V].astype(CD), ones_col], axis=1)
            num = _dot(s.astype(CD), v_aug) + w_inter * _dot(qh, cst.astype(CD))
            den = num[:, ML_V:ML_V + 1]
            h = num[:, :ML_V] / jnp.maximum(jnp.abs(den), jnp.exp(-m_t))
            h_ref[:, hd * ML_V:(hd + 1) * ML_V] = h
            b_end = b_c[last:last + 1, :]
            m_new = jnp.maximum(b_end + m_prev, jnp.max(b_end - b_c + i_c, axis=0, keepdims=True))
            w_s = jnp.exp(b_end - b_r + i_r - m_new)
            w_c = jnp.exp(b_end + m_prev - m_new)
            ktw = (kth.astype(F32) * w_s).astype(CD)
            c_ref[ch] = w_c * cst + _dot(ktw, v_aug)
            m_ref[ch] = jnp.broadcast_to(m_new, m_ref.shape[1:])


def ml_scan(qc, kc, p_ml, gates, state):
    t = qc.shape[0]
    ln = min(ML_CHUNK, t)
    assert t % ln == 0
    nc = t // ln
    kt = kc.T
    gr = gates[:, :4 * ML_HEADS].T
    c0, m0 = state
    vcol = 2 * ML_QKW // ML_VW
    fwd = lambda j: j
    bwd = lambda j: nc - 1 - j

    def specs(ix):
        return [pl.BlockSpec((ln, ML_QKW), lambda j: (ix(j), 0)),
                pl.BlockSpec((ML_QKW, ln), lambda j: (0, ix(j))),
                pl.BlockSpec((ln, ML_VW), lambda j: (ix(j), vcol)),
                pl.BlockSpec((ln, LANE), lambda j: (ix(j), 0)),
                pl.BlockSpec((4 * ML_HEADS, ln), lambda j: (0, ix(j)))]

    st_specs = [pl.BlockSpec(c0.shape, lambda j: (0, 0, 0)), pl.BlockSpec(m0.shape, lambda j: (0, 0, 0))]
    return pl.pallas_call(
        _ml_scan_kernel,
        grid=(nc,),
        in_specs=specs(fwd) + specs(bwd) + st_specs,
        out_specs=[pl.BlockSpec((ln, ML_VW), lambda j: (fwd(j), 0)),
                   pl.BlockSpec((ln, ML_VW), lambda j: (bwd(j), 0))] + st_specs,
        out_shape=[jax.ShapeDtypeStruct((t, ML_VW), F32), jax.ShapeDtypeStruct((t, ML_VW), F32),
                   jax.ShapeDtypeStruct(c0.shape, F32), jax.ShapeDtypeStruct(m0.shape, F32)],
        compiler_params=_cparams("arbitrary"),
        name="ml_scan",
    )(qc, kt, p_ml, gates, gr, qc, kt, p_ml, gates, gr, c0, m0)


def _ml_finish_kernel(hf_ref, hb_ref, o_ref, g_ref, y_ref):
    h = hf_ref[...] + hb_ref[...]
    og = _sigmoid(o_ref[...])
    g = g_ref[...]
    for hd in range(ML_HEADS):
        sl = slice(hd * ML_V, (hd + 1) * ML_V)
        hh = h[:, sl]
        y = hh * lax.rsqrt(jnp.mean(hh * hh, axis=-1, keepdims=True) + EPS) * g[:, sl]
        y_ref[:, sl] = (y * og[:, sl]).astype(y_ref.dtype)


def ml_finish(hf, hb, p_ml, norm_g):
    t = hf.shape[0]
    tm = _row_tile(t, 256)
    ocol = (2 * ML_QKW + ML_VW) // ML_VW
    blk = pl.BlockSpec((tm, ML_VW), lambda i: (i, 0))
    return pl.pallas_call(
        _ml_finish_kernel,
        grid=(t // tm,),
        in_specs=[blk, blk, pl.BlockSpec((tm, ML_VW), lambda i: (i, ocol)),
                  pl.BlockSpec((1, ML_VW), lambda i: (0, 0))],
        out_specs=blk,
        out_shape=jax.ShapeDtypeStruct((t, ML_VW), CD),
        compiler_params=_cparams("parallel"),
        name="ml_finish",
    )(hf, hb, p_ml, norm_g.reshape(1, ML_VW).astype(F32))


def _merge_kernel(h_ref, ya_ref, yb_ref, yc_ref, yd_ref, g0_ref, g1_ref, g2_ref, g3_ref, wb_ref, o_ref):
    h = h_ref[...]
    acc = None
    for i, (y_ref, g_ref) in enumerate(zip((ya_ref, yb_ref, yc_ref, yd_ref), (g0_ref, g1_ref, g2_ref, g3_ref))):
        term = _sigmoid(_dot(h, g_ref[...])) * _dot(y_ref[...], wb_ref[0, i])
        acc = term if acc is None else acc + term
    o_ref[...] = acc.astype(o_ref.dtype)


def gated_merge(h, ys, w_gate, w_branch, l, tn=512):
    t, d = h.shape
    tm = _row_tile(t, 1024)
    nj = d // tn
    yspec = pl.BlockSpec((tm, BRANCH_W), lambda j, i: (i, 0))
    gspec = lambda b: pl.BlockSpec((d, tn), lambda j, i: (0, b * nj + j))
    return pl.pallas_call(
        _merge_kernel,
        grid=(nj, t // tm),
        in_specs=[pl.BlockSpec((tm, d), lambda j, i: (i, 0)), yspec, yspec, yspec, yspec,
                  gspec(0), gspec(1), gspec(2), gspec(3),
                  pl.BlockSpec((1, N_BRANCH, BRANCH_W, tn), lambda j, i: (l, 0, 0, j))],
        out_specs=pl.BlockSpec((tm, tn), lambda j, i: (i, j)),
        out_shape=jax.ShapeDtypeStruct((t, d), CD),
        compiler_params=_cparams("parallel", "parallel"),
        name="gated_merge",
    )(h, *ys, w_gate, w_gate, w_gate, w_gate, w_branch)


def _ffn1_kernel(h_ref, wa_ref, wg_ref, o_ref, wa_sc, wg_sc):
    @pl.when(pl.program_id(1) == 0)
    def _():
        wa_sc[...] = wa_ref[0].astype(CD)
        wg_sc[...] = wg_ref[0].astype(CD)

    h = h_ref[...]
    a = _dot(h, wa_sc[...])
    g = _dot(h, wg_sc[...])
    o_ref[...] = (a * _sigmoid(a) * g).astype(o_ref.dtype)


def ffn_up(h, w, l, tn=512):
    t, d = h.shape
    dff = w.shape[2] // 2
    tm = _row_tile(t, 1024)
    nj = dff // tn
    return pl.pallas_call(
        _ffn1_kernel,
        grid=(nj, t // tm),
        in_specs=[pl.BlockSpec((tm, d), lambda j, i: (i, 0)),
                  pl.BlockSpec((1, d, tn), lambda j, i: (l, 0, j)),
                  pl.BlockSpec((1, d, tn), lambda j, i: (l, 0, nj + j))],
        out_specs=pl.BlockSpec((tm, tn), lambda j, i: (i, j)),
        out_shape=jax.ShapeDtypeStruct((t, dff), CD),
        scratch_shapes=[pltpu.VMEM((d, tn), CD), pltpu.VMEM((d, tn), CD)],
        compiler_params=_cparams("parallel", "arbitrary"),
        name="ffn_up",
    )(h, w, w)


def _rot_cols(w, dim):
    k, n = w.shape
    w4 = w.reshape(k, n // dim, 2, dim // 2)
    return jnp.stack([-w4[:, :, 1], w4[:, :, 0]], axis=2).reshape(k, n)


def _rope_tables(n_tokens, dim):
    a = dim // 2
    inv = 1.0 / (ROPE_THETA ** (jnp.arange(0, a, 2, dtype=F32) / a))
    rows = n_tokens // GRID_W
    assert rows * GRID_W == n_tokens
    ang_r = jnp.arange(rows, dtype=F32)[:, None] * inv
    ang_c = jnp.arange(GRID_W, dtype=F32)[:, None] * inv

    def expand(fr, fc):
        shape = (rows, GRID_W, a // 2)
        return jnp.concatenate([jnp.broadcast_to(fr[:, None, :], shape), jnp.broadcast_to(fc[None], shape)],
                               axis=-1).reshape(n_tokens, a)

    return expand(jnp.cos(ang_r), jnp.cos(ang_c)), expand(jnp.sin(ang_r), jnp.sin(ang_c))


def _layer_weights(w_in, w_in_c, l, mla_w_uq_l, mla_w_ukv_l):
    d = w_in.shape[1]
    o_na = ML_COLS
    o_sw = o_na + NA_COLS
    o_mla = o_sw + SW_COLS
    o_gate = o_mla + MLA_COLS
    f32c = lambda a, b: w_in[l, :, a:b]
    cdc = lambda a, b: w_in_c[l, :, a:b]
    w = {}
    w["ml"] = jnp.pad(cdc(0, ML_COLS), ((0, 0), (0, ML_PAD - ML_COLS)))
    nq = NA_HEADS * NA_DIM
    w["na"] = jnp.concatenate([(f32c(o_na, o_na + nq) * (NA_DIM ** -0.5 * LOG2E)).astype(CD),
                               cdc(o_na + nq, o_sw)], axis=1)
    sq = SW_HEADS * SW_DIM
    skv = SW_KV_HEADS * SW_DIM
    sw_q = f32c(o_sw, o_sw + sq) * (SW_DIM ** -0.5 * LOG2E)
    sw_k = f32c(o_sw + sq, o_sw + sq + skv)
    w["sw"] = jnp.concatenate([sw_q.astype(CD), cdc(o_sw + sq, o_mla)], axis=1)
    w["sw_rot"] = jnp.concatenate([_rot_cols(sw_q, SW_DIM), _rot_cols(sw_k, SW_DIM)], axis=1).astype(CD)
    o_kr = o_mla + MLA_Q_RANK + MLA_KV_RANK
    z = jnp.zeros((d, LANE - MLA_ROPE), CD)
    w["mla1"] = jnp.concatenate([cdc(o_mla, o_gate), z, _rot_cols(f32c(o_kr, o_gate), MLA_ROPE).astype(CD), z],
                                axis=1)
    uq = mla_w_uq_l.reshape(MLA_Q_RANK, MLA_HEADS, MLA_NOPE + MLA_ROPE)
    zq = jnp.zeros((MLA_Q_RANK, MLA_HEADS, MLA_HW - MLA_NOPE - MLA_ROPE), F32)
    w["mla_q"] = jnp.concatenate([uq, zq], axis=2).reshape(MLA_Q_RANK, -1).astype(CD)
    uqr = _rot_cols(uq[:, :, MLA_NOPE:].reshape(MLA_Q_RANK, -1), MLA_ROPE).reshape(MLA_Q_RANK, MLA_HEADS, MLA_ROPE)
    w["mla_q_rot"] = jnp.concatenate([jnp.zeros_like(uq[:, :, :MLA_NOPE]), uqr, zq], axis=2
                                     ).reshape(MLA_Q_RANK, -1).astype(CD)
    w["mla_kv"] = mla_w_ukv_l.astype(CD)
    w["gate"] = cdc(o_gate, w_in.shape[2])
    return w


def _tables(s, c):
    cos_sw, sin_sw = _rope_tables(s, SW_DIM)
    cos_m, sin_m = _rope_tables(s, MLA_ROPE)
    tile2 = lambda a: jnp.concatenate([a, a, a, a], axis=1)
    scale = (MLA_NOPE + MLA_ROPE) ** -0.5 * LOG2E
    zpad = jnp.zeros((s, MLA_HW - MLA_NOPE - MLA_ROPE), F32)
    lat = dict(
        sw_cos=tile2(cos_sw), sw_sin=tile2(sin_sw),
        q_cos=scale * jnp.concatenate([jnp.ones((s, MLA_NOPE), F32), cos_m, cos_m, zpad], axis=1),
        q_sin=scale * jnp.concatenate([jnp.zeros((s, MLA_NOPE), F32), sin_m, sin_m, zpad], axis=1),
        k_cos=jnp.concatenate([cos_m, cos_m, zpad], axis=1),
        k_sin=jnp.concatenate([sin_m, sin_m, zpad], axis=1))
    ctx = dict(
        sw_cos=jnp.ones((c, LANE), F32), sw_sin=jnp.zeros((c, LANE), F32),
        q_cos=jnp.full((c, MLA_HW), scale, F32), q_sin=jnp.zeros((c, MLA_HW), F32),
        k_cos=jnp.ones((c, LANE), F32), k_sin=jnp.zeros((c, LANE), F32))
    return lat, ctx


def _project(h, w, tab, gq, gkv, total_rows, mla_into=None):
    p_ml = matmul(h, w["ml"], F32)
    p_na = matmul(h, w["na"], CD)
    n_rope = (SW_HEADS + SW_KV_HEADS) * SW_DIM // LANE
    p_sw = rope_matmul(h, w["sw"], w["sw_rot"], tab["sw_cos"], tab["sw_sin"], n_rope)
    mla = mla_proj(h, w["mla1"], gq, gkv, w["mla_q"], w["mla_q_rot"], w["mla_kv"],
                   tab["q_cos"], tab["q_sin"], tab["k_cos"], tab["k_sin"], total_rows, mla_into)
    return p_ml, p_na, p_sw, mla


def _dense_tail(x, h, ys, w, l, w_branch, w_out, w_ffn_in, w_ffn_out, vt, g1, norm2, g2, norm_next, last):
    merged = gated_merge(h, ys, w["gate"], w_branch, l)
    x, h2 = matmul_residual_norm(merged, w_out, l, x, vt, g1, norm2, True, CD)
    u = ffn_up(h2, w_ffn_in, l)
    if last:
        return None, matmul_residual_norm(u, w_ffn_out, l, x, vt, g2, norm_next, False, F32, tn=512)
    return matmul_residual_norm(u, w_ffn_out, l, x, vt, g2, norm_next, True, CD, tn=512)


def kernel(x, c, ctx, c_ctx, w_ada, b_ada, norm1_g, norm2_g, w_in, ml_conv_w, ml_gate_b, ml_norm_g, na_rpb,
           sw_sink, mla_q_norm_g, mla_kv_norm_g, mla_w_uq, mla_w_ukv, w_branch, w_out, w_ffn_in, w_ffn_out,
           final_norm_g):
    bsz, s, d = x.shape
    assert bsz == 1
    depth = w_in.shape[0]
    n_ctx = ctx.shape[1]
    rows = s // GRID_W
    xl = x[0].astype(F32)
    xc = ctx[0].astype(F32)
    mod = adaln(jnp.stack([c[0], c_ctx], axis=1).astype(F32), w_ada, b_ada)
    vt = jnp.concatenate([mod.reshape(depth * 12, d), norm1_g.astype(F32), norm2_g.astype(F32),
                          final_norm_g.reshape(1, d).astype(F32), jnp.zeros((1, d), F32)]).reshape(-1, 1, d)
    mod_row = lambda l, stream, part: (l * 2 + stream) * 6 + part
    n1_row = lambda l: depth * 12 + l
    n2_row = lambda l: depth * 13 + l
    fin_row, zero_row = depth * 14, depth * 14 + 1
    norm1 = lambda l, st: (n1_row(l), mod_row(l, st, 0), mod_row(l, st, 1))
    norm2 = lambda l, st: (n2_row(l), mod_row(l, st, 3), mod_row(l, st, 4))
    tab_l, tab_c = _tables(s, n_ctx)
    zero_state = (jnp.zeros((2 * ML_HEADS, ML_QK, 2 * ML_V), F32), jnp.zeros((2 * ML_HEADS, 8, LANE), F32))

    w_in_c = w_in.astype(CD)
    wb = w_branch.astype(CD)
    wo = w_out.astype(CD)
    wf2 = w_ffn_out.astype(CD)

    hl = norm_mod(xl, vt, norm1(0, 0), CD)
    hc = norm_mod(xc, vt, norm1(0, 1), CD)
    for l in range(depth):
        last = l == depth - 1
        w = _layer_weights(w_in, w_in_c, l, mla_w_uq[l], mla_w_ukv[l])
        gq = mla_q_norm_g[l].reshape(1, -1).astype(F32)
        gkv = mla_kv_norm_g[l].reshape(1, -1).astype(F32)
        sink = sw_sink[l].reshape(1, -1).astype(F32)

        ml_l, na_l, sw_l, mla_lat = _project(hl, w, tab_l, gq, gkv, s + n_ctx)
        ml_c, na_c, sw_c, (mq, mk, mv) = _project(hc, w, tab_c, gq, gkv, s + n_ctx, mla_lat)

        qc_c, kc_c, gt_c = ml_prep(ml_c, ml_conv_w[l], ml_gate_b[l])
        qc_l, kc_l, gt_l = ml_prep(ml_l, ml_conv_w[l], ml_gate_b[l])
        hf_c, hb_c, cst, mst = ml_scan(qc_c, kc_c, ml_c, gt_c, zero_state)
        hf_l, hb_l, _, _ = ml_scan(qc_l, kc_l, ml_l, gt_l, (cst, mst))
        ya_l = ml_finish(hf_l, hb_l, ml_l, ml_norm_g[l])
        yb_l = na_attn(na_l, na_c, na_rpb[l], rows)
        yc_l = sw_attn(sw_l, sw_c, sink)
        yd_l = mla_flash(mq, mk, mv, s)

        nxt_l = (fin_row, zero_row, zero_row) if last else norm1(l + 1, 0)
        xl_new, hl_new = _dense_tail(xl, hl, (ya_l, yb_l, yc_l, yd_l), w, l, wb, wo, w_ffn_in, wf2, vt,
                                     mod_row(l, 0, 2), norm2(l, 0), mod_row(l, 0, 5), nxt_l, last)
        if not last:
            ya_c = ml_finish(hf_c, hb_c, ml_c, ml_norm_g[l])
            nw = NA_HEADS * NA_DIM
            yb_c = dense_attn([(na_c, nw, 0), (na_c, nw, 1), (na_c, nw, 2)], NA_HEADS, 1, NA_DIM, NA_DIM,
                              n_ctx, log2_scores=True)
            kcol = SW_HEADS * SW_DIM // LANE
            yc_c = dense_attn([(sw_c, SW_HEADS * SW_DIM, 0), (sw_c, LANE, kcol), (sw_c, LANE, kcol + 1)],
                              SW_HEADS, SW_HEADS // SW_KV_HEADS, SW_DIM, SW_DIM, n_ctx, sink=sink,
                              log2_scores=True)
            assert s % n_ctx == 0
            yd_c = dense_attn([(mq, MLA_HEADS * MLA_HW, 0), (mk, MLA_HEADS * MLA_HW, 0),
                               (mv, MLA_HEADS * MLA_V, 0)], MLA_HEADS, 1, MLA_HW, MLA_V, n_ctx,
                              log2_scores=True, row_block=s // n_ctx)
            xc, hc = _dense_tail(xc, hc, (ya_c, yb_c, yc_c, yd_c), w, l, wb, wo, w_ffn_in, wf2, vt,
                                 mod_row(l, 1, 2), norm2(l, 1), mod_row(l, 1, 5), norm1(l + 1, 1), False)
        xl, hl = xl_new, hl_new

    return hl[None].astype(x.dtype)
```

```python
import functools

import numpy as np
import jax
import jax.numpy as jnp
from jax import lax
from jax.experimental import pallas as pl
from jax.experimental.pallas import tpu as pltpu

F32 = jnp.float32
CD = jnp.bfloat16

GRID_W = 64
EPS = 1e-6
ROPE_THETA = 10000.0
ML_HEADS, ML_QK, ML_V, ML_CONV = 4, 64, 128, 3
NA_HEADS, NA_DIM, NA_WIN_R, NA_WIN_C = 8, 64, 8, 16
SW_HEADS, SW_KV_HEADS, SW_DIM, SW_WINDOW = 8, 2, 64, 128
MLA_HEADS, MLA_Q_RANK, MLA_KV_RANK, MLA_NOPE, MLA_ROPE, MLA_V = 4, 384, 256, 128, 64, 128
N_BRANCH, BRANCH_W = 4, 512

ML_QKW = ML_HEADS * ML_QK
ML_VW = ML_HEADS * ML_V
ML_COLS = 2 * ML_QKW + 2 * ML_VW + 4 * ML_HEADS
ML_PAD = 1664
NA_COLS = 3 * NA_HEADS * NA_DIM
SW_COLS = (SW_HEADS + 2 * SW_KV_HEADS) * SW_DIM
MLA_COLS = MLA_Q_RANK + MLA_KV_RANK + MLA_ROPE
MLA_HW = 256
LANE = 128
NEG = -1e30
LOG2E = float(np.log2(np.e))
VMEM_LIMIT = 56 * 1024 * 1024

NA_QROWS = 4
SW_TQ = 256
ML_CHUNK = 256
MLA_TK = 1280


def _cparams(*sem):
    return pltpu.CompilerParams(dimension_semantics=sem, vmem_limit_bytes=VMEM_LIMIT)


def _row_tile(t, pref=512):
    tm = min(pref, t)
    assert t % tm == 0
    return tm


def _dot(a, b):
    return jnp.dot(a, b, preferred_element_type=F32)


def _dot_nt(a, b):
    return lax.dot_general(a, b, (((1,), (1,)), ((), ())), preferred_element_type=F32)


def _sigmoid(x):
    return 1.0 / (1.0 + jnp.exp(-x))


def _ones_col(n, w):
    return jnp.where(lax.broadcasted_iota(jnp.int32, (n, w), 1) == 0, 1.0, 0.0).astype(CD)


def _adaln_kernel(c_ref, w_ref, b_ref, o_ref):
    w = w_ref[0]
    for r in range(2):
        c = c_ref[:, r:r + 1]
        o_ref[0, r:r + 1, :] = jnp.sum((c * _sigmoid(c)) * w, axis=0, keepdims=True) + b_ref[0]


def adaln(cc, w_ada, b_ada):
    nl, d, n = w_ada.shape
    tn = 1024
    return pl.pallas_call(
        _adaln_kernel,
        grid=(nl, n // tn),
        in_specs=[pl.BlockSpec((d, 2), lambda l, j: (0, 0)),
                  pl.BlockSpec((1, d, tn), lambda l, j: (l, 0, j)),
                  pl.BlockSpec((1, 1, tn), lambda l, j: (l, 0, j))],
        out_specs=pl.BlockSpec((1, 2, tn), lambda l, j: (l, 0, j)),
        out_shape=jax.ShapeDtypeStruct((nl, 2, n), F32),
        compiler_params=_cparams("parallel", "parallel"),
        name="adaln",
    )(cc, w_ada, b_ada.reshape(nl, 1, n))


def _vec_spec(d, idx):
    return pl.BlockSpec((1, 1, d), lambda *_: (idx, 0, 0))


def _norm_mod(x, g, shift, scale):
    y = x * lax.rsqrt(jnp.mean(x * x, axis=-1, keepdims=True) + EPS)
    return (y * g) * (1.0 + scale) + shift


def _norm_mod_kernel(x_ref, g_ref, sh_ref, sc_ref, o_ref):
    o_ref[...] = _norm_mod(x_ref[...], g_ref[0], sh_ref[0], sc_ref[0]).astype(o_ref.dtype)


def norm_mod(x, vt, norm_idx, out_dtype):
    t, d = x.shape
    tm = _row_tile(t, 256)
    return pl.pallas_call(
        _norm_mod_kernel,
        grid=(t // tm,),
        in_specs=[pl.BlockSpec((tm, d), lambda i: (i, 0))] + [_vec_spec(d, ix) for ix in norm_idx],
        out_specs=pl.BlockSpec((tm, d), lambda i: (i, 0)),
        out_shape=jax.ShapeDtypeStruct((t, d), out_dtype),
        compiler_params=_cparams("parallel"),
        name="norm_mod",
    )(x, vt, vt, vt)


def _mm_kernel(a_ref, b_ref, o_ref):
    o_ref[...] = _dot_nt(a_ref[...], b_ref[...]).astype(o_ref.dtype)


def matmul(a, bt, out_dtype, tn=None):
    t, k = a.shape
    n = bt.shape[0]
    tn = n if tn is None else tn
    tm = _row_tile(t)
    return pl.pallas_call(
        _mm_kernel,
        grid=(n // tn, t // tm),
        in_specs=[pl.BlockSpec((tm, k), lambda j, i: (i, 0)),
                  pl.BlockSpec((tn, k), lambda j, i: (j, 0))],
        out_specs=pl.BlockSpec((tm, tn), lambda j, i: (i, j)),
        out_shape=jax.ShapeDtypeStruct((t, n), out_dtype),
        compiler_params=_cparams("parallel", "parallel"),
        name="matmul",
    )(a, bt)


def _mm_res_norm_kernel(nj, tn, want_x, a_ref, b_ref, x_ref, gate_ref, g_ref, sh_ref, sc_ref, *rest):
    xbuf, ho_ref = (rest[0], rest[1]) if want_x else (rest[1], rest[0])
    j = pl.program_id(1)
    acc = _dot(a_ref[...], b_ref[0])
    for jj in range(nj):
        @pl.when(j == jj)
        def _(jj=jj):
            sl = slice(jj * tn, (jj + 1) * tn)
            xbuf[:, sl] = x_ref[:, sl] + gate_ref[0][:, sl] * acc

    @pl.when(j == nj - 1)
    def _():
        ho_ref[...] = _norm_mod(xbuf[...], g_ref[0], sh_ref[0], sc_ref[0]).astype(ho_ref.dtype)


def matmul_residual_norm(a, b, l, x, vt, gate_idx, norm_idx, want_x, h_dtype, tn=None):
    t, kdim = a.shape
    n = b.shape[2]
    tn = n if tn is None else tn
    nj = n // tn
    tm = _row_tile(t)
    row = pl.BlockSpec((tm, n), lambda i, j: (i, 0))
    out_shape = [jax.ShapeDtypeStruct((t, n), h_dtype)]
    if want_x:
        out_shape = [jax.ShapeDtypeStruct((t, n), F32)] + out_shape
    out = pl.pallas_call(
        functools.partial(_mm_res_norm_kernel, nj, tn, want_x),
        grid=(t // tm, nj),
        in_specs=[pl.BlockSpec((tm, kdim), lambda i, j: (i, 0)),
                  pl.BlockSpec((1, kdim, tn), lambda i, j: (l, 0, j)),
                  row, _vec_spec(n, gate_idx)] + [_vec_spec(n, ix) for ix in norm_idx],
        out_specs=[row] * len(out_shape),
        out_shape=out_shape,
        scratch_shapes=[] if want_x else [pltpu.VMEM((tm, n), F32)],
        compiler_params=_cparams("parallel", "arbitrary"),
        name="matmul_residual_norm",
    )(a, b, x, vt, vt, vt, vt)
    return out if want_x else out[0]


def _rope_mm_kernel(n_rope, a_ref, w_ref, wr_ref, cos_ref, sin_ref, o_ref):
    a = a_ref[...]
    p = _dot_nt(a, w_ref[...])
    pr = _dot_nt(a, wr_ref[...])
    cos = cos_ref[...]
    sin = sin_ref[...]
    for j in range(p.shape[1] // LANE):
        sl = slice(j * LANE, (j + 1) * LANE)
        if j < n_rope:
            o_ref[:, sl] = (p[:, sl] * cos + pr[:, sl] * sin).astype(o_ref.dtype)
        else:
            o_ref[:, sl] = p[:, sl].astype(o_ref.dtype)


def rope_matmul(a, w, w_rot, cos, sin, n_rope):
    t, k = a.shape
    n = w.shape[0]
    nr = n_rope * LANE
    tm = _row_tile(t)
    return pl.pallas_call(
        functools.partial(_rope_mm_kernel, n_rope),
        grid=(t // tm,),
        in_specs=[pl.BlockSpec((tm, k), lambda i: (i, 0)),
                  pl.BlockSpec((n, k), lambda i: (0, 0)),
                  pl.BlockSpec((nr, k), lambda i: (0, 0)),
                  pl.BlockSpec((tm, LANE), lambda i: (i, 0)),
                  pl.BlockSpec((tm, LANE), lambda i: (i, 0))],
        out_specs=pl.BlockSpec((tm, n), lambda i: (i, 0)),
        out_shape=jax.ShapeDtypeStruct((t, n), CD),
        compiler_params=_cparams("parallel"),
        name="rope_matmul",
    )(a, w, w_rot, cos, sin)


def _mla_proj_kernel(h_ref, w1_ref, gq_ref, gkv_ref, wq_ref, wqr_ref, wkv_ref,
                     cq_ref, sq_ref, ck_ref, sk_ref, q_out, k_out, v_out):
    p = _dot_nt(h_ref[...], w1_ref[...])
    cq = p[:, :MLA_Q_RANK]
    ckv = p[:, MLA_Q_RANK:MLA_Q_RANK + MLA_KV_RANK]
    o = MLA_Q_RANK + MLA_KV_RANK
    kr = p[:, o:o + LANE]
    krr = p[:, o + LANE:o + 2 * LANE]
    cqn = (cq * lax.rsqrt(jnp.mean(cq * cq, axis=-1, keepdims=True) + EPS) * gq_ref[...]).astype(CD)
    ckvn = (ckv * lax.rsqrt(jnp.mean(ckv * ckv, axis=-1, keepdims=True) + EPS) * gkv_ref[...]).astype(CD)
    qa = _dot(cqn, wq_ref[...])
    qb = _dot(cqn, wqr_ref[...])
    kv = _dot(ckvn, wkv_ref[...])
    krp = (kr * ck_ref[...] + krr * sk_ref[...]).astype(CD)
    cq_t = cq_ref[...]
    sq_t = sq_ref[...]
    for h in range(MLA_HEADS):
        qs = slice(h * MLA_HW, (h + 1) * MLA_HW)
        q_out[:, qs] = (qa[:, qs] * cq_t + qb[:, qs] * sq_t).astype(CD)
        k_out[:, h * MLA_HW:h * MLA_HW + MLA_NOPE] = kv[:, h * 256:h * 256 + MLA_NOPE].astype(CD)
        k_out[:, h * MLA_HW + MLA_NOPE:(h + 1) * MLA_HW] = krp
        v_out[:, h * MLA_V:(h + 1) * MLA_V] = kv[:, h * 256 + MLA_NOPE:(h + 1) * 256].astype(CD)


def mla_proj(h, w1, gq, gkv, wq, wqr, wkv, cos_q, sin_q, cos_k, sin_k, total_rows, into=None):
    t, d = h.shape
    tm = _row_tile(t)
    full = lambda a: pl.BlockSpec(a.shape, lambda i: (0,) * a.ndim)
    rows = lambda w: pl.BlockSpec((tm, w), lambda i: (i, 0))
    hw = MLA_HEADS * MLA_HW
    vw = MLA_HEADS * MLA_V
    args = [h, w1, gq, gkv, wq, wqr, wkv, cos_q, sin_q, cos_k, sin_k]
    in_specs = [rows(d), full(w1), full(gq), full(gkv), full(wq), full(wqr), full(wkv),
                rows(MLA_HW), rows(MLA_HW), rows(LANE), rows(LANE)]
    if into is None:
        off, aliases, kern = 0, {}, _mla_proj_kernel
    else:
        assert (total_rows - t) % tm == 0
        off = (total_rows - t) // tm
        aliases = {len(args) + i: i for i in range(3)}
        in_specs = in_specs + [pl.BlockSpec(memory_space=pl.ANY)] * 3
        args = args + list(into)
        kern = lambda *refs: _mla_proj_kernel(*refs[:11], *refs[14:])
    orow = lambda w: pl.BlockSpec((tm, w), lambda i: (off + i, 0))
    return pl.pallas_call(
        kern,
        grid=(t // tm,),
        in_specs=in_specs,
        out_specs=[orow(hw), orow(hw), orow(vw)],
        out_shape=[jax.ShapeDtypeStruct((total_rows, hw), CD), jax.ShapeDtypeStruct((total_rows, hw), CD),
                   jax.ShapeDtypeStruct((total_rows, vw), CD)],
        input_output_aliases=aliases,
        compiler_params=_cparams("parallel"),
        name="mla_proj",
    )(*args)


def _flash_kernel(tq, tk, unroll, q_ref, k_ref, v_ref, o_ref, s_sc, m_sc, acc_sc):
    nkb = k_ref.shape[0] // tk
    nsub = q_ref.shape[0] // tq
    total = nsub * nkb
    m_sc[...] = jnp.full(m_sc.shape, NEG, F32)
    acc_sc[...] = jnp.zeros(acc_sc.shape, F32)
    ones_col = jnp.where(lax.broadcasted_iota(jnp.int32, (tk, LANE), 1) == 0, 1.0, 0.0).astype(CD)

    def split(b):
        sub = b // nkb
        return sub, b - sub * nkb

    def scores(b):
        sub, j = split(b)
        q = q_ref[pl.ds(pl.multiple_of(sub * tq, tq), tq), :]
        k = k_ref[pl.ds(pl.multiple_of(j * tk, tk), tk), :]
        return _dot_nt(q, k)

    def softmax_pv(s, b):
        sub, j = split(b)
        v = v_ref[pl.ds(pl.multiple_of(j * tk, tk), tk), :]
        m_prev = m_sc[sub]
        m_new = jnp.maximum(m_prev, jnp.max(s, axis=-1, keepdims=True))
        alpha = jnp.exp2(m_prev - m_new)
        p = jnp.exp2(s - m_new).astype(CD)
        v_aug = jnp.concatenate([v, ones_col], axis=1)
        acc_sc[sub] = alpha * acc_sc[sub] + _dot(p, v_aug)
        m_sc[sub] = m_new

    s_sc[0] = scores(0)

    def body(g, carry):
        b0 = g * unroll
        for u in range(unroll):
            s_sc[(u + 1) % 2] = scores(jnp.minimum(b0 + u + 1, total - 1))
            softmax_pv(s_sc[u % 2], b0 + u)
        return carry

    lax.fori_loop(0, total // unroll, body, 0)
    for sub in range(nsub):
        acc = acc_sc[sub]
        o_ref[sub * tq:(sub + 1) * tq, :] = (acc[:, :MLA_V] / acc[:, MLA_V:MLA_V + 1]).astype(o_ref.dtype)


def mla_flash(q, k, v, s):
    nk = k.shape[0]
    tq = _row_tile(s, 512)
    tstep = _row_tile(s, 4 * tq)
    tk = MLA_TK
    total = (tstep // tq) * (nk // tk)
    assert nk % tk == 0 and total % 2 == 0
    unroll = 4 if total % 4 == 0 else 2
    return pl.pallas_call(
        functools.partial(_flash_kernel, tq, tk, unroll),
        grid=(MLA_HEADS, s // tstep),
        in_specs=[pl.BlockSpec((tstep, MLA_HW), lambda h, i: (i, h)),
                  pl.BlockSpec((nk, MLA_HW), lambda h, i: (0, h)),
                  pl.BlockSpec((nk, MLA_V), lambda h, i: (0, h))],
        out_specs=pl.BlockSpec((tstep, MLA_V), lambda h, i: (i, h)),
        out_shape=jax.ShapeDtypeStruct((s, MLA_HEADS * MLA_V), CD),
        scratch_shapes=[pltpu.VMEM((2, tq, tk), F32), pltpu.VMEM((tstep // tq, tq, 1), F32),
                        pltpu.VMEM((tstep // tq, tq, 2 * MLA_V), F32)],
        compiler_params=_cparams("parallel", "arbitrary"),
        name="mla_flash",
    )(q, k, v)


def _dense_attn_kernel(n_heads, group, dqk, dv, log2_scores, q_ref, k_ref, v_ref, sink_ref, o_ref):
    q = q_ref[...]
    k = k_ref[...]
    v = v_ref[...]
    ex = jnp.exp2 if log2_scores else jnp.exp
    for h in range(n_heads):
        g = h // group
        s = _dot_nt(q[:, h * dqk:(h + 1) * dqk], k[:, g * dqk:(g + 1) * dqk])
        m = jnp.max(s, axis=-1, keepdims=True)
        l = jnp.zeros_like(m)
        if sink_ref is not None:
            sk = sink_ref[0:1, h:h + 1] * (LOG2E if log2_scores else 1.0)
            m = jnp.maximum(m, sk)
            l = ex(sk - m)
        p = ex(s - m)
        l = l + jnp.sum(p, axis=-1, keepdims=True)
        o = _dot(p.astype(CD), v[:, g * dv:(g + 1) * dv]) / l
        o_ref[:, h * dv:(h + 1) * dv] = o.astype(o_ref.dtype)


def dense_attn(qkv_specs, n_heads, group, dqk, dv, c, sink=None, log2_scores=False, row_block=0):
    arrays = [a for a, _, _ in qkv_specs]
    specs = [pl.BlockSpec((c, w), functools.partial(lambda b, i: (row_block, b), b)) for _, w, b in qkv_specs]
    if sink is None:
        kern = lambda q, k, v, o: _dense_attn_kernel(n_heads, group, dqk, dv, log2_scores, q, k, v, None, o)
    else:
        kern = functools.partial(_dense_attn_kernel, n_heads, group, dqk, dv, log2_scores)
        arrays.append(sink)
        specs.append(pl.BlockSpec(sink.shape, lambda i: (0, 0)))
    return pl.pallas_call(
        kern,
        grid=(1,),
        in_specs=specs,
        out_specs=pl.BlockSpec((c, n_heads * dv), lambda i: (0, 0)),
        out_shape=jax.ShapeDtypeStruct((c, n_heads * dv), CD),
        compiler_params=_cparams("arbitrary"),
        name="dense_attn",
    )(*arrays)


def _na_kernel(q_ref, kp_ref, ko_ref, kn_ref, vp_ref, vo_ref, vn_ref, kc_ref, vc_ref, bias_ref, o_ref):
    q = q_ref[...]
    kcat = jnp.concatenate([kp_ref[...], ko_ref[...], kn_ref[...], kc_ref[...]], axis=0)
    vcat = jnp.concatenate([vp_ref[...], vo_ref[...], vn_ref[...], vc_ref[...]], axis=0)
    nloc = 3 * q.shape[0]
    ones = _ones_col(kcat.shape[0], NA_DIM)
    for h in range(NA_HEADS):
        hs = slice(h * NA_DIM, (h + 1) * NA_DIM)
        s = _dot_nt(q[:, hs], kcat[:, hs])
        s_loc = s[:, :nloc] + bias_ref[0, h]
        s_ctx = s[:, nloc:]
        m = jnp.maximum(jnp.max(s_loc, axis=-1, keepdims=True), jnp.max(s_ctx, axis=-1, keepdims=True))
        p = jnp.concatenate([jnp.exp2(s_loc - m), jnp.exp2(s_ctx - m)], axis=1).astype(CD)
        acc = _dot(p, jnp.concatenate([vcat[:, hs], ones], axis=1))
        o_ref[:, hs] = (acc[:, :NA_DIM] / acc[:, NA_DIM:NA_DIM + 1]).astype(o_ref.dtype)


def _na_bias(rpb, rows):
    qr = NA_QROWS
    tb = qr * GRID_W
    nb = rows // qr
    nkr = 3 * qr
    rsel = np.zeros((3, qr, nkr, 2 * NA_WIN_R - 1), np.float32)
    for v, b in enumerate((0, 1, nb - 1)):
        for a in range(qr):
            r = qr * b + a
            rs = min(max(r - NA_WIN_R // 2, 0), rows - NA_WIN_R)
            for u in range(nkr):
                kr = qr * (b - 1) + u
                if 0 <= b - 1 + u // qr < nb and rs <= kr < rs + NA_WIN_R:
                    rsel[v, a, u, kr - r + NA_WIN_R - 1] = 1.0
    csel = np.zeros((GRID_W, GRID_W, 2 * NA_WIN_C - 1), np.float32)
    for qc in range(GRID_W):
        cs = min(max(qc - NA_WIN_C // 2, 0), GRID_W - NA_WIN_C)
        for kc in range(cs, cs + NA_WIN_C):
            csel[qc, kc, kc - qc + NA_WIN_C - 1] = 1.0
    valid = (np.einsum('vaud,qkj->vaquk', rsel, csel) > 0).reshape(3, tb, 3 * tb)
    assert 2 * qr == NA_WIN_R
    nd = 2 * NA_WIN_R - 1
    rp = rpb.astype(F32) * LOG2E
    gap = jnp.zeros((NA_HEADS, nd, GRID_W + 1 - (2 * NA_WIN_C - 1)), F32)
    ring = jnp.concatenate([rp[..., NA_WIN_C - 1:], gap, rp[..., :NA_WIN_C - 1]], axis=-1)
    cols = jnp.tile(ring, (1, 1, GRID_W))[..., :GRID_W * GRID_W].reshape(NA_HEADS, nd, GRID_W, GRID_W)
    off = NA_WIN_R - 1 - qr
    slabs = [cols[:, off - a:off - a + nkr].transpose(0, 2, 1, 3).reshape(NA_HEADS, GRID_W, 3 * tb)
             for a in range(qr)]
    band = jnp.stack(slabs, axis=1).reshape(NA_HEADS, tb, 3 * tb)
    return jnp.where(valid[:, None], band[None], NEG)


def na_attn(p_lat, p_ctx, rpb, rows):
    s = p_lat.shape[0]
    c = p_ctx.shape[0]
    w = NA_HEADS * NA_DIM
    tb = NA_QROWS * GRID_W
    nb = s // tb
    assert rows % NA_QROWS == 0 and nb >= 3 and rows >= 2 * NA_WIN_R
    bias = _na_bias(rpb, rows)

    def blk(col, off):
        return pl.BlockSpec((tb, w), lambda i: (jnp.clip(i + off, 0, nb - 1), col))

    return pl.pallas_call(
        _na_kernel,
        grid=(nb,),
        in_specs=[blk(0, 0), blk(1, -1), blk(1, 0), blk(1, 1), blk(2, -1), blk(2, 0), blk(2, 1),
                  pl.BlockSpec((c, w), lambda i: (0, 1)),
                  pl.BlockSpec((c, w), lambda i: (0, 2)),
                  pl.BlockSpec((1, NA_HEADS, tb, 3 * tb),
                               lambda i: (jnp.where(i == 0, 0, jnp.where(i == nb - 1, 2, 1)), 0, 0, 0))],
        out_specs=pl.BlockSpec((tb, w), lambda i: (i, 0)),
        out_shape=jax.ShapeDtypeStruct((s, w), CD),
        compiler_params=_cparams("parallel"),
        name="na_attn",
    )(p_lat, p_lat, p_lat, p_lat, p_lat, p_lat, p_lat, p_ctx, p_ctx, bias)


def _sw_kernel(seq, q_ref, kp_ref, ko_ref, kn_ref, vp_ref, vo_ref, vn_ref, kc_ref, vc_ref, sink_ref, o_ref):
    i = pl.program_id(0)
    q = q_ref[...]
    tq = q.shape[0]
    kcat = jnp.concatenate([kp_ref[...], ko_ref[...], kn_ref[...], kc_ref[...]], axis=0)
    vcat = jnp.concatenate([vp_ref[...], vo_ref[...], vn_ref[...], vc_ref[...]], axis=0)
    nkk = tq + 2 * SW_WINDOW
    r = lax.broadcasted_iota(jnp.int32, (tq, nkk), 0)
    cc = lax.broadcasted_iota(jnp.int32, (tq, nkk), 1)
    kpos = i * tq - SW_WINDOW + cc
    d = cc - r
    mask = (d >= 0) & (d <= 2 * SW_WINDOW) & (kpos >= 0) & (kpos < seq)
    ones = _ones_col(kcat.shape[0], SW_DIM)
    grp = SW_HEADS // SW_KV_HEADS
    for g in range(SW_KV_HEADS):
        gs = slice(g * SW_DIM, (g + 1) * SW_DIM)
        kg = kcat[:, gs]
        v_aug = jnp.concatenate([vcat[:, gs], ones], axis=1)
        for h in range(g * grp, (g + 1) * grp):
            s = _dot_nt(q[:, h * SW_DIM:(h + 1) * SW_DIM], kg)
            s_loc = jnp.where(mask, s[:, :nkk], NEG)
            s_ctx = s[:, nkk:]
            sk = sink_ref[0:1, h:h + 1] * LOG2E
            m = jnp.maximum(jnp.maximum(jnp.max(s_loc, axis=-1, keepdims=True),
                                        jnp.max(s_ctx, axis=-1, keepdims=True)), sk)
            p = jnp.concatenate([jnp.exp2(s_loc - m), jnp.exp2(s_ctx - m)], axis=1).astype(CD)
            acc = _dot(p, v_aug)
            l = acc[:, SW_DIM:SW_DIM + 1] + jnp.exp2(sk - m)
            o_ref[:, h * SW_DIM:(h + 1) * SW_DIM] = (acc[:, :SW_DIM] / l).astype(o_ref.dtype)


def sw_attn(p_lat, p_ctx, sink):
    s = p_lat.shape[0]
    c = p_ctx.shape[0]
    tq = SW_TQ
    assert s % tq == 0 and tq % SW_WINDOW == 0
    nq = s // tq
    per = tq // SW_WINDOW
    nwb = s // SW_WINDOW
    kcol = SW_HEADS * SW_DIM // LANE
    prev = lambda col: pl.BlockSpec((SW_WINDOW, LANE), lambda i: (jnp.maximum(i * per - 1, 0), col))
    own = lambda col: pl.BlockSpec((tq, LANE), lambda i: (i, col))
    nxt = lambda col: pl.BlockSpec((SW_WINDOW, LANE), lambda i: (jnp.minimum((i + 1) * per, nwb - 1), col))
    return pl.pallas_call(
        functools.partial(_sw_kernel, s),
        grid=(nq,),
        in_specs=[pl.BlockSpec((tq, SW_HEADS * SW_DIM), lambda i: (i, 0)),
                  prev(kcol), own(kcol), nxt(kcol), prev(kcol + 1), own(kcol + 1), nxt(kcol + 1),
                  pl.BlockSpec((c, LANE), lambda i: (0, kcol)),
                  pl.BlockSpec((c, LANE), lambda i: (0, kcol + 1)),
                  pl.BlockSpec(sink.shape, lambda i: (0, 0))],
        out_specs=pl.BlockSpec((tq, SW_HEADS * SW_DIM), lambda i: (i, 0)),
        out_shape=jax.ShapeDtypeStruct((s, SW_HEADS * SW_DIM), CD),
        compiler_params=_cparams("parallel"),
        name="sw_attn",
    )(p_lat, p_lat, p_lat, p_lat, p_lat, p_lat, p_lat, p_ctx, p_ctx, sink)


def _ml_prep_kernel(nt, x_ref, xp_ref, xn_ref, g_ref, w_ref, b_ref, q_out, k_out, g_out):
    i = pl.program_id(0)
    x = x_ref[...]
    tm = x.shape[0]
    row = lax.broadcasted_iota(jnp.int32, x.shape, 0)
    has_prev = jnp.where(i > 0, 1.0, 0.0)
    has_next = jnp.where(i < nt - 1, 1.0, 0.0)
    prev_row = xp_ref[7:8, :] * has_prev
    next_row = xn_ref[0:1, :] * has_next
    xm = jnp.where(row == 0, prev_row, pltpu.roll(x, 1, axis=0))
    xq = jnp.where(row == tm - 1, next_row, pltpu.roll(x, tm - 1, axis=0))
    y = w_ref[0:1, :] * xm + w_ref[1:2, :] * x + w_ref[2:3, :] * xq
    y = y * _sigmoid(y)
    q_out[...] = y[:, :ML_QKW].astype(CD)
    k_out[...] = (y[:, ML_QKW:] * (ML_QK ** -0.5)).astype(CD)
    g = g_ref[...] + b_ref[...]
    lane = lax.broadcasted_iota(jnp.int32, g.shape, 1)
    logsig = jnp.minimum(g, 0.0) - jnp.log(1.0 + jnp.exp(-jnp.abs(g)))
    g_out[...] = jnp.where(lane < 2 * ML_HEADS, g, logsig)


def ml_prep(p_ml, conv_w, gate_b):
    t = p_ml.shape[0]
    tm = _row_tile(t, 256)
    nt = t // tm
    wq = 2 * ML_QKW
    gcol = (2 * ML_QKW + 2 * ML_VW) // LANE
    w8 = jnp.zeros((8, wq), F32).at[:ML_CONV].set(conv_w.astype(F32))
    b = jnp.zeros((1, LANE), F32).at[0, :4 * ML_HEADS].set(gate_b.astype(F32).reshape(-1))
    h8 = tm // 8
    return pl.pallas_call(
        functools.partial(_ml_prep_kernel, nt),
        grid=(nt,),
        in_specs=[pl.BlockSpec((tm, wq), lambda i: (i, 0)),
                  pl.BlockSpec((8, wq), lambda i: (jnp.maximum(i * h8 - 1, 0), 0)),
                  pl.BlockSpec((8, wq), lambda i: (jnp.minimum((i + 1) * h8, t // 8 - 1), 0)),
                  pl.BlockSpec((tm, LANE), lambda i: (i, gcol)),
                  pl.BlockSpec((8, wq), lambda i: (0, 0)),
                  pl.BlockSpec((1, LANE), lambda i: (0, 0))],
        out_specs=[pl.BlockSpec((tm, ML_QKW), lambda i: (i, 0)),
                   pl.BlockSpec((tm, ML_QKW), lambda i: (i, 0)),
                   pl.BlockSpec((tm, LANE), lambda i: (i, 0))],
        out_shape=[jax.ShapeDtypeStruct((t, ML_QKW), CD), jax.ShapeDtypeStruct((t, ML_QKW), CD),
                   jax.ShapeDtypeStruct((t, LANE), F32)],
        compiler_params=_cparams("parallel"),
        name="ml_prep",
    )(p_ml, p_ml, p_ml, p_ml, w8, b)


def _ml_scan_kernel(qf_ref, ktf_ref, vf_ref, gcf_ref, grf_ref,
                    qb_ref, ktb_ref, vb_ref, gcb_ref, grb_ref,
                    c0_ref, m0_ref, hf_ref, hb_ref, c_ref, m_ref):
    @pl.when(pl.program_id(0) == 0)
    def _():
        c_ref[...] = c0_ref[...]
        m_ref[...] = m0_ref[...]

    ln = qf_ref.shape[0]
    r = lax.broadcasted_iota(jnp.int32, (ln, ln), 0)
    c = lax.broadcasted_iota(jnp.int32, (ln, ln), 1)
    lower = r >= c
    upper = c >= r
    lower_f = jnp.where(lower, 1.0, 0.0)
    upper_f = jnp.where(upper, 1.0, 0.0)
    ones_col = jnp.where(lax.broadcasted_iota(jnp.int32, (ln, LANE), 1) == 0, 1.0, 0.0).astype(CD)
    hi = lax.Precision.HIGHEST

    streams = ((qf_ref, ktf_ref, vf_ref, gcf_ref, grf_ref, hf_ref, lower, lower_f, upper_f, ln - 1),
               (qb_ref, ktb_ref, vb_ref, gcb_ref, grb_ref, hb_ref, upper, upper_f, lower_f, 0))
    for d, (q_ref, kt_ref, v_ref, gc_ref, gr_ref, h_ref, mask, tri_c, tri_r, last) in enumerate(streams):
        gcol = gc_ref[...]
        grow = gr_ref[...]
        bcol = jnp.dot(tri_c, gcol, precision=hi, preferred_element_type=F32)
        brow = jnp.dot(grow, tri_r, precision=hi, preferred_element_type=F32)
        q = q_ref[...]
        kt = kt_ref[...]
        v = v_ref[...]
        for hd in range(ML_HEADS):
            ch = d * ML_HEADS + hd
            ic = d * ML_HEADS + hd
            fc = 2 * ML_HEADS + ic
            i_c = gcol[:, ic:ic + 1]
            b_c = bcol[:, fc:fc + 1]
            i_r = grow[ic:ic + 1, :]
            b_r = brow[fc:fc + 1, :]
            m_prev = m_ref[ch, 0:1, 0:1]
            cst = c_ref[ch]
            dmat = jnp.where(mask, b_c - b_r + i_r, NEG)
            inter = b_c + m_prev
            m_t = jnp.maximum(inter, jnp.max(dmat, axis=-1, keepdims=True))
            qh = q[:, hd * ML_QK:(hd + 1) * ML_QK]
            kth = kt[hd * ML_QK:(hd + 1) * ML_QK, :]
            s = lax.dot_general(qh, kth, (((1,), (0,)), ((), ())), preferred_element_type=F32)
            s = s * jnp.exp(dmat - m_t)
            w_inter = jnp.exp(inter - m_t)
            v_aug = jnp.concatenate([v[:, hd * ML_V:(hd + 1) * ML_V].astype(CD), ones_col], axis=1)
            num = _dot(s.astype(CD), v_aug) + w_inter * _dot(qh, cst.astype(CD))
            den = num[:, ML_V:ML_V + 1]
            h = num[:, :ML_V] / jnp.maximum(jnp.abs(den), jnp.exp(-m_t))
            h_ref[:, hd * ML_V:(hd + 1) * ML_V] = h
            b_end = b_c[last:last + 1, :]
            m_new = jnp.maximum(b_end + m_prev, jnp.max(b_end - b_c + i_c, axis=0, keepdims=True))
            w_s = jnp.exp(b_end - b_r + i_r - m_new)
            w_c = jnp.exp(b_end + m_prev - m_new)
            ktw = (kth.astype(F32) * w_s).astype(CD)
            c_ref[ch] = w_c * cst + _dot(ktw, v_aug)
            m_ref[ch] = jnp.broadcast_to(m_new, m_ref.shape[1:])


def ml_scan(qc, kc, p_ml, gates, state):
    t = qc.shape[0]
    ln = min(ML_CHUNK, t)
    assert t % ln == 0
    nc = t // ln
    kt = kc.T
    gr = gates[:, :4 * ML_HEADS].T
    c0, m0 = state
    vcol = 2 * ML_QKW // ML_VW
    fwd = lambda j: j
    bwd = lambda j: nc - 1 - j

    def specs(ix):
        return [pl.BlockSpec((ln, ML_QKW), lambda j: (ix(j), 0)),
                pl.BlockSpec((ML_QKW, ln), lambda j: (0, ix(j))),
                pl.BlockSpec((ln, ML_VW), lambda j: (ix(j), vcol)),
                pl.BlockSpec((ln, LANE), lambda j: (ix(j), 0)),
                pl.BlockSpec((4 * ML_HEADS, ln), lambda j: (0, ix(j)))]

    st_specs = [pl.BlockSpec(c0.shape, lambda j: (0, 0, 0)), pl.BlockSpec(m0.shape, lambda j: (0, 0, 0))]
    return pl.pallas_call(
        _ml_scan_kernel,
        grid=(nc,),
        in_specs=specs(fwd) + specs(bwd) + st_specs,
        out_specs=[pl.BlockSpec((ln, ML_VW), lambda j: (fwd(j), 0)),
                   pl.BlockSpec((ln, ML_VW), lambda j: (bwd(j), 0))] + st_specs,
        out_shape=[jax.ShapeDtypeStruct((t, ML_VW), F32), jax.ShapeDtypeStruct((t, ML_VW), F32),
                   jax.ShapeDtypeStruct(c0.shape, F32), jax.ShapeDtypeStruct(m0.shape, F32)],
        compiler_params=_cparams("arbitrary"),
        name="ml_scan",
    )(qc, kt, p_ml, gates, gr, qc, kt, p_ml, gates, gr, c0, m0)


def _ml_finish_kernel(hf_ref, hb_ref, o_ref, g_ref, y_ref):
    h = hf_ref[...] + hb_ref[...]
    og = _sigmoid(o_ref[...])
    g = g_ref[...]
    for hd in range(ML_HEADS):
        sl = slice(hd * ML_V, (hd + 1) * ML_V)
        hh = h[:, sl]
        y = hh * lax.rsqrt(jnp.mean(hh * hh, axis=-1, keepdims=True) + EPS) * g[:, sl]
        y_ref[:, sl] = (y * og[:, sl]).astype(y_ref.dtype)


def ml_finish(hf, hb, p_ml, norm_g):
    t = hf.shape[0]
    tm = _row_tile(t, 256)
    ocol = (2 * ML_QKW + ML_VW) // ML_VW
    blk = pl.BlockSpec((tm, ML_VW), lambda i: (i, 0))
    return pl.pallas_call(
        _ml_finish_kernel,
        grid=(t // tm,),
        in_specs=[blk, blk, pl.BlockSpec((tm, ML_VW), lambda i: (i, ocol)),
                  pl.BlockSpec((1, ML_VW), lambda i: (0, 0))],
        out_specs=blk,
        out_shape=jax.ShapeDtypeStruct((t, ML_VW), CD),
        compiler_params=_cparams("parallel"),
        name="ml_finish",
    )(hf, hb, p_ml, norm_g.reshape(1, ML_VW).astype(F32))


def _merge_kernel(h_ref, ya_ref, yb_ref, yc_ref, yd_ref, g0_ref, g1_ref, g2_ref, g3_ref, wb_ref, o_ref):
    h = h_ref[...]
    acc = None
    for i, (y_ref, g_ref) in enumerate(zip((ya_ref, yb_ref, yc_ref, yd_ref), (g0_ref, g1_ref, g2_ref, g3_ref))):
        term = _sigmoid(_dot_nt(h, g_ref[...])) * _dot(y_ref[...], wb_ref[0, i])
        acc = term if acc is None else acc + term
    o_ref[...] = acc.astype(o_ref.dtype)


def gated_merge(h, ys, w_gate, w_branch, l, tn=512):
    t, d = h.shape
    tm = _row_tile(t, 1024)
    nj = d // tn
    yspec = pl.BlockSpec((tm, BRANCH_W), lambda j, i: (i, 0))
    gspec = lambda b: pl.BlockSpec((tn, d), lambda j, i: (b * nj + j, 0))
    return pl.pallas_call(
        _merge_kernel,
        grid=(nj, t // tm),
        in_specs=[pl.BlockSpec((tm, d), lambda j, i: (i, 0)), yspec, yspec, yspec, yspec,
                  gspec(0), gspec(1), gspec(2), gspec(3),
                  pl.BlockSpec((1, N_BRANCH, BRANCH_W, tn), lambda j, i: (l, 0, 0, j))],
        out_specs=pl.BlockSpec((tm, tn), lambda j, i: (i, j)),
        out_shape=jax.ShapeDtypeStruct((t, d), CD),
        compiler_params=_cparams("parallel", "parallel"),
        name="gated_merge",
    )(h, *ys, w_gate, w_gate, w_gate, w_gate, w_branch)


def _ffn1_kernel(h_ref, wa_ref, wg_ref, o_ref, wa_sc, wg_sc):
    @pl.when(pl.program_id(1) == 0)
    def _():
        wa_sc[...] = wa_ref[0].astype(CD)
        wg_sc[...] = wg_ref[0].astype(CD)

    h = h_ref[...]
    a = _dot(h, wa_sc[...])
    g = _dot(h, wg_sc[...])
    o_ref[...] = (a * _sigmoid(a) * g).astype(o_ref.dtype)


def ffn_up(h, w, l, tn=512):
    t, d = h.shape
    dff = w.shape[2] // 2
    tm = _row_tile(t, 1024)
    nj = dff // tn
    return pl.pallas_call(
        _ffn1_kernel,
        grid=(nj, t // tm),
        in_specs=[pl.BlockSpec((tm, d), lambda j, i: (i, 0)),
                  pl.BlockSpec((1, d, tn), lambda j, i: (l, 0, j)),
                  pl.BlockSpec((1, d, tn), lambda j, i: (l, 0, nj + j))],
        out_specs=pl.BlockSpec((tm, tn), lambda j, i: (i, j)),
        out_shape=jax.ShapeDtypeStruct((t, dff), CD),
        scratch_shapes=[pltpu.VMEM((d, tn), CD), pltpu.VMEM((d, tn), CD)],
        compiler_params=_cparams("parallel", "arbitrary"),
        name="ffn_up",
    )(h, w, w)


def _rot_cols(w, dim):
    k, n = w.shape
    w4 = w.reshape(k, n // dim, 2, dim // 2)
    return jnp.stack([-w4[:, :, 1], w4[:, :, 0]], axis=2).reshape(k, n)


def _rope_tables(n_tokens, dim):
    a = dim // 2
    inv = 1.0 / (ROPE_THETA ** (jnp.arange(0, a, 2, dtype=F32) / a))
    rows = n_tokens // GRID_W
    assert rows * GRID_W == n_tokens
    ang_r = jnp.arange(rows, dtype=F32)[:, None] * inv
    ang_c = jnp.arange(GRID_W, dtype=F32)[:, None] * inv

    def expand(fr, fc):
        shape = (rows, GRID_W, a // 2)
        return jnp.concatenate([jnp.broadcast_to(fr[:, None, :], shape), jnp.broadcast_to(fc[None], shape)],
                               axis=-1).reshape(n_tokens, a)

    return expand(jnp.cos(ang_r), jnp.cos(ang_c)), expand(jnp.sin(ang_r), jnp.sin(ang_c))


_O_NA = ML_COLS
_O_SW = _O_NA + NA_COLS
_O_MLA = _O_SW + SW_COLS
_O_KR = _O_MLA + MLA_Q_RANK + MLA_KV_RANK
_O_GATE = _O_MLA + MLA_COLS


def _rot_half_rows(x, dim):
    half = dim // 2
    parts = []
    for b in range(x.shape[0] // dim):
        parts += [-x[b * dim + half:(b + 1) * dim], x[b * dim:b * dim + half]]
    return jnp.concatenate(parts, axis=0)


def _repack_kernel(w_ref, ml_ref, na_ref, sw_ref, swr_ref, mla_ref, gate_ref):
    w = w_ref[0]
    cols = w.shape[1]
    zeros = lambda n: jnp.zeros((n, cols), F32)
    ml_ref[...] = jnp.concatenate([w[:ML_COLS], zeros(ML_PAD - ML_COLS)], axis=0).astype(CD)
    nq = NA_HEADS * NA_DIM
    na_ref[...] = jnp.concatenate([w[_O_NA:_O_NA + nq] * (NA_DIM ** -0.5 * LOG2E), w[_O_NA + nq:_O_SW]],
                                  axis=0).astype(CD)
    sq = SW_HEADS * SW_DIM
    skv = SW_KV_HEADS * SW_DIM
    sw_q = w[_O_SW:_O_SW + sq] * (SW_DIM ** -0.5 * LOG2E)
    sw_k = w[_O_SW + sq:_O_SW + sq + skv]
    sw_ref[...] = jnp.concatenate([sw_q, w[_O_SW + sq:_O_MLA]], axis=0).astype(CD)
    swr_ref[...] = _rot_half_rows(jnp.concatenate([sw_q, sw_k], axis=0), SW_DIM).astype(CD)
    kr = w[_O_KR:_O_GATE]
    pad = zeros(LANE - MLA_ROPE)
    mla_ref[...] = jnp.concatenate([w[_O_MLA:_O_KR], kr, pad, _rot_half_rows(kr, MLA_ROPE), pad], axis=0).astype(CD)
    gate_ref[...] = w[_O_GATE:].astype(CD)


def _repack_w_in(w_in_t, l):
    _, n, d = w_in_t.shape
    tc = 256
    heights = (ML_PAD, NA_COLS, SW_COLS, (SW_HEADS + SW_KV_HEADS) * SW_DIM,
               MLA_Q_RANK + MLA_KV_RANK + 2 * LANE, n - _O_GATE)
    return pl.pallas_call(
        _repack_kernel,
        grid=(d // tc,),
        in_specs=[pl.BlockSpec((1, n, tc), lambda i: (l, 0, i))],
        out_specs=[pl.BlockSpec((ht, tc), lambda i: (0, i)) for ht in heights],
        out_shape=[jax.ShapeDtypeStruct((ht, d), CD) for ht in heights],
        compiler_params=_cparams("parallel"),
        name="repack_w_in",
    )(w_in_t)


def _layer_weights(w_in_t, l, mla_w_uq_l, mla_w_ukv_l):
    w = {}
    w["ml"], w["na"], w["sw"], w["sw_rot"], w["mla1"], w["gate"] = _repack_w_in(w_in_t, l)
    uq = mla_w_uq_l.reshape(MLA_Q_RANK, MLA_HEADS, MLA_NOPE + MLA_ROPE)
    zq = jnp.zeros((MLA_Q_RANK, MLA_HEADS, MLA_HW - MLA_NOPE - MLA_ROPE), F32)
    w["mla_q"] = jnp.concatenate([uq, zq], axis=2).reshape(MLA_Q_RANK, -1).astype(CD)
    uqr = _rot_cols(uq[:, :, MLA_NOPE:].reshape(MLA_Q_RANK, -1), MLA_ROPE).reshape(MLA_Q_RANK, MLA_HEADS, MLA_ROPE)
    w["mla_q_rot"] = jnp.concatenate([jnp.zeros_like(uq[:, :, :MLA_NOPE]), uqr, zq], axis=2
                                     ).reshape(MLA_Q_RANK, -1).astype(CD)
    w["mla_kv"] = mla_w_ukv_l.astype(CD)
    return w


def _tables(s, c):
    cos_sw, sin_sw = _rope_tables(s, SW_DIM)
    cos_m, sin_m = _rope_tables(s, MLA_ROPE)
    tile2 = lambda a: jnp.concatenate([a, a, a, a], axis=1)
    scale = (MLA_NOPE + MLA_ROPE) ** -0.5 * LOG2E
    zpad = jnp.zeros((s, MLA_HW - MLA_NOPE - MLA_ROPE), F32)
    lat = dict(
        sw_cos=tile2(cos_sw), sw_sin=tile2(sin_sw),
        q_cos=scale * jnp.concatenate([jnp.ones((s, MLA_NOPE), F32), cos_m, cos_m, zpad], axis=1),
        q_sin=scale * jnp.concatenate([jnp.zeros((s, MLA_NOPE), F32), sin_m, sin_m, zpad], axis=1),
        k_cos=jnp.concatenate([cos_m, cos_m, zpad], axis=1),
        k_sin=jnp.concatenate([sin_m, sin_m, zpad], axis=1))
    ctx = dict(
        sw_cos=jnp.ones((c, LANE), F32), sw_sin=jnp.zeros((c, LANE), F32),
        q_cos=jnp.full((c, MLA_HW), scale, F32), q_sin=jnp.zeros((c, MLA_HW), F32),
        k_cos=jnp.ones((c, LANE), F32), k_sin=jnp.zeros((c, LANE), F32))
    return lat, ctx


def _project(h, w, tab, gq, gkv, total_rows, mla_into=None):
    p_ml = matmul(h, w["ml"], F32)
    p_na = matmul(h, w["na"], CD)
    n_rope = (SW_HEADS + SW_KV_HEADS) * SW_DIM // LANE
    p_sw = rope_matmul(h, w["sw"], w["sw_rot"], tab["sw_cos"], tab["sw_sin"], n_rope)
    mla = mla_proj(h, w["mla1"], gq, gkv, w["mla_q"], w["mla_q_rot"], w["mla_kv"],
                   tab["q_cos"], tab["q_sin"], tab["k_cos"], tab["k_sin"], total_rows, mla_into)
    return p_ml, p_na, p_sw, mla


def _dense_tail(x, h, ys, w, l, w_branch, w_out, w_ffn_in, w_ffn_out, vt, g1, norm2, g2, norm_next, last):
    merged = gated_merge(h, ys, w["gate"], w_branch, l)
    x, h2 = matmul_residual_norm(merged, w_out, l, x, vt, g1, norm2, True, CD)
    u = ffn_up(h2, w_ffn_in, l)
    if last:
        return None, matmul_residual_norm(u, w_ffn_out, l, x, vt, g2, norm_next, False, F32, tn=512)
    return matmul_residual_norm(u, w_ffn_out, l, x, vt, g2, norm_next, True, CD, tn=512)


def kernel(x, c, ctx, c_ctx, w_ada, b_ada, norm1_g, norm2_g, w_in, ml_conv_w, ml_gate_b, ml_norm_g, na_rpb,
           sw_sink, mla_q_norm_g, mla_kv_norm_g, mla_w_uq, mla_w_ukv, w_branch, w_out, w_ffn_in, w_ffn_out,
           final_norm_g):
    bsz, s, d = x.shape
    assert bsz == 1
    depth = w_in.shape[0]
    n_ctx = ctx.shape[1]
    rows = s // GRID_W
    xl = x[0].astype(F32)
    xc = ctx[0].astype(F32)
    mod = adaln(jnp.stack([c[0], c_ctx], axis=1).astype(F32), w_ada, b_ada)
    vt = jnp.concatenate([mod.reshape(depth * 12, d), norm1_g.astype(F32), norm2_g.astype(F32),
                          final_norm_g.reshape(1, d).astype(F32), jnp.zeros((1, d), F32)]).reshape(-1, 1, d)
    mod_row = lambda l, stream, part: (l * 2 + stream) * 6 + part
    n1_row = lambda l: depth * 12 + l
    n2_row = lambda l: depth * 13 + l
    fin_row, zero_row = depth * 14, depth * 14 + 1
    norm1 = lambda l, st: (n1_row(l), mod_row(l, st, 0), mod_row(l, st, 1))
    norm2 = lambda l, st: (n2_row(l), mod_row(l, st, 3), mod_row(l, st, 4))
    tab_l, tab_c = _tables(s, n_ctx)
    zero_state = (jnp.zeros((2 * ML_HEADS, ML_QK, 2 * ML_V), F32), jnp.zeros((2 * ML_HEADS, 8, LANE), F32))

    w_in_t = jnp.transpose(w_in, (0, 2, 1))
    wb = w_branch.astype(CD)
    wo = w_out.astype(CD)
    wf2 = w_ffn_out.astype(CD)

    hl = norm_mod(xl, vt, norm1(0, 0), CD)
    hc = norm_mod(xc, vt, norm1(0, 1), CD)
    for l in range(depth):
        last = l == depth - 1
        w = _layer_weights(w_in_t, l, mla_w_uq[l], mla_w_ukv[l])
        gq = mla_q_norm_g[l].reshape(1, -1).astype(F32)
        gkv = mla_kv_norm_g[l].reshape(1, -1).astype(F32)
        sink = sw_sink[l].reshape(1, -1).astype(F32)

        ml_l, na_l, sw_l, mla_lat = _project(hl, w, tab_l, gq, gkv, s + n_ctx)
        ml_c, na_c, sw_c, (mq, mk, mv) = _project(hc, w, tab_c, gq, gkv, s + n_ctx, mla_lat)

        qc_c, kc_c, gt_c = ml_prep(ml_c, ml_conv_w[l], ml_gate_b[l])
        qc_l, kc_l, gt_l = ml_prep(ml_l, ml_conv_w[l], ml_gate_b[l])
        hf_c, hb_c, cst, mst = ml_scan(qc_c, kc_c, ml_c, gt_c, zero_state)
        hf_l, hb_l, _, _ = ml_scan(qc_l, kc_l, ml_l, gt_l, (cst, mst))
        ya_l = ml_finish(hf_l, hb_l, ml_l, ml_norm_g[l])
        yb_l = na_attn(na_l, na_c, na_rpb[l], rows)
        yc_l = sw_attn(sw_l, sw_c, sink)
        yd_l = mla_flash(mq, mk, mv, s)

        nxt_l = (fin_row, zero_row, zero_row) if last else norm1(l + 1, 0)
        xl_new, hl_new = _dense_tail(xl, hl, (ya_l, yb_l, yc_l, yd_l), w, l, wb, wo, w_ffn_in, wf2, vt,
                                     mod_row(l, 0, 2), norm2(l, 0), mod_row(l, 0, 5), nxt_l, last)
        if not last:
            ya_c = ml_finish(hf_c, hb_c, ml_c, ml_norm_g[l])
            nw = NA_HEADS * NA_DIM
            yb_c = dense_attn([(na_c, nw, 0), (na_c, nw, 1), (na_c, nw, 2)], NA_HEADS, 1, NA_DIM, NA_DIM,
                              n_ctx, log2_scores=True)
            kcol = SW_HEADS * SW_DIM // LANE
            yc_c = dense_attn([(sw_c, SW_HEADS * SW_DIM, 0), (sw_c, LANE, kcol), (sw_c, LANE, kcol + 1)],
                              SW_HEADS, SW_HEADS // SW_KV_HEADS, SW_DIM, SW_DIM, n_ctx, sink=sink,
                              log2_scores=True)
            assert s % n_ctx == 0
            yd_c = dense_attn([(mq, MLA_HEADS * MLA_HW, 0), (mk, MLA_HEADS * MLA_HW, 0),
                               (mv, MLA_HEADS * MLA_V, 0)], MLA_HEADS, 1, MLA_HW, MLA_V, n_ctx,
                              log2_scores=True, row_block=s // n_ctx)
            xc, hc = _dense_tail(xc, hc, (ya_c, yb_c, yc_c, yd_c), w, l, wb, wo, w_ffn_in, wf2, vt,
                                 mod_row(l, 1, 2), norm2(l, 1), mod_row(l, 1, 5), norm1(l + 1, 1), False)
        xl, hl = xl_new, hl_new

    return hl[None].astype(x.dtype)
```

```python
import functools

import numpy as np
import jax
import jax.numpy as jnp
from jax import lax
from jax.experimental import pallas as pl
from jax.experimental.pallas import tpu as pltpu

F32 = jnp.float32
CD = jnp.bfloat16

GRID_W = 64
EPS = 1e-6
ROPE_THETA = 10000.0
ML_HEADS, ML_QK, ML_V, ML_CONV = 4, 64, 128, 3
NA_HEADS, NA_DIM, NA_WIN_R, NA_WIN_C = 8, 64, 8, 16
SW_HEADS, SW_KV_HEADS, SW_DIM, SW_WINDOW = 8, 2, 64, 128
MLA_HEADS, MLA_Q_RANK, MLA_KV_RANK, MLA_NOPE, MLA_ROPE, MLA_V = 4, 384, 256, 128, 64, 128
N_BRANCH, BRANCH_W = 4, 512

ML_QKW = ML_HEADS * ML_QK
ML_VW = ML_HEADS * ML_V
ML_COLS = 2 * ML_QKW + 2 * ML_VW + 4 * ML_HEADS
ML_PAD = 1664
NA_COLS = 3 * NA_HEADS * NA_DIM
SW_COLS = (SW_HEADS + 2 * SW_KV_HEADS) * SW_DIM
MLA_COLS = MLA_Q_RANK + MLA_KV_RANK + MLA_ROPE
MLA_HW = 256
LANE = 128
NEG = -1e30
LOG2E = float(np.log2(np.e))
VMEM_LIMIT = 56 * 1024 * 1024

NA_QROWS = 4
SW_TQ = 256
ML_CHUNK = 256
MLA_TK = 1280


def _cparams(*sem):
    return pltpu.CompilerParams(dimension_semantics=sem, vmem_limit_bytes=VMEM_LIMIT)


def _row_tile(t, pref=512):
    tm = min(pref, t)
    assert t % tm == 0
    return tm


def _dot(a, b):
    return jnp.dot(a, b, preferred_element_type=F32)


def _dot_nt(a, b):
    return lax.dot_general(a, b, (((1,), (1,)), ((), ())), preferred_element_type=F32)


def _sigmoid(x):
    return 1.0 / (1.0 + jnp.exp(-x))


def _ones_col(n, w):
    return jnp.where(lax.broadcasted_iota(jnp.int32, (n, w), 1) == 0, 1.0, 0.0).astype(CD)


def _adaln_kernel(c_ref, w_ref, b_ref, o_ref):
    w = w_ref[0]
    for r in range(2):
        c = c_ref[:, r:r + 1]
        o_ref[0, r:r + 1, :] = jnp.sum((c * _sigmoid(c)) * w, axis=0, keepdims=True) + b_ref[0]


def adaln(cc, w_ada, b_ada):
    nl, d, n = w_ada.shape
    tn = 1024
    return pl.pallas_call(
        _adaln_kernel,
        grid=(nl, n // tn),
        in_specs=[pl.BlockSpec((d, 2), lambda l, j: (0, 0)),
                  pl.BlockSpec((1, d, tn), lambda l, j: (l, 0, j)),
                  pl.BlockSpec((1, 1, tn), lambda l, j: (l, 0, j))],
        out_specs=pl.BlockSpec((1, 2, tn), lambda l, j: (l, 0, j)),
        out_shape=jax.ShapeDtypeStruct((nl, 2, n), F32),
        compiler_params=_cparams("parallel", "parallel"),
        name="adaln",
    )(cc, w_ada, b_ada.reshape(nl, 1, n))


def _vec_spec(d, idx):
    return pl.BlockSpec((1, 1, d), lambda *_: (idx, 0, 0))


def _norm_mod(x, g, shift, scale):
    y = x * lax.rsqrt(jnp.mean(x * x, axis=-1, keepdims=True) + EPS)
    return (y * g) * (1.0 + scale) + shift


def _norm_mod_kernel(x_ref, g_ref, sh_ref, sc_ref, o_ref):
    o_ref[...] = _norm_mod(x_ref[...], g_ref[0], sh_ref[0], sc_ref[0]).astype(o_ref.dtype)


def norm_mod(x, vt, norm_idx, out_dtype):
    t, d = x.shape
    tm = _row_tile(t, 256)
    return pl.pallas_call(
        _norm_mod_kernel,
        grid=(t // tm,),
        in_specs=[pl.BlockSpec((tm, d), lambda i: (i, 0))] + [_vec_spec(d, ix) for ix in norm_idx],
        out_specs=pl.BlockSpec((tm, d), lambda i: (i, 0)),
        out_shape=jax.ShapeDtypeStruct((t, d), out_dtype),
        compiler_params=_cparams("parallel"),
        name="norm_mod",
    )(x, vt, vt, vt)


def _mm_kernel(a_ref, b_ref, o_ref):
    o_ref[...] = _dot_nt(a_ref[...], b_ref[...]).astype(o_ref.dtype)


def matmul(a, bt, out_dtype, tn=None):
    t, k = a.shape
    n = bt.shape[0]
    tn = n if tn is None else tn
    tm = _row_tile(t)
    return pl.pallas_call(
        _mm_kernel,
        grid=(n // tn, t // tm),
        in_specs=[pl.BlockSpec((tm, k), lambda j, i: (i, 0)),
                  pl.BlockSpec((tn, k), lambda j, i: (j, 0))],
        out_specs=pl.BlockSpec((tm, tn), lambda j, i: (i, j)),
        out_shape=jax.ShapeDtypeStruct((t, n), out_dtype),
        compiler_params=_cparams("parallel", "parallel"),
        name="matmul",
    )(a, bt)


def _mm_res_norm_kernel(nj, tn, want_x, a_ref, b_ref, x_ref, gate_ref, g_ref, sh_ref, sc_ref, *rest):
    xbuf, ho_ref = (rest[0], rest[1]) if want_x else (rest[1], rest[0])
    j = pl.program_id(1)
    acc = _dot(a_ref[...], b_ref[0])
    for jj in range(nj):
        @pl.when(j == jj)
        def _(jj=jj):
            sl = slice(jj * tn, (jj + 1) * tn)
            xbuf[:, sl] = x_ref[:, sl] + gate_ref[0][:, sl] * acc

    @pl.when(j == nj - 1)
    def _():
        ho_ref[...] = _norm_mod(xbuf[...], g_ref[0], sh_ref[0], sc_ref[0]).astype(ho_ref.dtype)


def matmul_residual_norm(a, b, l, x, vt, gate_idx, norm_idx, want_x, h_dtype, tn=None):
    t, kdim = a.shape
    n = b.shape[2]
    tn = n if tn is None else tn
    nj = n // tn
    tm = _row_tile(t)
    row = pl.BlockSpec((tm, n), lambda i, j: (i, 0))
    out_shape = [jax.ShapeDtypeStruct((t, n), h_dtype)]
    if want_x:
        out_shape = [jax.ShapeDtypeStruct((t, n), F32)] + out_shape
    out = pl.pallas_call(
        functools.partial(_mm_res_norm_kernel, nj, tn, want_x),
        grid=(t // tm, nj),
        in_specs=[pl.BlockSpec((tm, kdim), lambda i, j: (i, 0)),
                  pl.BlockSpec((1, kdim, tn), lambda i, j: (l, 0, j)),
                  row, _vec_spec(n, gate_idx)] + [_vec_spec(n, ix) for ix in norm_idx],
        out_specs=[row] * len(out_shape),
        out_shape=out_shape,
        scratch_shapes=[] if want_x else [pltpu.VMEM((tm, n), F32)],
        compiler_params=_cparams("parallel", "arbitrary"),
        name="matmul_residual_norm",
    )(a, b, x, vt, vt, vt, vt)
    return out if want_x else out[0]


def _rope_mm_kernel(n_rope, a_ref, w_ref, wr_ref, cos_ref, sin_ref, o_ref):
    a = a_ref[...]
    p = _dot_nt(a, w_ref[...])
    pr = _dot_nt(a, wr_ref[...])
    cos = cos_ref[...]
    sin = sin_ref[...]
    for j in range(p.shape[1] // LANE):
        sl = slice(j * LANE, (j + 1) * LANE)
        if j < n_rope:
            o_ref[:, sl] = (p[:, sl] * cos + pr[:, sl] * sin).astype(o_ref.dtype)
        else:
            o_ref[:, sl] = p[:, sl].astype(o_ref.dtype)


def rope_matmul(a, w, w_rot, cos, sin, n_rope):
    t, k = a.shape
    n = w.shape[0]
    nr = n_rope * LANE
    tm = _row_tile(t)
    return pl.pallas_call(
        functools.partial(_rope_mm_kernel, n_rope),
        grid=(t // tm,),
        in_specs=[pl.BlockSpec((tm, k), lambda i: (i, 0)),
                  pl.BlockSpec((n, k), lambda i: (0, 0)),
                  pl.BlockSpec((nr, k), lambda i: (0, 0)),
                  pl.BlockSpec((tm, LANE), lambda i: (i, 0)),
                  pl.BlockSpec((tm, LANE), lambda i: (i, 0))],
        out_specs=pl.BlockSpec((tm, n), lambda i: (i, 0)),
        out_shape=jax.ShapeDtypeStruct((t, n), CD),
        compiler_params=_cparams("parallel"),
        name="rope_matmul",
    )(a, w, w_rot, cos, sin)


def _mla_proj_kernel(h_ref, w1_ref, gq_ref, gkv_ref, wq_ref, wqr_ref, wkv_ref,
                     cq_ref, sq_ref, ck_ref, sk_ref, q_out, k_out, v_out):
    p = _dot_nt(h_ref[...], w1_ref[...])
    cq = p[:, :MLA_Q_RANK]
    ckv = p[:, MLA_Q_RANK:MLA_Q_RANK + MLA_KV_RANK]
    o = MLA_Q_RANK + MLA_KV_RANK
    kr = p[:, o:o + LANE]
    krr = p[:, o + LANE:o + 2 * LANE]
    cqn = (cq * lax.rsqrt(jnp.mean(cq * cq, axis=-1, keepdims=True) + EPS) * gq_ref[...]).astype(CD)
    ckvn = (ckv * lax.rsqrt(jnp.mean(ckv * ckv, axis=-1, keepdims=True) + EPS) * gkv_ref[...]).astype(CD)
    qa = _dot(cqn, wq_ref[...])
    qb = _dot(cqn, wqr_ref[...])
    kv = _dot(ckvn, wkv_ref[...])
    krp = (kr * ck_ref[...] + krr * sk_ref[...]).astype(CD)
    cq_t = cq_ref[...]
    sq_t = sq_ref[...]
    for h in range(MLA_HEADS):
        qs = slice(h * MLA_HW, (h + 1) * MLA_HW)
        q_out[:, qs] = (qa[:, qs] * cq_t + qb[:, qs] * sq_t).astype(CD)
        k_out[:, h * MLA_HW:h * MLA_HW + MLA_NOPE] = kv[:, h * 256:h * 256 + MLA_NOPE].astype(CD)
        k_out[:, h * MLA_HW + MLA_NOPE:(h + 1) * MLA_HW] = krp
        v_out[:, h * MLA_V:(h + 1) * MLA_V] = kv[:, h * 256 + MLA_NOPE:(h + 1) * 256].astype(CD)


def mla_proj(h, w1, gq, gkv, wq, wqr, wkv, cos_q, sin_q, cos_k, sin_k, total_rows, into=None):
    t, d = h.shape
    tm = _row_tile(t)
    full = lambda a: pl.BlockSpec(a.shape, lambda i: (0,) * a.ndim)
    rows = lambda w: pl.BlockSpec((tm, w), lambda i: (i, 0))
    hw = MLA_HEADS * MLA_HW
    vw = MLA_HEADS * MLA_V
    args = [h, w1, gq, gkv, wq, wqr, wkv, cos_q, sin_q, cos_k, sin_k]
    in_specs = [rows(d), full(w1), full(gq), full(gkv), full(wq), full(wqr), full(wkv),
                rows(MLA_HW), rows(MLA_HW), rows(LANE), rows(LANE)]
    if into is None:
        off, aliases, kern = 0, {}, _mla_proj_kernel
    else:
        assert (total_rows - t) % tm == 0
        off = (total_rows - t) // tm
        aliases = {len(args) + i: i for i in range(3)}
        in_specs = in_specs + [pl.BlockSpec(memory_space=pl.ANY)] * 3
        args = args + list(into)
        kern = lambda *refs: _mla_proj_kernel(*refs[:11], *refs[14:])
    orow = lambda w: pl.BlockSpec((tm, w), lambda i: (off + i, 0))
    return pl.pallas_call(
        kern,
        grid=(t // tm,),
        in_specs=in_specs,
        out_specs=[orow(hw), orow(hw), orow(vw)],
        out_shape=[jax.ShapeDtypeStruct((total_rows, hw), CD), jax.ShapeDtypeStruct((total_rows, hw), CD),
                   jax.ShapeDtypeStruct((total_rows, vw), CD)],
        input_output_aliases=aliases,
        compiler_params=_cparams("parallel"),
        name="mla_proj",
    )(*args)


def _flash_kernel(tq, tk, unroll, q_ref, k_ref, v_ref, o_ref, s_sc, m_sc, acc_sc):
    nkb = k_ref.shape[0] // tk
    nsub = q_ref.shape[0] // tq
    total = nsub * nkb
    m_sc[...] = jnp.full(m_sc.shape, NEG, F32)
    acc_sc[...] = jnp.zeros(acc_sc.shape, F32)
    ones_col = jnp.where(lax.broadcasted_iota(jnp.int32, (tk, LANE), 1) == 0, 1.0, 0.0).astype(CD)

    def split(b):
        sub = b // nkb
        return sub, b - sub * nkb

    def scores(b):
        sub, j = split(b)
        q = q_ref[pl.ds(pl.multiple_of(sub * tq, tq), tq), :]
        k = k_ref[pl.ds(pl.multiple_of(j * tk, tk), tk), :]
        return _dot_nt(q, k)

    def softmax_pv(s, b):
        sub, j = split(b)
        v = v_ref[pl.ds(pl.multiple_of(j * tk, tk), tk), :]
        m_prev = m_sc[sub]
        m_new = jnp.maximum(m_prev, jnp.max(s, axis=-1, keepdims=True))
        alpha = jnp.exp2(m_prev - m_new)
        p = jnp.exp2(s - m_new).astype(CD)
        v_aug = jnp.concatenate([v, ones_col], axis=1)
        acc_sc[sub] = alpha * acc_sc[sub] + _dot(p, v_aug)
        m_sc[sub] = m_new

    s_sc[0] = scores(0)

    def body(g, carry):
        b0 = g * unroll
        for u in range(unroll):
            s_sc[(u + 1) % 2] = scores(jnp.minimum(b0 + u + 1, total - 1))
            softmax_pv(s_sc[u % 2], b0 + u)
        return carry

    lax.fori_loop(0, total // unroll, body, 0)
    for sub in range(nsub):
        acc = acc_sc[sub]
        o_ref[sub * tq:(sub + 1) * tq, :] = (acc[:, :MLA_V] / acc[:, MLA_V:MLA_V + 1]).astype(o_ref.dtype)


def mla_flash(q, k, v, s):
    nk = k.shape[0]
    tq = _row_tile(s, 512)
    tstep = _row_tile(s, 8 * tq)
    tk = MLA_TK
    total = (tstep // tq) * (nk // tk)
    assert nk % tk == 0 and total % 2 == 0
    unroll = 8 if total % 8 == 0 else (4 if total % 4 == 0 else 2)
    return pl.pallas_call(
        functools.partial(_flash_kernel, tq, tk, unroll),
        grid=(MLA_HEADS, s // tstep),
        in_specs=[pl.BlockSpec((tstep, MLA_HW), lambda h, i: (i, h)),
                  pl.BlockSpec((nk, MLA_HW), lambda h, i: (0, h)),
                  pl.BlockSpec((nk, MLA_V), lambda h, i: (0, h))],
        out_specs=pl.BlockSpec((tstep, MLA_V), lambda h, i: (i, h)),
        out_shape=jax.ShapeDtypeStruct((s, MLA_HEADS * MLA_V), CD),
        scratch_shapes=[pltpu.VMEM((2, tq, tk), F32), pltpu.VMEM((tstep // tq, tq, 1), F32),
                        pltpu.VMEM((tstep // tq, tq, 2 * MLA_V), F32)],
        compiler_params=_cparams("parallel", "arbitrary"),
        name="mla_flash",
    )(q, k, v)


def _dense_attn_kernel(n_heads, group, dqk, dv, log2_scores, q_ref, k_ref, v_ref, sink_ref, o_ref):
    q = q_ref[...]
    k = k_ref[...]
    v = v_ref[...]
    ex = jnp.exp2 if log2_scores else jnp.exp
    for h in range(n_heads):
        g = h // group
        s = _dot_nt(q[:, h * dqk:(h + 1) * dqk], k[:, g * dqk:(g + 1) * dqk])
        m = jnp.max(s, axis=-1, keepdims=True)
        l = jnp.zeros_like(m)
        if sink_ref is not None:
            sk = sink_ref[0:1, h:h + 1] * (LOG2E if log2_scores else 1.0)
            m = jnp.maximum(m, sk)
            l = ex(sk - m)
        p = ex(s - m)
        l = l + jnp.sum(p, axis=-1, keepdims=True)
        o = _dot(p.astype(CD), v[:, g * dv:(g + 1) * dv]) / l
        o_ref[:, h * dv:(h + 1) * dv] = o.astype(o_ref.dtype)


def dense_attn(qkv_specs, n_heads, group, dqk, dv, c, sink=None, log2_scores=False, row_block=0):
    arrays = [a for a, _, _ in qkv_specs]
    specs = [pl.BlockSpec((c, w), functools.partial(lambda b, i: (row_block, b), b)) for _, w, b in qkv_specs]
    if sink is None:
        kern = lambda q, k, v, o: _dense_attn_kernel(n_heads, group, dqk, dv, log2_scores, q, k, v, None, o)
    else:
        kern = functools.partial(_dense_attn_kernel, n_heads, group, dqk, dv, log2_scores)
        arrays.append(sink)
        specs.append(pl.BlockSpec(sink.shape, lambda i: (0, 0)))
    return pl.pallas_call(
        kern,
        grid=(1,),
        in_specs=specs,
        out_specs=pl.BlockSpec((c, n_heads * dv), lambda i: (0, 0)),
        out_shape=jax.ShapeDtypeStruct((c, n_heads * dv), CD),
        compiler_params=_cparams("arbitrary"),
        name="dense_attn",
    )(*arrays)


def _na_kernel(q_ref, kp_ref, ko_ref, kn_ref, vp_ref, vo_ref, vn_ref, kc_ref, vc_ref, bias_ref, o_ref):
    q = q_ref[...]
    kcat = jnp.concatenate([kp_ref[...], ko_ref[...], kn_ref[...], kc_ref[...]], axis=0)
    vcat = jnp.concatenate([vp_ref[...], vo_ref[...], vn_ref[...], vc_ref[...]], axis=0)
    nloc = 3 * q.shape[0]
    ones = _ones_col(kcat.shape[0], NA_DIM)
    for h in range(NA_HEADS):
        hs = slice(h * NA_DIM, (h + 1) * NA_DIM)
        s = _dot_nt(q[:, hs], kcat[:, hs])
        s_loc = s[:, :nloc] + bias_ref[0, h]
        s_ctx = s[:, nloc:]
        m = jnp.maximum(jnp.max(s_loc, axis=-1, keepdims=True), jnp.max(s_ctx, axis=-1, keepdims=True))
        p = jnp.concatenate([jnp.exp2(s_loc - m), jnp.exp2(s_ctx - m)], axis=1).astype(CD)
        acc = _dot(p, jnp.concatenate([vcat[:, hs], ones], axis=1))
        o_ref[:, hs] = (acc[:, :NA_DIM] / acc[:, NA_DIM:NA_DIM + 1]).astype(o_ref.dtype)


def _na_bias(rpb, rows):
    qr = NA_QROWS
    tb = qr * GRID_W
    nb = rows // qr
    nkr = 3 * qr
    rsel = np.zeros((3, qr, nkr, 2 * NA_WIN_R - 1), np.float32)
    for v, b in enumerate((0, 1, nb - 1)):
        for a in range(qr):
            r = qr * b + a
            rs = min(max(r - NA_WIN_R // 2, 0), rows - NA_WIN_R)
            for u in range(nkr):
                kr = qr * (b - 1) + u
                if 0 <= b - 1 + u // qr < nb and rs <= kr < rs + NA_WIN_R:
                    rsel[v, a, u, kr - r + NA_WIN_R - 1] = 1.0
    csel = np.zeros((GRID_W, GRID_W, 2 * NA_WIN_C - 1), np.float32)
    for qc in range(GRID_W):
        cs = min(max(qc - NA_WIN_C // 2, 0), GRID_W - NA_WIN_C)
        for kc in range(cs, cs + NA_WIN_C):
            csel[qc, kc, kc - qc + NA_WIN_C - 1] = 1.0
    valid = (np.einsum('vaud,qkj->vaquk', rsel, csel) > 0).reshape(3, tb, 3 * tb)
    assert 2 * qr == NA_WIN_R
    nd = 2 * NA_WIN_R - 1
    rp = rpb.astype(F32) * LOG2E
    gap = jnp.zeros((NA_HEADS, nd, GRID_W + 1 - (2 * NA_WIN_C - 1)), F32)
    ring = jnp.concatenate([rp[..., NA_WIN_C - 1:], gap, rp[..., :NA_WIN_C - 1]], axis=-1)
    cols = jnp.tile(ring, (1, 1, GRID_W))[..., :GRID_W * GRID_W].reshape(NA_HEADS, nd, GRID_W, GRID_W)
    off = NA_WIN_R - 1 - qr
    slabs = [cols[:, off - a:off - a + nkr].transpose(0, 2, 1, 3).reshape(NA_HEADS, GRID_W, 3 * tb)
             for a in range(qr)]
    band = jnp.stack(slabs, axis=1).reshape(NA_HEADS, tb, 3 * tb)
    return jnp.where(valid[:, None], band[None], NEG)


def na_attn(p_lat, p_ctx, rpb, rows):
    s = p_lat.shape[0]
    c = p_ctx.shape[0]
    w = NA_HEADS * NA_DIM
    tb = NA_QROWS * GRID_W
    nb = s // tb
    assert rows % NA_QROWS == 0 and nb >= 3 and rows >= 2 * NA_WIN_R
    bias = _na_bias(rpb, rows)

    def blk(col, off):
        return pl.BlockSpec((tb, w), lambda i: (jnp.clip(i + off, 0, nb - 1), col))

    return pl.pallas_call(
        _na_kernel,
        grid=(nb,),
        in_specs=[blk(0, 0), blk(1, -1), blk(1, 0), blk(1, 1), blk(2, -1), blk(2, 0), blk(2, 1),
                  pl.BlockSpec((c, w), lambda i: (0, 1)),
                  pl.BlockSpec((c, w), lambda i: (0, 2)),
                  pl.BlockSpec((1, NA_HEADS, tb, 3 * tb),
                               lambda i: (jnp.where(i == 0, 0, jnp.where(i == nb - 1, 2, 1)), 0, 0, 0))],
        out_specs=pl.BlockSpec((tb, w), lambda i: (i, 0)),
        out_shape=jax.ShapeDtypeStruct((s, w), CD),
        compiler_params=_cparams("parallel"),
        name="na_attn",
    )(p_lat, p_lat, p_lat, p_lat, p_lat, p_lat, p_lat, p_ctx, p_ctx, bias)


def _sw_kernel(seq, q_ref, kp_ref, ko_ref, kn_ref, vp_ref, vo_ref, vn_ref, kc_ref, vc_ref, sink_ref, o_ref):
    i = pl.program_id(0)
    q = q_ref[...]
    tq = q.shape[0]
    kcat = jnp.concatenate([kp_ref[...], ko_ref[...], kn_ref[...], kc_ref[...]], axis=0)
    vcat = jnp.concatenate([vp_ref[...], vo_ref[...], vn_ref[...], vc_ref[...]], axis=0)
    nkk = tq + 2 * SW_WINDOW
    r = lax.broadcasted_iota(jnp.int32, (tq, nkk), 0)
    cc = lax.broadcasted_iota(jnp.int32, (tq, nkk), 1)
    kpos = i * tq - SW_WINDOW + cc
    d = cc - r
    mask = (d >= 0) & (d <= 2 * SW_WINDOW) & (kpos >= 0) & (kpos < seq)
    ones = _ones_col(kcat.shape[0], SW_DIM)
    grp = SW_HEADS // SW_KV_HEADS
    for g in range(SW_KV_HEADS):
        gs = slice(g * SW_DIM, (g + 1) * SW_DIM)
        kg = kcat[:, gs]
        v_aug = jnp.concatenate([vcat[:, gs], ones], axis=1)
        for h in range(g * grp, (g + 1) * grp):
            s = _dot_nt(q[:, h * SW_DIM:(h + 1) * SW_DIM], kg)
            s_loc = jnp.where(mask, s[:, :nkk], NEG)
            s_ctx = s[:, nkk:]
            sk = sink_ref[0:1, h:h + 1] * LOG2E
            m = jnp.maximum(jnp.maximum(jnp.max(s_loc, axis=-1, keepdims=True),
                                        jnp.max(s_ctx, axis=-1, keepdims=True)), sk)
            p = jnp.concatenate([jnp.exp2(s_loc - m), jnp.exp2(s_ctx - m)], axis=1).astype(CD)
            acc = _dot(p, v_aug)
            l = acc[:, SW_DIM:SW_DIM + 1] + jnp.exp2(sk - m)
            o_ref[:, h * SW_DIM:(h + 1) * SW_DIM] = (acc[:, :SW_DIM] / l).astype(o_ref.dtype)


def sw_attn(p_lat, p_ctx, sink):
    s = p_lat.shape[0]
    c = p_ctx.shape[0]
    tq = SW_TQ
    assert s % tq == 0 and tq % SW_WINDOW == 0
    nq = s // tq
    per = tq // SW_WINDOW
    nwb = s // SW_WINDOW
    kcol = SW_HEADS * SW_DIM // LANE
    prev = lambda col: pl.BlockSpec((SW_WINDOW, LANE), lambda i: (jnp.maximum(i * per - 1, 0), col))
    own = lambda col: pl.BlockSpec((tq, LANE), lambda i: (i, col))
    nxt = lambda col: pl.BlockSpec((SW_WINDOW, LANE), lambda i: (jnp.minimum((i + 1) * per, nwb - 1), col))
    return pl.pallas_call(
        functools.partial(_sw_kernel, s),
        grid=(nq,),
        in_specs=[pl.BlockSpec((tq, SW_HEADS * SW_DIM), lambda i: (i, 0)),
                  prev(kcol), own(kcol), nxt(kcol), prev(kcol + 1), own(kcol + 1), nxt(kcol + 1),
                  pl.BlockSpec((c, LANE), lambda i: (0, kcol)),
                  pl.BlockSpec((c, LANE), lambda i: (0, kcol + 1)),
                  pl.BlockSpec(sink.shape, lambda i: (0, 0))],
        out_specs=pl.BlockSpec((tq, SW_HEADS * SW_DIM), lambda i: (i, 0)),
        out_shape=jax.ShapeDtypeStruct((s, SW_HEADS * SW_DIM), CD),
        compiler_params=_cparams("parallel"),
        name="sw_attn",
    )(p_lat, p_lat, p_lat, p_lat, p_lat, p_lat, p_lat, p_ctx, p_ctx, sink)


def _ml_prep_kernel(nt, x_ref, xp_ref, xn_ref, g_ref, w_ref, b_ref, q_out, k_out, g_out):
    i = pl.program_id(0)
    x = x_ref[...]
    tm = x.shape[0]
    row = lax.broadcasted_iota(jnp.int32, x.shape, 0)
    has_prev = jnp.where(i > 0, 1.0, 0.0)
    has_next = jnp.where(i < nt - 1, 1.0, 0.0)
    prev_row = xp_ref[7:8, :] * has_prev
    next_row = xn_ref[0:1, :] * has_next
    xm = jnp.where(row == 0, prev_row, pltpu.roll(x, 1, axis=0))
    xq = jnp.where(row == tm - 1, next_row, pltpu.roll(x, tm - 1, axis=0))
    y = w_ref[0:1, :] * xm + w_ref[1:2, :] * x + w_ref[2:3, :] * xq
    y = y * _sigmoid(y)
    q_out[...] = y[:, :ML_QKW].astype(CD)
    k_out[...] = (y[:, ML_QKW:] * (ML_QK ** -0.5)).astype(CD)
    g = g_ref[...] + b_ref[...]
    lane = lax.broadcasted_iota(jnp.int32, g.shape, 1)
    logsig = jnp.minimum(g, 0.0) - jnp.log(1.0 + jnp.exp(-jnp.abs(g)))
    g_out[...] = jnp.where(lane < 2 * ML_HEADS, g, logsig)


def ml_prep(p_ml, conv_w, gate_b):
    t = p_ml.shape[0]
    tm = _row_tile(t, 256)
    nt = t // tm
    wq = 2 * ML_QKW
    gcol = (2 * ML_QKW + 2 * ML_VW) // LANE
    w8 = jnp.zeros((8, wq), F32).at[:ML_CONV].set(conv_w.astype(F32))
    b = jnp.zeros((1, LANE), F32).at[0, :4 * ML_HEADS].set(gate_b.astype(F32).reshape(-1))
    h8 = tm // 8
    return pl.pallas_call(
        functools.partial(_ml_prep_kernel, nt),
        grid=(nt,),
        in_specs=[pl.BlockSpec((tm, wq), lambda i: (i, 0)),
                  pl.BlockSpec((8, wq), lambda i: (jnp.maximum(i * h8 - 1, 0), 0)),
                  pl.BlockSpec((8, wq), lambda i: (jnp.minimum((i + 1) * h8, t // 8 - 1), 0)),
                  pl.BlockSpec((tm, LANE), lambda i: (i, gcol)),
                  pl.BlockSpec((8, wq), lambda i: (0, 0)),
                  pl.BlockSpec((1, LANE), lambda i: (0, 0))],
        out_specs=[pl.BlockSpec((tm, ML_QKW), lambda i: (i, 0)),
                   pl.BlockSpec((tm, ML_QKW), lambda i: (i, 0)),
                   pl.BlockSpec((tm, LANE), lambda i: (i, 0))],
        out_shape=[jax.ShapeDtypeStruct((t, ML_QKW), CD), jax.ShapeDtypeStruct((t, ML_QKW), CD),
                   jax.ShapeDtypeStruct((t, LANE), F32)],
        compiler_params=_cparams("parallel"),
        name="ml_prep",
    )(p_ml, p_ml, p_ml, p_ml, w8, b)


def _ml_scan_kernel(qf_ref, ktf_ref, vf_ref, gcf_ref, grf_ref,
                    qb_ref, ktb_ref, vb_ref, gcb_ref, grb_ref,
                    c0_ref, m0_ref, hf_ref, hb_ref, c_ref, m_ref):
    @pl.when(pl.program_id(0) == 0)
    def _():
        c_ref[...] = c0_ref[...]
        m_ref[...] = m0_ref[...]

    ln = qf_ref.shape[0]
    r = lax.broadcasted_iota(jnp.int32, (ln, ln), 0)
    c = lax.broadcasted_iota(jnp.int32, (ln, ln), 1)
    lower = r >= c
    upper = c >= r
    lower_f = jnp.where(lower, 1.0, 0.0)
    upper_f = jnp.where(upper, 1.0, 0.0)
    ones_col = jnp.where(lax.broadcasted_iota(jnp.int32, (ln, LANE), 1) == 0, 1.0, 0.0).astype(CD)
    hi = lax.Precision.HIGHEST

    streams = ((qf_ref, ktf_ref, vf_ref, gcf_ref, grf_ref, hf_ref, lower, lower_f, upper_f, ln - 1),
               (qb_ref, ktb_ref, vb_ref, gcb_ref, grb_ref, hb_ref, upper, upper_f, lower_f, 0))
    for d, (q_ref, kt_ref, v_ref, gc_ref, gr_ref, h_ref, mask, tri_c, tri_r, last) in enumerate(streams):
        gcol = gc_ref[...]
        grow = gr_ref[...]
        bcol = jnp.dot(tri_c, gcol, precision=hi, preferred_element_type=F32)
        brow = jnp.dot(grow, tri_r, precision=hi, preferred_element_type=F32)
        q = q_ref[...]
        kt = kt_ref[...]
        v = v_ref[...]
        for hd in range(ML_HEADS):
            ch = d * ML_HEADS + hd
            ic = d * ML_HEADS + hd
            fc = 2 * ML_HEADS + ic
            i_c = gcol[:, ic:ic + 1]
            b_c = bcol[:, fc:fc + 1]
            i_r = grow[ic:ic + 1, :]
            b_r = brow[fc:fc + 1, :]
            m_prev = m_ref[ch, 0:1, 0:1]
            cst = c_ref[ch]
            dmat = jnp.where(mask, b_c - b_r + i_r, NEG)
            inter = b_c + m_prev
            m_t = jnp.maximum(inter, jnp.max(dmat, axis=-1, keepdims=True))
            qh = q[:, hd * ML_QK:(hd + 1) * ML_QK]
            kth = kt[hd * ML_QK:(hd + 1) * ML_QK, :]
            s = lax.dot_general(qh, kth, (((1,), (0,)), ((), ())), preferred_element_type=F32)
            s = s * jnp.exp(dmat - m_t)
            w_inter = jnp.exp(inter - m_t)
            v_aug = jnp.concatenate([v[:, hd * ML_V:(hd + 1) * ML_V].astype(CD), ones_col], axis=1)
            num = _dot(s.astype(CD), v_aug) + w_inter * _dot(qh, cst.astype(CD))
            den = num[:, ML_V:ML_V + 1]
            h = num[:, :ML_V] / jnp.maximum(jnp.abs(den), jnp.exp(-m_t))
            h_ref[:, hd * ML_V:(hd + 1) * ML_V] = h
            b_end = b_c[last:last + 1, :]
            m_new = jnp.maximum(b_end + m_prev, jnp.max(b_end - b_c + i_c, axis=0, keepdims=True))
            w_s = jnp.exp(b_end - b_r + i_r - m_new)
            w_c = jnp.exp(b_end + m_prev - m_new)
            ktw = (kth.astype(F32) * w_s).astype(CD)
            c_ref[ch] = w_c * cst + _dot(ktw, v_aug)
            m_ref[ch] = jnp.broadcast_to(m_new, m_ref.shape[1:])


def ml_scan(qc, kc, p_ml, gates, state):
    t = qc.shape[0]
    ln = min(ML_CHUNK, t)
    assert t % ln == 0
    nc = t // ln
    kt = kc.T
    gr = gates[:, :4 * ML_HEADS].T
    c0, m0 = state
    vcol = 2 * ML_QKW // ML_VW
    fwd = lambda j: j
    bwd = lambda j: nc - 1 - j

    def specs(ix):
        return [pl.BlockSpec((ln, ML_QKW), lambda j: (ix(j), 0)),
                pl.BlockSpec((ML_QKW, ln), lambda j: (0, ix(j))),
                pl.BlockSpec((ln, ML_VW), lambda j: (ix(j), vcol)),
                pl.BlockSpec((ln, LANE), lambda j: (ix(j), 0)),
                pl.BlockSpec((4 * ML_HEADS, ln), lambda j: (0, ix(j)))]

    st_specs = [pl.BlockSpec(c0.shape, lambda j: (0, 0, 0)), pl.BlockSpec(m0.shape, lambda j: (0, 0, 0))]
    return pl.pallas_call(
        _ml_scan_kernel,
        grid=(nc,),
        in_specs=specs(fwd) + specs(bwd) + st_specs,
        out_specs=[pl.BlockSpec((ln, ML_VW), lambda j: (fwd(j), 0)),
                   pl.BlockSpec((ln, ML_VW), lambda j: (bwd(j), 0))] + st_specs,
        out_shape=[jax.ShapeDtypeStruct((t, ML_VW), F32), jax.ShapeDtypeStruct((t, ML_VW), F32),
                   jax.ShapeDtypeStruct(c0.shape, F32), jax.ShapeDtypeStruct(m0.shape, F32)],
        compiler_params=_cparams("arbitrary"),
        name="ml_scan",
    )(qc, kt, p_ml, gates, gr, qc, kt, p_ml, gates, gr, c0, m0)


def _ml_finish_kernel(hf_ref, hb_ref, o_ref, g_ref, y_ref):
    h = hf_ref[...] + hb_ref[...]
    og = _sigmoid(o_ref[...])
    g = g_ref[...]
    for hd in range(ML_HEADS):
        sl = slice(hd * ML_V, (hd + 1) * ML_V)
        hh = h[:, sl]
        y = hh * lax.rsqrt(jnp.mean(hh * hh, axis=-1, keepdims=True) + EPS) * g[:, sl]
        y_ref[:, sl] = (y * og[:, sl]).astype(y_ref.dtype)


def ml_finish(hf, hb, p_ml, norm_g):
    t = hf.shape[0]
    tm = _row_tile(t, 256)
    ocol = (2 * ML_QKW + ML_VW) // ML_VW
    blk = pl.BlockSpec((tm, ML_VW), lambda i: (i, 0))
    return pl.pallas_call(
        _ml_finish_kernel,
        grid=(t // tm,),
        in_specs=[blk, blk, pl.BlockSpec((tm, ML_VW), lambda i: (i, ocol)),
                  pl.BlockSpec((1, ML_VW), lambda i: (0, 0))],
        out_specs=blk,
        out_shape=jax.ShapeDtypeStruct((t, ML_VW), CD),
        compiler_params=_cparams("parallel"),
        name="ml_finish",
    )(hf, hb, p_ml, norm_g.reshape(1, ML_VW).astype(F32))


def _merge_kernel(h_ref, ya_ref, yb_ref, yc_ref, yd_ref, g0_ref, g1_ref, g2_ref, g3_ref, wb_ref, o_ref):
    h = h_ref[...]
    acc = None
    for i, (y_ref, g_ref) in enumerate(zip((ya_ref, yb_ref, yc_ref, yd_ref), (g0_ref, g1_ref, g2_ref, g3_ref))):
        term = _sigmoid(_dot_nt(h, g_ref[...])) * _dot(y_ref[...], wb_ref[0, i])
        acc = term if acc is None else acc + term
    o_ref[...] = acc.astype(o_ref.dtype)


def gated_merge(h, ys, w_gate, w_branch, l, tn=512):
    t, d = h.shape
    tm = _row_tile(t, 1024)
    nj = d // tn
    yspec = pl.BlockSpec((tm, BRANCH_W), lambda j, i: (i, 0))
    gspec = lambda b: pl.BlockSpec((tn, d), lambda j, i: (b * nj + j, 0))
    return pl.pallas_call(
        _merge_kernel,
        grid=(nj, t // tm),
        in_specs=[pl.BlockSpec((tm, d), lambda j, i: (i, 0)), yspec, yspec, yspec, yspec,
                  gspec(0), gspec(1), gspec(2), gspec(3),
                  pl.BlockSpec((1, N_BRANCH, BRANCH_W, tn), lambda j, i: (l, 0, 0, j))],
        out_specs=pl.BlockSpec((tm, tn), lambda j, i: (i, j)),
        out_shape=jax.ShapeDtypeStruct((t, d), CD),
        compiler_params=_cparams("parallel", "parallel"),
        name="gated_merge",
    )(h, *ys, w_gate, w_gate, w_gate, w_gate, w_branch)


def _ffn1_kernel(h_ref, wa_ref, wg_ref, o_ref, wa_sc, wg_sc):
    @pl.when(pl.program_id(1) == 0)
    def _():
        wa_sc[...] = wa_ref[0].astype(CD)
        wg_sc[...] = wg_ref[0].astype(CD)

    h = h_ref[...]
    a = _dot(h, wa_sc[...])
    g = _dot(h, wg_sc[...])
    o_ref[...] = (a * _sigmoid(a) * g).astype(o_ref.dtype)


def ffn_up(h, w, l, tn=512):
    t, d = h.shape
    dff = w.shape[2] // 2
    tm = _row_tile(t, 1024)
    nj = dff // tn
    return pl.pallas_call(
        _ffn1_kernel,
        grid=(nj, t // tm),
        in_specs=[pl.BlockSpec((tm, d), lambda j, i: (i, 0)),
                  pl.BlockSpec((1, d, tn), lambda j, i: (l, 0, j)),
                  pl.BlockSpec((1, d, tn), lambda j, i: (l, 0, nj + j))],
        out_specs=pl.BlockSpec((tm, tn), lambda j, i: (i, j)),
        out_shape=jax.ShapeDtypeStruct((t, dff), CD),
        scratch_shapes=[pltpu.VMEM((d, tn), CD), pltpu.VMEM((d, tn), CD)],
        compiler_params=_cparams("parallel", "arbitrary"),
        name="ffn_up",
    )(h, w, w)


def _rot_cols(w, dim):
    k, n = w.shape
    w4 = w.reshape(k, n // dim, 2, dim // 2)
    return jnp.stack([-w4[:, :, 1], w4[:, :, 0]], axis=2).reshape(k, n)


def _rope_tables(n_tokens, dim):
    a = dim // 2
    inv = 1.0 / (ROPE_THETA ** (jnp.arange(0, a, 2, dtype=F32) / a))
    rows = n_tokens // GRID_W
    assert rows * GRID_W == n_tokens
    ang_r = jnp.arange(rows, dtype=F32)[:, None] * inv
    ang_c = jnp.arange(GRID_W, dtype=F32)[:, None] * inv

    def expand(fr, fc):
        shape = (rows, GRID_W, a // 2)
        return jnp.concatenate([jnp.broadcast_to(fr[:, None, :], shape), jnp.broadcast_to(fc[None], shape)],
                               axis=-1).reshape(n_tokens, a)

    return expand(jnp.cos(ang_r), jnp.cos(ang_c)), expand(jnp.sin(ang_r), jnp.sin(ang_c))


_O_NA = ML_COLS
_O_SW = _O_NA + NA_COLS
_O_MLA = _O_SW + SW_COLS
_O_KR = _O_MLA + MLA_Q_RANK + MLA_KV_RANK
_O_GATE = _O_MLA + MLA_COLS


def _rot_half_rows(x, dim):
    half = dim // 2
    parts = []
    for b in range(x.shape[0] // dim):
        parts += [-x[b * dim + half:(b + 1) * dim], x[b * dim:b * dim + half]]
    return jnp.concatenate(parts, axis=0)


def _repack_kernel(w_ref, ml_ref, na_ref, sw_ref, swr_ref, mla_ref, gate_ref):
    w = w_ref[0]
    cols = w.shape[1]
    zeros = lambda n: jnp.zeros((n, cols), F32)
    ml_ref[...] = jnp.concatenate([w[:ML_COLS], zeros(ML_PAD - ML_COLS)], axis=0).astype(CD)
    nq = NA_HEADS * NA_DIM
    na_ref[...] = jnp.concatenate([w[_O_NA:_O_NA + nq] * (NA_DIM ** -0.5 * LOG2E), w[_O_NA + nq:_O_SW]],
                                  axis=0).astype(CD)
    sq = SW_HEADS * SW_DIM
    skv = SW_KV_HEADS * SW_DIM
    sw_q = w[_O_SW:_O_SW + sq] * (SW_DIM ** -0.5 * LOG2E)
    sw_k = w[_O_SW + sq:_O_SW + sq + skv]
    sw_ref[...] = jnp.concatenate([sw_q, w[_O_SW + sq:_O_MLA]], axis=0).astype(CD)
    swr_ref[...] = _rot_half_rows(jnp.concatenate([sw_q, sw_k], axis=0), SW_DIM).astype(CD)
    kr = w[_O_KR:_O_GATE]
    pad = zeros(LANE - MLA_ROPE)
    mla_ref[...] = jnp.concatenate([w[_O_MLA:_O_KR], kr, pad, _rot_half_rows(kr, MLA_ROPE), pad], axis=0).astype(CD)
    gate_ref[...] = w[_O_GATE:].astype(CD)


def _repack_w_in(w_in_t, l):
    _, n, d = w_in_t.shape
    tc = 256
    heights = (ML_PAD, NA_COLS, SW_COLS, (SW_HEADS + SW_KV_HEADS) * SW_DIM,
               MLA_Q_RANK + MLA_KV_RANK + 2 * LANE, n - _O_GATE)
    return pl.pallas_call(
        _repack_kernel,
        grid=(d // tc,),
        in_specs=[pl.BlockSpec((1, n, tc), lambda i: (l, 0, i))],
        out_specs=[pl.BlockSpec((ht, tc), lambda i: (0, i)) for ht in heights],
        out_shape=[jax.ShapeDtypeStruct((ht, d), CD) for ht in heights],
        compiler_params=_cparams("parallel"),
        name="repack_w_in",
    )(w_in_t)


def _layer_weights(w_in_t, l, mla_w_uq_l, mla_w_ukv_l):
    w = {}
    w["ml"], w["na"], w["sw"], w["sw_rot"], w["mla1"], w["gate"] = _repack_w_in(w_in_t, l)
    uq = mla_w_uq_l.reshape(MLA_Q_RANK, MLA_HEADS, MLA_NOPE + MLA_ROPE)
    zq = jnp.zeros((MLA_Q_RANK, MLA_HEADS, MLA_HW - MLA_NOPE - MLA_ROPE), F32)
    w["mla_q"] = jnp.concatenate([uq, zq], axis=2).reshape(MLA_Q_RANK, -1).astype(CD)
    uqr = _rot_cols(uq[:, :, MLA_NOPE:].reshape(MLA_Q_RANK, -1), MLA_ROPE).reshape(MLA_Q_RANK, MLA_HEADS, MLA_ROPE)
    w["mla_q_rot"] = jnp.concatenate([jnp.zeros_like(uq[:, :, :MLA_NOPE]), uqr, zq], axis=2
                                     ).reshape(MLA_Q_RANK, -1).astype(CD)
    w["mla_kv"] = mla_w_ukv_l.astype(CD)
    return w


def _tables(s, c):
    cos_sw, sin_sw = _rope_tables(s, SW_DIM)
    cos_m, sin_m = _rope_tables(s, MLA_ROPE)
    tile2 = lambda a: jnp.concatenate([a, a, a, a], axis=1)
    scale = (MLA_NOPE + MLA_ROPE) ** -0.5 * LOG2E
    zpad = jnp.zeros((s, MLA_HW - MLA_NOPE - MLA_ROPE), F32)
    lat = dict(
        sw_cos=tile2(cos_sw), sw_sin=tile2(sin_sw),
        q_cos=scale * jnp.concatenate([jnp.ones((s, MLA_NOPE), F32), cos_m, cos_m, zpad], axis=1),
        q_sin=scale * jnp.concatenate([jnp.zeros((s, MLA_NOPE), F32), sin_m, sin_m, zpad], axis=1),
        k_cos=jnp.concatenate([cos_m, cos_m, zpad], axis=1),
        k_sin=jnp.concatenate([sin_m, sin_m, zpad], axis=1))
    ctx = dict(
        sw_cos=jnp.ones((c, LANE), F32), sw_sin=jnp.zeros((c, LANE), F32),
        q_cos=jnp.full((c, MLA_HW), scale, F32), q_sin=jnp.zeros((c, MLA_HW), F32),
        k_cos=jnp.ones((c, LANE), F32), k_sin=jnp.zeros((c, LANE), F32))
    return lat, ctx


def _project(h, w, tab, gq, gkv, total_rows, mla_into=None):
    p_ml = matmul(h, w["ml"], F32)
    p_na = matmul(h, w["na"], CD)
    n_rope = (SW_HEADS + SW_KV_HEADS) * SW_DIM // LANE
    p_sw = rope_matmul(h, w["sw"], w["sw_rot"], tab["sw_cos"], tab["sw_sin"], n_rope)
    mla = mla_proj(h, w["mla1"], gq, gkv, w["mla_q"], w["mla_q_rot"], w["mla_kv"],
                   tab["q_cos"], tab["q_sin"], tab["k_cos"], tab["k_sin"], total_rows, mla_into)
    return p_ml, p_na, p_sw, mla


def _dense_tail(x, h, ys, w, l, w_branch, w_out, w_ffn_in, w_ffn_out, vt, g1, norm2, g2, norm_next, last):
    merged = gated_merge(h, ys, w["gate"], w_branch, l)
    x, h2 = matmul_residual_norm(merged, w_out, l, x, vt, g1, norm2, True, CD)
    u = ffn_up(h2, w_ffn_in, l)
    if last:
        return None, matmul_residual_norm(u, w_ffn_out, l, x, vt, g2, norm_next, False, F32, tn=512)
    return matmul_residual_norm(u, w_ffn_out, l, x, vt, g2, norm_next, True, CD, tn=512)


def kernel(x, c, ctx, c_ctx, w_ada, b_ada, norm1_g, norm2_g, w_in, ml_conv_w, ml_gate_b, ml_norm_g, na_rpb,
           sw_sink, mla_q_norm_g, mla_kv_norm_g, mla_w_uq, mla_w_ukv, w_branch, w_out, w_ffn_in, w_ffn_out,
           final_norm_g):
    bsz, s, d = x.shape
    assert bsz == 1
    depth = w_in.shape[0]
    n_ctx = ctx.shape[1]
    rows = s // GRID_W
    xl = x[0].astype(F32)
    xc = ctx[0].astype(F32)
    mod = adaln(jnp.stack([c[0], c_ctx], axis=1).astype(F32), w_ada, b_ada)
    vt = jnp.concatenate([mod.reshape(depth * 12, d), norm1_g.astype(F32), norm2_g.astype(F32),
                          final_norm_g.reshape(1, d).astype(F32), jnp.zeros((1, d), F32)]).reshape(-1, 1, d)
    mod_row = lambda l, stream, part: (l * 2 + stream) * 6 + part
    n1_row = lambda l: depth * 12 + l
    n2_row = lambda l: depth * 13 + l
    fin_row, zero_row = depth * 14, depth * 14 + 1
    norm1 = lambda l, st: (n1_row(l), mod_row(l, st, 0), mod_row(l, st, 1))
    norm2 = lambda l, st: (n2_row(l), mod_row(l, st, 3), mod_row(l, st, 4))
    tab_l, tab_c = _tables(s, n_ctx)
    zero_state = (jnp.zeros((2 * ML_HEADS, ML_QK, 2 * ML_V), F32), jnp.zeros((2 * ML_HEADS, 8, LANE), F32))

    w_in_t = jnp.transpose(w_in, (0, 2, 1))
    wb = w_branch.astype(CD)
    wo = w_out.astype(CD)
    wf2 = w_ffn_out.astype(CD)

    hl = norm_mod(xl, vt, norm1(0, 0), CD)
    hc = norm_mod(xc, vt, norm1(0, 1), CD)
    for l in range(depth):
        last = l == depth - 1
        w = _layer_weights(w_in_t, l, mla_w_uq[l], mla_w_ukv[l])
        gq = mla_q_norm_g[l].reshape(1, -1).astype(F32)
        gkv = mla_kv_norm_g[l].reshape(1, -1).astype(F32)
        sink = sw_sink[l].reshape(1, -1).astype(F32)

        ml_l, na_l, sw_l, mla_lat = _project(hl, w, tab_l, gq, gkv, s + n_ctx)
        ml_c, na_c, sw_c, (mq, mk, mv) = _project(hc, w, tab_c, gq, gkv, s + n_ctx, mla_lat)

        qc_c, kc_c, gt_c = ml_prep(ml_c, ml_conv_w[l], ml_gate_b[l])
        qc_l, kc_l, gt_l = ml_prep(ml_l, ml_conv_w[l], ml_gate_b[l])
        hf_c, hb_c, cst, mst = ml_scan(qc_c, kc_c, ml_c, gt_c, zero_state)
        hf_l, hb_l, _, _ = ml_scan(qc_l, kc_l, ml_l, gt_l, (cst, mst))
        ya_l = ml_finish(hf_l, hb_l, ml_l, ml_norm_g[l])
        yb_l = na_attn(na_l, na_c, na_rpb[l], rows)
        yc_l = sw_attn(sw_l, sw_c, sink)
        yd_l = mla_flash(mq, mk, mv, s)

        nxt_l = (fin_row, zero_row, zero_row) if last else norm1(l + 1, 0)
        xl_new, hl_new = _dense_tail(xl, hl, (ya_l, yb_l, yc_l, yd_l), w, l, wb, wo, w_ffn_in, wf2, vt,
                                     mod_row(l, 0, 2), norm2(l, 0), mod_row(l, 0, 5), nxt_l, last)
        if not last:
            ya_c = ml_finish(hf_c, hb_c, ml_c, ml_norm_g[l])
            nw = NA_HEADS * NA_DIM
            yb_c = dense_attn([(na_c, nw, 0), (na_c, nw, 1), (na_c, nw, 2)], NA_HEADS, 1, NA_DIM, NA_DIM,
                              n_ctx, log2_scores=True)
            kcol = SW_HEADS * SW_DIM // LANE
            yc_c = dense_attn([(sw_c, SW_HEADS * SW_DIM, 0), (sw_c, LANE, kcol), (sw_c, LANE, kcol + 1)],
                              SW_HEADS, SW_HEADS // SW_KV_HEADS, SW_DIM, SW_DIM, n_ctx, sink=sink,
                              log2_scores=True)
            assert s % n_ctx == 0
            yd_c = dense_attn([(mq, MLA_HEADS * MLA_HW, 0), (mk, MLA_HEADS * MLA_HW, 0),
                               (mv, MLA_HEADS * MLA_V, 0)], MLA_HEADS, 1, MLA_HW, MLA_V, n_ctx,
                              log2_scores=True, row_block=s // n_ctx)
            xc, hc = _dense_tail(xc, hc, (ya_c, yb_c, yc_c, yd_c), w, l, wb, wo, w_ffn_in, wf2, vt,
                                 mod_row(l, 1, 2), norm2(l, 1), mod_row(l, 1, 5), norm1(l + 1, 1), False)
        xl, hl = xl_new, hl_new

    return hl[None].astype(x.dtype)
```

```python
import functools

import numpy as np
import jax
import jax.numpy as jnp
from jax import lax
from jax.experimental import pallas as pl
from jax.experimental.pallas import tpu as pltpu

F32 = jnp.float32
CD = jnp.bfloat16

GRID_W = 64
EPS = 1e-6
ROPE_THETA = 10000.0
ML_HEADS, ML_QK, ML_V, ML_CONV = 4, 64, 128, 3
NA_HEADS, NA_DIM, NA_WIN_R, NA_WIN_C = 8, 64, 8, 16
SW_HEADS, SW_KV_HEADS, SW_DIM, SW_WINDOW = 8, 2, 64, 128
MLA_HEADS, MLA_Q_RANK, MLA_KV_RANK, MLA_NOPE, MLA_ROPE, MLA_V = 4, 384, 256, 128, 64, 128
N_BRANCH, BRANCH_W = 4, 512

ML_QKW = ML_HEADS * ML_QK
ML_VW = ML_HEADS * ML_V
ML_COLS = 2 * ML_QKW + 2 * ML_VW + 4 * ML_HEADS
ML_PAD = 1664
NA_COLS = 3 * NA_HEADS * NA_DIM
SW_COLS = (SW_HEADS + 2 * SW_KV_HEADS) * SW_DIM
MLA_COLS = MLA_Q_RANK + MLA_KV_RANK + MLA_ROPE
MLA_HW = 256
LANE = 128
NEG = -1e30
LOG2E = float(np.log2(np.e))
VMEM_LIMIT = 56 * 1024 * 1024

NA_QROWS = 4
SW_TQ = 256
ML_CHUNK = 256
MLA_TK = 1280


def _cparams(*sem):
    return pltpu.CompilerParams(dimension_semantics=sem, vmem_limit_bytes=VMEM_LIMIT)


def _row_tile(t, pref=512):
    tm = min(pref, t)
    assert t % tm == 0
    return tm


def _dot(a, b):
    return jnp.dot(a, b, preferred_element_type=F32)


def _dot_nt(a, b):
    return lax.dot_general(a, b, (((1,), (1,)), ((), ())), preferred_element_type=F32)


def _sigmoid(x):
    return 1.0 / (1.0 + jnp.exp(-x))


def _lanes(col, n):
    tile = jnp.broadcast_to(col, (col.shape[0], LANE))
    return tile if n == LANE else jnp.tile(tile, (1, n // LANE))


def _ones_col(n, w):
    return jnp.where(lax.broadcasted_iota(jnp.int32, (n, w), 1) == 0, 1.0, 0.0).astype(CD)


def _adaln_kernel(c_ref, w_ref, b_ref, o_ref):
    w = w_ref[0]
    for r in range(2):
        c = c_ref[:, r:r + 1]
        o_ref[0, r:r + 1, :] = jnp.sum((c * _sigmoid(c)) * w, axis=0, keepdims=True) + b_ref[0]


def adaln(cc, w_ada, b_ada):
    nl, d, n = w_ada.shape
    tn = 1024
    return pl.pallas_call(
        _adaln_kernel,
        grid=(nl, n // tn),
        in_specs=[pl.BlockSpec((d, 2), lambda l, j: (0, 0)),
                  pl.BlockSpec((1, d, tn), lambda l, j: (l, 0, j)),
                  pl.BlockSpec((1, 1, tn), lambda l, j: (l, 0, j))],
        out_specs=pl.BlockSpec((1, 2, tn), lambda l, j: (l, 0, j)),
        out_shape=jax.ShapeDtypeStruct((nl, 2, n), F32),
        compiler_params=_cparams("parallel", "parallel"),
        name="adaln",
    )(cc, w_ada, b_ada.reshape(nl, 1, n))


def _vec_spec(d, idx):
    return pl.BlockSpec((1, 1, d), lambda *_: (idx, 0, 0))


def _norm_mod(x, g, shift, scale):
    y = x * lax.rsqrt(jnp.mean(x * x, axis=-1, keepdims=True) + EPS)
    return (y * g) * (1.0 + scale) + shift


def _norm_mod_kernel(x_ref, g_ref, sh_ref, sc_ref, o_ref):
    o_ref[...] = _norm_mod(x_ref[...], g_ref[0], sh_ref[0], sc_ref[0]).astype(o_ref.dtype)


def norm_mod(x, vt, norm_idx, out_dtype):
    t, d = x.shape
    tm = _row_tile(t, 256)
    return pl.pallas_call(
        _norm_mod_kernel,
        grid=(t // tm,),
        in_specs=[pl.BlockSpec((tm, d), lambda i: (i, 0))] + [_vec_spec(d, ix) for ix in norm_idx],
        out_specs=pl.BlockSpec((tm, d), lambda i: (i, 0)),
        out_shape=jax.ShapeDtypeStruct((t, d), out_dtype),
        compiler_params=_cparams("parallel"),
        name="norm_mod",
    )(x, vt, vt, vt)


def _mm_kernel(a_ref, b_ref, o_ref):
    o_ref[...] = _dot_nt(a_ref[...], b_ref[...]).astype(o_ref.dtype)


def matmul(a, bt, out_dtype, tn=None):
    t, k = a.shape
    n = bt.shape[0]
    tn = n if tn is None else tn
    tm = _row_tile(t)
    return pl.pallas_call(
        _mm_kernel,
        grid=(n // tn, t // tm),
        in_specs=[pl.BlockSpec((tm, k), lambda j, i: (i, 0)),
                  pl.BlockSpec((tn, k), lambda j, i: (j, 0))],
        out_specs=pl.BlockSpec((tm, tn), lambda j, i: (i, j)),
        out_shape=jax.ShapeDtypeStruct((t, n), out_dtype),
        compiler_params=_cparams("parallel", "parallel"),
        name="matmul",
    )(a, bt)


def _mm_res_norm_kernel(nj, tn, want_x, a_ref, b_ref, x_ref, gate_ref, g_ref, sh_ref, sc_ref, *rest):
    xbuf, ho_ref = (rest[0], rest[1]) if want_x else (rest[1], rest[0])
    j = pl.program_id(1)
    acc = _dot(a_ref[...], b_ref[0])
    for jj in range(nj):
        @pl.when(j == jj)
        def _(jj=jj):
            sl = slice(jj * tn, (jj + 1) * tn)
            xbuf[:, sl] = x_ref[:, sl] + gate_ref[0][:, sl] * acc

    @pl.when(j == nj - 1)
    def _():
        ho_ref[...] = _norm_mod(xbuf[...], g_ref[0], sh_ref[0], sc_ref[0]).astype(ho_ref.dtype)


def matmul_residual_norm(a, b, l, x, vt, gate_idx, norm_idx, want_x, h_dtype, tn=None):
    t, kdim = a.shape
    n = b.shape[2]
    tn = n if tn is None else tn
    nj = n // tn
    tm = _row_tile(t)
    row = pl.BlockSpec((tm, n), lambda i, j: (i, 0))
    out_shape = [jax.ShapeDtypeStruct((t, n), h_dtype)]
    if want_x:
        out_shape = [jax.ShapeDtypeStruct((t, n), F32)] + out_shape
    out = pl.pallas_call(
        functools.partial(_mm_res_norm_kernel, nj, tn, want_x),
        grid=(t // tm, nj),
        in_specs=[pl.BlockSpec((tm, kdim), lambda i, j: (i, 0)),
                  pl.BlockSpec((1, kdim, tn), lambda i, j: (l, 0, j)),
                  row, _vec_spec(n, gate_idx)] + [_vec_spec(n, ix) for ix in norm_idx],
        out_specs=[row] * len(out_shape),
        out_shape=out_shape,
        scratch_shapes=[] if want_x else [pltpu.VMEM((tm, n), F32)],
        compiler_params=_cparams("parallel", "arbitrary"),
        name="matmul_residual_norm",
    )(a, b, x, vt, vt, vt, vt)
    return out if want_x else out[0]


def _rope_mm_kernel(n_rope, a_ref, w_ref, wr_ref, cos_ref, sin_ref, o_ref):
    a = a_ref[...]
    p = _dot_nt(a, w_ref[...])
    pr = _dot_nt(a, wr_ref[...])
    cos = cos_ref[...]
    sin = sin_ref[...]
    for j in range(p.shape[1] // LANE):
        sl = slice(j * LANE, (j + 1) * LANE)
        if j < n_rope:
            o_ref[:, sl] = (p[:, sl] * cos + pr[:, sl] * sin).astype(o_ref.dtype)
        else:
            o_ref[:, sl] = p[:, sl].astype(o_ref.dtype)


def rope_matmul(a, w, w_rot, cos, sin, n_rope):
    t, k = a.shape
    n = w.shape[0]
    nr = n_rope * LANE
    tm = _row_tile(t)
    return pl.pallas_call(
        functools.partial(_rope_mm_kernel, n_rope),
        grid=(t // tm,),
        in_specs=[pl.BlockSpec((tm, k), lambda i: (i, 0)),
                  pl.BlockSpec((n, k), lambda i: (0, 0)),
                  pl.BlockSpec((nr, k), lambda i: (0, 0)),
                  pl.BlockSpec((tm, LANE), lambda i: (i, 0)),
                  pl.BlockSpec((tm, LANE), lambda i: (i, 0))],
        out_specs=pl.BlockSpec((tm, n), lambda i: (i, 0)),
        out_shape=jax.ShapeDtypeStruct((t, n), CD),
        compiler_params=_cparams("parallel"),
        name="rope_matmul",
    )(a, w, w_rot, cos, sin)


def _mla_proj_kernel(h_ref, w1_ref, gq_ref, gkv_ref, wq_ref, wqr_ref, wkv_ref,
                     cq_ref, sq_ref, ck_ref, sk_ref, q_out, k_out, v_out):
    p = _dot_nt(h_ref[...], w1_ref[...])
    cq = p[:, :MLA_Q_RANK]
    ckv = p[:, MLA_Q_RANK:MLA_Q_RANK + MLA_KV_RANK]
    o = MLA_Q_RANK + MLA_KV_RANK
    kr = p[:, o:o + LANE]
    krr = p[:, o + LANE:o + 2 * LANE]
    cqn = (cq * lax.rsqrt(jnp.mean(cq * cq, axis=-1, keepdims=True) + EPS) * gq_ref[...]).astype(CD)
    ckvn = (ckv * lax.rsqrt(jnp.mean(ckv * ckv, axis=-1, keepdims=True) + EPS) * gkv_ref[...]).astype(CD)
    qa = _dot(cqn, wq_ref[...])
    qb = _dot(cqn, wqr_ref[...])
    kv = _dot(ckvn, wkv_ref[...])
    krp = (kr * ck_ref[...] + krr * sk_ref[...]).astype(CD)
    cq_t = cq_ref[...]
    sq_t = sq_ref[...]
    for h in range(MLA_HEADS):
        qs = slice(h * MLA_HW, (h + 1) * MLA_HW)
        q_out[:, qs] = (qa[:, qs] * cq_t + qb[:, qs] * sq_t).astype(CD)
        k_out[:, h * MLA_HW:h * MLA_HW + MLA_NOPE] = kv[:, h * 256:h * 256 + MLA_NOPE].astype(CD)
        k_out[:, h * MLA_HW + MLA_NOPE:(h + 1) * MLA_HW] = krp
        v_out[:, h * MLA_V:(h + 1) * MLA_V] = kv[:, h * 256 + MLA_NOPE:(h + 1) * 256].astype(CD)


def mla_proj(h, w1, gq, gkv, wq, wqr, wkv, cos_q, sin_q, cos_k, sin_k, total_rows, into=None):
    t, d = h.shape
    tm = _row_tile(t)
    full = lambda a: pl.BlockSpec(a.shape, lambda i: (0,) * a.ndim)
    rows = lambda w: pl.BlockSpec((tm, w), lambda i: (i, 0))
    hw = MLA_HEADS * MLA_HW
    vw = MLA_HEADS * MLA_V
    args = [h, w1, gq, gkv, wq, wqr, wkv, cos_q, sin_q, cos_k, sin_k]
    in_specs = [rows(d), full(w1), full(gq), full(gkv), full(wq), full(wqr), full(wkv),
                rows(MLA_HW), rows(MLA_HW), rows(LANE), rows(LANE)]
    if into is None:
        off, aliases, kern = 0, {}, _mla_proj_kernel
    else:
        assert (total_rows - t) % tm == 0
        off = (total_rows - t) // tm
        aliases = {len(args) + i: i for i in range(3)}
        in_specs = in_specs + [pl.BlockSpec(memory_space=pl.ANY)] * 3
        args = args + list(into)
        kern = lambda *refs: _mla_proj_kernel(*refs[:11], *refs[14:])
    orow = lambda w: pl.BlockSpec((tm, w), lambda i: (off + i, 0))
    return pl.pallas_call(
        kern,
        grid=(t // tm,),
        in_specs=in_specs,
        out_specs=[orow(hw), orow(hw), orow(vw)],
        out_shape=[jax.ShapeDtypeStruct((total_rows, hw), CD), jax.ShapeDtypeStruct((total_rows, hw), CD),
                   jax.ShapeDtypeStruct((total_rows, vw), CD)],
        input_output_aliases=aliases,
        compiler_params=_cparams("parallel"),
        name="mla_proj",
    )(*args)


def _flash_kernel(tq, tk, unroll, q_ref, k_ref, v_ref, o_ref, s_sc, m_sc, acc_sc):
    nkb = k_ref.shape[0] // tk
    nsub = q_ref.shape[0] // tq
    total = nsub * nkb
    m_sc[...] = jnp.full(m_sc.shape, NEG, F32)
    acc_sc[...] = jnp.zeros(acc_sc.shape, F32)
    ones_col = jnp.where(lax.broadcasted_iota(jnp.int32, (tk, LANE), 1) == 0, 1.0, 0.0).astype(CD)

    def split(b):
        sub = b // nkb
        return sub, b - sub * nkb

    def scores(b):
        sub, j = split(b)
        q = q_ref[pl.ds(pl.multiple_of(sub * tq, tq), tq), :]
        k = k_ref[pl.ds(pl.multiple_of(j * tk, tk), tk), :]
        return _dot_nt(q, k)

    def softmax_pv(s, b):
        sub, j = split(b)
        v = v_ref[pl.ds(pl.multiple_of(j * tk, tk), tk), :]
        m_prev = m_sc[sub]
        m_new = jnp.maximum(m_prev, jnp.max(s, axis=-1, keepdims=True))
        alpha = jnp.exp2(m_prev - m_new)
        p = jnp.exp2(s - m_new).astype(CD)
        v_aug = jnp.concatenate([v, ones_col], axis=1)
        acc_sc[sub] = alpha * acc_sc[sub] + _dot(p, v_aug)
        m_sc[sub] = m_new

    s_sc[0] = scores(0)

    def body(g, carry):
        b0 = g * unroll
        for u in range(unroll):
            s_sc[(u + 1) % 2] = scores(jnp.minimum(b0 + u + 1, total - 1))
            softmax_pv(s_sc[u % 2], b0 + u)
        return carry

    lax.fori_loop(0, total // unroll, body, 0)
    for sub in range(nsub):
        acc = acc_sc[sub]
        o_ref[sub * tq:(sub + 1) * tq, :] = (acc[:, :MLA_V] / acc[:, MLA_V:MLA_V + 1]).astype(o_ref.dtype)


def mla_flash(q, k, v, s):
    nk = k.shape[0]
    tq = _row_tile(s, 512)
    tstep = _row_tile(s, 8 * tq)
    tk = MLA_TK
    total = (tstep // tq) * (nk // tk)
    assert nk % tk == 0 and total % 2 == 0
    unroll = 8 if total % 8 == 0 else (4 if total % 4 == 0 else 2)
    return pl.pallas_call(
        functools.partial(_flash_kernel, tq, tk, unroll),
        grid=(MLA_HEADS, s // tstep),
        in_specs=[pl.BlockSpec((tstep, MLA_HW), lambda h, i: (i, h)),
                  pl.BlockSpec((nk, MLA_HW), lambda h, i: (0, h)),
                  pl.BlockSpec((nk, MLA_V), lambda h, i: (0, h))],
        out_specs=pl.BlockSpec((tstep, MLA_V), lambda h, i: (i, h)),
        out_shape=jax.ShapeDtypeStruct((s, MLA_HEADS * MLA_V), CD),
        scratch_shapes=[pltpu.VMEM((2, tq, tk), F32), pltpu.VMEM((tstep // tq, tq, 1), F32),
                        pltpu.VMEM((tstep // tq, tq, 2 * MLA_V), F32)],
        compiler_params=_cparams("parallel", "arbitrary"),
        name="mla_flash",
    )(q, k, v)


def _dense_attn_kernel(n_heads, group, dqk, dv, log2_scores, q_ref, k_ref, v_ref, sink_ref, o_ref):
    q = q_ref[...]
    k = k_ref[...]
    v = v_ref[...]
    ex = jnp.exp2 if log2_scores else jnp.exp
    for h in range(n_heads):
        g = h // group
        s = _dot_nt(q[:, h * dqk:(h + 1) * dqk], k[:, g * dqk:(g + 1) * dqk])
        m = jnp.max(s, axis=-1, keepdims=True)
        l = jnp.zeros_like(m)
        if sink_ref is not None:
            sk = sink_ref[0:1, h:h + 1] * (LOG2E if log2_scores else 1.0)
            m = jnp.maximum(m, sk)
            l = ex(sk - m)
        p = ex(s - m)
        l = l + jnp.sum(p, axis=-1, keepdims=True)
        o = _dot(p.astype(CD), v[:, g * dv:(g + 1) * dv]) / l
        o_ref[:, h * dv:(h + 1) * dv] = o.astype(o_ref.dtype)


def dense_attn(qkv_specs, n_heads, group, dqk, dv, c, sink=None, log2_scores=False, row_block=0):
    arrays = [a for a, _, _ in qkv_specs]
    specs = [pl.BlockSpec((c, w), functools.partial(lambda b, i: (row_block, b), b)) for _, w, b in qkv_specs]
    if sink is None:
        kern = lambda q, k, v, o: _dense_attn_kernel(n_heads, group, dqk, dv, log2_scores, q, k, v, None, o)
    else:
        kern = functools.partial(_dense_attn_kernel, n_heads, group, dqk, dv, log2_scores)
        arrays.append(sink)
        specs.append(pl.BlockSpec(sink.shape, lambda i: (0, 0)))
    return pl.pallas_call(
        kern,
        grid=(1,),
        in_specs=specs,
        out_specs=pl.BlockSpec((c, n_heads * dv), lambda i: (0, 0)),
        out_shape=jax.ShapeDtypeStruct((c, n_heads * dv), CD),
        compiler_params=_cparams("arbitrary"),
        name="dense_attn",
    )(*arrays)


def _na_kernel(q_ref, kp_ref, ko_ref, kn_ref, vp_ref, vo_ref, vn_ref, kc_ref, vc_ref, bias_ref, o_ref):
    q = q_ref[...]
    kcat = jnp.concatenate([kp_ref[...], ko_ref[...], kn_ref[...], kc_ref[...]], axis=0)
    vcat = jnp.concatenate([vp_ref[...], vo_ref[...], vn_ref[...], vc_ref[...]], axis=0)
    tb = q.shape[0]
    nloc = 3 * tb
    ones = _ones_col(kcat.shape[0], LANE)
    first_half = lax.broadcasted_iota(jnp.int32, (tb, LANE), 1) < NA_DIM
    zero = jnp.zeros((tb, LANE), CD)
    for j in range(NA_HEADS // 2):
        ps = slice(j * LANE, (j + 1) * LANE)
        q2 = q[:, ps]
        k2 = kcat[:, ps]
        v_aug = jnp.concatenate([vcat[:, ps], ones], axis=1)
        outs = []
        for e in range(2):
            qm = jnp.where(first_half, q2, zero) if e == 0 else jnp.where(first_half, zero, q2)
            s = _dot_nt(qm, k2)
            s = jnp.concatenate([s[:, :nloc] + bias_ref[0, 2 * j + e], s[:, nloc:]], axis=1)
            m = jnp.max(s, axis=-1, keepdims=True)
            p = jnp.exp2(s - _lanes(m, s.shape[1])).astype(CD)
            acc = _dot(p, v_aug)
            outs.append(acc[:, :LANE] / _lanes(acc[:, LANE:LANE + 1], LANE))
        o_ref[:, ps] = jnp.where(first_half, outs[0], outs[1]).astype(o_ref.dtype)


def _na_bias(rpb, rows):
    qr = NA_QROWS
    tb = qr * GRID_W
    nb = rows // qr
    nkr = 3 * qr
    rsel = np.zeros((3, qr, nkr, 2 * NA_WIN_R - 1), np.float32)
    for v, b in enumerate((0, 1, nb - 1)):
        for a in range(qr):
            r = qr * b + a
            rs = min(max(r - NA_WIN_R // 2, 0), rows - NA_WIN_R)
            for u in range(nkr):
                kr = qr * (b - 1) + u
                if 0 <= b - 1 + u // qr < nb and rs <= kr < rs + NA_WIN_R:
                    rsel[v, a, u, kr - r + NA_WIN_R - 1] = 1.0
    csel = np.zeros((GRID_W, GRID_W, 2 * NA_WIN_C - 1), np.float32)
    for qc in range(GRID_W):
        cs = min(max(qc - NA_WIN_C // 2, 0), GRID_W - NA_WIN_C)
        for kc in range(cs, cs + NA_WIN_C):
            csel[qc, kc, kc - qc + NA_WIN_C - 1] = 1.0
    valid = (np.einsum('vaud,qkj->vaquk', rsel, csel) > 0).reshape(3, tb, 3 * tb)
    assert 2 * qr == NA_WIN_R
    nd = 2 * NA_WIN_R - 1
    rp = rpb.astype(F32) * LOG2E
    gap = jnp.zeros((NA_HEADS, nd, GRID_W + 1 - (2 * NA_WIN_C - 1)), F32)
    ring = jnp.concatenate([rp[..., NA_WIN_C - 1:], gap, rp[..., :NA_WIN_C - 1]], axis=-1)
    cols = jnp.tile(ring, (1, 1, GRID_W))[..., :GRID_W * GRID_W].reshape(NA_HEADS, nd, GRID_W, GRID_W)
    off = NA_WIN_R - 1 - qr
    slabs = [cols[:, off - a:off - a + nkr].transpose(0, 2, 1, 3).reshape(NA_HEADS, GRID_W, 3 * tb)
             for a in range(qr)]
    band = jnp.stack(slabs, axis=1).reshape(NA_HEADS, tb, 3 * tb)
    return jnp.where(valid[:, None], band[None], NEG)


def na_attn(p_lat, p_ctx, rpb, rows):
    s = p_lat.shape[0]
    c = p_ctx.shape[0]
    w = NA_HEADS * NA_DIM
    tb = NA_QROWS * GRID_W
    nb = s // tb
    assert rows % NA_QROWS == 0 and nb >= 3 and rows >= 2 * NA_WIN_R
    bias = _na_bias(rpb, rows)

    def blk(col, off):
        return pl.BlockSpec((tb, w), lambda i: (jnp.clip(i + off, 0, nb - 1), col))

    return pl.pallas_call(
        _na_kernel,
        grid=(nb,),
        in_specs=[blk(0, 0), blk(1, -1), blk(1, 0), blk(1, 1), blk(2, -1), blk(2, 0), blk(2, 1),
                  pl.BlockSpec((c, w), lambda i: (0, 1)),
                  pl.BlockSpec((c, w), lambda i: (0, 2)),
                  pl.BlockSpec((1, NA_HEADS, tb, 3 * tb),
                               lambda i: (jnp.where(i == 0, 0, jnp.where(i == nb - 1, 2, 1)), 0, 0, 0))],
        out_specs=pl.BlockSpec((tb, w), lambda i: (i, 0)),
        out_shape=jax.ShapeDtypeStruct((s, w), CD),
        compiler_params=_cparams("parallel"),
        name="na_attn",
    )(p_lat, p_lat, p_lat, p_lat, p_lat, p_lat, p_lat, p_ctx, p_ctx, bias)


def _sw_kernel(seq, q_ref, kp_ref, ko_ref, kn_ref, vp_ref, vo_ref, vn_ref, kc_ref, vc_ref, sink_ref, o_ref):
    i = pl.program_id(0)
    q = q_ref[...]
    tq = q.shape[0]
    kcat = jnp.concatenate([kp_ref[...], ko_ref[...], kn_ref[...], kc_ref[...]], axis=0)
    vcat = jnp.concatenate([vp_ref[...], vo_ref[...], vn_ref[...], vc_ref[...]], axis=0)
    nkk = tq + 2 * SW_WINDOW
    r = lax.broadcasted_iota(jnp.int32, (tq, nkk), 0)
    cc = lax.broadcasted_iota(jnp.int32, (tq, nkk), 1)
    kpos = i * tq - SW_WINDOW + cc
    d = cc - r
    mask = (d >= 0) & (d <= 2 * SW_WINDOW) & (kpos >= 0) & (kpos < seq)
    assert SW_KV_HEADS == 2 and SW_DIM * 2 == LANE
    ones = _ones_col(kcat.shape[0], LANE)
    swap = lambda x: jnp.concatenate([x[:, SW_DIM:], x[:, :SW_DIM]], axis=1)
    k_by_order = (kcat, swap(kcat))
    v_by_order = (jnp.concatenate([vcat, ones], axis=1), jnp.concatenate([swap(vcat), ones], axis=1))
    first_half = lax.broadcasted_iota(jnp.int32, (tq, LANE), 1) < SW_DIM
    zero = jnp.zeros((tq, LANE), CD)
    grp = SW_HEADS // SW_KV_HEADS
    for j in range(SW_HEADS // 2):
        ps = slice(j * LANE, (j + 1) * LANE)
        q2 = q[:, ps]
        outs = []
        for e in range(2):
            h = 2 * j + e
            order = 0 if h // grp == e else 1
            qm = jnp.where(first_half, q2, zero) if e == 0 else jnp.where(first_half, zero, q2)
            s = _dot_nt(qm, k_by_order[order])
            s = jnp.concatenate([jnp.where(mask, s[:, :nkk], NEG), s[:, nkk:]], axis=1)
            sk = sink_ref[0:1, h:h + 1] * LOG2E
            m = jnp.maximum(jnp.max(s, axis=-1, keepdims=True), sk)
            p = jnp.exp2(s - _lanes(m, s.shape[1])).astype(CD)
            acc = _dot(p, v_by_order[order])
            outs.append(acc[:, :LANE] / _lanes(acc[:, LANE:LANE + 1] + jnp.exp2(sk - m), LANE))
        o_ref[:, ps] = jnp.where(first_half, outs[0], outs[1]).astype(o_ref.dtype)


def sw_attn(p_lat, p_ctx, sink):
    s = p_lat.shape[0]
    c = p_ctx.shape[0]
    tq = SW_TQ
    assert s % tq == 0 and tq % SW_WINDOW == 0
    nq = s // tq
    per = tq // SW_WINDOW
    nwb = s // SW_WINDOW
    kcol = SW_HEADS * SW_DIM // LANE
    prev = lambda col: pl.BlockSpec((SW_WINDOW, LANE), lambda i: (jnp.maximum(i * per - 1, 0), col))
    own = lambda col: pl.BlockSpec((tq, LANE), lambda i: (i, col))
    nxt = lambda col: pl.BlockSpec((SW_WINDOW, LANE), lambda i: (jnp.minimum((i + 1) * per, nwb - 1), col))
    return pl.pallas_call(
        functools.partial(_sw_kernel, s),
        grid=(nq,),
        in_specs=[pl.BlockSpec((tq, SW_HEADS * SW_DIM), lambda i: (i, 0)),
                  prev(kcol), own(kcol), nxt(kcol), prev(kcol + 1), own(kcol + 1), nxt(kcol + 1),
                  pl.BlockSpec((c, LANE), lambda i: (0, kcol)),
                  pl.BlockSpec((c, LANE), lambda i: (0, kcol + 1)),
                  pl.BlockSpec(sink.shape, lambda i: (0, 0))],
        out_specs=pl.BlockSpec((tq, SW_HEADS * SW_DIM), lambda i: (i, 0)),
        out_shape=jax.ShapeDtypeStruct((s, SW_HEADS * SW_DIM), CD),
        compiler_params=_cparams("parallel"),
        name="sw_attn",
    )(p_lat, p_lat, p_lat, p_lat, p_lat, p_lat, p_lat, p_ctx, p_ctx, sink)


def _ml_prep_kernel(nt, x_ref, xp_ref, xn_ref, g_ref, w_ref, b_ref, q_out, k_out, g_out):
    i = pl.program_id(0)
    x = x_ref[...]
    tm = x.shape[0]
    row = lax.broadcasted_iota(jnp.int32, x.shape, 0)
    has_prev = jnp.where(i > 0, 1.0, 0.0)
    has_next = jnp.where(i < nt - 1, 1.0, 0.0)
    prev_row = xp_ref[7:8, :] * has_prev
    next_row = xn_ref[0:1, :] * has_next
    xm = jnp.where(row == 0, prev_row, pltpu.roll(x, 1, axis=0))
    xq = jnp.where(row == tm - 1, next_row, pltpu.roll(x, tm - 1, axis=0))
    y = w_ref[0:1, :] * xm + w_ref[1:2, :] * x + w_ref[2:3, :] * xq
    y = y * _sigmoid(y)
    q_out[...] = y[:, :ML_QKW].astype(CD)
    k_out[...] = (y[:, ML_QKW:] * (ML_QK ** -0.5)).astype(CD)
    g = g_ref[...] + b_ref[...]
    lane = lax.broadcasted_iota(jnp.int32, g.shape, 1)
    logsig = jnp.minimum(g, 0.0) - jnp.log(1.0 + jnp.exp(-jnp.abs(g)))
    g_out[...] = jnp.where(lane < 2 * ML_HEADS, g, logsig)


def ml_prep(p_ml, conv_w, gate_b):
    t = p_ml.shape[0]
    tm = _row_tile(t, 256)
    nt = t // tm
    wq = 2 * ML_QKW
    gcol = (2 * ML_QKW + 2 * ML_VW) // LANE
    w8 = jnp.zeros((8, wq), F32).at[:ML_CONV].set(conv_w.astype(F32))
    b = jnp.zeros((1, LANE), F32).at[0, :4 * ML_HEADS].set(gate_b.astype(F32).reshape(-1))
    h8 = tm // 8
    return pl.pallas_call(
        functools.partial(_ml_prep_kernel, nt),
        grid=(nt,),
        in_specs=[pl.BlockSpec((tm, wq), lambda i: (i, 0)),
                  pl.BlockSpec((8, wq), lambda i: (jnp.maximum(i * h8 - 1, 0), 0)),
                  pl.BlockSpec((8, wq), lambda i: (jnp.minimum((i + 1) * h8, t // 8 - 1), 0)),
                  pl.BlockSpec((tm, LANE), lambda i: (i, gcol)),
                  pl.BlockSpec((8, wq), lambda i: (0, 0)),
                  pl.BlockSpec((1, LANE), lambda i: (0, 0))],
        out_specs=[pl.BlockSpec((tm, ML_QKW), lambda i: (i, 0)),
                   pl.BlockSpec((tm, ML_QKW), lambda i: (i, 0)),
                   pl.BlockSpec((tm, LANE), lambda i: (i, 0))],
        out_shape=[jax.ShapeDtypeStruct((t, ML_QKW), CD), jax.ShapeDtypeStruct((t, ML_QKW), CD),
                   jax.ShapeDtypeStruct((t, LANE), F32)],
        compiler_params=_cparams("parallel"),
        name="ml_prep",
    )(p_ml, p_ml, p_ml, p_ml, w8, b)


def _ml_scan_kernel(qf_ref, ktf_ref, vf_ref, gcf_ref, grf_ref,
                    qb_ref, ktb_ref, vb_ref, gcb_ref, grb_ref,
                    c0_ref, m0_ref, hf_ref, hb_ref, c_ref, m_ref):
    @pl.when(pl.program_id(0) == 0)
    def _():
        c_ref[...] = c0_ref[...]
        m_ref[...] = m0_ref[...]

    ln = qf_ref.shape[0]
    r = lax.broadcasted_iota(jnp.int32, (ln, ln), 0)
    c = lax.broadcasted_iota(jnp.int32, (ln, ln), 1)
    lower = r >= c
    upper = c >= r
    lower_f = jnp.where(lower, 1.0, 0.0)
    upper_f = jnp.where(upper, 1.0, 0.0)
    ones_col = jnp.where(lax.broadcasted_iota(jnp.int32, (ln, LANE), 1) == 0, 1.0, 0.0).astype(CD)
    hi = lax.Precision.HIGHEST

    streams = ((qf_ref, ktf_ref, vf_ref, gcf_ref, grf_ref, hf_ref, lower, lower_f, upper_f, ln - 1),
               (qb_ref, ktb_ref, vb_ref, gcb_ref, grb_ref, hb_ref, upper, upper_f, lower_f, 0))
    for d, (q_ref, kt_ref, v_ref, gc_ref, gr_ref, h_ref, mask, tri_c, tri_r, last) in enumerate(streams):
        gcol = gc_ref[...]
        grow = gr_ref[...]
        bcol = jnp.dot(tri_c, gcol, precision=hi, preferred_element_type=F32)
        brow = jnp.dot(grow, tri_r, precision=hi, preferred_element_type=F32)
        q = q_ref[...]
        kt = kt_ref[...]
        v = v_ref[...]
        for hd in range(ML_HEADS):
            ch = d * ML_HEADS + hd
            ic = d * ML_HEADS + hd
            fc = 2 * ML_HEADS + ic
            i_c = gcol[:, ic:ic + 1]
            b_c = bcol[:, fc:fc + 1]
            i_r = grow[ic:ic + 1, :]
            b_r = brow[fc:fc + 1, :]
            m_prev = m_ref[ch, 0:1, 0:1]
            cst = c_ref[ch]
            dmat = jnp.where(mask, b_c - b_r + i_r, NEG)
            inter = b_c + m_prev
            m_t = jnp.maximum(inter, jnp.max(dmat, axis=-1, keepdims=True))
            qh = q[:, hd * ML_QK:(hd + 1) * ML_QK]
            kth = kt[hd * ML_QK:(hd + 1) * ML_QK, :]
            s = lax.dot_general(qh, kth, (((1,), (0,)), ((), ())), preferred_element_type=F32)
            s = s * jnp.exp(dmat - m_t)
            w_inter = jnp.exp(inter - m_t)
            v_aug = jnp.concatenate([v[:, hd * ML_V:(hd + 1) * ML_V].astype(CD), ones_col], axis=1)
            num = _dot(s.astype(CD), v_aug) + w_inter * _dot(qh, cst.astype(CD))
            den = num[:, ML_V:ML_V + 1]
            h = num[:, :ML_V] / jnp.maximum(jnp.abs(den), jnp.exp(-m_t))
            h_ref[:, hd * ML_V:(hd + 1) * ML_V] = h
            b_end = b_c[last:last + 1, :]
            m_new = jnp.maximum(b_end + m_prev, jnp.max(b_end - b_c + i_c, axis=0, keepdims=True))
            w_s = jnp.exp(b_end - b_r + i_r - m_new)
            w_c = jnp.exp(b_end + m_prev - m_new)
            ktw = (kth.astype(F32) * w_s).astype(CD)
            c_ref[ch] = w_c * cst + _dot(ktw, v_aug)
            m_ref[ch] = jnp.broadcast_to(m_new, m_ref.shape[1:])


def ml_scan(qc, kc, p_ml, gates, state):
    t = qc.shape[0]
    ln = min(ML_CHUNK, t)
    assert t % ln == 0
    nc = t // ln
    kt = kc.T
    gr = gates[:, :4 * ML_HEADS].T
    c0, m0 = state
    vcol = 2 * ML_QKW // ML_VW
    fwd = lambda j: j
    bwd = lambda j: nc - 1 - j

    def specs(ix):
        return [pl.BlockSpec((ln, ML_QKW), lambda j: (ix(j), 0)),
                pl.BlockSpec((ML_QKW, ln), lambda j: (0, ix(j))),
                pl.BlockSpec((ln, ML_VW), lambda j: (ix(j), vcol)),
                pl.BlockSpec((ln, LANE), lambda j: (ix(j), 0)),
                pl.BlockSpec((4 * ML_HEADS, ln), lambda j: (0, ix(j)))]

    st_specs = [pl.BlockSpec(c0.shape, lambda j: (0, 0, 0)), pl.BlockSpec(m0.shape, lambda j: (0, 0, 0))]
    return pl.pallas_call(
        _ml_scan_kernel,
        grid=(nc,),
        in_specs=specs(fwd) + specs(bwd) + st_specs,
        out_specs=[pl.BlockSpec((ln, ML_VW), lambda j: (fwd(j), 0)),
                   pl.BlockSpec((ln, ML_VW), lambda j: (bwd(j), 0))] + st_specs,
        out_shape=[jax.ShapeDtypeStruct((t, ML_VW), F32), jax.ShapeDtypeStruct((t, ML_VW), F32),
                   jax.ShapeDtypeStruct(c0.shape, F32), jax.ShapeDtypeStruct(m0.shape, F32)],
        compiler_params=_cparams("arbitrary"),
        name="ml_scan",
    )(qc, kt, p_ml, gates, gr, qc, kt, p_ml, gates, gr, c0, m0)


def _ml_finish_kernel(hf_ref, hb_ref, o_ref, g_ref, y_ref):
    h = hf_ref[...] + hb_ref[...]
    og = _sigmoid(o_ref[...])
    g = g_ref[...]
    for hd in range(ML_HEADS):
        sl = slice(hd * ML_V, (hd + 1) * ML_V)
        hh = h[:, sl]
        y = hh * lax.rsqrt(jnp.mean(hh * hh, axis=-1, keepdims=True) + EPS) * g[:, sl]
        y_ref[:, sl] = (y * og[:, sl]).astype(y_ref.dtype)


def ml_finish(hf, hb, p_ml, norm_g):
    t = hf.shape[0]
    tm = _row_tile(t, 256)
    ocol = (2 * ML_QKW + ML_VW) // ML_VW
    blk = pl.BlockSpec((tm, ML_VW), lambda i: (i, 0))
    return pl.pallas_call(
        _ml_finish_kernel,
        grid=(t // tm,),
        in_specs=[blk, blk, pl.BlockSpec((tm, ML_VW), lambda i: (i, ocol)),
                  pl.BlockSpec((1, ML_VW), lambda i: (0, 0))],
        out_specs=blk,
        out_shape=jax.ShapeDtypeStruct((t, ML_VW), CD),
        compiler_params=_cparams("parallel"),
        name="ml_finish",
    )(hf, hb, p_ml, norm_g.reshape(1, ML_VW).astype(F32))


def _merge_kernel(h_ref, ya_ref, yb_ref, yc_ref, yd_ref, g0_ref, g1_ref, g2_ref, g3_ref, wb_ref, o_ref):
    h = h_ref[...]
    acc = None
    for i, (y_ref, g_ref) in enumerate(zip((ya_ref, yb_ref, yc_ref, yd_ref), (g0_ref, g1_ref, g2_ref, g3_ref))):
        term = _sigmoid(_dot_nt(h, g_ref[...])) * _dot(y_ref[...], wb_ref[0, i])
        acc = term if acc is None else acc + term
    o_ref[...] = acc.astype(o_ref.dtype)


def gated_merge(h, ys, w_gate, w_branch, l, tn=512):
    t, d = h.shape
    tm = _row_tile(t, 1024)
    nj = d // tn
    yspec = pl.BlockSpec((tm, BRANCH_W), lambda j, i: (i, 0))
    gspec = lambda b: pl.BlockSpec((tn, d), lambda j, i: (b * nj + j, 0))
    return pl.pallas_call(
        _merge_kernel,
        grid=(nj, t // tm),
        in_specs=[pl.BlockSpec((tm, d), lambda j, i: (i, 0)), yspec, yspec, yspec, yspec,
                  gspec(0), gspec(1), gspec(2), gspec(3),
                  pl.BlockSpec((1, N_BRANCH, BRANCH_W, tn), lambda j, i: (l, 0, 0, j))],
        out_specs=pl.BlockSpec((tm, tn), lambda j, i: (i, j)),
        out_shape=jax.ShapeDtypeStruct((t, d), CD),
        compiler_params=_cparams("parallel", "parallel"),
        name="gated_merge",
    )(h, *ys, w_gate, w_gate, w_gate, w_gate, w_branch)


def _ffn1_kernel(h_ref, wa_ref, wg_ref, o_ref, wa_sc, wg_sc):
    @pl.when(pl.program_id(1) == 0)
    def _():
        wa_sc[...] = wa_ref[0].astype(CD)
        wg_sc[...] = wg_ref[0].astype(CD)

    h = h_ref[...]
    a = _dot(h, wa_sc[...])
    g = _dot(h, wg_sc[...])
    o_ref[...] = (a * _sigmoid(a) * g).astype(o_ref.dtype)


def ffn_up(h, w, l, tn=512):
    t, d = h.shape
    dff = w.shape[2] // 2
    tm = _row_tile(t, 1024)
    nj = dff // tn
    return pl.pallas_call(
        _ffn1_kernel,
        grid=(nj, t // tm),
        in_specs=[pl.BlockSpec((tm, d), lambda j, i: (i, 0)),
                  pl.BlockSpec((1, d, tn), lambda j, i: (l, 0, j)),
                  pl.BlockSpec((1, d, tn), lambda j, i: (l, 0, nj + j))],
        out_specs=pl.BlockSpec((tm, tn), lambda j, i: (i, j)),
        out_shape=jax.ShapeDtypeStruct((t, dff), CD),
        scratch_shapes=[pltpu.VMEM((d, tn), CD), pltpu.VMEM((d, tn), CD)],
        compiler_params=_cparams("parallel", "arbitrary"),
        name="ffn_up",
    )(h, w, w)


def _rot_cols(w, dim):
    k, n = w.shape
    w4 = w.reshape(k, n // dim, 2, dim // 2)
    return jnp.stack([-w4[:, :, 1], w4[:, :, 0]], axis=2).reshape(k, n)


def _rope_tables(n_tokens, dim):
    a = dim // 2
    inv = 1.0 / (ROPE_THETA ** (jnp.arange(0, a, 2, dtype=F32) / a))
    rows = n_tokens // GRID_W
    assert rows * GRID_W == n_tokens
    ang_r = jnp.arange(rows, dtype=F32)[:, None] * inv
    ang_c = jnp.arange(GRID_W, dtype=F32)[:, None] * inv

    def expand(fr, fc):
        shape = (rows, GRID_W, a // 2)
        return jnp.concatenate([jnp.broadcast_to(fr[:, None, :], shape), jnp.broadcast_to(fc[None], shape)],
                               axis=-1).reshape(n_tokens, a)

    return expand(jnp.cos(ang_r), jnp.cos(ang_c)), expand(jnp.sin(ang_r), jnp.sin(ang_c))


_O_NA = ML_COLS
_O_SW = _O_NA + NA_COLS
_O_MLA = _O_SW + SW_COLS
_O_KR = _O_MLA + MLA_Q_RANK + MLA_KV_RANK
_O_GATE = _O_MLA + MLA_COLS


def _rot_half_rows(x, dim):
    half = dim // 2
    parts = []
    for b in range(x.shape[0] // dim):
        parts += [-x[b * dim + half:(b + 1) * dim], x[b * dim:b * dim + half]]
    return jnp.concatenate(parts, axis=0)


def _repack_kernel(w_ref, ml_ref, na_ref, sw_ref, swr_ref, mla_ref, gate_ref):
    w = w_ref[0]
    cols = w.shape[1]
    zeros = lambda n: jnp.zeros((n, cols), F32)
    ml_ref[...] = jnp.concatenate([w[:ML_COLS], zeros(ML_PAD - ML_COLS)], axis=0).astype(CD)
    nq = NA_HEADS * NA_DIM
    na_ref[...] = jnp.concatenate([w[_O_NA:_O_NA + nq] * (NA_DIM ** -0.5 * LOG2E), w[_O_NA + nq:_O_SW]],
                                  axis=0).astype(CD)
    sq = SW_HEADS * SW_DIM
    skv = SW_KV_HEADS * SW_DIM
    sw_q = w[_O_SW:_O_SW + sq] * (SW_DIM ** -0.5 * LOG2E)
    sw_k = w[_O_SW + sq:_O_SW + sq + skv]
    sw_ref[...] = jnp.concatenate([sw_q, w[_O_SW + sq:_O_MLA]], axis=0).astype(CD)
    swr_ref[...] = _rot_half_rows(jnp.concatenate([sw_q, sw_k], axis=0), SW_DIM).astype(CD)
    kr = w[_O_KR:_O_GATE]
    pad = zeros(LANE - MLA_ROPE)
    mla_ref[...] = jnp.concatenate([w[_O_MLA:_O_KR], kr, pad, _rot_half_rows(kr, MLA_ROPE), pad], axis=0).astype(CD)
    gate_ref[...] = w[_O_GATE:].astype(CD)


def _repack_w_in(w_in_t, l):
    _, n, d = w_in_t.shape
    tc = 256
    heights = (ML_PAD, NA_COLS, SW_COLS, (SW_HEADS + SW_KV_HEADS) * SW_DIM,
               MLA_Q_RANK + MLA_KV_RANK + 2 * LANE, n - _O_GATE)
    return pl.pallas_call(
        _repack_kernel,
        grid=(d // tc,),
        in_specs=[pl.BlockSpec((1, n, tc), lambda i: (l, 0, i))],
        out_specs=[pl.BlockSpec((ht, tc), lambda i: (0, i)) for ht in heights],
        out_shape=[jax.ShapeDtypeStruct((ht, d), CD) for ht in heights],
        compiler_params=_cparams("parallel"),
        name="repack_w_in",
    )(w_in_t)


def _layer_weights(w_in_t, l, mla_w_uq_l, mla_w_ukv_l):
    w = {}
    w["ml"], w["na"], w["sw"], w["sw_rot"], w["mla1"], w["gate"] = _repack_w_in(w_in_t, l)
    uq = mla_w_uq_l.reshape(MLA_Q_RANK, MLA_HEADS, MLA_NOPE + MLA_ROPE)
    zq = jnp.zeros((MLA_Q_RANK, MLA_HEADS, MLA_HW - MLA_NOPE - MLA_ROPE), F32)
    w["mla_q"] = jnp.concatenate([uq, zq], axis=2).reshape(MLA_Q_RANK, -1).astype(CD)
    uqr = _rot_cols(uq[:, :, MLA_NOPE:].reshape(MLA_Q_RANK, -1), MLA_ROPE).reshape(MLA_Q_RANK, MLA_HEADS, MLA_ROPE)
    w["mla_q_rot"] = jnp.concatenate([jnp.zeros_like(uq[:, :, :MLA_NOPE]), uqr, zq], axis=2
                                     ).reshape(MLA_Q_RANK, -1).astype(CD)
    w["mla_kv"] = mla_w_ukv_l.astype(CD)
    return w


def _tables(s, c):
    cos_sw, sin_sw = _rope_tables(s, SW_DIM)
    cos_m, sin_m = _rope_tables(s, MLA_ROPE)
    tile2 = lambda a: jnp.concatenate([a, a, a, a], axis=1)
    scale = (MLA_NOPE + MLA_ROPE) ** -0.5 * LOG2E
    zpad = jnp.zeros((s, MLA_HW - MLA_NOPE - MLA_ROPE), F32)
    lat = dict(
        sw_cos=tile2(cos_sw), sw_sin=tile2(sin_sw),
        q_cos=scale * jnp.concatenate([jnp.ones((s, MLA_NOPE), F32), cos_m, cos_m, zpad], axis=1),
        q_sin=scale * jnp.concatenate([jnp.zeros((s, MLA_NOPE), F32), sin_m, sin_m, zpad], axis=1),
        k_cos=jnp.concatenate([cos_m, cos_m, zpad], axis=1),
        k_sin=jnp.concatenate([sin_m, sin_m, zpad], axis=1))
    ctx = dict(
        sw_cos=jnp.ones((c, LANE), F32), sw_sin=jnp.zeros((c, LANE), F32),
        q_cos=jnp.full((c, MLA_HW), scale, F32), q_sin=jnp.zeros((c, MLA_HW), F32),
        k_cos=jnp.ones((c, LANE), F32), k_sin=jnp.zeros((c, LANE), F32))
    return lat, ctx


def _project(h, w, tab, gq, gkv, total_rows, mla_into=None):
    p_ml = matmul(h, w["ml"], F32)
    p_na = matmul(h, w["na"], CD)
    n_rope = (SW_HEADS + SW_KV_HEADS) * SW_DIM // LANE
    p_sw = rope_matmul(h, w["sw"], w["sw_rot"], tab["sw_cos"], tab["sw_sin"], n_rope)
    mla = mla_proj(h, w["mla1"], gq, gkv, w["mla_q"], w["mla_q_rot"], w["mla_kv"],
                   tab["q_cos"], tab["q_sin"], tab["k_cos"], tab["k_sin"], total_rows, mla_into)
    return p_ml, p_na, p_sw, mla


def _dense_tail(x, h, ys, w, l, w_branch, w_out, w_ffn_in, w_ffn_out, vt, g1, norm2, g2, norm_next, last):
    merged = gated_merge(h, ys, w["gate"], w_branch, l)
    x, h2 = matmul_residual_norm(merged, w_out, l, x, vt, g1, norm2, True, CD)
    u = ffn_up(h2, w_ffn_in, l)
    if last:
        return None, matmul_residual_norm(u, w_ffn_out, l, x, vt, g2, norm_next, False, F32, tn=512)
    return matmul_residual_norm(u, w_ffn_out, l, x, vt, g2, norm_next, True, CD, tn=512)


def kernel(x, c, ctx, c_ctx, w_ada, b_ada, norm1_g, norm2_g, w_in, ml_conv_w, ml_gate_b, ml_norm_g, na_rpb,
           sw_sink, mla_q_norm_g, mla_kv_norm_g, mla_w_uq, mla_w_ukv, w_branch, w_out, w_ffn_in, w_ffn_out,
           final_norm_g):
    bsz, s, d = x.shape
    assert bsz == 1
    depth = w_in.shape[0]
    n_ctx = ctx.shape[1]
    rows = s // GRID_W
    xl = x[0].astype(F32)
    xc = ctx[0].astype(F32)
    mod = adaln(jnp.stack([c[0], c_ctx], axis=1).astype(F32), w_ada, b_ada)
    vt = jnp.concatenate([mod.reshape(depth * 12, d), norm1_g.astype(F32), norm2_g.astype(F32),
                          final_norm_g.reshape(1, d).astype(F32), jnp.zeros((1, d), F32)]).reshape(-1, 1, d)
    mod_row = lambda l, stream, part: (l * 2 + stream) * 6 + part
    n1_row = lambda l: depth * 12 + l
    n2_row = lambda l: depth * 13 + l
    fin_row, zero_row = depth * 14, depth * 14 + 1
    norm1 = lambda l, st: (n1_row(l), mod_row(l, st, 0), mod_row(l, st, 1))
    norm2 = lambda l, st: (n2_row(l), mod_row(l, st, 3), mod_row(l, st, 4))
    tab_l, tab_c = _tables(s, n_ctx)
    zero_state = (jnp.zeros((2 * ML_HEADS, ML_QK, 2 * ML_V), F32), jnp.zeros((2 * ML_HEADS, 8, LANE), F32))

    w_in_t = jnp.transpose(w_in, (0, 2, 1))
    wb = w_branch.astype(CD)
    wo = w_out.astype(CD)
    wf2 = w_ffn_out.astype(CD)

    hl = norm_mod(xl, vt, norm1(0, 0), CD)
    hc = norm_mod(xc, vt, norm1(0, 1), CD)
    for l in range(depth):
        last = l == depth - 1
        w = _layer_weights(w_in_t, l, mla_w_uq[l], mla_w_ukv[l])
        gq = mla_q_norm_g[l].reshape(1, -1).astype(F32)
        gkv = mla_kv_norm_g[l].reshape(1, -1).astype(F32)
        sink = sw_sink[l].reshape(1, -1).astype(F32)

        ml_l, na_l, sw_l, mla_lat = _project(hl, w, tab_l, gq, gkv, s + n_ctx)
        ml_c, na_c, sw_c, (mq, mk, mv) = _project(hc, w, tab_c, gq, gkv, s + n_ctx, mla_lat)

        qc_c, kc_c, gt_c = ml_prep(ml_c, ml_conv_w[l], ml_gate_b[l])
        qc_l, kc_l, gt_l = ml_prep(ml_l, ml_conv_w[l], ml_gate_b[l])
        hf_c, hb_c, cst, mst = ml_scan(qc_c, kc_c, ml_c, gt_c, zero_state)
        hf_l, hb_l, _, _ = ml_scan(qc_l, kc_l, ml_l, gt_l, (cst, mst))
        ya_l = ml_finish(hf_l, hb_l, ml_l, ml_norm_g[l])
        yb_l = na_attn(na_l, na_c, na_rpb[l], rows)
        yc_l = sw_attn(sw_l, sw_c, sink)
        yd_l = mla_flash(mq, mk, mv, s)

        nxt_l = (fin_row, zero_row, zero_row) if last else norm1(l + 1, 0)
        xl_new, hl_new = _dense_tail(xl, hl, (ya_l, yb_l, yc_l, yd_l), w, l, wb, wo, w_ffn_in, wf2, vt,
                                     mod_row(l, 0, 2), norm2(l, 0), mod_row(l, 0, 5), nxt_l, last)
        if not last:
            ya_c = ml_finish(hf_c, hb_c, ml_c, ml_norm_g[l])
            nw = NA_HEADS * NA_DIM
            yb_c = dense_attn([(na_c, nw, 0), (na_c, nw, 1), (na_c, nw, 2)], NA_HEADS, 1, NA_DIM, NA_DIM,
                              n_ctx, log2_scores=True)
            kcol = SW_HEADS * SW_DIM // LANE
            yc_c = dense_attn([(sw_c, SW_HEADS * SW_DIM, 0), (sw_c, LANE, kcol), (sw_c, LANE, kcol + 1)],
                              SW_HEADS, SW_HEADS // SW_KV_HEADS, SW_DIM, SW_DIM, n_ctx, sink=sink,
                              log2_scores=True)
            assert s % n_ctx == 0
            yd_c = dense_attn([(mq, MLA_HEADS * MLA_HW, 0), (mk, MLA_HEADS * MLA_HW, 0),
                               (mv, MLA_HEADS * MLA_V, 0)], MLA_HEADS, 1, MLA_HW, MLA_V, n_ctx,
                              log2_scores=True, row_block=s // n_ctx)
            xc, hc = _dense_tail(xc, hc, (ya_c, yb_c, yc_c, yd_c), w, l, wb, wo, w_ffn_in, wf2, vt,
                                 mod_row(l, 1, 2), norm2(l, 1), mod_row(l, 1, 5), norm1(l + 1, 1), False)
        xl, hl = xl_new, hl_new

    return hl[None].astype(x.dtype)
```

```python
import functools

import numpy as np
import jax
import jax.numpy as jnp
from jax import lax
from jax.experimental import pallas as pl
from jax.experimental.pallas import tpu as pltpu

F32 = jnp.float32
CD = jnp.bfloat16

GRID_W = 64
EPS = 1e-6
ROPE_THETA = 10000.0
ML_HEADS, ML_QK, ML_V, ML_CONV = 4, 64, 128, 3
NA_HEADS, NA_DIM, NA_WIN_R, NA_WIN_C = 8, 64, 8, 16
SW_HEADS, SW_KV_HEADS, SW_DIM, SW_WINDOW = 8, 2, 64, 128
MLA_HEADS, MLA_Q_RANK, MLA_KV_RANK, MLA_NOPE, MLA_ROPE, MLA_V = 4, 384, 256, 128, 64, 128
N_BRANCH, BRANCH_W = 4, 512

ML_QKW = ML_HEADS * ML_QK
ML_VW = ML_HEADS * ML_V
ML_COLS = 2 * ML_QKW + 2 * ML_VW + 4 * ML_HEADS
ML_PAD = 1664
NA_COLS = 3 * NA_HEADS * NA_DIM
SW_COLS = (SW_HEADS + 2 * SW_KV_HEADS) * SW_DIM
MLA_COLS = MLA_Q_RANK + MLA_KV_RANK + MLA_ROPE
MLA_HW = 256
LANE = 128
NEG = -1e30
LOG2E = float(np.log2(np.e))
VMEM_LIMIT = 56 * 1024 * 1024

NA_QROWS = 4
SW_TQ = 256
ML_CHUNK = 256
MLA_TK = 1280


def _cparams(*sem):
    return pltpu.CompilerParams(dimension_semantics=sem, vmem_limit_bytes=VMEM_LIMIT)


def _row_tile(t, pref=512):
    tm = min(pref, t)
    assert t % tm == 0
    return tm


def _dot(a, b):
    return jnp.dot(a, b, preferred_element_type=F32)


def _dot_nt(a, b):
    return lax.dot_general(a, b, (((1,), (1,)), ((), ())), preferred_element_type=F32)


def _sigmoid(x):
    return 1.0 / (1.0 + jnp.exp(-x))


def _lanes(col, n):
    tile = jnp.broadcast_to(col, (col.shape[0], LANE))
    return tile if n == LANE else jnp.tile(tile, (1, n // LANE))


def _ones_col(n, w):
    return jnp.where(lax.broadcasted_iota(jnp.int32, (n, w), 1) == 0, 1.0, 0.0).astype(CD)


def _adaln_kernel(c_ref, w_ref, b_ref, o_ref):
    w = w_ref[0]
    for r in range(2):
        c = c_ref[:, r:r + 1]
        o_ref[0, r:r + 1, :] = jnp.sum((c * _sigmoid(c)) * w, axis=0, keepdims=True) + b_ref[0]


def adaln(cc, w_ada, b_ada):
    nl, d, n = w_ada.shape
    tn = 1024
    return pl.pallas_call(
        _adaln_kernel,
        grid=(nl, n // tn),
        in_specs=[pl.BlockSpec((d, 2), lambda l, j: (0, 0)),
                  pl.BlockSpec((1, d, tn), lambda l, j: (l, 0, j)),
                  pl.BlockSpec((1, 1, tn), lambda l, j: (l, 0, j))],
        out_specs=pl.BlockSpec((1, 2, tn), lambda l, j: (l, 0, j)),
        out_shape=jax.ShapeDtypeStruct((nl, 2, n), F32),
        compiler_params=_cparams("parallel", "parallel"),
        name="adaln",
    )(cc, w_ada, b_ada.reshape(nl, 1, n))


def _vec_spec(d, idx):
    return pl.BlockSpec((1, 1, d), lambda *_: (idx, 0, 0))


def _norm_mod(x, g, shift, scale):
    y = x * lax.rsqrt(jnp.mean(x * x, axis=-1, keepdims=True) + EPS)
    return (y * g) * (1.0 + scale) + shift


def _norm_mod_kernel(x_ref, g_ref, sh_ref, sc_ref, o_ref):
    o_ref[...] = _norm_mod(x_ref[...], g_ref[0], sh_ref[0], sc_ref[0]).astype(o_ref.dtype)


def norm_mod(x, vt, norm_idx, out_dtype):
    t, d = x.shape
    tm = _row_tile(t, 256)
    return pl.pallas_call(
        _norm_mod_kernel,
        grid=(t // tm,),
        in_specs=[pl.BlockSpec((tm, d), lambda i: (i, 0))] + [_vec_spec(d, ix) for ix in norm_idx],
        out_specs=pl.BlockSpec((tm, d), lambda i: (i, 0)),
        out_shape=jax.ShapeDtypeStruct((t, d), out_dtype),
        compiler_params=_cparams("parallel"),
        name="norm_mod",
    )(x, vt, vt, vt)


def _mm_kernel(a_ref, b_ref, o_ref):
    o_ref[...] = _dot_nt(a_ref[...], b_ref[...]).astype(o_ref.dtype)


def matmul(a, bt, out_dtype, tn=None):
    t, k = a.shape
    n = bt.shape[0]
    tn = n if tn is None else tn
    tm = _row_tile(t, 1024)
    return pl.pallas_call(
        _mm_kernel,
        grid=(n // tn, t // tm),
        in_specs=[pl.BlockSpec((tm, k), lambda j, i: (i, 0)),
                  pl.BlockSpec((tn, k), lambda j, i: (j, 0))],
        out_specs=pl.BlockSpec((tm, tn), lambda j, i: (i, j)),
        out_shape=jax.ShapeDtypeStruct((t, n), out_dtype),
        compiler_params=_cparams("parallel", "parallel"),
        name="matmul",
    )(a, bt)


def _mm_res_norm_kernel(nj, tn, want_x, a_ref, b_ref, x_ref, gate_ref, g_ref, sh_ref, sc_ref, *rest):
    xbuf, ho_ref = (rest[0], rest[1]) if want_x else (rest[1], rest[0])
    j = pl.program_id(1)
    acc = _dot(a_ref[...], b_ref[0])
    for jj in range(nj):
        @pl.when(j == jj)
        def _(jj=jj):
            sl = slice(jj * tn, (jj + 1) * tn)
            xbuf[:, sl] = x_ref[:, sl] + gate_ref[0][:, sl] * acc

    @pl.when(j == nj - 1)
    def _():
        ho_ref[...] = _norm_mod(xbuf[...], g_ref[0], sh_ref[0], sc_ref[0]).astype(ho_ref.dtype)


def matmul_residual_norm(a, b, l, x, vt, gate_idx, norm_idx, want_x, h_dtype, tn=None):
    t, kdim = a.shape
    n = b.shape[2]
    tn = n if tn is None else tn
    nj = n // tn
    tm = _row_tile(t)
    row = pl.BlockSpec((tm, n), lambda i, j: (i, 0))
    out_shape = [jax.ShapeDtypeStruct((t, n), h_dtype)]
    if want_x:
        out_shape = [jax.ShapeDtypeStruct((t, n), F32)] + out_shape
    out = pl.pallas_call(
        functools.partial(_mm_res_norm_kernel, nj, tn, want_x),
        grid=(t // tm, nj),
        in_specs=[pl.BlockSpec((tm, kdim), lambda i, j: (i, 0)),
                  pl.BlockSpec((1, kdim, tn), lambda i, j: (l, 0, j)),
                  row, _vec_spec(n, gate_idx)] + [_vec_spec(n, ix) for ix in norm_idx],
        out_specs=[row] * len(out_shape),
        out_shape=out_shape,
        scratch_shapes=[] if want_x else [pltpu.VMEM((tm, n), F32)],
        compiler_params=_cparams("parallel", "arbitrary"),
        name="matmul_residual_norm",
    )(a, b, x, vt, vt, vt, vt)
    return out if want_x else out[0]


def _rope_mm_kernel(n_rope, a_ref, w_ref, wr_ref, cos_ref, sin_ref, o_ref):
    a = a_ref[...]
    p = _dot_nt(a, w_ref[...])
    pr = _dot_nt(a, wr_ref[...])
    cos = cos_ref[...]
    sin = sin_ref[...]
    for j in range(p.shape[1] // LANE):
        sl = slice(j * LANE, (j + 1) * LANE)
        if j < n_rope:
            o_ref[:, sl] = (p[:, sl] * cos + pr[:, sl] * sin).astype(o_ref.dtype)
        else:
            o_ref[:, sl] = p[:, sl].astype(o_ref.dtype)


def rope_matmul(a, w, w_rot, cos, sin, n_rope):
    t, k = a.shape
    n = w.shape[0]
    nr = n_rope * LANE
    tm = _row_tile(t, 1024)
    return pl.pallas_call(
        functools.partial(_rope_mm_kernel, n_rope),
        grid=(t // tm,),
        in_specs=[pl.BlockSpec((tm, k), lambda i: (i, 0)),
                  pl.BlockSpec((n, k), lambda i: (0, 0)),
                  pl.BlockSpec((nr, k), lambda i: (0, 0)),
                  pl.BlockSpec((tm, LANE), lambda i: (i, 0)),
                  pl.BlockSpec((tm, LANE), lambda i: (i, 0))],
        out_specs=pl.BlockSpec((tm, n), lambda i: (i, 0)),
        out_shape=jax.ShapeDtypeStruct((t, n), CD),
        compiler_params=_cparams("parallel"),
        name="rope_matmul",
    )(a, w, w_rot, cos, sin)


def _mla_proj_kernel(h_ref, w1_ref, gq_ref, gkv_ref, wq_ref, wqr_ref, wkv_ref,
                     cq_ref, sq_ref, ck_ref, sk_ref, q_out, k_out, v_out):
    p = _dot_nt(h_ref[...], w1_ref[...])
    cq = p[:, :MLA_Q_RANK]
    ckv = p[:, MLA_Q_RANK:MLA_Q_RANK + MLA_KV_RANK]
    o = MLA_Q_RANK + MLA_KV_RANK
    kr = p[:, o:o + LANE]
    krr = p[:, o + LANE:o + 2 * LANE]
    cqn = (cq * lax.rsqrt(jnp.mean(cq * cq, axis=-1, keepdims=True) + EPS) * gq_ref[...]).astype(CD)
    ckvn = (ckv * lax.rsqrt(jnp.mean(ckv * ckv, axis=-1, keepdims=True) + EPS) * gkv_ref[...]).astype(CD)
    qa = _dot(cqn, wq_ref[...])
    qb = _dot(cqn, wqr_ref[...])
    kv = _dot(ckvn, wkv_ref[...])
    krp = (kr * ck_ref[...] + krr * sk_ref[...]).astype(CD)
    cq_t = cq_ref[...]
    sq_t = sq_ref[...]
    for h in range(MLA_HEADS):
        qs = slice(h * MLA_HW, (h + 1) * MLA_HW)
        q_out[:, qs] = (qa[:, qs] * cq_t + qb[:, qs] * sq_t).astype(CD)
        k_out[:, h * MLA_HW:h * MLA_HW + MLA_NOPE] = kv[:, h * 256:h * 256 + MLA_NOPE].astype(CD)
        k_out[:, h * MLA_HW + MLA_NOPE:(h + 1) * MLA_HW] = krp
        v_out[:, h * MLA_V:(h + 1) * MLA_V] = kv[:, h * 256 + MLA_NOPE:(h + 1) * 256].astype(CD)


def mla_proj(h, w1, gq, gkv, wq, wqr, wkv, cos_q, sin_q, cos_k, sin_k, total_rows, into=None):
    t, d = h.shape
    tm = _row_tile(t)
    full = lambda a: pl.BlockSpec(a.shape, lambda i: (0,) * a.ndim)
    rows = lambda w: pl.BlockSpec((tm, w), lambda i: (i, 0))
    hw = MLA_HEADS * MLA_HW
    vw = MLA_HEADS * MLA_V
    args = [h, w1, gq, gkv, wq, wqr, wkv, cos_q, sin_q, cos_k, sin_k]
    in_specs = [rows(d), full(w1), full(gq), full(gkv), full(wq), full(wqr), full(wkv),
                rows(MLA_HW), rows(MLA_HW), rows(LANE), rows(LANE)]
    if into is None:
        off, aliases, kern = 0, {}, _mla_proj_kernel
    else:
        assert (total_rows - t) % tm == 0
        off = (total_rows - t) // tm
        aliases = {len(args) + i: i for i in range(3)}
        in_specs = in_specs + [pl.BlockSpec(memory_space=pl.ANY)] * 3
        args = args + list(into)
        kern = lambda *refs: _mla_proj_kernel(*refs[:11], *refs[14:])
    orow = lambda w: pl.BlockSpec((tm, w), lambda i: (off + i, 0))
    return pl.pallas_call(
        kern,
        grid=(t // tm,),
        in_specs=in_specs,
        out_specs=[orow(hw), orow(hw), orow(vw)],
        out_shape=[jax.ShapeDtypeStruct((total_rows, hw), CD), jax.ShapeDtypeStruct((total_rows, hw), CD),
                   jax.ShapeDtypeStruct((total_rows, vw), CD)],
        input_output_aliases=aliases,
        compiler_params=_cparams("parallel"),
        name="mla_proj",
    )(*args)


def _flash_kernel(tq, tk, unroll, q_ref, k_ref, v_ref, o_ref, s_sc, m_sc, acc_sc):
    nkb = k_ref.shape[0] // tk
    nsub = q_ref.shape[0] // tq
    total = nsub * nkb
    m_sc[...] = jnp.full(m_sc.shape, NEG, F32)
    acc_sc[...] = jnp.zeros(acc_sc.shape, F32)
    ones_col = jnp.where(lax.broadcasted_iota(jnp.int32, (tk, LANE), 1) == 0, 1.0, 0.0).astype(CD)

    def split(b):
        sub = b // nkb
        return sub, b - sub * nkb

    def scores(b):
        sub, j = split(b)
        q = q_ref[pl.ds(pl.multiple_of(sub * tq, tq), tq), :]
        k = k_ref[pl.ds(pl.multiple_of(j * tk, tk), tk), :]
        return _dot_nt(q, k)

    def softmax_pv(s, b):
        sub, j = split(b)
        v = v_ref[pl.ds(pl.multiple_of(j * tk, tk), tk), :]
        m_prev = m_sc[sub]
        m_new = jnp.maximum(m_prev, jnp.max(s, axis=-1, keepdims=True))
        alpha = jnp.exp2(m_prev - m_new)
        p = jnp.exp2(s - m_new).astype(CD)
        v_aug = jnp.concatenate([v, ones_col], axis=1)
        acc_sc[sub] = alpha * acc_sc[sub] + _dot(p, v_aug)
        m_sc[sub] = m_new

    s_sc[0] = scores(0)

    def body(g, carry):
        b0 = g * unroll
        for u in range(unroll):
            s_sc[(u + 1) % 2] = scores(jnp.minimum(b0 + u + 1, total - 1))
            softmax_pv(s_sc[u % 2], b0 + u)
        return carry

    lax.fori_loop(0, total // unroll, body, 0)
    for sub in range(nsub):
        acc = acc_sc[sub]
        o_ref[sub * tq:(sub + 1) * tq, :] = (acc[:, :MLA_V] / acc[:, MLA_V:MLA_V + 1]).astype(o_ref.dtype)


def mla_flash(q, k, v, s):
    nk = k.shape[0]
    tq = _row_tile(s, 512)
    tstep = _row_tile(s, 8 * tq)
    tk = MLA_TK
    total = (tstep // tq) * (nk // tk)
    assert nk % tk == 0 and total % 2 == 0
    unroll = 8 if total % 8 == 0 else (4 if total % 4 == 0 else 2)
    return pl.pallas_call(
        functools.partial(_flash_kernel, tq, tk, unroll),
        grid=(MLA_HEADS, s // tstep),
        in_specs=[pl.BlockSpec((tstep, MLA_HW), lambda h, i: (i, h)),
                  pl.BlockSpec((nk, MLA_HW), lambda h, i: (0, h)),
                  pl.BlockSpec((nk, MLA_V), lambda h, i: (0, h))],
        out_specs=pl.BlockSpec((tstep, MLA_V), lambda h, i: (i, h)),
        out_shape=jax.ShapeDtypeStruct((s, MLA_HEADS * MLA_V), CD),
        scratch_shapes=[pltpu.VMEM((2, tq, tk), F32), pltpu.VMEM((tstep // tq, tq, 1), F32),
                        pltpu.VMEM((tstep // tq, tq, 2 * MLA_V), F32)],
        compiler_params=_cparams("parallel", "arbitrary"),
        name="mla_flash",
    )(q, k, v)


def _dense_attn_kernel(n_heads, group, dqk, dv, log2_scores, q_ref, k_ref, v_ref, sink_ref, o_ref):
    q = q_ref[...]
    k = k_ref[...]
    v = v_ref[...]
    ex = jnp.exp2 if log2_scores else jnp.exp
    for h in range(n_heads):
        g = h // group
        s = _dot_nt(q[:, h * dqk:(h + 1) * dqk], k[:, g * dqk:(g + 1) * dqk])
        m = jnp.max(s, axis=-1, keepdims=True)
        l = jnp.zeros_like(m)
        if sink_ref is not None:
            sk = sink_ref[0:1, h:h + 1] * (LOG2E if log2_scores else 1.0)
            m = jnp.maximum(m, sk)
            l = ex(sk - m)
        p = ex(s - m)
        l = l + jnp.sum(p, axis=-1, keepdims=True)
        o = _dot(p.astype(CD), v[:, g * dv:(g + 1) * dv]) / l
        o_ref[:, h * dv:(h + 1) * dv] = o.astype(o_ref.dtype)


def dense_attn(qkv_specs, n_heads, group, dqk, dv, c, sink=None, log2_scores=False, row_block=0):
    arrays = [a for a, _, _ in qkv_specs]
    specs = [pl.BlockSpec((c, w), functools.partial(lambda b, i: (row_block, b), b)) for _, w, b in qkv_specs]
    if sink is None:
        kern = lambda q, k, v, o: _dense_attn_kernel(n_heads, group, dqk, dv, log2_scores, q, k, v, None, o)
    else:
        kern = functools.partial(_dense_attn_kernel, n_heads, group, dqk, dv, log2_scores)
        arrays.append(sink)
        specs.append(pl.BlockSpec(sink.shape, lambda i: (0, 0)))
    return pl.pallas_call(
        kern,
        grid=(1,),
        in_specs=specs,
        out_specs=pl.BlockSpec((c, n_heads * dv), lambda i: (0, 0)),
        out_shape=jax.ShapeDtypeStruct((c, n_heads * dv), CD),
        compiler_params=_cparams("arbitrary"),
        name="dense_attn",
    )(*arrays)


def _na_kernel(q_ref, kp_ref, ko_ref, kn_ref, vp_ref, vo_ref, vn_ref, kc_ref, vc_ref, bias_ref, o_ref):
    q = q_ref[...]
    kcat = jnp.concatenate([kp_ref[...], ko_ref[...], kn_ref[...], kc_ref[...]], axis=0)
    vcat = jnp.concatenate([vp_ref[...], vo_ref[...], vn_ref[...], vc_ref[...]], axis=0)
    tb = q.shape[0]
    nloc = 3 * tb
    ones = _ones_col(kcat.shape[0], LANE)
    first_half = lax.broadcasted_iota(jnp.int32, (tb, LANE), 1) < NA_DIM
    zero = jnp.zeros((tb, LANE), CD)
    for j in range(NA_HEADS // 2):
        ps = slice(j * LANE, (j + 1) * LANE)
        q2 = q[:, ps]
        k2 = kcat[:, ps]
        v_aug = jnp.concatenate([vcat[:, ps], ones], axis=1)
        outs = []
        for e in range(2):
            qm = jnp.where(first_half, q2, zero) if e == 0 else jnp.where(first_half, zero, q2)
            s = _dot_nt(qm, k2)
            s = jnp.concatenate([s[:, :nloc] + bias_ref[0, 2 * j + e], s[:, nloc:]], axis=1)
            m = jnp.max(s, axis=-1, keepdims=True)
            p = jnp.exp2(s - _lanes(m, s.shape[1])).astype(CD)
            acc = _dot(p, v_aug)
            outs.append(acc[:, :LANE] / _lanes(acc[:, LANE:LANE + 1], LANE))
        o_ref[:, ps] = jnp.where(first_half, outs[0], outs[1]).astype(o_ref.dtype)


def _na_bias(rpb, rows):
    qr = NA_QROWS
    tb = qr * GRID_W
    nb = rows // qr
    nkr = 3 * qr
    rsel = np.zeros((3, qr, nkr, 2 * NA_WIN_R - 1), np.float32)
    for v, b in enumerate((0, 1, nb - 1)):
        for a in range(qr):
            r = qr * b + a
            rs = min(max(r - NA_WIN_R // 2, 0), rows - NA_WIN_R)
            for u in range(nkr):
                kr = qr * (b - 1) + u
                if 0 <= b - 1 + u // qr < nb and rs <= kr < rs + NA_WIN_R:
                    rsel[v, a, u, kr - r + NA_WIN_R - 1] = 1.0
    csel = np.zeros((GRID_W, GRID_W, 2 * NA_WIN_C - 1), np.float32)
    for qc in range(GRID_W):
        cs = min(max(qc - NA_WIN_C // 2, 0), GRID_W - NA_WIN_C)
        for kc in range(cs, cs + NA_WIN_C):
            csel[qc, kc, kc - qc + NA_WIN_C - 1] = 1.0
    valid = (np.einsum('vaud,qkj->vaquk', rsel, csel) > 0).reshape(3, tb, 3 * tb)
    assert 2 * qr == NA_WIN_R
    nd = 2 * NA_WIN_R - 1
    rp = rpb.astype(F32) * LOG2E
    gap = jnp.zeros((NA_HEADS, nd, GRID_W + 1 - (2 * NA_WIN_C - 1)), F32)
    ring = jnp.concatenate([rp[..., NA_WIN_C - 1:], gap, rp[..., :NA_WIN_C - 1]], axis=-1)
    cols = jnp.tile(ring, (1, 1, GRID_W))[..., :GRID_W * GRID_W].reshape(NA_HEADS, nd, GRID_W, GRID_W)
    off = NA_WIN_R - 1 - qr
    slabs = [cols[:, off - a:off - a + nkr].transpose(0, 2, 1, 3).reshape(NA_HEADS, GRID_W, 3 * tb)
             for a in range(qr)]
    band = jnp.stack(slabs, axis=1).reshape(NA_HEADS, tb, 3 * tb)
    return jnp.where(valid[:, None], band[None], NEG)


def na_attn(p_lat, p_ctx, rpb, rows):
    s = p_lat.shape[0]
    c = p_ctx.shape[0]
    w = NA_HEADS * NA_DIM
    tb = NA_QROWS * GRID_W
    nb = s // tb
    assert rows % NA_QROWS == 0 and nb >= 3 and rows >= 2 * NA_WIN_R
    bias = _na_bias(rpb, rows)

    def blk(col, off):
        return pl.BlockSpec((tb, w), lambda i: (jnp.clip(i + off, 0, nb - 1), col))

    return pl.pallas_call(
        _na_kernel,
        grid=(nb,),
        in_specs=[blk(0, 0), blk(1, -1), blk(1, 0), blk(1, 1), blk(2, -1), blk(2, 0), blk(2, 1),
                  pl.BlockSpec((c, w), lambda i: (0, 1)),
                  pl.BlockSpec((c, w), lambda i: (0, 2)),
                  pl.BlockSpec((1, NA_HEADS, tb, 3 * tb),
                               lambda i: (jnp.where(i == 0, 0, jnp.where(i == nb - 1, 2, 1)), 0, 0, 0))],
        out_specs=pl.BlockSpec((tb, w), lambda i: (i, 0)),
        out_shape=jax.ShapeDtypeStruct((s, w), CD),
        compiler_params=_cparams("parallel"),
        name="na_attn",
    )(p_lat, p_lat, p_lat, p_lat, p_lat, p_lat, p_lat, p_ctx, p_ctx, bias)


def _sw_kernel(seq, q_ref, kp_ref, ko_ref, kn_ref, vp_ref, vo_ref, vn_ref, kc_ref, vc_ref, sink_ref, o_ref):
    i = pl.program_id(0)
    q = q_ref[...]
    tq = q.shape[0]
    kcat = jnp.concatenate([kp_ref[...], ko_ref[...], kn_ref[...], kc_ref[...]], axis=0)
    vcat = jnp.concatenate([vp_ref[...], vo_ref[...], vn_ref[...], vc_ref[...]], axis=0)
    nkk = tq + 2 * SW_WINDOW
    r = lax.broadcasted_iota(jnp.int32, (tq, nkk), 0)
    cc = lax.broadcasted_iota(jnp.int32, (tq, nkk), 1)
    kpos = i * tq - SW_WINDOW + cc
    d = cc - r
    mask = (d >= 0) & (d <= 2 * SW_WINDOW) & (kpos >= 0) & (kpos < seq)
    assert SW_KV_HEADS == 2 and SW_DIM * 2 == LANE
    ones = _ones_col(kcat.shape[0], LANE)
    swap = lambda x: jnp.concatenate([x[:, SW_DIM:], x[:, :SW_DIM]], axis=1)
    k_by_order = (kcat, swap(kcat))
    v_by_order = (jnp.concatenate([vcat, ones], axis=1), jnp.concatenate([swap(vcat), ones], axis=1))
    first_half = lax.broadcasted_iota(jnp.int32, (tq, LANE), 1) < SW_DIM
    zero = jnp.zeros((tq, LANE), CD)
    grp = SW_HEADS // SW_KV_HEADS
    for j in range(SW_HEADS // 2):
        ps = slice(j * LANE, (j + 1) * LANE)
        q2 = q[:, ps]
        outs = []
        for e in range(2):
            h = 2 * j + e
            order = 0 if h // grp == e else 1
            qm = jnp.where(first_half, q2, zero) if e == 0 else jnp.where(first_half, zero, q2)
            s = _dot_nt(qm, k_by_order[order])
            s = jnp.concatenate([jnp.where(mask, s[:, :nkk], NEG), s[:, nkk:]], axis=1)
            sk = sink_ref[0:1, h:h + 1] * LOG2E
            m = jnp.maximum(jnp.max(s, axis=-1, keepdims=True), sk)
            p = jnp.exp2(s - _lanes(m, s.shape[1])).astype(CD)
            acc = _dot(p, v_by_order[order])
            outs.append(acc[:, :LANE] / _lanes(acc[:, LANE:LANE + 1] + jnp.exp2(sk - m), LANE))
        o_ref[:, ps] = jnp.where(first_half, outs[0], outs[1]).astype(o_ref.dtype)


def sw_attn(p_lat, p_ctx, sink):
    s = p_lat.shape[0]
    c = p_ctx.shape[0]
    tq = SW_TQ
    assert s % tq == 0 and tq % SW_WINDOW == 0
    nq = s // tq
    per = tq // SW_WINDOW
    nwb = s // SW_WINDOW
    kcol = SW_HEADS * SW_DIM // LANE
    prev = lambda col: pl.BlockSpec((SW_WINDOW, LANE), lambda i: (jnp.maximum(i * per - 1, 0), col))
    own = lambda col: pl.BlockSpec((tq, LANE), lambda i: (i, col))
    nxt = lambda col: pl.BlockSpec((SW_WINDOW, LANE), lambda i: (jnp.minimum((i + 1) * per, nwb - 1), col))
    return pl.pallas_call(
        functools.partial(_sw_kernel, s),
        grid=(nq,),
        in_specs=[pl.BlockSpec((tq, SW_HEADS * SW_DIM), lambda i: (i, 0)),
                  prev(kcol), own(kcol), nxt(kcol), prev(kcol + 1), own(kcol + 1), nxt(kcol + 1),
                  pl.BlockSpec((c, LANE), lambda i: (0, kcol)),
                  pl.BlockSpec((c, LANE), lambda i: (0, kcol + 1)),
                  pl.BlockSpec(sink.shape, lambda i: (0, 0))],
        out_specs=pl.BlockSpec((tq, SW_HEADS * SW_DIM), lambda i: (i, 0)),
        out_shape=jax.ShapeDtypeStruct((s, SW_HEADS * SW_DIM), CD),
        compiler_params=_cparams("parallel"),
        name="sw_attn",
    )(p_lat, p_lat, p_lat, p_lat, p_lat, p_lat, p_lat, p_ctx, p_ctx, sink)


def _ml_prep_kernel(nt, x_ref, xp_ref, xn_ref, g_ref, w_ref, b_ref, q_out, k_out, g_out):
    i = pl.program_id(0)
    x = x_ref[...]
    tm = x.shape[0]
    row = lax.broadcasted_iota(jnp.int32, x.shape, 0)
    has_prev = jnp.where(i > 0, 1.0, 0.0)
    has_next = jnp.where(i < nt - 1, 1.0, 0.0)
    prev_row = xp_ref[7:8, :] * has_prev
    next_row = xn_ref[0:1, :] * has_next
    xm = jnp.where(row == 0, prev_row, pltpu.roll(x, 1, axis=0))
    xq = jnp.where(row == tm - 1, next_row, pltpu.roll(x, tm - 1, axis=0))
    y = w_ref[0:1, :] * xm + w_ref[1:2, :] * x + w_ref[2:3, :] * xq
    y = y * _sigmoid(y)
    q_out[...] = y[:, :ML_QKW].astype(CD)
    k_out[...] = (y[:, ML_QKW:] * (ML_QK ** -0.5)).astype(CD)
    g = g_ref[...] + b_ref[...]
    lane = lax.broadcasted_iota(jnp.int32, g.shape, 1)
    logsig = jnp.minimum(g, 0.0) - jnp.log(1.0 + jnp.exp(-jnp.abs(g)))
    g_out[...] = jnp.where(lane < 2 * ML_HEADS, g, logsig)


def ml_prep(p_ml, conv_w, gate_b):
    t = p_ml.shape[0]
    tm = _row_tile(t, 256)
    nt = t // tm
    wq = 2 * ML_QKW
    gcol = (2 * ML_QKW + 2 * ML_VW) // LANE
    w8 = jnp.zeros((8, wq), F32).at[:ML_CONV].set(conv_w.astype(F32))
    b = jnp.zeros((1, LANE), F32).at[0, :4 * ML_HEADS].set(gate_b.astype(F32).reshape(-1))
    h8 = tm // 8
    return pl.pallas_call(
        functools.partial(_ml_prep_kernel, nt),
        grid=(nt,),
        in_specs=[pl.BlockSpec((tm, wq), lambda i: (i, 0)),
                  pl.BlockSpec((8, wq), lambda i: (jnp.maximum(i * h8 - 1, 0), 0)),
                  pl.BlockSpec((8, wq), lambda i: (jnp.minimum((i + 1) * h8, t // 8 - 1), 0)),
                  pl.BlockSpec((tm, LANE), lambda i: (i, gcol)),
                  pl.BlockSpec((8, wq), lambda i: (0, 0)),
                  pl.BlockSpec((1, LANE), lambda i: (0, 0))],
        out_specs=[pl.BlockSpec((tm, ML_QKW), lambda i: (i, 0)),
                   pl.BlockSpec((tm, ML_QKW), lambda i: (i, 0)),
                   pl.BlockSpec((tm, LANE), lambda i: (i, 0))],
        out_shape=[jax.ShapeDtypeStruct((t, ML_QKW), CD), jax.ShapeDtypeStruct((t, ML_QKW), CD),
                   jax.ShapeDtypeStruct((t, LANE), F32)],
        compiler_params=_cparams("parallel"),
        name="ml_prep",
    )(p_ml, p_ml, p_ml, p_ml, w8, b)


def _ml_scan_kernel(qf_ref, ktf_ref, vf_ref, gcf_ref, grf_ref,
                    qb_ref, ktb_ref, vb_ref, gcb_ref, grb_ref,
                    c0_ref, m0_ref, hf_ref, hb_ref, c_ref, m_ref):
    @pl.when(pl.program_id(0) == 0)
    def _():
        c_ref[...] = c0_ref[...]
        m_ref[...] = m0_ref[...]

    ln = qf_ref.shape[0]
    r = lax.broadcasted_iota(jnp.int32, (ln, ln), 0)
    c = lax.broadcasted_iota(jnp.int32, (ln, ln), 1)
    lower = r >= c
    upper = c >= r
    lower_f = jnp.where(lower, 1.0, 0.0)
    upper_f = jnp.where(upper, 1.0, 0.0)
    ones_col = jnp.where(lax.broadcasted_iota(jnp.int32, (ln, LANE), 1) == 0, 1.0, 0.0).astype(CD)
    hi = lax.Precision.HIGHEST
    first_half = lax.broadcasted_iota(jnp.int32, (ln, LANE), 1) < ML_QK
    zero_q = jnp.zeros((ln, LANE), CD)
    new_states = []

    streams = ((qf_ref, ktf_ref, vf_ref, gcf_ref, grf_ref, hf_ref, lower, lower_f, upper_f, ln - 1),
               (qb_ref, ktb_ref, vb_ref, gcb_ref, grb_ref, hb_ref, upper, upper_f, lower_f, 0))
    for d, (q_ref, kt_ref, v_ref, gc_ref, gr_ref, h_ref, mask, tri_c, tri_r, last) in enumerate(streams):
        gcol = gc_ref[...]
        grow = gr_ref[...]
        bcol = jnp.dot(tri_c, gcol, precision=hi, preferred_element_type=F32)
        brow = jnp.dot(grow, tri_r, precision=hi, preferred_element_type=F32)
        q = q_ref[...]
        kt = kt_ref[...]
        v = v_ref[...]
        for hd in range(ML_HEADS):
            ch = d * ML_HEADS + hd
            ic = d * ML_HEADS + hd
            fc = 2 * ML_HEADS + ic
            pair, half = hd // 2, hd % 2
            b_c = bcol[:, fc:fc + 1]
            r_row = grow[ic:ic + 1, :] - brow[fc:fc + 1, :]
            m_prev = m_ref[ch, 0:1, 0:1]
            cst = c_ref[ch]
            rmat = jnp.where(mask, r_row, NEG)
            mx = jnp.maximum(jnp.max(rmat, axis=-1, keepdims=True), m_prev)
            q2 = q[:, pair * LANE:(pair + 1) * LANE]
            qm = jnp.where(first_half, q2, zero_q) if half == 0 else jnp.where(first_half, zero_q, q2)
            s = _dot(qm, kt[pair * LANE:(pair + 1) * LANE, :]) * jnp.exp(rmat - mx)
            w_inter = jnp.exp(m_prev - mx)
            v_aug = jnp.concatenate([v[:, hd * ML_V:(hd + 1) * ML_V].astype(CD), ones_col], axis=1)
            c_pair = jnp.concatenate([c_ref[ch - half], c_ref[ch - half + 1]], axis=0).astype(CD)
            num = _dot(s.astype(CD), v_aug) + w_inter * _dot(qm, c_pair)
            den = num[:, ML_V:ML_V + 1]
            h = num[:, :ML_V] / jnp.maximum(jnp.abs(den), jnp.exp(-(b_c + mx)))
            h_ref[:, hd * ML_V:(hd + 1) * ML_V] = h
            b_end = b_c[last:last + 1, :]
            m_new = b_end + jnp.maximum(m_prev, jnp.max(r_row, axis=-1, keepdims=True))
            w_s = jnp.exp(b_end + r_row - m_new)
            w_c = jnp.exp(b_end + m_prev - m_new)
            kth = kt[hd * ML_QK:(hd + 1) * ML_QK, :]
            ktw = (kth.astype(F32) * w_s).astype(CD)
            new_states.append((ch, w_c * cst + _dot(ktw, v_aug), m_new))
    for ch, c_new, m_new in new_states:
        c_ref[ch] = c_new
        m_ref[ch] = jnp.broadcast_to(m_new, m_ref.shape[1:])


def ml_scan(qc, kc, p_ml, gates, state):
    t = qc.shape[0]
    ln = min(ML_CHUNK, t)
    assert t % ln == 0
    nc = t // ln
    kt = kc.T
    gr = gates[:, :4 * ML_HEADS].T
    c0, m0 = state
    vcol = 2 * ML_QKW // ML_VW
    fwd = lambda j: j
    bwd = lambda j: nc - 1 - j

    def specs(ix):
        return [pl.BlockSpec((ln, ML_QKW), lambda j: (ix(j), 0)),
                pl.BlockSpec((ML_QKW, ln), lambda j: (0, ix(j))),
                pl.BlockSpec((ln, ML_VW), lambda j: (ix(j), vcol)),
                pl.BlockSpec((ln, LANE), lambda j: (ix(j), 0)),
                pl.BlockSpec((4 * ML_HEADS, ln), lambda j: (0, ix(j)))]

    st_specs = [pl.BlockSpec(c0.shape, lambda j: (0, 0, 0)), pl.BlockSpec(m0.shape, lambda j: (0, 0, 0))]
    return pl.pallas_call(
        _ml_scan_kernel,
        grid=(nc,),
        in_specs=specs(fwd) + specs(bwd) + st_specs,
        out_specs=[pl.BlockSpec((ln, ML_VW), lambda j: (fwd(j), 0)),
                   pl.BlockSpec((ln, ML_VW), lambda j: (bwd(j), 0))] + st_specs,
        out_shape=[jax.ShapeDtypeStruct((t, ML_VW), F32), jax.ShapeDtypeStruct((t, ML_VW), F32),
                   jax.ShapeDtypeStruct(c0.shape, F32), jax.ShapeDtypeStruct(m0.shape, F32)],
        compiler_params=_cparams("arbitrary"),
        name="ml_scan",
    )(qc, kt, p_ml, gates, gr, qc, kt, p_ml, gates, gr, c0, m0)


def _ml_finish_kernel(hf_ref, hb_ref, o_ref, g_ref, y_ref):
    h = hf_ref[...] + hb_ref[...]
    og = _sigmoid(o_ref[...])
    g = g_ref[...]
    for hd in range(ML_HEADS):
        sl = slice(hd * ML_V, (hd + 1) * ML_V)
        hh = h[:, sl]
        y = hh * lax.rsqrt(jnp.mean(hh * hh, axis=-1, keepdims=True) + EPS) * g[:, sl]
        y_ref[:, sl] = (y * og[:, sl]).astype(y_ref.dtype)


def ml_finish(hf, hb, p_ml, norm_g):
    t = hf.shape[0]
    tm = _row_tile(t, 256)
    ocol = (2 * ML_QKW + ML_VW) // ML_VW
    blk = pl.BlockSpec((tm, ML_VW), lambda i: (i, 0))
    return pl.pallas_call(
        _ml_finish_kernel,
        grid=(t // tm,),
        in_specs=[blk, blk, pl.BlockSpec((tm, ML_VW), lambda i: (i, ocol)),
                  pl.BlockSpec((1, ML_VW), lambda i: (0, 0))],
        out_specs=blk,
        out_shape=jax.ShapeDtypeStruct((t, ML_VW), CD),
        compiler_params=_cparams("parallel"),
        name="ml_finish",
    )(hf, hb, p_ml, norm_g.reshape(1, ML_VW).astype(F32))


def _merge_kernel(h_ref, ya_ref, yb_ref, yc_ref, yd_ref, g0_ref, g1_ref, g2_ref, g3_ref, wb_ref, o_ref):
    h = h_ref[...]
    acc = None
    for i, (y_ref, g_ref) in enumerate(zip((ya_ref, yb_ref, yc_ref, yd_ref), (g0_ref, g1_ref, g2_ref, g3_ref))):
        term = _sigmoid(_dot_nt(h, g_ref[...])) * _dot(y_ref[...], wb_ref[0, i])
        acc = term if acc is None else acc + term
    o_ref[...] = acc.astype(o_ref.dtype)


def gated_merge(h, ys, w_gate, w_branch, l, tn=512):
    t, d = h.shape
    tm = _row_tile(t, 1024)
    nj = d // tn
    yspec = pl.BlockSpec((tm, BRANCH_W), lambda j, i: (i, 0))
    gspec = lambda b: pl.BlockSpec((tn, d), lambda j, i: (b * nj + j, 0))
    return pl.pallas_call(
        _merge_kernel,
        grid=(nj, t // tm),
        in_specs=[pl.BlockSpec((tm, d), lambda j, i: (i, 0)), yspec, yspec, yspec, yspec,
                  gspec(0), gspec(1), gspec(2), gspec(3),
                  pl.BlockSpec((1, N_BRANCH, BRANCH_W, tn), lambda j, i: (l, 0, 0, j))],
        out_specs=pl.BlockSpec((tm, tn), lambda j, i: (i, j)),
        out_shape=jax.ShapeDtypeStruct((t, d), CD),
        compiler_params=_cparams("parallel", "parallel"),
        name="gated_merge",
    )(h, *ys, w_gate, w_gate, w_gate, w_gate, w_branch)


def _ffn1_kernel(h_ref, wa_ref, wg_ref, o_ref, wa_sc, wg_sc):
    @pl.when(pl.program_id(1) == 0)
    def _():
        wa_sc[...] = wa_ref[0].astype(CD)
        wg_sc[...] = wg_ref[0].astype(CD)

    h = h_ref[...]
    a = _dot(h, wa_sc[...])
    g = _dot(h, wg_sc[...])
    o_ref[...] = (a * _sigmoid(a) * g).astype(o_ref.dtype)


def ffn_up(h, w, l, tn=512):
    t, d = h.shape
    dff = w.shape[2] // 2
    tm = _row_tile(t, 1024)
    nj = dff // tn
    return pl.pallas_call(
        _ffn1_kernel,
        grid=(nj, t // tm),
        in_specs=[pl.BlockSpec((tm, d), lambda j, i: (i, 0)),
                  pl.BlockSpec((1, d, tn), lambda j, i: (l, 0, j)),
                  pl.BlockSpec((1, d, tn), lambda j, i: (l, 0, nj + j))],
        out_specs=pl.BlockSpec((tm, tn), lambda j, i: (i, j)),
        out_shape=jax.ShapeDtypeStruct((t, dff), CD),
        scratch_shapes=[pltpu.VMEM((d, tn), CD), pltpu.VMEM((d, tn), CD)],
        compiler_params=_cparams("parallel", "arbitrary"),
        name="ffn_up",
    )(h, w, w)


def _rot_cols(w, dim):
    k, n = w.shape
    w4 = w.reshape(k, n // dim, 2, dim // 2)
    return jnp.stack([-w4[:, :, 1], w4[:, :, 0]], axis=2).reshape(k, n)


def _rope_tables(n_tokens, dim):
    a = dim // 2
    inv = 1.0 / (ROPE_THETA ** (jnp.arange(0, a, 2, dtype=F32) / a))
    rows = n_tokens // GRID_W
    assert rows * GRID_W == n_tokens
    ang_r = jnp.arange(rows, dtype=F32)[:, None] * inv
    ang_c = jnp.arange(GRID_W, dtype=F32)[:, None] * inv

    def expand(fr, fc):
        shape = (rows, GRID_W, a // 2)
        return jnp.concatenate([jnp.broadcast_to(fr[:, None, :], shape), jnp.broadcast_to(fc[None], shape)],
                               axis=-1).reshape(n_tokens, a)

    return expand(jnp.cos(ang_r), jnp.cos(ang_c)), expand(jnp.sin(ang_r), jnp.sin(ang_c))


_O_NA = ML_COLS
_O_SW = _O_NA + NA_COLS
_O_MLA = _O_SW + SW_COLS
_O_KR = _O_MLA + MLA_Q_RANK + MLA_KV_RANK
_O_GATE = _O_MLA + MLA_COLS


def _rot_half_rows(x, dim):
    half = dim // 2
    parts = []
    for b in range(x.shape[0] // dim):
        parts += [-x[b * dim + half:(b + 1) * dim], x[b * dim:b * dim + half]]
    return jnp.concatenate(parts, axis=0)


def _repack_kernel(w_ref, ml_ref, na_ref, sw_ref, swr_ref, mla_ref, gate_ref):
    w = w_ref[0]
    cols = w.shape[1]
    zeros = lambda n: jnp.zeros((n, cols), F32)
    ml_ref[...] = jnp.concatenate([w[:ML_COLS], zeros(ML_PAD - ML_COLS)], axis=0).astype(CD)
    nq = NA_HEADS * NA_DIM
    na_ref[...] = jnp.concatenate([w[_O_NA:_O_NA + nq] * (NA_DIM ** -0.5 * LOG2E), w[_O_NA + nq:_O_SW]],
                                  axis=0).astype(CD)
    sq = SW_HEADS * SW_DIM
    skv = SW_KV_HEADS * SW_DIM
    sw_q = w[_O_SW:_O_SW + sq] * (SW_DIM ** -0.5 * LOG2E)
    sw_k = w[_O_SW + sq:_O_SW + sq + skv]
    sw_ref[...] = jnp.concatenate([sw_q, w[_O_SW + sq:_O_MLA]], axis=0).astype(CD)
    swr_ref[...] = _rot_half_rows(jnp.concatenate([sw_q, sw_k], axis=0), SW_DIM).astype(CD)
    kr = w[_O_KR:_O_GATE]
    pad = zeros(LANE - MLA_ROPE)
    mla_ref[...] = jnp.concatenate([w[_O_MLA:_O_KR], kr, pad, _rot_half_rows(kr, MLA_ROPE), pad], axis=0).astype(CD)
    gate_ref[...] = w[_O_GATE:].astype(CD)


def _repack_w_in(w_in_t, l):
    _, n, d = w_in_t.shape
    tc = 256
    heights = (ML_PAD, NA_COLS, SW_COLS, (SW_HEADS + SW_KV_HEADS) * SW_DIM,
               MLA_Q_RANK + MLA_KV_RANK + 2 * LANE, n - _O_GATE)
    return pl.pallas_call(
        _repack_kernel,
        grid=(d // tc,),
        in_specs=[pl.BlockSpec((1, n, tc), lambda i: (l, 0, i))],
        out_specs=[pl.BlockSpec((ht, tc), lambda i: (0, i)) for ht in heights],
        out_shape=[jax.ShapeDtypeStruct((ht, d), CD) for ht in heights],
        compiler_params=_cparams("parallel"),
        name="repack_w_in",
    )(w_in_t)


def _layer_weights(w_in_t, l, mla_w_uq_l, mla_w_ukv_l):
    w = {}
    w["ml"], w["na"], w["sw"], w["sw_rot"], w["mla1"], w["gate"] = _repack_w_in(w_in_t, l)
    uq = mla_w_uq_l.reshape(MLA_Q_RANK, MLA_HEADS, MLA_NOPE + MLA_ROPE)
    zq = jnp.zeros((MLA_Q_RANK, MLA_HEADS, MLA_HW - MLA_NOPE - MLA_ROPE), F32)
    w["mla_q"] = jnp.concatenate([uq, zq], axis=2).reshape(MLA_Q_RANK, -1).astype(CD)
    uqr = _rot_cols(uq[:, :, MLA_NOPE:].reshape(MLA_Q_RANK, -1), MLA_ROPE).reshape(MLA_Q_RANK, MLA_HEADS, MLA_ROPE)
    w["mla_q_rot"] = jnp.concatenate([jnp.zeros_like(uq[:, :, :MLA_NOPE]), uqr, zq], axis=2
                                     ).reshape(MLA_Q_RANK, -1).astype(CD)
    w["mla_kv"] = mla_w_ukv_l.astype(CD)
    return w


def _tables(s, c):
    cos_sw, sin_sw = _rope_tables(s, SW_DIM)
    cos_m, sin_m = _rope_tables(s, MLA_ROPE)
    tile2 = lambda a: jnp.concatenate([a, a, a, a], axis=1)
    scale = (MLA_NOPE + MLA_ROPE) ** -0.5 * LOG2E
    zpad = jnp.zeros((s, MLA_HW - MLA_NOPE - MLA_ROPE), F32)
    lat = dict(
        sw_cos=tile2(cos_sw), sw_sin=tile2(sin_sw),
        q_cos=scale * jnp.concatenate([jnp.ones((s, MLA_NOPE), F32), cos_m, cos_m, zpad], axis=1),
        q_sin=scale * jnp.concatenate([jnp.zeros((s, MLA_NOPE), F32), sin_m, sin_m, zpad], axis=1),
        k_cos=jnp.concatenate([cos_m, cos_m, zpad], axis=1),
        k_sin=jnp.concatenate([sin_m, sin_m, zpad], axis=1))
    ctx = dict(
        sw_cos=jnp.ones((c, LANE), F32), sw_sin=jnp.zeros((c, LANE), F32),
        q_cos=jnp.full((c, MLA_HW), scale, F32), q_sin=jnp.zeros((c, MLA_HW), F32),
        k_cos=jnp.ones((c, LANE), F32), k_sin=jnp.zeros((c, LANE), F32))
    return lat, ctx


def _project(h, w, tab, gq, gkv, total_rows, mla_into=None):
    p_ml = matmul(h, w["ml"], F32)
    p_na = matmul(h, w["na"], CD)
    n_rope = (SW_HEADS + SW_KV_HEADS) * SW_DIM // LANE
    p_sw = rope_matmul(h, w["sw"], w["sw_rot"], tab["sw_cos"], tab["sw_sin"], n_rope)
    mla = mla_proj(h, w["mla1"], gq, gkv, w["mla_q"], w["mla_q_rot"], w["mla_kv"],
                   tab["q_cos"], tab["q_sin"], tab["k_cos"], tab["k_sin"], total_rows, mla_into)
    return p_ml, p_na, p_sw, mla


def _dense_tail(x, h, ys, w, l, w_branch, w_out, w_ffn_in, w_ffn_out, vt, g1, norm2, g2, norm_next, last):
    merged = gated_merge(h, ys, w["gate"], w_branch, l)
    x, h2 = matmul_residual_norm(merged, w_out, l, x, vt, g1, norm2, True, CD)
    u = ffn_up(h2, w_ffn_in, l)
    if last:
        return None, matmul_residual_norm(u, w_ffn_out, l, x, vt, g2, norm_next, False, F32, tn=512)
    return matmul_residual_norm(u, w_ffn_out, l, x, vt, g2, norm_next, True, CD, tn=512)


def kernel(x, c, ctx, c_ctx, w_ada, b_ada, norm1_g, norm2_g, w_in, ml_conv_w, ml_gate_b, ml_norm_g, na_rpb,
           sw_sink, mla_q_norm_g, mla_kv_norm_g, mla_w_uq, mla_w_ukv, w_branch, w_out, w_ffn_in, w_ffn_out,
           final_norm_g):
    bsz, s, d = x.shape
    assert bsz == 1
    depth = w_in.shape[0]
    n_ctx = ctx.shape[1]
    rows = s // GRID_W
    xl = x[0].astype(F32)
    xc = ctx[0].astype(F32)
    mod = adaln(jnp.stack([c[0], c_ctx], axis=1).astype(F32), w_ada, b_ada)
    vt = jnp.concatenate([mod.reshape(depth * 12, d), norm1_g.astype(F32), norm2_g.astype(F32),
                          final_norm_g.reshape(1, d).astype(F32), jnp.zeros((1, d), F32)]).reshape(-1, 1, d)
    mod_row = lambda l, stream, part: (l * 2 + stream) * 6 + part
    n1_row = lambda l: depth * 12 + l
    n2_row = lambda l: depth * 13 + l
    fin_row, zero_row = depth * 14, depth * 14 + 1
    norm1 = lambda l, st: (n1_row(l), mod_row(l, st, 0), mod_row(l, st, 1))
    norm2 = lambda l, st: (n2_row(l), mod_row(l, st, 3), mod_row(l, st, 4))
    tab_l, tab_c = _tables(s, n_ctx)
    zero_state = (jnp.zeros((2 * ML_HEADS, ML_QK, 2 * ML_V), F32), jnp.zeros((2 * ML_HEADS, 8, LANE), F32))

    w_in_t = jnp.transpose(w_in, (0, 2, 1))
    wb = w_branch.astype(CD)
    wo = w_out.astype(CD)
    wf2 = w_ffn_out.astype(CD)

    hl = norm_mod(xl, vt, norm1(0, 0), CD)
    hc = norm_mod(xc, vt, norm1(0, 1), CD)
    for l in range(depth):
        last = l == depth - 1
        w = _layer_weights(w_in_t, l, mla_w_uq[l], mla_w_ukv[l])
        gq = mla_q_norm_g[l].reshape(1, -1).astype(F32)
        gkv = mla_kv_norm_g[l].reshape(1, -1).astype(F32)
        sink = sw_sink[l].reshape(1, -1).astype(F32)

        ml_l, na_l, sw_l, mla_lat = _project(hl, w, tab_l, gq, gkv, s + n_ctx)
        ml_c, na_c, sw_c, (mq, mk, mv) = _project(hc, w, tab_c, gq, gkv, s + n_ctx, mla_lat)

        qc_c, kc_c, gt_c = ml_prep(ml_c, ml_conv_w[l], ml_gate_b[l])
        qc_l, kc_l, gt_l = ml_prep(ml_l, ml_conv_w[l], ml_gate_b[l])
        hf_c, hb_c, cst, mst = ml_scan(qc_c, kc_c, ml_c, gt_c, zero_state)
        hf_l, hb_l, _, _ = ml_scan(qc_l, kc_l, ml_l, gt_l, (cst, mst))
        ya_l = ml_finish(hf_l, hb_l, ml_l, ml_norm_g[l])
        yb_l = na_attn(na_l, na_c, na_rpb[l], rows)
        yc_l = sw_attn(sw_l, sw_c, sink)
        yd_l = mla_flash(mq, mk, mv, s)

        nxt_l = (fin_row, zero_row, zero_row) if last else norm1(l + 1, 0)
        xl_new, hl_new = _dense_tail(xl, hl, (ya_l, yb_l, yc_l, yd_l), w, l, wb, wo, w_ffn_in, wf2, vt,
                                     mod_row(l, 0, 2), norm2(l, 0), mod_row(l, 0, 5), nxt_l, last)
        if not last:
            ya_c = ml_finish(hf_c, hb_c, ml_c, ml_norm_g[l])
            nw = NA_HEADS * NA_DIM
            yb_c = dense_attn([(na_c, nw, 0), (na_c, nw, 1), (na_c, nw, 2)], NA_HEADS, 1, NA_DIM, NA_DIM,
                              n_ctx, log2_scores=True)
            kcol = SW_HEADS * SW_DIM // LANE
            yc_c = dense_attn([(sw_c, SW_HEADS * SW_DIM, 0), (sw_c, LANE, kcol), (sw_c, LANE, kcol + 1)],
                              SW_HEADS, SW_HEADS // SW_KV_HEADS, SW_DIM, SW_DIM, n_ctx, sink=sink,
                              log2_scores=True)
            assert s % n_ctx == 0
            yd_c = dense_attn([(mq, MLA_HEADS * MLA_HW, 0), (mk, MLA_HEADS * MLA_HW, 0),
                               (mv, MLA_HEADS * MLA_V, 0)], MLA_HEADS, 1, MLA_HW, MLA_V, n_ctx,
                              log2_scores=True, row_block=s // n_ctx)
            xc, hc = _dense_tail(xc, hc, (ya_c, yb_c, yc_c, yd_c), w, l, wb, wo, w_ffn_in, wf2, vt,
                                 mod_row(l, 1, 2), norm2(l, 1), mod_row(l, 1, 5), norm1(l + 1, 1), False)
        xl, hl = xl_new, hl_new

    return hl[None].astype(x.dtype)
```

```python
import functools

import numpy as np
import jax
import jax.numpy as jnp
from jax import lax
from jax.experimental import pallas as pl
from jax.experimental.pallas import tpu as pltpu

F32 = jnp.float32
CD = jnp.bfloat16

GRID_W = 64
EPS = 1e-6
ROPE_THETA = 10000.0
ML_HEADS, ML_QK, ML_V, ML_CONV = 4, 64, 128, 3
NA_HEADS, NA_DIM, NA_WIN_R, NA_WIN_C = 8, 64, 8, 16
SW_HEADS, SW_KV_HEADS, SW_DIM, SW_WINDOW = 8, 2, 64, 128
MLA_HEADS, MLA_Q_RANK, MLA_KV_RANK, MLA_NOPE, MLA_ROPE, MLA_V = 4, 384, 256, 128, 64, 128
N_BRANCH, BRANCH_W = 4, 512

ML_QKW = ML_HEADS * ML_QK
ML_VW = ML_HEADS * ML_V
ML_COLS = 2 * ML_QKW + 2 * ML_VW + 4 * ML_HEADS
ML_PAD = 1664
NA_COLS = 3 * NA_HEADS * NA_DIM
SW_COLS = (SW_HEADS + 2 * SW_KV_HEADS) * SW_DIM
MLA_COLS = MLA_Q_RANK + MLA_KV_RANK + MLA_ROPE
MLA_HW = 256
LANE = 128
NEG = -1e30
LOG2E = float(np.log2(np.e))
VMEM_LIMIT = 56 * 1024 * 1024

NA_QROWS = 4
SW_TQ = 256
ML_CHUNK = 256
MLA_TK = 1280


def _cparams(*sem):
    return pltpu.CompilerParams(dimension_semantics=sem, vmem_limit_bytes=VMEM_LIMIT)


def _row_tile(t, pref=512):
    tm = min(pref, t)
    assert t % tm == 0
    return tm


def _dot(a, b):
    return jnp.dot(a, b, preferred_element_type=F32)


def _dot_nt(a, b):
    return lax.dot_general(a, b, (((1,), (1,)), ((), ())), preferred_element_type=F32)


def _sigmoid(x):
    return 1.0 / (1.0 + jnp.exp(-x))


def _lanes(col, n):
    tile = jnp.broadcast_to(col, (col.shape[0], LANE))
    return tile if n == LANE else jnp.tile(tile, (1, n // LANE))


def _ones_col(n, w):
    return jnp.where(lax.broadcasted_iota(jnp.int32, (n, w), 1) == 0, 1.0, 0.0).astype(CD)


def _adaln_kernel(c_ref, w_ref, b_ref, o_ref):
    w = w_ref[0]
    for r in range(2):
        c = c_ref[:, r:r + 1]
        o_ref[0, r:r + 1, :] = jnp.sum((c * _sigmoid(c)) * w, axis=0, keepdims=True) + b_ref[0]


def adaln(cc, w_ada, b_ada):
    nl, d, n = w_ada.shape
    tn = 1024
    return pl.pallas_call(
        _adaln_kernel,
        grid=(nl, n // tn),
        in_specs=[pl.BlockSpec((d, 2), lambda l, j: (0, 0)),
                  pl.BlockSpec((1, d, tn), lambda l, j: (l, 0, j)),
                  pl.BlockSpec((1, 1, tn), lambda l, j: (l, 0, j))],
        out_specs=pl.BlockSpec((1, 2, tn), lambda l, j: (l, 0, j)),
        out_shape=jax.ShapeDtypeStruct((nl, 2, n), F32),
        compiler_params=_cparams("parallel", "parallel"),
        name="adaln",
    )(cc, w_ada, b_ada.reshape(nl, 1, n))


def _vec_spec(d, idx):
    return pl.BlockSpec((1, 1, d), lambda *_: (idx, 0, 0))


def _norm_mod(x, g, shift, scale):
    y = x * lax.rsqrt(jnp.mean(x * x, axis=-1, keepdims=True) + EPS)
    return (y * g) * (1.0 + scale) + shift


def _norm_mod_kernel(x_ref, g_ref, sh_ref, sc_ref, o_ref):
    o_ref[...] = _norm_mod(x_ref[...], g_ref[0], sh_ref[0], sc_ref[0]).astype(o_ref.dtype)


def norm_mod(x, vt, norm_idx, out_dtype):
    t, d = x.shape
    tm = _row_tile(t, 1024)
    return pl.pallas_call(
        _norm_mod_kernel,
        grid=(t // tm,),
        in_specs=[pl.BlockSpec((tm, d), lambda i: (i, 0))] + [_vec_spec(d, ix) for ix in norm_idx],
        out_specs=pl.BlockSpec((tm, d), lambda i: (i, 0)),
        out_shape=jax.ShapeDtypeStruct((t, d), out_dtype),
        compiler_params=_cparams("parallel"),
        name="norm_mod",
    )(x, vt, vt, vt)


def _mm_kernel(a_ref, b_ref, o_ref):
    o_ref[...] = _dot_nt(a_ref[...], b_ref[...]).astype(o_ref.dtype)


def matmul(a, bt, out_dtype, tn=None):
    t, k = a.shape
    n = bt.shape[0]
    tn = n if tn is None else tn
    tm = _row_tile(t, 1024)
    return pl.pallas_call(
        _mm_kernel,
        grid=(n // tn, t // tm),
        in_specs=[pl.BlockSpec((tm, k), lambda j, i: (i, 0)),
                  pl.BlockSpec((tn, k), lambda j, i: (j, 0))],
        out_specs=pl.BlockSpec((tm, tn), lambda j, i: (i, j)),
        out_shape=jax.ShapeDtypeStruct((t, n), out_dtype),
        compiler_params=_cparams("parallel", "parallel"),
        name="matmul",
    )(a, bt)


def _mm_res_norm_kernel(nj, tn, want_x, a_ref, b_ref, x_ref, gate_ref, g_ref, sh_ref, sc_ref, *rest):
    xbuf, ho_ref = (rest[0], rest[1]) if want_x else (rest[1], rest[0])
    j = pl.program_id(1)
    acc = _dot(a_ref[...], b_ref[0])
    for jj in range(nj):
        @pl.when(j == jj)
        def _(jj=jj):
            sl = slice(jj * tn, (jj + 1) * tn)
            xbuf[:, sl] = x_ref[:, sl] + gate_ref[0][:, sl] * acc

    @pl.when(j == nj - 1)
    def _():
        ho_ref[...] = _norm_mod(xbuf[...], g_ref[0], sh_ref[0], sc_ref[0]).astype(ho_ref.dtype)


def matmul_residual_norm(a, b, l, x, vt, gate_idx, norm_idx, want_x, h_dtype, tn=None):
    t, kdim = a.shape
    n = b.shape[2]
    tn = n if tn is None else tn
    nj = n // tn
    tm = _row_tile(t)
    row = pl.BlockSpec((tm, n), lambda i, j: (i, 0))
    out_shape = [jax.ShapeDtypeStruct((t, n), h_dtype)]
    if want_x:
        out_shape = [jax.ShapeDtypeStruct((t, n), F32)] + out_shape
    out = pl.pallas_call(
        functools.partial(_mm_res_norm_kernel, nj, tn, want_x),
        grid=(t // tm, nj),
        in_specs=[pl.BlockSpec((tm, kdim), lambda i, j: (i, 0)),
                  pl.BlockSpec((1, kdim, tn), lambda i, j: (l, 0, j)),
                  row, _vec_spec(n, gate_idx)] + [_vec_spec(n, ix) for ix in norm_idx],
        out_specs=[row] * len(out_shape),
        out_shape=out_shape,
        scratch_shapes=[] if want_x else [pltpu.VMEM((tm, n), F32)],
        compiler_params=_cparams("parallel", "arbitrary"),
        name="matmul_residual_norm",
    )(a, b, x, vt, vt, vt, vt)
    return out if want_x else out[0]


def _rope_mm_kernel(n_rope, a_ref, w_ref, wr_ref, cos_ref, sin_ref, o_ref):
    a = a_ref[...]
    p = _dot_nt(a, w_ref[...])
    pr = _dot_nt(a, wr_ref[...])
    cos = cos_ref[...]
    sin = sin_ref[...]
    for j in range(p.shape[1] // LANE):
        sl = slice(j * LANE, (j + 1) * LANE)
        if j < n_rope:
            o_ref[:, sl] = (p[:, sl] * cos + pr[:, sl] * sin).astype(o_ref.dtype)
        else:
            o_ref[:, sl] = p[:, sl].astype(o_ref.dtype)


def rope_matmul(a, w, w_rot, cos, sin, n_rope):
    t, k = a.shape
    n = w.shape[0]
    nr = n_rope * LANE
    tm = _row_tile(t, 1024)
    return pl.pallas_call(
        functools.partial(_rope_mm_kernel, n_rope),
        grid=(t // tm,),
        in_specs=[pl.BlockSpec((tm, k), lambda i: (i, 0)),
                  pl.BlockSpec((n, k), lambda i: (0, 0)),
                  pl.BlockSpec((nr, k), lambda i: (0, 0)),
                  pl.BlockSpec((tm, LANE), lambda i: (i, 0)),
                  pl.BlockSpec((tm, LANE), lambda i: (i, 0))],
        out_specs=pl.BlockSpec((tm, n), lambda i: (i, 0)),
        out_shape=jax.ShapeDtypeStruct((t, n), CD),
        compiler_params=_cparams("parallel"),
        name="rope_matmul",
    )(a, w, w_rot, cos, sin)


def _mla_proj_kernel(h_ref, w1_ref, gq_ref, gkv_ref, wq_ref, wqr_ref, wkv_ref, cos_ref, sin_ref,
                     q_out, k_out, v_out):
    p = _dot_nt(h_ref[...], w1_ref[...])
    cq = p[:, :MLA_Q_RANK]
    ckv = p[:, MLA_Q_RANK:MLA_Q_RANK + MLA_KV_RANK]
    o = MLA_Q_RANK + MLA_KV_RANK
    kr = p[:, o:o + LANE]
    krr = p[:, o + LANE:o + 2 * LANE]
    cqn = (cq * lax.rsqrt(jnp.mean(cq * cq, axis=-1, keepdims=True) + EPS) * gq_ref[...]).astype(CD)
    ckvn = (ckv * lax.rsqrt(jnp.mean(ckv * ckv, axis=-1, keepdims=True) + EPS) * gkv_ref[...]).astype(CD)
    qa = _dot(cqn, wq_ref[...])
    qb = _dot(cqn, wqr_ref[...])
    kv = _dot(ckvn, wkv_ref[...])
    rope_lane = lax.broadcasted_iota(jnp.int32, cos_ref.shape, 1) < MLA_ROPE
    ck = jnp.where(rope_lane, cos_ref[...], 0.0)
    sk = jnp.where(rope_lane, sin_ref[...], 0.0)
    krp = (kr * ck + krr * sk).astype(CD)
    scale = (MLA_NOPE + MLA_ROPE) ** -0.5 * LOG2E
    cq_t = jnp.concatenate([jnp.full(ck.shape, scale, F32), ck * scale], axis=1)
    sq_t = jnp.concatenate([jnp.zeros(sk.shape, F32), sk * scale], axis=1)
    for h in range(MLA_HEADS):
        qs = slice(h * MLA_HW, (h + 1) * MLA_HW)
        q_out[:, qs] = (qa[:, qs] * cq_t + qb[:, qs] * sq_t).astype(CD)
        k_out[:, h * MLA_HW:h * MLA_HW + MLA_NOPE] = kv[:, h * 256:h * 256 + MLA_NOPE].astype(CD)
        k_out[:, h * MLA_HW + MLA_NOPE:(h + 1) * MLA_HW] = krp
        v_out[:, h * MLA_V:(h + 1) * MLA_V] = kv[:, h * 256 + MLA_NOPE:(h + 1) * 256].astype(CD)


def mla_proj(h, w1, gq, gkv, wq, wqr, wkv, cos, sin, total_rows, into=None):
    t, d = h.shape
    tm = _row_tile(t)
    full = lambda a: pl.BlockSpec(a.shape, lambda i: (0,) * a.ndim)
    rows = lambda w: pl.BlockSpec((tm, w), lambda i: (i, 0))
    hw = MLA_HEADS * MLA_HW
    vw = MLA_HEADS * MLA_V
    args = [h, w1, gq, gkv, wq, wqr, wkv, cos, sin]
    in_specs = [rows(d), full(w1), full(gq), full(gkv), full(wq), full(wqr), full(wkv),
                rows(LANE), rows(LANE)]
    if into is None:
        off, aliases, kern = 0, {}, _mla_proj_kernel
    else:
        assert (total_rows - t) % tm == 0
        off = (total_rows - t) // tm
        aliases = {len(args) + i: i for i in range(3)}
        in_specs = in_specs + [pl.BlockSpec(memory_space=pl.ANY)] * 3
        args = args + list(into)
        kern = lambda *refs: _mla_proj_kernel(*refs[:9], *refs[12:])
    orow = lambda w: pl.BlockSpec((tm, w), lambda i: (off + i, 0))
    return pl.pallas_call(
        kern,
        grid=(t // tm,),
        in_specs=in_specs,
        out_specs=[orow(hw), orow(hw), orow(vw)],
        out_shape=[jax.ShapeDtypeStruct((total_rows, hw), CD), jax.ShapeDtypeStruct((total_rows, hw), CD),
                   jax.ShapeDtypeStruct((total_rows, vw), CD)],
        input_output_aliases=aliases,
        compiler_params=_cparams("parallel"),
        name="mla_proj",
    )(*args)


def _flash_kernel(tq, tk, unroll, q_ref, k_ref, v_ref, o_ref, s_sc, m_sc, acc_sc):
    nkb = k_ref.shape[0] // tk
    nsub = q_ref.shape[0] // tq
    total = nsub * nkb
    m_sc[...] = jnp.full(m_sc.shape, NEG, F32)
    acc_sc[...] = jnp.zeros(acc_sc.shape, F32)
    ones_col = jnp.where(lax.broadcasted_iota(jnp.int32, (tk, LANE), 1) == 0, 1.0, 0.0).astype(CD)

    def split(b):
        sub = b // nkb
        return sub, b - sub * nkb

    def scores(b):
        sub, j = split(b)
        q = q_ref[pl.ds(pl.multiple_of(sub * tq, tq), tq), :]
        k = k_ref[pl.ds(pl.multiple_of(j * tk, tk), tk), :]
        return _dot_nt(q, k)

    def softmax_pv(s, b):
        sub, j = split(b)
        v = v_ref[pl.ds(pl.multiple_of(j * tk, tk), tk), :]
        m_prev = m_sc[sub]
        m_new = jnp.maximum(m_prev, jnp.max(s, axis=-1, keepdims=True))
        alpha = jnp.exp2(m_prev - m_new)
        p = jnp.exp2(s - m_new).astype(CD)
        v_aug = jnp.concatenate([v, ones_col], axis=1)
        acc_sc[sub] = alpha * acc_sc[sub] + _dot(p, v_aug)
        m_sc[sub] = m_new

    s_sc[0] = scores(0)

    def body(g, carry):
        b0 = g * unroll
        for u in range(unroll):
            s_sc[(u + 1) % 2] = scores(jnp.minimum(b0 + u + 1, total - 1))
            softmax_pv(s_sc[u % 2], b0 + u)
        return carry

    lax.fori_loop(0, total // unroll, body, 0)
    for sub in range(nsub):
        acc = acc_sc[sub]
        o_ref[sub * tq:(sub + 1) * tq, :] = (acc[:, :MLA_V] / acc[:, MLA_V:MLA_V + 1]).astype(o_ref.dtype)


def mla_flash(q, k, v, s):
    nk = k.shape[0]
    tq = _row_tile(s, 512)
    tstep = _row_tile(s, 8 * tq)
    tk = MLA_TK
    total = (tstep // tq) * (nk // tk)
    assert nk % tk == 0 and total % 2 == 0
    unroll = 8 if total % 8 == 0 else (4 if total % 4 == 0 else 2)
    return pl.pallas_call(
        functools.partial(_flash_kernel, tq, tk, unroll),
        grid=(MLA_HEADS, s // tstep),
        in_specs=[pl.BlockSpec((tstep, MLA_HW), lambda h, i: (i, h)),
                  pl.BlockSpec((nk, MLA_HW), lambda h, i: (0, h)),
                  pl.BlockSpec((nk, MLA_V), lambda h, i: (0, h))],
        out_specs=pl.BlockSpec((tstep, MLA_V), lambda h, i: (i, h)),
        out_shape=jax.ShapeDtypeStruct((s, MLA_HEADS * MLA_V), CD),
        scratch_shapes=[pltpu.VMEM((2, tq, tk), F32), pltpu.VMEM((tstep // tq, tq, 1), F32),
                        pltpu.VMEM((tstep // tq, tq, 2 * MLA_V), F32)],
        compiler_params=_cparams("parallel", "arbitrary"),
        name="mla_flash",
    )(q, k, v)


def _dense_attn_kernel(n_heads, group, dqk, dv, log2_scores, q_ref, k_ref, v_ref, sink_ref, o_ref):
    q = q_ref[...]
    k = k_ref[...]
    v = v_ref[...]
    ex = jnp.exp2 if log2_scores else jnp.exp
    for h in range(n_heads):
        g = h // group
        s = _dot_nt(q[:, h * dqk:(h + 1) * dqk], k[:, g * dqk:(g + 1) * dqk])
        m = jnp.max(s, axis=-1, keepdims=True)
        l = jnp.zeros_like(m)
        if sink_ref is not None:
            sk = sink_ref[0:1, h:h + 1] * (LOG2E if log2_scores else 1.0)
            m = jnp.maximum(m, sk)
            l = ex(sk - m)
        p = ex(s - m)
        l = l + jnp.sum(p, axis=-1, keepdims=True)
        o = _dot(p.astype(CD), v[:, g * dv:(g + 1) * dv]) / l
        o_ref[:, h * dv:(h + 1) * dv] = o.astype(o_ref.dtype)


def dense_attn(qkv_specs, n_heads, group, dqk, dv, c, sink=None, log2_scores=False, row_block=0):
    arrays = [a for a, _, _ in qkv_specs]
    specs = [pl.BlockSpec((c, w), functools.partial(lambda b, i: (row_block, b), b)) for _, w, b in qkv_specs]
    if sink is None:
        kern = lambda q, k, v, o: _dense_attn_kernel(n_heads, group, dqk, dv, log2_scores, q, k, v, None, o)
    else:
        kern = functools.partial(_dense_attn_kernel, n_heads, group, dqk, dv, log2_scores)
        arrays.append(sink)
        specs.append(pl.BlockSpec(sink.shape, lambda i: (0, 0)))
    return pl.pallas_call(
        kern,
        grid=(1,),
        in_specs=specs,
        out_specs=pl.BlockSpec((c, n_heads * dv), lambda i: (0, 0)),
        out_shape=jax.ShapeDtypeStruct((c, n_heads * dv), CD),
        compiler_params=_cparams("arbitrary"),
        name="dense_attn",
    )(*arrays)


def _na_kernel(q_ref, kp_ref, ko_ref, kn_ref, vp_ref, vo_ref, vn_ref, kc_ref, vc_ref, bias_ref, o_ref):
    q = q_ref[...]
    kcat = jnp.concatenate([kp_ref[...], ko_ref[...], kn_ref[...], kc_ref[...]], axis=0)
    vcat = jnp.concatenate([vp_ref[...], vo_ref[...], vn_ref[...], vc_ref[...]], axis=0)
    tb = q.shape[0]
    nloc = 3 * tb
    ones = _ones_col(kcat.shape[0], LANE)
    first_half = lax.broadcasted_iota(jnp.int32, (tb, LANE), 1) < NA_DIM
    zero = jnp.zeros((tb, LANE), CD)
    for j in range(NA_HEADS // 2):
        ps = slice(j * LANE, (j + 1) * LANE)
        q2 = q[:, ps]
        k2 = kcat[:, ps]
        v_aug = jnp.concatenate([vcat[:, ps], ones], axis=1)
        outs = []
        for e in range(2):
            qm = jnp.where(first_half, q2, zero) if e == 0 else jnp.where(first_half, zero, q2)
            s = _dot_nt(qm, k2)
            s = jnp.concatenate([s[:, :nloc] + bias_ref[0, 2 * j + e], s[:, nloc:]], axis=1)
            m = jnp.max(s, axis=-1, keepdims=True)
            p = jnp.exp2(s - _lanes(m, s.shape[1])).astype(CD)
            acc = _dot(p, v_aug)
            outs.append(acc[:, :LANE] / _lanes(acc[:, LANE:LANE + 1], LANE))
        o_ref[:, ps] = jnp.where(first_half, outs[0], outs[1]).astype(o_ref.dtype)


def _na_bias(rpb, rows):
    qr = NA_QROWS
    tb = qr * GRID_W
    nb = rows // qr
    nkr = 3 * qr
    rsel = np.zeros((3, qr, nkr, 2 * NA_WIN_R - 1), np.float32)
    for v, b in enumerate((0, 1, nb - 1)):
        for a in range(qr):
            r = qr * b + a
            rs = min(max(r - NA_WIN_R // 2, 0), rows - NA_WIN_R)
            for u in range(nkr):
                kr = qr * (b - 1) + u
                if 0 <= b - 1 + u // qr < nb and rs <= kr < rs + NA_WIN_R:
                    rsel[v, a, u, kr - r + NA_WIN_R - 1] = 1.0
    csel = np.zeros((GRID_W, GRID_W, 2 * NA_WIN_C - 1), np.float32)
    for qc in range(GRID_W):
        cs = min(max(qc - NA_WIN_C // 2, 0), GRID_W - NA_WIN_C)
        for kc in range(cs, cs + NA_WIN_C):
            csel[qc, kc, kc - qc + NA_WIN_C - 1] = 1.0
    valid = (np.einsum('vaud,qkj->vaquk', rsel, csel) > 0).reshape(3, tb, 3 * tb)
    assert 2 * qr == NA_WIN_R
    nd = 2 * NA_WIN_R - 1
    rp = rpb.astype(F32) * LOG2E
    gap = jnp.zeros((NA_HEADS, nd, GRID_W + 1 - (2 * NA_WIN_C - 1)), F32)
    ring = jnp.concatenate([rp[..., NA_WIN_C - 1:], gap, rp[..., :NA_WIN_C - 1]], axis=-1)
    cols = jnp.tile(ring, (1, 1, GRID_W))[..., :GRID_W * GRID_W].reshape(NA_HEADS, nd, GRID_W, GRID_W)
    off = NA_WIN_R - 1 - qr
    slabs = [cols[:, off - a:off - a + nkr].transpose(0, 2, 1, 3).reshape(NA_HEADS, GRID_W, 3 * tb)
             for a in range(qr)]
    band = jnp.stack(slabs, axis=1).reshape(NA_HEADS, tb, 3 * tb)
    return jnp.where(valid[:, None], band[None], NEG)


def na_attn(p_lat, p_ctx, rpb, rows):
    s = p_lat.shape[0]
    c = p_ctx.shape[0]
    w = NA_HEADS * NA_DIM
    tb = NA_QROWS * GRID_W
    nb = s // tb
    assert rows % NA_QROWS == 0 and nb >= 3 and rows >= 2 * NA_WIN_R
    bias = _na_bias(rpb, rows)

    def blk(col, off):
        return pl.BlockSpec((tb, w), lambda i: (jnp.clip(i + off, 0, nb - 1), col))

    return pl.pallas_call(
        _na_kernel,
        grid=(nb,),
        in_specs=[blk(0, 0), blk(1, -1), blk(1, 0), blk(1, 1), blk(2, -1), blk(2, 0), blk(2, 1),
                  pl.BlockSpec((c, w), lambda i: (0, 1)),
                  pl.BlockSpec((c, w), lambda i: (0, 2)),
                  pl.BlockSpec((1, NA_HEADS, tb, 3 * tb),
                               lambda i: (jnp.where(i == 0, 0, jnp.where(i == nb - 1, 2, 1)), 0, 0, 0))],
        out_specs=pl.BlockSpec((tb, w), lambda i: (i, 0)),
        out_shape=jax.ShapeDtypeStruct((s, w), CD),
        compiler_params=_cparams("parallel"),
        name="na_attn",
    )(p_lat, p_lat, p_lat, p_lat, p_lat, p_lat, p_lat, p_ctx, p_ctx, bias)


def _sw_kernel(seq, q_ref, kp_ref, ko_ref, kn_ref, vp_ref, vo_ref, vn_ref, kc_ref, vc_ref, sink_ref, o_ref):
    i = pl.program_id(0)
    q = q_ref[...]
    tq = q.shape[0]
    kcat = jnp.concatenate([kp_ref[...], ko_ref[...], kn_ref[...], kc_ref[...]], axis=0)
    vcat = jnp.concatenate([vp_ref[...], vo_ref[...], vn_ref[...], vc_ref[...]], axis=0)
    nkk = tq + 2 * SW_WINDOW
    r = lax.broadcasted_iota(jnp.int32, (tq, nkk), 0)
    cc = lax.broadcasted_iota(jnp.int32, (tq, nkk), 1)
    kpos = i * tq - SW_WINDOW + cc
    d = cc - r
    mask = (d >= 0) & (d <= 2 * SW_WINDOW) & (kpos >= 0) & (kpos < seq)
    assert SW_KV_HEADS == 2 and SW_DIM * 2 == LANE
    ones = _ones_col(kcat.shape[0], LANE)
    swap = lambda x: jnp.concatenate([x[:, SW_DIM:], x[:, :SW_DIM]], axis=1)
    k_by_order = (kcat, swap(kcat))
    v_by_order = (jnp.concatenate([vcat, ones], axis=1), jnp.concatenate([swap(vcat), ones], axis=1))
    first_half = lax.broadcasted_iota(jnp.int32, (tq, LANE), 1) < SW_DIM
    zero = jnp.zeros((tq, LANE), CD)
    grp = SW_HEADS // SW_KV_HEADS
    for j in range(SW_HEADS // 2):
        ps = slice(j * LANE, (j + 1) * LANE)
        q2 = q[:, ps]
        outs = []
        for e in range(2):
            h = 2 * j + e
            order = 0 if h // grp == e else 1
            qm = jnp.where(first_half, q2, zero) if e == 0 else jnp.where(first_half, zero, q2)
            s = _dot_nt(qm, k_by_order[order])
            s = jnp.concatenate([jnp.where(mask, s[:, :nkk], NEG), s[:, nkk:]], axis=1)
            sk = sink_ref[0:1, h:h + 1] * LOG2E
            m = jnp.maximum(jnp.max(s, axis=-1, keepdims=True), sk)
            p = jnp.exp2(s - _lanes(m, s.shape[1])).astype(CD)
            acc = _dot(p, v_by_order[order])
            outs.append(acc[:, :LANE] / _lanes(acc[:, LANE:LANE + 1] + jnp.exp2(sk - m), LANE))
        o_ref[:, ps] = jnp.where(first_half, outs[0], outs[1]).astype(o_ref.dtype)


def sw_attn(p_lat, p_ctx, sink):
    s = p_lat.shape[0]
    c = p_ctx.shape[0]
    tq = SW_TQ
    assert s % tq == 0 and tq % SW_WINDOW == 0
    nq = s // tq
    per = tq // SW_WINDOW
    nwb = s // SW_WINDOW
    kcol = SW_HEADS * SW_DIM // LANE
    prev = lambda col: pl.BlockSpec((SW_WINDOW, LANE), lambda i: (jnp.maximum(i * per - 1, 0), col))
    own = lambda col: pl.BlockSpec((tq, LANE), lambda i: (i, col))
    nxt = lambda col: pl.BlockSpec((SW_WINDOW, LANE), lambda i: (jnp.minimum((i + 1) * per, nwb - 1), col))
    return pl.pallas_call(
        functools.partial(_sw_kernel, s),
        grid=(nq,),
        in_specs=[pl.BlockSpec((tq, SW_HEADS * SW_DIM), lambda i: (i, 0)),
                  prev(kcol), own(kcol), nxt(kcol), prev(kcol + 1), own(kcol + 1), nxt(kcol + 1),
                  pl.BlockSpec((c, LANE), lambda i: (0, kcol)),
                  pl.BlockSpec((c, LANE), lambda i: (0, kcol + 1)),
                  pl.BlockSpec(sink.shape, lambda i: (0, 0))],
        out_specs=pl.BlockSpec((tq, SW_HEADS * SW_DIM), lambda i: (i, 0)),
        out_shape=jax.ShapeDtypeStruct((s, SW_HEADS * SW_DIM), CD),
        compiler_params=_cparams("parallel"),
        name="sw_attn",
    )(p_lat, p_lat, p_lat, p_lat, p_lat, p_lat, p_lat, p_ctx, p_ctx, sink)


def _ml_prep_kernel(nt, x_ref, xp_ref, xn_ref, g_ref, w_ref, b_ref, q_out, k_out, g_out):
    i = pl.program_id(0)
    x = x_ref[...]
    tm = x.shape[0]
    row = lax.broadcasted_iota(jnp.int32, x.shape, 0)
    has_prev = jnp.where(i > 0, 1.0, 0.0)
    has_next = jnp.where(i < nt - 1, 1.0, 0.0)
    prev_row = xp_ref[7:8, :] * has_prev
    next_row = xn_ref[0:1, :] * has_next
    xm = jnp.where(row == 0, prev_row, pltpu.roll(x, 1, axis=0))
    xq = jnp.where(row == tm - 1, next_row, pltpu.roll(x, tm - 1, axis=0))
    y = w_ref[0:1, :] * xm + w_ref[1:2, :] * x + w_ref[2:3, :] * xq
    y = y * _sigmoid(y)
    q_out[...] = y[:, :ML_QKW].astype(CD)
    k_out[...] = (y[:, ML_QKW:] * (ML_QK ** -0.5)).astype(CD)
    g = g_ref[...] + b_ref[...]
    lane = lax.broadcasted_iota(jnp.int32, g.shape, 1)
    logsig = jnp.minimum(g, 0.0) - jnp.log(1.0 + jnp.exp(-jnp.abs(g)))
    g_out[...] = jnp.where(lane < 2 * ML_HEADS, g, logsig)


def ml_prep(p_ml, conv_w, gate_b):
    t = p_ml.shape[0]
    tm = _row_tile(t, 1024)
    nt = t // tm
    wq = 2 * ML_QKW
    gcol = (2 * ML_QKW + 2 * ML_VW) // LANE
    w8 = jnp.zeros((8, wq), F32).at[:ML_CONV].set(conv_w.astype(F32))
    b = jnp.zeros((1, LANE), F32).at[0, :4 * ML_HEADS].set(gate_b.astype(F32).reshape(-1))
    h8 = tm // 8
    return pl.pallas_call(
        functools.partial(_ml_prep_kernel, nt),
        grid=(nt,),
        in_specs=[pl.BlockSpec((tm, wq), lambda i: (i, 0)),
                  pl.BlockSpec((8, wq), lambda i: (jnp.maximum(i * h8 - 1, 0), 0)),
                  pl.BlockSpec((8, wq), lambda i: (jnp.minimum((i + 1) * h8, t // 8 - 1), 0)),
                  pl.BlockSpec((tm, LANE), lambda i: (i, gcol)),
                  pl.BlockSpec((8, wq), lambda i: (0, 0)),
                  pl.BlockSpec((1, LANE), lambda i: (0, 0))],
        out_specs=[pl.BlockSpec((tm, ML_QKW), lambda i: (i, 0)),
                   pl.BlockSpec((tm, ML_QKW), lambda i: (i, 0)),
                   pl.BlockSpec((tm, LANE), lambda i: (i, 0))],
        out_shape=[jax.ShapeDtypeStruct((t, ML_QKW), CD), jax.ShapeDtypeStruct((t, ML_QKW), CD),
                   jax.ShapeDtypeStruct((t, LANE), F32)],
        compiler_params=_cparams("parallel"),
        name="ml_prep",
    )(p_ml, p_ml, p_ml, p_ml, w8, b)


def _ml_scan_kernel(qf_ref, ktf_ref, vf_ref, gcf_ref, grf_ref,
                    qb_ref, ktb_ref, vb_ref, gcb_ref, grb_ref,
                    c0_ref, m0_ref, hf_ref, hb_ref, c_ref, m_ref):
    @pl.when(pl.program_id(0) == 0)
    def _():
        c_ref[...] = c0_ref[...]
        m_ref[...] = m0_ref[...]

    ln = qf_ref.shape[0]
    r = lax.broadcasted_iota(jnp.int32, (ln, ln), 0)
    c = lax.broadcasted_iota(jnp.int32, (ln, ln), 1)
    lower = r >= c
    upper = c >= r
    lower_f = jnp.where(lower, 1.0, 0.0)
    upper_f = jnp.where(upper, 1.0, 0.0)
    ones_col = jnp.where(lax.broadcasted_iota(jnp.int32, (ln, LANE), 1) == 0, 1.0, 0.0).astype(CD)
    hi = lax.Precision.HIGHEST
    first_half = lax.broadcasted_iota(jnp.int32, (ln, LANE), 1) < ML_QK
    zero_q = jnp.zeros((ln, LANE), CD)
    new_states = []

    streams = ((qf_ref, ktf_ref, vf_ref, gcf_ref, grf_ref, hf_ref, lower, lower_f, upper_f, ln - 1),
               (qb_ref, ktb_ref, vb_ref, gcb_ref, grb_ref, hb_ref, upper, upper_f, lower_f, 0))
    for d, (q_ref, kt_ref, v_ref, gc_ref, gr_ref, h_ref, mask, tri_c, tri_r, last) in enumerate(streams):
        gcol = gc_ref[...]
        grow = gr_ref[...]
        bcol = jnp.dot(tri_c, gcol, precision=hi, preferred_element_type=F32)
        brow = jnp.dot(grow, tri_r, precision=hi, preferred_element_type=F32)
        q = q_ref[...]
        kt = kt_ref[...]
        v = v_ref[...]
        for hd in range(ML_HEADS):
            ch = d * ML_HEADS + hd
            ic = d * ML_HEADS + hd
            fc = 2 * ML_HEADS + ic
            pair, half = hd // 2, hd % 2
            b_c = bcol[:, fc:fc + 1]
            r_row = grow[ic:ic + 1, :] - brow[fc:fc + 1, :]
            m_prev = m_ref[ch, 0:1, 0:1]
            cst = c_ref[ch]
            rmat = jnp.where(mask, r_row, NEG)
            mx = jnp.maximum(jnp.max(rmat, axis=-1, keepdims=True), m_prev)
            q2 = q[:, pair * LANE:(pair + 1) * LANE]
            qm = jnp.where(first_half, q2, zero_q) if half == 0 else jnp.where(first_half, zero_q, q2)
            s = _dot(qm, kt[pair * LANE:(pair + 1) * LANE, :]) * jnp.exp(rmat - mx)
            w_inter = jnp.exp(m_prev - mx)
            v_aug = jnp.concatenate([v[:, hd * ML_V:(hd + 1) * ML_V].astype(CD), ones_col], axis=1)
            c_pair = jnp.concatenate([c_ref[ch - half], c_ref[ch - half + 1]], axis=0).astype(CD)
            num = _dot(s.astype(CD), v_aug) + w_inter * _dot(qm, c_pair)
            den = num[:, ML_V:ML_V + 1]
            h = num[:, :ML_V] / jnp.maximum(jnp.abs(den), jnp.exp(-(b_c + mx)))
            h_ref[:, hd * ML_V:(hd + 1) * ML_V] = h
            b_end = b_c[last:last + 1, :]
            m_new = b_end + jnp.maximum(m_prev, jnp.max(r_row, axis=-1, keepdims=True))
            w_s = jnp.exp(b_end + r_row - m_new)
            w_c = jnp.exp(b_end + m_prev - m_new)
            kth = kt[hd * ML_QK:(hd + 1) * ML_QK, :]
            ktw = (kth.astype(F32) * w_s).astype(CD)
            new_states.append((ch, w_c * cst + _dot(ktw, v_aug), m_new))
    for ch, c_new, m_new in new_states:
        c_ref[ch] = c_new
        m_ref[ch] = jnp.broadcast_to(m_new, m_ref.shape[1:])


def ml_scan(qc, kc, p_ml, gates, state):
    t = qc.shape[0]
    ln = min(ML_CHUNK, t)
    assert t % ln == 0
    nc = t // ln
    kt = kc.T
    gr = gates[:, :4 * ML_HEADS].T
    c0, m0 = state
    vcol = 2 * ML_QKW // ML_VW
    fwd = lambda j: j
    bwd = lambda j: nc - 1 - j

    def specs(ix):
        return [pl.BlockSpec((ln, ML_QKW), lambda j: (ix(j), 0)),
                pl.BlockSpec((ML_QKW, ln), lambda j: (0, ix(j))),
                pl.BlockSpec((ln, ML_VW), lambda j: (ix(j), vcol)),
                pl.BlockSpec((ln, LANE), lambda j: (ix(j), 0)),
                pl.BlockSpec((4 * ML_HEADS, ln), lambda j: (0, ix(j)))]

    st_specs = [pl.BlockSpec(c0.shape, lambda j: (0, 0, 0)), pl.BlockSpec(m0.shape, lambda j: (0, 0, 0))]
    return pl.pallas_call(
        _ml_scan_kernel,
        grid=(nc,),
        in_specs=specs(fwd) + specs(bwd) + st_specs,
        out_specs=[pl.BlockSpec((ln, ML_VW), lambda j: (fwd(j), 0)),
                   pl.BlockSpec((ln, ML_VW), lambda j: (bwd(j), 0))] + st_specs,
        out_shape=[jax.ShapeDtypeStruct((t, ML_VW), F32), jax.ShapeDtypeStruct((t, ML_VW), F32),
                   jax.ShapeDtypeStruct(c0.shape, F32), jax.ShapeDtypeStruct(m0.shape, F32)],
        compiler_params=_cparams("arbitrary"),
        name="ml_scan",
    )(qc, kt, p_ml, gates, gr, qc, kt, p_ml, gates, gr, c0, m0)


def _ml_finish_kernel(hf_ref, hb_ref, o_ref, g_ref, y_ref):
    h = hf_ref[...] + hb_ref[...]
    og = _sigmoid(o_ref[...])
    g = g_ref[...]
    for hd in range(ML_HEADS):
        sl = slice(hd * ML_V, (hd + 1) * ML_V)
        hh = h[:, sl]
        y = hh * lax.rsqrt(jnp.mean(hh * hh, axis=-1, keepdims=True) + EPS) * g[:, sl]
        y_ref[:, sl] = (y * og[:, sl]).astype(y_ref.dtype)


def ml_finish(hf, hb, p_ml, norm_g):
    t = hf.shape[0]
    tm = _row_tile(t, 1024)
    ocol = (2 * ML_QKW + ML_VW) // ML_VW
    blk = pl.BlockSpec((tm, ML_VW), lambda i: (i, 0))
    return pl.pallas_call(
        _ml_finish_kernel,
        grid=(t // tm,),
        in_specs=[blk, blk, pl.BlockSpec((tm, ML_VW), lambda i: (i, ocol)),
                  pl.BlockSpec((1, ML_VW), lambda i: (0, 0))],
        out_specs=blk,
        out_shape=jax.ShapeDtypeStruct((t, ML_VW), CD),
        compiler_params=_cparams("parallel"),
        name="ml_finish",
    )(hf, hb, p_ml, norm_g.reshape(1, ML_VW).astype(F32))


def _merge_kernel(h_ref, ya_ref, yb_ref, yc_ref, yd_ref, g0_ref, g1_ref, g2_ref, g3_ref, wb_ref, o_ref):
    h = h_ref[...]
    acc = None
    for i, (y_ref, g_ref) in enumerate(zip((ya_ref, yb_ref, yc_ref, yd_ref), (g0_ref, g1_ref, g2_ref, g3_ref))):
        term = _sigmoid(_dot_nt(h, g_ref[...])) * _dot(y_ref[...], wb_ref[0, i])
        acc = term if acc is None else acc + term
    o_ref[...] = acc.astype(o_ref.dtype)


def gated_merge(h, ys, w_gate, w_branch, l, tn=512):
    t, d = h.shape
    tm = _row_tile(t, 1024)
    nj = d // tn
    yspec = pl.BlockSpec((tm, BRANCH_W), lambda j, i: (i, 0))
    gspec = lambda b: pl.BlockSpec((tn, d), lambda j, i: (b * nj + j, 0))
    return pl.pallas_call(
        _merge_kernel,
        grid=(nj, t // tm),
        in_specs=[pl.BlockSpec((tm, d), lambda j, i: (i, 0)), yspec, yspec, yspec, yspec,
                  gspec(0), gspec(1), gspec(2), gspec(3),
                  pl.BlockSpec((1, N_BRANCH, BRANCH_W, tn), lambda j, i: (l, 0, 0, j))],
        out_specs=pl.BlockSpec((tm, tn), lambda j, i: (i, j)),
        out_shape=jax.ShapeDtypeStruct((t, d), CD),
        compiler_params=_cparams("parallel", "parallel"),
        name="gated_merge",
    )(h, *ys, w_gate, w_gate, w_gate, w_gate, w_branch)


def _ffn1_kernel(h_ref, wa_ref, wg_ref, o_ref, wa_sc, wg_sc):
    @pl.when(pl.program_id(1) == 0)
    def _():
        wa_sc[...] = wa_ref[0].astype(CD)
        wg_sc[...] = wg_ref[0].astype(CD)

    h = h_ref[...]
    a = _dot(h, wa_sc[...])
    g = _dot(h, wg_sc[...])
    o_ref[...] = (a * _sigmoid(a) * g).astype(o_ref.dtype)


def ffn_up(h, w, l, tn=512):
    t, d = h.shape
    dff = w.shape[2] // 2
    tm = _row_tile(t, 1024)
    nj = dff // tn
    return pl.pallas_call(
        _ffn1_kernel,
        grid=(nj, t // tm),
        in_specs=[pl.BlockSpec((tm, d), lambda j, i: (i, 0)),
                  pl.BlockSpec((1, d, tn), lambda j, i: (l, 0, j)),
                  pl.BlockSpec((1, d, tn), lambda j, i: (l, 0, nj + j))],
        out_specs=pl.BlockSpec((tm, tn), lambda j, i: (i, j)),
        out_shape=jax.ShapeDtypeStruct((t, dff), CD),
        scratch_shapes=[pltpu.VMEM((d, tn), CD), pltpu.VMEM((d, tn), CD)],
        compiler_params=_cparams("parallel", "arbitrary"),
        name="ffn_up",
    )(h, w, w)


def _rot_cols(w, dim):
    k, n = w.shape
    w4 = w.reshape(k, n // dim, 2, dim // 2)
    return jnp.stack([-w4[:, :, 1], w4[:, :, 0]], axis=2).reshape(k, n)


def _rope_tables(n_tokens, dim):
    a = dim // 2
    inv = 1.0 / (ROPE_THETA ** (jnp.arange(0, a, 2, dtype=F32) / a))
    rows = n_tokens // GRID_W
    assert rows * GRID_W == n_tokens
    ang_r = jnp.arange(rows, dtype=F32)[:, None] * inv
    ang_c = jnp.arange(GRID_W, dtype=F32)[:, None] * inv

    def expand(fr, fc):
        shape = (rows, GRID_W, a // 2)
        return jnp.concatenate([jnp.broadcast_to(fr[:, None, :], shape), jnp.broadcast_to(fc[None], shape)],
                               axis=-1).reshape(n_tokens, a)

    return expand(jnp.cos(ang_r), jnp.cos(ang_c)), expand(jnp.sin(ang_r), jnp.sin(ang_c))


_O_NA = ML_COLS
_O_SW = _O_NA + NA_COLS
_O_MLA = _O_SW + SW_COLS
_O_KR = _O_MLA + MLA_Q_RANK + MLA_KV_RANK
_O_GATE = _O_MLA + MLA_COLS


def _rot_half_rows(x, dim):
    half = dim // 2
    parts = []
    for b in range(x.shape[0] // dim):
        parts += [-x[b * dim + half:(b + 1) * dim], x[b * dim:b * dim + half]]
    return jnp.concatenate(parts, axis=0)


def _repack_kernel(w_ref, ml_ref, na_ref, sw_ref, swr_ref, mla_ref, gate_ref):
    w = w_ref[0]
    cols = w.shape[1]
    zeros = lambda n: jnp.zeros((n, cols), F32)
    ml_ref[...] = jnp.concatenate([w[:ML_COLS], zeros(ML_PAD - ML_COLS)], axis=0).astype(CD)
    nq = NA_HEADS * NA_DIM
    na_ref[...] = jnp.concatenate([w[_O_NA:_O_NA + nq] * (NA_DIM ** -0.5 * LOG2E), w[_O_NA + nq:_O_SW]],
                                  axis=0).astype(CD)
    sq = SW_HEADS * SW_DIM
    skv = SW_KV_HEADS * SW_DIM
    sw_q = w[_O_SW:_O_SW + sq] * (SW_DIM ** -0.5 * LOG2E)
    sw_k = w[_O_SW + sq:_O_SW + sq + skv]
    sw_ref[...] = jnp.concatenate([sw_q, w[_O_SW + sq:_O_MLA]], axis=0).astype(CD)
    swr_ref[...] = _rot_half_rows(jnp.concatenate([sw_q, sw_k], axis=0), SW_DIM).astype(CD)
    kr = w[_O_KR:_O_GATE]
    pad = zeros(LANE - MLA_ROPE)
    mla_ref[...] = jnp.concatenate([w[_O_MLA:_O_KR], kr, pad, _rot_half_rows(kr, MLA_ROPE), pad], axis=0).astype(CD)
    gate_ref[...] = w[_O_GATE:].astype(CD)


def _repack_w_in(w_in_t, l):
    _, n, d = w_in_t.shape
    tc = 256
    heights = (ML_PAD, NA_COLS, SW_COLS, (SW_HEADS + SW_KV_HEADS) * SW_DIM,
               MLA_Q_RANK + MLA_KV_RANK + 2 * LANE, n - _O_GATE)
    return pl.pallas_call(
        _repack_kernel,
        grid=(d // tc,),
        in_specs=[pl.BlockSpec((1, n, tc), lambda i: (l, 0, i))],
        out_specs=[pl.BlockSpec((ht, tc), lambda i: (0, i)) for ht in heights],
        out_shape=[jax.ShapeDtypeStruct((ht, d), CD) for ht in heights],
        compiler_params=_cparams("parallel"),
        name="repack_w_in",
    )(w_in_t)


def _layer_weights(w_in_t, l, mla_w_uq_l, mla_w_ukv_l):
    w = {}
    w["ml"], w["na"], w["sw"], w["sw_rot"], w["mla1"], w["gate"] = _repack_w_in(w_in_t, l)
    uq = mla_w_uq_l.reshape(MLA_Q_RANK, MLA_HEADS, MLA_NOPE + MLA_ROPE)
    zq = jnp.zeros((MLA_Q_RANK, MLA_HEADS, MLA_HW - MLA_NOPE - MLA_ROPE), F32)
    w["mla_q"] = jnp.concatenate([uq, zq], axis=2).reshape(MLA_Q_RANK, -1).astype(CD)
    uqr = _rot_cols(uq[:, :, MLA_NOPE:].reshape(MLA_Q_RANK, -1), MLA_ROPE).reshape(MLA_Q_RANK, MLA_HEADS, MLA_ROPE)
    w["mla_q_rot"] = jnp.concatenate([jnp.zeros_like(uq[:, :, :MLA_NOPE]), uqr, zq], axis=2
                                     ).reshape(MLA_Q_RANK, -1).astype(CD)
    w["mla_kv"] = mla_w_ukv_l.astype(CD)
    return w


def _tables(s, c):
    assert SW_DIM == MLA_ROPE
    cos, sin = _rope_tables(s, SW_DIM)
    tile4 = lambda a: jnp.concatenate([a, a, a, a], axis=1)
    return ((tile4(cos), tile4(sin)), (jnp.ones((c, LANE), F32), jnp.zeros((c, LANE), F32)))


def _project(h, w, tab, gq, gkv, total_rows, mla_into=None):
    p_ml = matmul(h, w["ml"], F32)
    p_na = matmul(h, w["na"], CD)
    n_rope = (SW_HEADS + SW_KV_HEADS) * SW_DIM // LANE
    p_sw = rope_matmul(h, w["sw"], w["sw_rot"], tab[0], tab[1], n_rope)
    mla = mla_proj(h, w["mla1"], gq, gkv, w["mla_q"], w["mla_q_rot"], w["mla_kv"],
                   tab[0], tab[1], total_rows, mla_into)
    return p_ml, p_na, p_sw, mla


def _dense_tail(x, h, ys, w, l, w_branch, w_out, w_ffn_in, w_ffn_out, vt, g1, norm2, g2, norm_next, last):
    merged = gated_merge(h, ys, w["gate"], w_branch, l)
    x, h2 = matmul_residual_norm(merged, w_out, l, x, vt, g1, norm2, True, CD)
    u = ffn_up(h2, w_ffn_in, l)
    if last:
        return None, matmul_residual_norm(u, w_ffn_out, l, x, vt, g2, norm_next, False, F32, tn=512)
    return matmul_residual_norm(u, w_ffn_out, l, x, vt, g2, norm_next, True, CD, tn=512)


def kernel(x, c, ctx, c_ctx, w_ada, b_ada, norm1_g, norm2_g, w_in, ml_conv_w, ml_gate_b, ml_norm_g, na_rpb,
           sw_sink, mla_q_norm_g, mla_kv_norm_g, mla_w_uq, mla_w_ukv, w_branch, w_out, w_ffn_in, w_ffn_out,
           final_norm_g):
    bsz, s, d = x.shape
    assert bsz == 1
    depth = w_in.shape[0]
    n_ctx = ctx.shape[1]
    rows = s // GRID_W
    xl = x[0].astype(F32)
    xc = ctx[0].astype(F32)
    mod = adaln(jnp.stack([c[0], c_ctx], axis=1).astype(F32), w_ada, b_ada)
    vt = jnp.concatenate([mod.reshape(depth * 12, d), norm1_g.astype(F32), norm2_g.astype(F32),
                          final_norm_g.reshape(1, d).astype(F32), jnp.zeros((1, d), F32)]).reshape(-1, 1, d)
    mod_row = lambda l, stream, part: (l * 2 + stream) * 6 + part
    n1_row = lambda l: depth * 12 + l
    n2_row = lambda l: depth * 13 + l
    fin_row, zero_row = depth * 14, depth * 14 + 1
    norm1 = lambda l, st: (n1_row(l), mod_row(l, st, 0), mod_row(l, st, 1))
    norm2 = lambda l, st: (n2_row(l), mod_row(l, st, 3), mod_row(l, st, 4))
    tab_l, tab_c = _tables(s, n_ctx)
    zero_state = (jnp.zeros((2 * ML_HEADS, ML_QK, 2 * ML_V), F32), jnp.zeros((2 * ML_HEADS, 8, LANE), F32))

    w_in_t = jnp.transpose(w_in, (0, 2, 1))
    wb = w_branch.astype(CD)
    wo = w_out.astype(CD)
    wf2 = w_ffn_out.astype(CD)

    hl = norm_mod(xl, vt, norm1(0, 0), CD)
    hc = norm_mod(xc, vt, norm1(0, 1), CD)
    for l in range(depth):
        last = l == depth - 1
        w = _layer_weights(w_in_t, l, mla_w_uq[l], mla_w_ukv[l])
        gq = mla_q_norm_g[l].reshape(1, -1).astype(F32)
        gkv = mla_kv_norm_g[l].reshape(1, -1).astype(F32)
        sink = sw_sink[l].reshape(1, -1).astype(F32)

        ml_l, na_l, sw_l, mla_lat = _project(hl, w, tab_l, gq, gkv, s + n_ctx)
        ml_c, na_c, sw_c, (mq, mk, mv) = _project(hc, w, tab_c, gq, gkv, s + n_ctx, mla_lat)

        qc_c, kc_c, gt_c = ml_prep(ml_c, ml_conv_w[l], ml_gate_b[l])
        qc_l, kc_l, gt_l = ml_prep(ml_l, ml_conv_w[l], ml_gate_b[l])
        hf_c, hb_c, cst, mst = ml_scan(qc_c, kc_c, ml_c, gt_c, zero_state)
        hf_l, hb_l, _, _ = ml_scan(qc_l, kc_l, ml_l, gt_l, (cst, mst))
        ya_l = ml_finish(hf_l, hb_l, ml_l, ml_norm_g[l])
        yb_l = na_attn(na_l, na_c, na_rpb[l], rows)
        yc_l = sw_attn(sw_l, sw_c, sink)
        yd_l = mla_flash(mq, mk, mv, s)

        nxt_l = (fin_row, zero_row, zero_row) if last else norm1(l + 1, 0)
        xl_new, hl_new = _dense_tail(xl, hl, (ya_l, yb_l, yc_l, yd_l), w, l, wb, wo, w_ffn_in, wf2, vt,
                                     mod_row(l, 0, 2), norm2(l, 0), mod_row(l, 0, 5), nxt_l, last)
        if not last:
            ya_c = ml_finish(hf_c, hb_c, ml_c, ml_norm_g[l])
            nw = NA_HEADS * NA_DIM
            yb_c = dense_attn([(na_c, nw, 0), (na_c, nw, 1), (na_c, nw, 2)], NA_HEADS, 1, NA_DIM, NA_DIM,
                              n_ctx, log2_scores=True)
            kcol = SW_HEADS * SW_DIM // LANE
            yc_c = dense_attn([(sw_c, SW_HEADS * SW_DIM, 0), (sw_c, LANE, kcol), (sw_c, LANE, kcol + 1)],
                              SW_HEADS, SW_HEADS // SW_KV_HEADS, SW_DIM, SW_DIM, n_ctx, sink=sink,
                              log2_scores=True)
            assert s % n_ctx == 0
            yd_c = dense_attn([(mq, MLA_HEADS * MLA_HW, 0), (mk, MLA_HEADS * MLA_HW, 0),
                               (mv, MLA_HEADS * MLA_V, 0)], MLA_HEADS, 1, MLA_HW, MLA_V, n_ctx,
                              log2_scores=True, row_block=s // n_ctx)
            xc, hc = _dense_tail(xc, hc, (ya_c, yb_c, yc_c, yd_c), w, l, wb, wo, w_ffn_in, wf2, vt,
                                 mod_row(l, 1, 2), norm2(l, 1), mod_row(l, 1, 5), norm1(l + 1, 1), False)
        xl, hl = xl_new, hl_new

    return hl[None].astype(x.dtype)
```

```python
import functools

import numpy as np
import jax
import jax.numpy as jnp
from jax import lax
from jax.experimental import pallas as pl
from jax.experimental.pallas import tpu as pltpu

F32 = jnp.float32
CD = jnp.bfloat16

GRID_W = 64
EPS = 1e-6
ROPE_THETA = 10000.0
ML_HEADS, ML_QK, ML_V, ML_CONV = 4, 64, 128, 3
NA_HEADS, NA_DIM, NA_WIN_R, NA_WIN_C = 8, 64, 8, 16
SW_HEADS, SW_KV_HEADS, SW_DIM, SW_WINDOW = 8, 2, 64, 128
MLA_HEADS, MLA_Q_RANK, MLA_KV_RANK, MLA_NOPE, MLA_ROPE, MLA_V = 4, 384, 256, 128, 64, 128
N_BRANCH, BRANCH_W = 4, 512

ML_QKW = ML_HEADS * ML_QK
ML_VW = ML_HEADS * ML_V
ML_COLS = 2 * ML_QKW + 2 * ML_VW + 4 * ML_HEADS
ML_PAD = 1664
NA_COLS = 3 * NA_HEADS * NA_DIM
SW_COLS = (SW_HEADS + 2 * SW_KV_HEADS) * SW_DIM
MLA_COLS = MLA_Q_RANK + MLA_KV_RANK + MLA_ROPE
MLA_HW = 256
LANE = 128
NEG = -1e30
LOG2E = float(np.log2(np.e))
VMEM_LIMIT = 56 * 1024 * 1024

NA_QROWS = 4
SW_TQ = 256
ML_CHUNK = 256
MLA_TK = 1280


def _cparams(*sem):
    return pltpu.CompilerParams(dimension_semantics=sem, vmem_limit_bytes=VMEM_LIMIT)


def _row_tile(t, pref=512):
    tm = min(pref, t)
    assert t % tm == 0
    return tm


def _dot(a, b):
    return jnp.dot(a, b, preferred_element_type=F32)


def _dot_nt(a, b):
    return lax.dot_general(a, b, (((1,), (1,)), ((), ())), preferred_element_type=F32)


def _sigmoid(x):
    return 1.0 / (1.0 + jnp.exp(-x))


def _lanes(col, n):
    tile = jnp.broadcast_to(col, (col.shape[0], LANE))
    return tile if n == LANE else jnp.tile(tile, (1, n // LANE))


def _ones_col(n, w):
    return jnp.where(lax.broadcasted_iota(jnp.int32, (n, w), 1) == 0, 1.0, 0.0).astype(CD)


def _adaln_kernel(c_ref, w_ref, b_ref, o_ref):
    w = w_ref[0]
    for r in range(2):
        c = c_ref[:, r:r + 1]
        o_ref[0, r:r + 1, :] = jnp.sum((c * _sigmoid(c)) * w, axis=0, keepdims=True) + b_ref[0]


def adaln(cc, w_ada, b_ada):
    nl, d, n = w_ada.shape
    tn = 1024
    return pl.pallas_call(
        _adaln_kernel,
        grid=(nl, n // tn),
        in_specs=[pl.BlockSpec((d, 2), lambda l, j: (0, 0)),
                  pl.BlockSpec((1, d, tn), lambda l, j: (l, 0, j)),
                  pl.BlockSpec((1, 1, tn), lambda l, j: (l, 0, j))],
        out_specs=pl.BlockSpec((1, 2, tn), lambda l, j: (l, 0, j)),
        out_shape=jax.ShapeDtypeStruct((nl, 2, n), F32),
        compiler_params=_cparams("parallel", "parallel"),
        name="adaln",
    )(cc, w_ada, b_ada.reshape(nl, 1, n))


def _vec_spec(d, idx):
    return pl.BlockSpec((1, 1, d), lambda *_: (idx, 0, 0))


def _norm_mod(x, g, shift, scale):
    y = x * lax.rsqrt(jnp.mean(x * x, axis=-1, keepdims=True) + EPS)
    return (y * g) * (1.0 + scale) + shift


def _norm_mod_kernel(x_ref, g_ref, sh_ref, sc_ref, o_ref):
    o_ref[...] = _norm_mod(x_ref[...], g_ref[0], sh_ref[0], sc_ref[0]).astype(o_ref.dtype)


def norm_mod(x, vt, norm_idx, out_dtype):
    t, d = x.shape
    tm = _row_tile(t, 1024)
    return pl.pallas_call(
        _norm_mod_kernel,
        grid=(t // tm,),
        in_specs=[pl.BlockSpec((tm, d), lambda i: (i, 0))] + [_vec_spec(d, ix) for ix in norm_idx],
        out_specs=pl.BlockSpec((tm, d), lambda i: (i, 0)),
        out_shape=jax.ShapeDtypeStruct((t, d), out_dtype),
        compiler_params=_cparams("parallel"),
        name="norm_mod",
    )(x, vt, vt, vt)


def _mm_kernel(a_ref, b_ref, o_ref):
    o_ref[...] = _dot_nt(a_ref[...], b_ref[...]).astype(o_ref.dtype)


def matmul(a, bt, out_dtype, tn=None):
    t, k = a.shape
    n = bt.shape[0]
    tn = n if tn is None else tn
    tm = _row_tile(t, 1024)
    return pl.pallas_call(
        _mm_kernel,
        grid=(n // tn, t // tm),
        in_specs=[pl.BlockSpec((tm, k), lambda j, i: (i, 0)),
                  pl.BlockSpec((tn, k), lambda j, i: (j, 0))],
        out_specs=pl.BlockSpec((tm, tn), lambda j, i: (i, j)),
        out_shape=jax.ShapeDtypeStruct((t, n), out_dtype),
        compiler_params=_cparams("parallel", "parallel"),
        name="matmul",
    )(a, bt)


def _mm_res_norm_kernel(nj, tn, want_x, a_ref, b_ref, x_ref, gate_ref, g_ref, sh_ref, sc_ref, *rest):
    xbuf, ho_ref = (rest[0], rest[1]) if want_x else (rest[1], rest[0])
    j = pl.program_id(1)
    acc = _dot(a_ref[...], b_ref[0])
    for jj in range(nj):
        @pl.when(j == jj)
        def _(jj=jj):
            sl = slice(jj * tn, (jj + 1) * tn)
            xbuf[:, sl] = x_ref[:, sl] + gate_ref[0][:, sl] * acc

    @pl.when(j == nj - 1)
    def _():
        ho_ref[...] = _norm_mod(xbuf[...], g_ref[0], sh_ref[0], sc_ref[0]).astype(ho_ref.dtype)


def matmul_residual_norm(a, b, l, x, vt, gate_idx, norm_idx, want_x, h_dtype, tn=None):
    t, kdim = a.shape
    n = b.shape[2]
    tn = n if tn is None else tn
    nj = n // tn
    tm = _row_tile(t)
    row = pl.BlockSpec((tm, n), lambda i, j: (i, 0))
    out_shape = [jax.ShapeDtypeStruct((t, n), h_dtype)]
    if want_x:
        out_shape = [jax.ShapeDtypeStruct((t, n), F32)] + out_shape
    out = pl.pallas_call(
        functools.partial(_mm_res_norm_kernel, nj, tn, want_x),
        grid=(t // tm, nj),
        in_specs=[pl.BlockSpec((tm, kdim), lambda i, j: (i, 0)),
                  pl.BlockSpec((1, kdim, tn), lambda i, j: (l, 0, j)),
                  row, _vec_spec(n, gate_idx)] + [_vec_spec(n, ix) for ix in norm_idx],
        out_specs=[row] * len(out_shape),
        out_shape=out_shape,
        scratch_shapes=[] if want_x else [pltpu.VMEM((tm, n), F32)],
        compiler_params=_cparams("parallel", "arbitrary"),
        name="matmul_residual_norm",
    )(a, b, x, vt, vt, vt, vt)
    return out if want_x else out[0]


def _rope_mm_kernel(n_rope, a_ref, w_ref, wr_ref, cos_ref, sin_ref, o_ref):
    a = a_ref[...]
    p = _dot_nt(a, w_ref[...])
    pr = _dot_nt(a, wr_ref[...])
    cos = cos_ref[...]
    sin = sin_ref[...]
    for j in range(p.shape[1] // LANE):
        sl = slice(j * LANE, (j + 1) * LANE)
        if j < n_rope:
            o_ref[:, sl] = (p[:, sl] * cos + pr[:, sl] * sin).astype(o_ref.dtype)
        else:
            o_ref[:, sl] = p[:, sl].astype(o_ref.dtype)


def rope_matmul(a, w, w_rot, cos, sin, n_rope):
    t, k = a.shape
    n = w.shape[0]
    nr = n_rope * LANE
    tm = _row_tile(t, 1024)
    return pl.pallas_call(
        functools.partial(_rope_mm_kernel, n_rope),
        grid=(t // tm,),
        in_specs=[pl.BlockSpec((tm, k), lambda i: (i, 0)),
                  pl.BlockSpec((n, k), lambda i: (0, 0)),
                  pl.BlockSpec((nr, k), lambda i: (0, 0)),
                  pl.BlockSpec((tm, LANE), lambda i: (i, 0)),
                  pl.BlockSpec((tm, LANE), lambda i: (i, 0))],
        out_specs=pl.BlockSpec((tm, n), lambda i: (i, 0)),
        out_shape=jax.ShapeDtypeStruct((t, n), CD),
        compiler_params=_cparams("parallel"),
        name="rope_matmul",
    )(a, w, w_rot, cos, sin)


def _mla_proj_kernel(h_ref, w1_ref, gq_ref, gkv_ref, wq_ref, wqr_ref, wkv_ref, cos_ref, sin_ref,
                     q_out, k_out, v_out):
    p = _dot_nt(h_ref[...], w1_ref[...])
    cq = p[:, :MLA_Q_RANK]
    ckv = p[:, MLA_Q_RANK:MLA_Q_RANK + MLA_KV_RANK]
    o = MLA_Q_RANK + MLA_KV_RANK
    kr = p[:, o:o + LANE]
    krr = p[:, o + LANE:o + 2 * LANE]
    cqn = (cq * lax.rsqrt(jnp.mean(cq * cq, axis=-1, keepdims=True) + EPS) * gq_ref[...]).astype(CD)
    ckvn = (ckv * lax.rsqrt(jnp.mean(ckv * ckv, axis=-1, keepdims=True) + EPS) * gkv_ref[...]).astype(CD)
    qa = _dot(cqn, wq_ref[...])
    qb = _dot(cqn, wqr_ref[...])
    kv = _dot(ckvn, wkv_ref[...])
    rope_lane = lax.broadcasted_iota(jnp.int32, cos_ref.shape, 1) < MLA_ROPE
    ck = jnp.where(rope_lane, cos_ref[...], 0.0)
    sk = jnp.where(rope_lane, sin_ref[...], 0.0)
    krp = (kr * ck + krr * sk).astype(CD)
    scale = (MLA_NOPE + MLA_ROPE) ** -0.5 * LOG2E
    cq_t = jnp.concatenate([jnp.full(ck.shape, scale, F32), ck * scale], axis=1)
    sq_t = jnp.concatenate([jnp.zeros(sk.shape, F32), sk * scale], axis=1)
    for h in range(MLA_HEADS):
        qs = slice(h * MLA_HW, (h + 1) * MLA_HW)
        q_out[:, qs] = (qa[:, qs] * cq_t + qb[:, qs] * sq_t).astype(CD)
        k_out[:, h * MLA_HW:h * MLA_HW + MLA_NOPE] = kv[:, h * 256:h * 256 + MLA_NOPE].astype(CD)
        k_out[:, h * MLA_HW + MLA_NOPE:(h + 1) * MLA_HW] = krp
        v_out[:, h * MLA_V:(h + 1) * MLA_V] = kv[:, h * 256 + MLA_NOPE:(h + 1) * 256].astype(CD)


def mla_proj(h, w1, gq, gkv, wq, wqr, wkv, cos, sin, total_rows, into=None):
    t, d = h.shape
    tm = _row_tile(t)
    full = lambda a: pl.BlockSpec(a.shape, lambda i: (0,) * a.ndim)
    rows = lambda w: pl.BlockSpec((tm, w), lambda i: (i, 0))
    hw = MLA_HEADS * MLA_HW
    vw = MLA_HEADS * MLA_V
    args = [h, w1, gq, gkv, wq, wqr, wkv, cos, sin]
    in_specs = [rows(d), full(w1), full(gq), full(gkv), full(wq), full(wqr), full(wkv),
                rows(LANE), rows(LANE)]
    if into is None:
        off, aliases, kern = 0, {}, _mla_proj_kernel
    else:
        assert (total_rows - t) % tm == 0
        off = (total_rows - t) // tm
        aliases = {len(args) + i: i for i in range(3)}
        in_specs = in_specs + [pl.BlockSpec(memory_space=pl.ANY)] * 3
        args = args + list(into)
        kern = lambda *refs: _mla_proj_kernel(*refs[:9], *refs[12:])
    orow = lambda w: pl.BlockSpec((tm, w), lambda i: (off + i, 0))
    return pl.pallas_call(
        kern,
        grid=(t // tm,),
        in_specs=in_specs,
        out_specs=[orow(hw), orow(hw), orow(vw)],
        out_shape=[jax.ShapeDtypeStruct((total_rows, hw), CD), jax.ShapeDtypeStruct((total_rows, hw), CD),
                   jax.ShapeDtypeStruct((total_rows, vw), CD)],
        input_output_aliases=aliases,
        compiler_params=_cparams("parallel"),
        name="mla_proj",
    )(*args)


def _flash_kernel(tq, tk, unroll, q_ref, k_ref, v_ref, o_ref, s_sc, m_sc, acc_sc):
    nkb = k_ref.shape[0] // tk
    nsub = q_ref.shape[0] // tq
    total = nsub * nkb
    m_sc[...] = jnp.full(m_sc.shape, NEG, F32)
    acc_sc[...] = jnp.zeros(acc_sc.shape, F32)
    ones_col = jnp.where(lax.broadcasted_iota(jnp.int32, (tk, LANE), 1) == 0, 1.0, 0.0).astype(CD)

    def split(b):
        sub = b // nkb
        return sub, b - sub * nkb

    def scores(b):
        sub, j = split(b)
        q = q_ref[pl.ds(pl.multiple_of(sub * tq, tq), tq), :]
        k = k_ref[pl.ds(pl.multiple_of(j * tk, tk), tk), :]
        return _dot_nt(q, k)

    def softmax_pv(s, b):
        sub, j = split(b)
        v = v_ref[pl.ds(pl.multiple_of(j * tk, tk), tk), :]
        m_prev = m_sc[sub]
        m_new = jnp.maximum(m_prev, jnp.max(s, axis=-1, keepdims=True))
        alpha = jnp.exp2(m_prev - m_new)
        p = jnp.exp2(s - m_new).astype(CD)
        v_aug = jnp.concatenate([v, ones_col], axis=1)
        acc_sc[sub] = alpha * acc_sc[sub] + _dot(p, v_aug)
        m_sc[sub] = m_new

    s_sc[0] = scores(0)

    def body(g, carry):
        b0 = g * unroll
        for u in range(unroll):
            s_sc[(u + 1) % 2] = scores(jnp.minimum(b0 + u + 1, total - 1))
            softmax_pv(s_sc[u % 2], b0 + u)
        return carry

    lax.fori_loop(0, total // unroll, body, 0)
    for sub in range(nsub):
        acc = acc_sc[sub]
        o_ref[sub * tq:(sub + 1) * tq, :] = (acc[:, :MLA_V] / acc[:, MLA_V:MLA_V + 1]).astype(o_ref.dtype)


def mla_flash(q, k, v, s):
    nk = k.shape[0]
    tq = _row_tile(s, 512)
    tstep = _row_tile(s, 16 * tq)
    tk = MLA_TK
    total = (tstep // tq) * (nk // tk)
    assert nk % tk == 0 and total % 2 == 0
    unroll = 16 if total % 16 == 0 else (4 if total % 4 == 0 else 2)
    return pl.pallas_call(
        functools.partial(_flash_kernel, tq, tk, unroll),
        grid=(MLA_HEADS, s // tstep),
        in_specs=[pl.BlockSpec((tstep, MLA_HW), lambda h, i: (i, h)),
                  pl.BlockSpec((nk, MLA_HW), lambda h, i: (0, h), pipeline_mode=pl.Buffered(1)),
                  pl.BlockSpec((nk, MLA_V), lambda h, i: (0, h), pipeline_mode=pl.Buffered(1))],
        out_specs=pl.BlockSpec((tstep, MLA_V), lambda h, i: (i, h)),
        out_shape=jax.ShapeDtypeStruct((s, MLA_HEADS * MLA_V), CD),
        scratch_shapes=[pltpu.VMEM((2, tq, tk), F32), pltpu.VMEM((tstep // tq, tq, 1), F32),
                        pltpu.VMEM((tstep // tq, tq, 2 * MLA_V), F32)],
        compiler_params=_cparams("parallel", "arbitrary"),
        name="mla_flash",
    )(q, k, v)


def _dense_attn_kernel(n_heads, group, dqk, dv, log2_scores, q_ref, k_ref, v_ref, sink_ref, o_ref):
    q = q_ref[...]
    k = k_ref[...]
    v = v_ref[...]
    ex = jnp.exp2 if log2_scores else jnp.exp
    for h in range(n_heads):
        g = h // group
        s = _dot_nt(q[:, h * dqk:(h + 1) * dqk], k[:, g * dqk:(g + 1) * dqk])
        m = jnp.max(s, axis=-1, keepdims=True)
        l = jnp.zeros_like(m)
        if sink_ref is not None:
            sk = sink_ref[0:1, h:h + 1] * (LOG2E if log2_scores else 1.0)
            m = jnp.maximum(m, sk)
            l = ex(sk - m)
        p = ex(s - m)
        l = l + jnp.sum(p, axis=-1, keepdims=True)
        o = _dot(p.astype(CD), v[:, g * dv:(g + 1) * dv]) / l
        o_ref[:, h * dv:(h + 1) * dv] = o.astype(o_ref.dtype)


def dense_attn(qkv_specs, n_heads, group, dqk, dv, c, sink=None, log2_scores=False, row_block=0):
    arrays = [a for a, _, _ in qkv_specs]
    specs = [pl.BlockSpec((c, w), functools.partial(lambda b, i: (row_block, b), b)) for _, w, b in qkv_specs]
    if sink is None:
        kern = lambda q, k, v, o: _dense_attn_kernel(n_heads, group, dqk, dv, log2_scores, q, k, v, None, o)
    else:
        kern = functools.partial(_dense_attn_kernel, n_heads, group, dqk, dv, log2_scores)
        arrays.append(sink)
        specs.append(pl.BlockSpec(sink.shape, lambda i: (0, 0)))
    return pl.pallas_call(
        kern,
        grid=(1,),
        in_specs=specs,
        out_specs=pl.BlockSpec((c, n_heads * dv), lambda i: (0, 0)),
        out_shape=jax.ShapeDtypeStruct((c, n_heads * dv), CD),
        compiler_params=_cparams("arbitrary"),
        name="dense_attn",
    )(*arrays)


def _na_kernel(q_ref, kp_ref, ko_ref, kn_ref, vp_ref, vo_ref, vn_ref, kc_ref, vc_ref, bias_ref, o_ref):
    q = q_ref[...]
    kcat = jnp.concatenate([kp_ref[...], ko_ref[...], kn_ref[...], kc_ref[...]], axis=0)
    vcat = jnp.concatenate([vp_ref[...], vo_ref[...], vn_ref[...], vc_ref[...]], axis=0)
    tb = q.shape[0]
    nloc = 3 * tb
    ones = _ones_col(kcat.shape[0], LANE)
    first_half = lax.broadcasted_iota(jnp.int32, (tb, LANE), 1) < NA_DIM
    zero = jnp.zeros((tb, LANE), CD)
    for j in range(NA_HEADS // 2):
        ps = slice(j * LANE, (j + 1) * LANE)
        q2 = q[:, ps]
        k2 = kcat[:, ps]
        v_aug = jnp.concatenate([vcat[:, ps], ones], axis=1)
        outs = []
        for e in range(2):
            qm = jnp.where(first_half, q2, zero) if e == 0 else jnp.where(first_half, zero, q2)
            s = _dot_nt(qm, k2)
            s = jnp.concatenate([s[:, :nloc] + bias_ref[0, 2 * j + e], s[:, nloc:]], axis=1)
            m = jnp.max(s, axis=-1, keepdims=True)
            p = jnp.exp2(s - _lanes(m, s.shape[1])).astype(CD)
            acc = _dot(p, v_aug)
            outs.append(acc[:, :LANE] / _lanes(acc[:, LANE:LANE + 1], LANE))
        o_ref[:, ps] = jnp.where(first_half, outs[0], outs[1]).astype(o_ref.dtype)


def _na_bias(rpb, rows):
    qr = NA_QROWS
    tb = qr * GRID_W
    nb = rows // qr
    nkr = 3 * qr
    rsel = np.zeros((3, qr, nkr, 2 * NA_WIN_R - 1), np.float32)
    for v, b in enumerate((0, 1, nb - 1)):
        for a in range(qr):
            r = qr * b + a
            rs = min(max(r - NA_WIN_R // 2, 0), rows - NA_WIN_R)
            for u in range(nkr):
                kr = qr * (b - 1) + u
                if 0 <= b - 1 + u // qr < nb and rs <= kr < rs + NA_WIN_R:
                    rsel[v, a, u, kr - r + NA_WIN_R - 1] = 1.0
    csel = np.zeros((GRID_W, GRID_W, 2 * NA_WIN_C - 1), np.float32)
    for qc in range(GRID_W):
        cs = min(max(qc - NA_WIN_C // 2, 0), GRID_W - NA_WIN_C)
        for kc in range(cs, cs + NA_WIN_C):
            csel[qc, kc, kc - qc + NA_WIN_C - 1] = 1.0
    valid = (np.einsum('vaud,qkj->vaquk', rsel, csel) > 0).reshape(3, tb, 3 * tb)
    assert 2 * qr == NA_WIN_R
    nd = 2 * NA_WIN_R - 1
    rp = rpb.astype(F32) * LOG2E
    gap = jnp.zeros((NA_HEADS, nd, GRID_W + 1 - (2 * NA_WIN_C - 1)), F32)
    ring = jnp.concatenate([rp[..., NA_WIN_C - 1:], gap, rp[..., :NA_WIN_C - 1]], axis=-1)
    cols = jnp.tile(ring, (1, 1, GRID_W))[..., :GRID_W * GRID_W].reshape(NA_HEADS, nd, GRID_W, GRID_W)
    off = NA_WIN_R - 1 - qr
    slabs = [cols[:, off - a:off - a + nkr].transpose(0, 2, 1, 3).reshape(NA_HEADS, GRID_W, 3 * tb)
             for a in range(qr)]
    band = jnp.stack(slabs, axis=1).reshape(NA_HEADS, tb, 3 * tb)
    return jnp.where(valid[:, None], band[None], NEG)


def na_attn(p_lat, p_ctx, rpb, rows):
    s = p_lat.shape[0]
    c = p_ctx.shape[0]
    w = NA_HEADS * NA_DIM
    tb = NA_QROWS * GRID_W
    nb = s // tb
    assert rows % NA_QROWS == 0 and nb >= 3 and rows >= 2 * NA_WIN_R
    bias = _na_bias(rpb, rows)

    def blk(col, off):
        return pl.BlockSpec((tb, w), lambda i: (jnp.clip(i + off, 0, nb - 1), col))

    return pl.pallas_call(
        _na_kernel,
        grid=(nb,),
        in_specs=[blk(0, 0), blk(1, -1), blk(1, 0), blk(1, 1), blk(2, -1), blk(2, 0), blk(2, 1),
                  pl.BlockSpec((c, w), lambda i: (0, 1)),
                  pl.BlockSpec((c, w), lambda i: (0, 2)),
                  pl.BlockSpec((1, NA_HEADS, tb, 3 * tb),
                               lambda i: (jnp.where(i == 0, 0, jnp.where(i == nb - 1, 2, 1)), 0, 0, 0))],
        out_specs=pl.BlockSpec((tb, w), lambda i: (i, 0)),
        out_shape=jax.ShapeDtypeStruct((s, w), CD),
        compiler_params=_cparams("parallel"),
        name="na_attn",
    )(p_lat, p_lat, p_lat, p_lat, p_lat, p_lat, p_lat, p_ctx, p_ctx, bias)


def _sw_kernel(seq, q_ref, kp_ref, ko_ref, kn_ref, vp_ref, vo_ref, vn_ref, kc_ref, vc_ref, sink_ref, o_ref):
    i = pl.program_id(0)
    q = q_ref[...]
    tq = q.shape[0]
    kcat = jnp.concatenate([kp_ref[...], ko_ref[...], kn_ref[...], kc_ref[...]], axis=0)
    vcat = jnp.concatenate([vp_ref[...], vo_ref[...], vn_ref[...], vc_ref[...]], axis=0)
    nkk = tq + 2 * SW_WINDOW
    r = lax.broadcasted_iota(jnp.int32, (tq, nkk), 0)
    cc = lax.broadcasted_iota(jnp.int32, (tq, nkk), 1)
    kpos = i * tq - SW_WINDOW + cc
    d = cc - r
    mask = (d >= 0) & (d <= 2 * SW_WINDOW) & (kpos >= 0) & (kpos < seq)
    assert SW_KV_HEADS == 2 and SW_DIM * 2 == LANE
    ones = _ones_col(kcat.shape[0], LANE)
    swap = lambda x: jnp.concatenate([x[:, SW_DIM:], x[:, :SW_DIM]], axis=1)
    k_by_order = (kcat, swap(kcat))
    v_by_order = (jnp.concatenate([vcat, ones], axis=1), jnp.concatenate([swap(vcat), ones], axis=1))
    first_half = lax.broadcasted_iota(jnp.int32, (tq, LANE), 1) < SW_DIM
    zero = jnp.zeros((tq, LANE), CD)
    grp = SW_HEADS // SW_KV_HEADS
    for j in range(SW_HEADS // 2):
        ps = slice(j * LANE, (j + 1) * LANE)
        q2 = q[:, ps]
        outs = []
        for e in range(2):
            h = 2 * j + e
            order = 0 if h // grp == e else 1
            qm = jnp.where(first_half, q2, zero) if e == 0 else jnp.where(first_half, zero, q2)
            s = _dot_nt(qm, k_by_order[order])
            s = jnp.concatenate([jnp.where(mask, s[:, :nkk], NEG), s[:, nkk:]], axis=1)
            sk = sink_ref[0:1, h:h + 1] * LOG2E
            m = jnp.maximum(jnp.max(s, axis=-1, keepdims=True), sk)
            p = jnp.exp2(s - _lanes(m, s.shape[1])).astype(CD)
            acc = _dot(p, v_by_order[order])
            outs.append(acc[:, :LANE] / _lanes(acc[:, LANE:LANE + 1] + jnp.exp2(sk - m), LANE))
        o_ref[:, ps] = jnp.where(first_half, outs[0], outs[1]).astype(o_ref.dtype)


def sw_attn(p_lat, p_ctx, sink):
    s = p_lat.shape[0]
    c = p_ctx.shape[0]
    tq = SW_TQ
    assert s % tq == 0 and tq % SW_WINDOW == 0
    nq = s // tq
    per = tq // SW_WINDOW
    nwb = s // SW_WINDOW
    kcol = SW_HEADS * SW_DIM // LANE
    prev = lambda col: pl.BlockSpec((SW_WINDOW, LANE), lambda i: (jnp.maximum(i * per - 1, 0), col))
    own = lambda col: pl.BlockSpec((tq, LANE), lambda i: (i, col))
    nxt = lambda col: pl.BlockSpec((SW_WINDOW, LANE), lambda i: (jnp.minimum((i + 1) * per, nwb - 1), col))
    return pl.pallas_call(
        functools.partial(_sw_kernel, s),
        grid=(nq,),
        in_specs=[pl.BlockSpec((tq, SW_HEADS * SW_DIM), lambda i: (i, 0)),
                  prev(kcol), own(kcol), nxt(kcol), prev(kcol + 1), own(kcol + 1), nxt(kcol + 1),
                  pl.BlockSpec((c, LANE), lambda i: (0, kcol)),
                  pl.BlockSpec((c, LANE), lambda i: (0, kcol + 1)),
                  pl.BlockSpec(sink.shape, lambda i: (0, 0))],
        out_specs=pl.BlockSpec((tq, SW_HEADS * SW_DIM), lambda i: (i, 0)),
        out_shape=jax.ShapeDtypeStruct((s, SW_HEADS * SW_DIM), CD),
        compiler_params=_cparams("parallel"),
        name="sw_attn",
    )(p_lat, p_lat, p_lat, p_lat, p_lat, p_lat, p_lat, p_ctx, p_ctx, sink)


def _ml_prep_kernel(nt, x_ref, xp_ref, xn_ref, g_ref, w_ref, b_ref, q_out, k_out, g_out):
    i = pl.program_id(0)
    x = x_ref[...]
    tm = x.shape[0]
    row = lax.broadcasted_iota(jnp.int32, x.shape, 0)
    has_prev = jnp.where(i > 0, 1.0, 0.0)
    has_next = jnp.where(i < nt - 1, 1.0, 0.0)
    prev_row = xp_ref[7:8, :] * has_prev
    next_row = xn_ref[0:1, :] * has_next
    xm = jnp.where(row == 0, prev_row, pltpu.roll(x, 1, axis=0))
    xq = jnp.where(row == tm - 1, next_row, pltpu.roll(x, tm - 1, axis=0))
    y = w_ref[0:1, :] * xm + w_ref[1:2, :] * x + w_ref[2:3, :] * xq
    y = y * _sigmoid(y)
    q_out[...] = y[:, :ML_QKW].astype(CD)
    k_out[...] = (y[:, ML_QKW:] * (ML_QK ** -0.5)).astype(CD)
    g = g_ref[...] + b_ref[...]
    lane = lax.broadcasted_iota(jnp.int32, g.shape, 1)
    logsig = jnp.minimum(g, 0.0) - jnp.log(1.0 + jnp.exp(-jnp.abs(g)))
    g_out[...] = jnp.where(lane < 2 * ML_HEADS, g, logsig)


def ml_prep(p_ml, conv_w, gate_b):
    t = p_ml.shape[0]
    tm = _row_tile(t, 1024)
    nt = t // tm
    wq = 2 * ML_QKW
    gcol = (2 * ML_QKW + 2 * ML_VW) // LANE
    w8 = jnp.zeros((8, wq), F32).at[:ML_CONV].set(conv_w.astype(F32))
    b = jnp.zeros((1, LANE), F32).at[0, :4 * ML_HEADS].set(gate_b.astype(F32).reshape(-1))
    h8 = tm // 8
    return pl.pallas_call(
        functools.partial(_ml_prep_kernel, nt),
        grid=(nt,),
        in_specs=[pl.BlockSpec((tm, wq), lambda i: (i, 0)),
                  pl.BlockSpec((8, wq), lambda i: (jnp.maximum(i * h8 - 1, 0), 0)),
                  pl.BlockSpec((8, wq), lambda i: (jnp.minimum((i + 1) * h8, t // 8 - 1), 0)),
                  pl.BlockSpec((tm, LANE), lambda i: (i, gcol)),
                  pl.BlockSpec((8, wq), lambda i: (0, 0)),
                  pl.BlockSpec((1, LANE), lambda i: (0, 0))],
        out_specs=[pl.BlockSpec((tm, ML_QKW), lambda i: (i, 0)),
                   pl.BlockSpec((tm, ML_QKW), lambda i: (i, 0)),
                   pl.BlockSpec((tm, LANE), lambda i: (i, 0))],
        out_shape=[jax.ShapeDtypeStruct((t, ML_QKW), CD), jax.ShapeDtypeStruct((t, ML_QKW), CD),
                   jax.ShapeDtypeStruct((t, LANE), F32)],
        compiler_params=_cparams("parallel"),
        name="ml_prep",
    )(p_ml, p_ml, p_ml, p_ml, w8, b)


def _ml_scan_kernel(qf_ref, ktf_ref, vf_ref, gcf_ref, grf_ref,
                    qb_ref, ktb_ref, vb_ref, gcb_ref, grb_ref,
                    c0_ref, m0_ref, hf_ref, hb_ref, c_ref, m_ref):
    @pl.when(pl.program_id(0) == 0)
    def _():
        c_ref[...] = c0_ref[...]
        m_ref[...] = m0_ref[...]

    ln = qf_ref.shape[0]
    r = lax.broadcasted_iota(jnp.int32, (ln, ln), 0)
    c = lax.broadcasted_iota(jnp.int32, (ln, ln), 1)
    lower = r >= c
    upper = c >= r
    lower_f = jnp.where(lower, 1.0, 0.0)
    upper_f = jnp.where(upper, 1.0, 0.0)
    ones_col = jnp.where(lax.broadcasted_iota(jnp.int32, (ln, LANE), 1) == 0, 1.0, 0.0).astype(CD)
    hi = lax.Precision.HIGHEST
    first_half = lax.broadcasted_iota(jnp.int32, (ln, LANE), 1) < ML_QK
    zero_q = jnp.zeros((ln, LANE), CD)
    new_states = []

    streams = ((qf_ref, ktf_ref, vf_ref, gcf_ref, grf_ref, hf_ref, lower, lower_f, upper_f, ln - 1),
               (qb_ref, ktb_ref, vb_ref, gcb_ref, grb_ref, hb_ref, upper, upper_f, lower_f, 0))
    for d, (q_ref, kt_ref, v_ref, gc_ref, gr_ref, h_ref, mask, tri_c, tri_r, last) in enumerate(streams):
        gcol = gc_ref[...]
        grow = gr_ref[...]
        bcol = jnp.dot(tri_c, gcol, precision=hi, preferred_element_type=F32)
        brow = jnp.dot(grow, tri_r, precision=hi, preferred_element_type=F32)
        q = q_ref[...]
        kt = kt_ref[...]
        v = v_ref[...]
        for hd in range(ML_HEADS):
            ch = d * ML_HEADS + hd
            ic = d * ML_HEADS + hd
            fc = 2 * ML_HEADS + ic
            pair, half = hd // 2, hd % 2
            b_c = bcol[:, fc:fc + 1]
            r_row = grow[ic:ic + 1, :] - brow[fc:fc + 1, :]
            m_prev = m_ref[ch, 0:1, 0:1]
            cst = c_ref[ch]
            rmat = jnp.where(mask, r_row, NEG)
            mx = jnp.maximum(jnp.max(rmat, axis=-1, keepdims=True), m_prev)
            q2 = q[:, pair * LANE:(pair + 1) * LANE]
            qm = jnp.where(first_half, q2, zero_q) if half == 0 else jnp.where(first_half, zero_q, q2)
            s = _dot(qm, kt[pair * LANE:(pair + 1) * LANE, :]) * jnp.exp(rmat - mx)
            w_inter = jnp.exp(m_prev - mx)
            v_aug = jnp.concatenate([v[:, hd * ML_V:(hd + 1) * ML_V].astype(CD), ones_col], axis=1)
            c_pair = jnp.concatenate([c_ref[ch - half], c_ref[ch - half + 1]], axis=0).astype(CD)
            num = _dot(s.astype(CD), v_aug) + w_inter * _dot(qm, c_pair)
            den = num[:, ML_V:ML_V + 1]
            h = num[:, :ML_V] / jnp.maximum(jnp.abs(den), jnp.exp(-(b_c + mx)))
            h_ref[:, hd * ML_V:(hd + 1) * ML_V] = h
            b_end = b_c[last:last + 1, :]
            m_new = b_end + jnp.maximum(m_prev, jnp.max(r_row, axis=-1, keepdims=True))
            w_s = jnp.exp(b_end + r_row - m_new)
            w_c = jnp.exp(b_end + m_prev - m_new)
            kth = kt[hd * ML_QK:(hd + 1) * ML_QK, :]
            ktw = (kth.astype(F32) * w_s).astype(CD)
            new_states.append((ch, w_c * cst + _dot(ktw, v_aug), m_new))
    for ch, c_new, m_new in new_states:
        c_ref[ch] = c_new
        m_ref[ch] = jnp.broadcast_to(m_new, m_ref.shape[1:])


def ml_scan(qc, kc, p_ml, gates, state):
    t = qc.shape[0]
    ln = min(ML_CHUNK, t)
    assert t % ln == 0
    nc = t // ln
    kt = kc.T
    gr = gates[:, :4 * ML_HEADS].T
    c0, m0 = state
    vcol = 2 * ML_QKW // ML_VW
    fwd = lambda j: j
    bwd = lambda j: nc - 1 - j

    def specs(ix):
        return [pl.BlockSpec((ln, ML_QKW), lambda j: (ix(j), 0)),
                pl.BlockSpec((ML_QKW, ln), lambda j: (0, ix(j))),
                pl.BlockSpec((ln, ML_VW), lambda j: (ix(j), vcol)),
                pl.BlockSpec((ln, LANE), lambda j: (ix(j), 0)),
                pl.BlockSpec((4 * ML_HEADS, ln), lambda j: (0, ix(j)))]

    st_specs = [pl.BlockSpec(c0.shape, lambda j: (0, 0, 0)), pl.BlockSpec(m0.shape, lambda j: (0, 0, 0))]
    return pl.pallas_call(
        _ml_scan_kernel,
        grid=(nc,),
        in_specs=specs(fwd) + specs(bwd) + st_specs,
        out_specs=[pl.BlockSpec((ln, ML_VW), lambda j: (fwd(j), 0)),
                   pl.BlockSpec((ln, ML_VW), lambda j: (bwd(j), 0))] + st_specs,
        out_shape=[jax.ShapeDtypeStruct((t, ML_VW), F32), jax.ShapeDtypeStruct((t, ML_VW), F32),
                   jax.ShapeDtypeStruct(c0.shape, F32), jax.ShapeDtypeStruct(m0.shape, F32)],
        compiler_params=_cparams("arbitrary"),
        name="ml_scan",
    )(qc, kt, p_ml, gates, gr, qc, kt, p_ml, gates, gr, c0, m0)


def _ml_finish_kernel(hf_ref, hb_ref, o_ref, g_ref, y_ref):
    h = hf_ref[...] + hb_ref[...]
    og = _sigmoid(o_ref[...])
    g = g_ref[...]
    for hd in range(ML_HEADS):
        sl = slice(hd * ML_V, (hd + 1) * ML_V)
        hh = h[:, sl]
        y = hh * lax.rsqrt(jnp.mean(hh * hh, axis=-1, keepdims=True) + EPS) * g[:, sl]
        y_ref[:, sl] = (y * og[:, sl]).astype(y_ref.dtype)


def ml_finish(hf, hb, p_ml, norm_g):
    t = hf.shape[0]
    tm = _row_tile(t, 1024)
    ocol = (2 * ML_QKW + ML_VW) // ML_VW
    blk = pl.BlockSpec((tm, ML_VW), lambda i: (i, 0))
    return pl.pallas_call(
        _ml_finish_kernel,
        grid=(t // tm,),
        in_specs=[blk, blk, pl.BlockSpec((tm, ML_VW), lambda i: (i, ocol)),
                  pl.BlockSpec((1, ML_VW), lambda i: (0, 0))],
        out_specs=blk,
        out_shape=jax.ShapeDtypeStruct((t, ML_VW), CD),
        compiler_params=_cparams("parallel"),
        name="ml_finish",
    )(hf, hb, p_ml, norm_g.reshape(1, ML_VW).astype(F32))


def _merge_kernel(h_ref, ya_ref, yb_ref, yc_ref, yd_ref, g0_ref, g1_ref, g2_ref, g3_ref, wb_ref, o_ref):
    h = h_ref[...]
    acc = None
    for i, (y_ref, g_ref) in enumerate(zip((ya_ref, yb_ref, yc_ref, yd_ref), (g0_ref, g1_ref, g2_ref, g3_ref))):
        term = _sigmoid(_dot_nt(h, g_ref[...])) * _dot(y_ref[...], wb_ref[0, i])
        acc = term if acc is None else acc + term
    o_ref[...] = acc.astype(o_ref.dtype)


def gated_merge(h, ys, w_gate, w_branch, l, tn=512):
    t, d = h.shape
    tm = _row_tile(t, 1024)
    nj = d // tn
    yspec = pl.BlockSpec((tm, BRANCH_W), lambda j, i: (i, 0))
    gspec = lambda b: pl.BlockSpec((tn, d), lambda j, i: (b * nj + j, 0))
    return pl.pallas_call(
        _merge_kernel,
        grid=(nj, t // tm),
        in_specs=[pl.BlockSpec((tm, d), lambda j, i: (i, 0)), yspec, yspec, yspec, yspec,
                  gspec(0), gspec(1), gspec(2), gspec(3),
                  pl.BlockSpec((1, N_BRANCH, BRANCH_W, tn), lambda j, i: (l, 0, 0, j))],
        out_specs=pl.BlockSpec((tm, tn), lambda j, i: (i, j)),
        out_shape=jax.ShapeDtypeStruct((t, d), CD),
        compiler_params=_cparams("parallel", "parallel"),
        name="gated_merge",
    )(h, *ys, w_gate, w_gate, w_gate, w_gate, w_branch)


def _ffn1_kernel(h_ref, wa_ref, wg_ref, o_ref, wa_sc, wg_sc):
    @pl.when(pl.program_id(1) == 0)
    def _():
        wa_sc[...] = wa_ref[0].astype(CD)
        wg_sc[...] = wg_ref[0].astype(CD)

    h = h_ref[...]
    a = _dot(h, wa_sc[...])
    g = _dot(h, wg_sc[...])
    o_ref[...] = (a * _sigmoid(a) * g).astype(o_ref.dtype)


def ffn_up(h, w, l, tn=512):
    t, d = h.shape
    dff = w.shape[2] // 2
    tm = _row_tile(t, 1024)
    nj = dff // tn
    return pl.pallas_call(
        _ffn1_kernel,
        grid=(nj, t // tm),
        in_specs=[pl.BlockSpec((tm, d), lambda j, i: (i, 0)),
                  pl.BlockSpec((1, d, tn), lambda j, i: (l, 0, j)),
                  pl.BlockSpec((1, d, tn), lambda j, i: (l, 0, nj + j))],
        out_specs=pl.BlockSpec((tm, tn), lambda j, i: (i, j)),
        out_shape=jax.ShapeDtypeStruct((t, dff), CD),
        scratch_shapes=[pltpu.VMEM((d, tn), CD), pltpu.VMEM((d, tn), CD)],
        compiler_params=_cparams("parallel", "arbitrary"),
        name="ffn_up",
    )(h, w, w)


def _rot_cols(w, dim):
    k, n = w.shape
    w4 = w.reshape(k, n // dim, 2, dim // 2)
    return jnp.stack([-w4[:, :, 1], w4[:, :, 0]], axis=2).reshape(k, n)


def _rope_tables(n_tokens, dim):
    a = dim // 2
    inv = 1.0 / (ROPE_THETA ** (jnp.arange(0, a, 2, dtype=F32) / a))
    rows = n_tokens // GRID_W
    assert rows * GRID_W == n_tokens
    ang_r = jnp.arange(rows, dtype=F32)[:, None] * inv
    ang_c = jnp.arange(GRID_W, dtype=F32)[:, None] * inv

    def expand(fr, fc):
        shape = (rows, GRID_W, a // 2)
        return jnp.concatenate([jnp.broadcast_to(fr[:, None, :], shape), jnp.broadcast_to(fc[None], shape)],
                               axis=-1).reshape(n_tokens, a)

    return expand(jnp.cos(ang_r), jnp.cos(ang_c)), expand(jnp.sin(ang_r), jnp.sin(ang_c))


_O_NA = ML_COLS
_O_SW = _O_NA + NA_COLS
_O_MLA = _O_SW + SW_COLS
_O_KR = _O_MLA + MLA_Q_RANK + MLA_KV_RANK
_O_GATE = _O_MLA + MLA_COLS


def _rot_half_rows(x, dim):
    half = dim // 2
    parts = []
    for b in range(x.shape[0] // dim):
        parts += [-x[b * dim + half:(b + 1) * dim], x[b * dim:b * dim + half]]
    return jnp.concatenate(parts, axis=0)


def _repack_kernel(w_ref, ml_ref, na_ref, sw_ref, swr_ref, mla_ref, gate_ref):
    w = w_ref[0]
    cols = w.shape[1]
    zeros = lambda n: jnp.zeros((n, cols), F32)
    ml_ref[...] = jnp.concatenate([w[:ML_COLS], zeros(ML_PAD - ML_COLS)], axis=0).astype(CD)
    nq = NA_HEADS * NA_DIM
    na_ref[...] = jnp.concatenate([w[_O_NA:_O_NA + nq] * (NA_DIM ** -0.5 * LOG2E), w[_O_NA + nq:_O_SW]],
                                  axis=0).astype(CD)
    sq = SW_HEADS * SW_DIM
    skv = SW_KV_HEADS * SW_DIM
    sw_q = w[_O_SW:_O_SW + sq] * (SW_DIM ** -0.5 * LOG2E)
    sw_k = w[_O_SW + sq:_O_SW + sq + skv]
    sw_ref[...] = jnp.concatenate([sw_q, w[_O_SW + sq:_O_MLA]], axis=0).astype(CD)
    swr_ref[...] = _rot_half_rows(jnp.concatenate([sw_q, sw_k], axis=0), SW_DIM).astype(CD)
    kr = w[_O_KR:_O_GATE]
    pad = zeros(LANE - MLA_ROPE)
    mla_ref[...] = jnp.concatenate([w[_O_MLA:_O_KR], kr, pad, _rot_half_rows(kr, MLA_ROPE), pad], axis=0).astype(CD)
    gate_ref[...] = w[_O_GATE:].astype(CD)


def _repack_w_in(w_in_t, l):
    _, n, d = w_in_t.shape
    tc = 256
    heights = (ML_PAD, NA_COLS, SW_COLS, (SW_HEADS + SW_KV_HEADS) * SW_DIM,
               MLA_Q_RANK + MLA_KV_RANK + 2 * LANE, n - _O_GATE)
    return pl.pallas_call(
        _repack_kernel,
        grid=(d // tc,),
        in_specs=[pl.BlockSpec((1, n, tc), lambda i: (l, 0, i))],
        out_specs=[pl.BlockSpec((ht, tc), lambda i: (0, i)) for ht in heights],
        out_shape=[jax.ShapeDtypeStruct((ht, d), CD) for ht in heights],
        compiler_params=_cparams("parallel"),
        name="repack_w_in",
    )(w_in_t)


def _layer_weights(w_in_t, l, mla_w_uq_l, mla_w_ukv_l):
    w = {}
    w["ml"], w["na"], w["sw"], w["sw_rot"], w["mla1"], w["gate"] = _repack_w_in(w_in_t, l)
    uq = mla_w_uq_l.reshape(MLA_Q_RANK, MLA_HEADS, MLA_NOPE + MLA_ROPE)
    zq = jnp.zeros((MLA_Q_RANK, MLA_HEADS, MLA_HW - MLA_NOPE - MLA_ROPE), F32)
    w["mla_q"] = jnp.concatenate([uq, zq], axis=2).reshape(MLA_Q_RANK, -1).astype(CD)
    uqr = _rot_cols(uq[:, :, MLA_NOPE:].reshape(MLA_Q_RANK, -1), MLA_ROPE).reshape(MLA_Q_RANK, MLA_HEADS, MLA_ROPE)
    w["mla_q_rot"] = jnp.concatenate([jnp.zeros_like(uq[:, :, :MLA_NOPE]), uqr, zq], axis=2
                                     ).reshape(MLA_Q_RANK, -1).astype(CD)
    w["mla_kv"] = mla_w_ukv_l.astype(CD)
    return w


def _tables(s, c):
    assert SW_DIM == MLA_ROPE
    cos, sin = _rope_tables(s, SW_DIM)
    tile4 = lambda a: jnp.concatenate([a, a, a, a], axis=1)
    return ((tile4(cos), tile4(sin)), (jnp.ones((c, LANE), F32), jnp.zeros((c, LANE), F32)))


def _project(h, w, tab, gq, gkv, total_rows, mla_into=None):
    p_ml = matmul(h, w["ml"], F32)
    p_na = matmul(h, w["na"], CD)
    n_rope = (SW_HEADS + SW_KV_HEADS) * SW_DIM // LANE
    p_sw = rope_matmul(h, w["sw"], w["sw_rot"], tab[0], tab[1], n_rope)
    mla = mla_proj(h, w["mla1"], gq, gkv, w["mla_q"], w["mla_q_rot"], w["mla_kv"],
                   tab[0], tab[1], total_rows, mla_into)
    return p_ml, p_na, p_sw, mla


def _dense_tail(x, h, ys, w, l, w_branch, w_out, w_ffn_in, w_ffn_out, vt, g1, norm2, g2, norm_next, last):
    merged = gated_merge(h, ys, w["gate"], w_branch, l)
    x, h2 = matmul_residual_norm(merged, w_out, l, x, vt, g1, norm2, True, CD)
    u = ffn_up(h2, w_ffn_in, l)
    if last:
        return None, matmul_residual_norm(u, w_ffn_out, l, x, vt, g2, norm_next, False, F32, tn=512)
    return matmul_residual_norm(u, w_ffn_out, l, x, vt, g2, norm_next, True, CD, tn=512)


def kernel(x, c, ctx, c_ctx, w_ada, b_ada, norm1_g, norm2_g, w_in, ml_conv_w, ml_gate_b, ml_norm_g, na_rpb,
           sw_sink, mla_q_norm_g, mla_kv_norm_g, mla_w_uq, mla_w_ukv, w_branch, w_out, w_ffn_in, w_ffn_out,
           final_norm_g):
    bsz, s, d = x.shape
    assert bsz == 1
    depth = w_in.shape[0]
    n_ctx = ctx.shape[1]
    rows = s // GRID_W
    xl = x[0].astype(F32)
    xc = ctx[0].astype(F32)
    mod = adaln(jnp.stack([c[0], c_ctx], axis=1).astype(F32), w_ada, b_ada)
    vt = jnp.concatenate([mod.reshape(depth * 12, d), norm1_g.astype(F32), norm2_g.astype(F32),
                          final_norm_g.reshape(1, d).astype(F32), jnp.zeros((1, d), F32)]).reshape(-1, 1, d)
    mod_row = lambda l, stream, part: (l * 2 + stream) * 6 + part
    n1_row = lambda l: depth * 12 + l
    n2_row = lambda l: depth * 13 + l
    fin_row, zero_row = depth * 14, depth * 14 + 1
    norm1 = lambda l, st: (n1_row(l), mod_row(l, st, 0), mod_row(l, st, 1))
    norm2 = lambda l, st: (n2_row(l), mod_row(l, st, 3), mod_row(l, st, 4))
    tab_l, tab_c = _tables(s, n_ctx)
    zero_state = (jnp.zeros((2 * ML_HEADS, ML_QK, 2 * ML_V), F32), jnp.zeros((2 * ML_HEADS, 8, LANE), F32))

    w_in_t = jnp.transpose(w_in, (0, 2, 1))
    wb = w_branch.astype(CD)
    wo = w_out.astype(CD)
    wf2 = w_ffn_out.astype(CD)

    hl = norm_mod(xl, vt, norm1(0, 0), CD)
    hc = norm_mod(xc, vt, norm1(0, 1), CD)
    for l in range(depth):
        last = l == depth - 1
        w = _layer_weights(w_in_t, l, mla_w_uq[l], mla_w_ukv[l])
        gq = mla_q_norm_g[l].reshape(1, -1).astype(F32)
        gkv = mla_kv_norm_g[l].reshape(1, -1).astype(F32)
        sink = sw_sink[l].reshape(1, -1).astype(F32)

        ml_l, na_l, sw_l, mla_lat = _project(hl, w, tab_l, gq, gkv, s + n_ctx)
        ml_c, na_c, sw_c, (mq, mk, mv) = _project(hc, w, tab_c, gq, gkv, s + n_ctx, mla_lat)

        qc_c, kc_c, gt_c = ml_prep(ml_c, ml_conv_w[l], ml_gate_b[l])
        qc_l, kc_l, gt_l = ml_prep(ml_l, ml_conv_w[l], ml_gate_b[l])
        hf_c, hb_c, cst, mst = ml_scan(qc_c, kc_c, ml_c, gt_c, zero_state)
        hf_l, hb_l, _, _ = ml_scan(qc_l, kc_l, ml_l, gt_l, (cst, mst))
        ya_l = ml_finish(hf_l, hb_l, ml_l, ml_norm_g[l])
        yb_l = na_attn(na_l, na_c, na_rpb[l], rows)
        yc_l = sw_attn(sw_l, sw_c, sink)
        yd_l = mla_flash(mq, mk, mv, s)

        nxt_l = (fin_row, zero_row, zero_row) if last else norm1(l + 1, 0)
        xl_new, hl_new = _dense_tail(xl, hl, (ya_l, yb_l, yc_l, yd_l), w, l, wb, wo, w_ffn_in, wf2, vt,
                                     mod_row(l, 0, 2), norm2(l, 0), mod_row(l, 0, 5), nxt_l, last)
        if not last:
            ya_c = ml_finish(hf_c, hb_c, ml_c, ml_norm_g[l])
            nw = NA_HEADS * NA_DIM
            yb_c = dense_attn([(na_c, nw, 0), (na_c, nw, 1), (na_c, nw, 2)], NA_HEADS, 1, NA_DIM, NA_DIM,
                              n_ctx, log2_scores=True)
            kcol = SW_HEADS * SW_DIM // LANE
            yc_c = dense_attn([(sw_c, SW_HEADS * SW_DIM, 0), (sw_c, LANE, kcol), (sw_c, LANE, kcol + 1)],
                              SW_HEADS, SW_HEADS // SW_KV_HEADS, SW_DIM, SW_DIM, n_ctx, sink=sink,
                              log2_scores=True)
            assert s % n_ctx == 0
            yd_c = dense_attn([(mq, MLA_HEADS * MLA_HW, 0), (mk, MLA_HEADS * MLA_HW, 0),
                               (mv, MLA_HEADS * MLA_V, 0)], MLA_HEADS, 1, MLA_HW, MLA_V, n_ctx,
                              log2_scores=True, row_block=s // n_ctx)
            xc, hc = _dense_tail(xc, hc, (ya_c, yb_c, yc_c, yd_c), w, l, wb, wo, w_ffn_in, wf2, vt,
                                 mod_row(l, 1, 2), norm2(l, 1), mod_row(l, 1, 5), norm1(l + 1, 1), False)
        xl, hl = xl_new, hl_new

    return hl[None].astype(x.dtype)
```

```python
import functools

import numpy as np
import jax
import jax.numpy as jnp
from jax import lax
from jax.experimental import pallas as pl
from jax.experimental.pallas import tpu as pltpu

F32 = jnp.float32
CD = jnp.bfloat16

GRID_W = 64
EPS = 1e-6
ROPE_THETA = 10000.0
ML_HEADS, ML_QK, ML_V, ML_CONV = 4, 64, 128, 3
NA_HEADS, NA_DIM, NA_WIN_R, NA_WIN_C = 8, 64, 8, 16
SW_HEADS, SW_KV_HEADS, SW_DIM, SW_WINDOW = 8, 2, 64, 128
MLA_HEADS, MLA_Q_RANK, MLA_KV_RANK, MLA_NOPE, MLA_ROPE, MLA_V = 4, 384, 256, 128, 64, 128
N_BRANCH, BRANCH_W = 4, 512

ML_QKW = ML_HEADS * ML_QK
ML_VW = ML_HEADS * ML_V
ML_COLS = 2 * ML_QKW + 2 * ML_VW + 4 * ML_HEADS
ML_PAD = -(-ML_COLS // 128) * 128
NA_COLS = 3 * NA_HEADS * NA_DIM
SW_COLS = (SW_HEADS + 2 * SW_KV_HEADS) * SW_DIM
MLA_COLS = MLA_Q_RANK + MLA_KV_RANK + MLA_ROPE
MLA_HW = 256
LANE = 128
NEG = -1e30
LOG2E = float(np.log2(np.e))
VMEM_LIMIT = 56 * 1024 * 1024

NA_QROWS = 4
SW_TQ = 256
ML_CHUNK = 256
MLA_TK = 1280


def _cparams(*sem):
    return pltpu.CompilerParams(dimension_semantics=sem, vmem_limit_bytes=VMEM_LIMIT)


def _row_tile(t, pref=512):
    tm = min(pref, t)
    assert t % tm == 0
    return tm


def _dot(a, b):
    return jnp.dot(a, b, preferred_element_type=F32)


def _dot_nt(a, b):
    return lax.dot_general(a, b, (((1,), (1,)), ((), ())), preferred_element_type=F32)


def _sigmoid(x):
    return 1.0 / (1.0 + jnp.exp(-x))


def _lanes(col, n):
    tile = jnp.broadcast_to(col, (col.shape[0], LANE))
    return tile if n == LANE else jnp.tile(tile, (1, n // LANE))


def _ones_col(n, w):
    return jnp.where(lax.broadcasted_iota(jnp.int32, (n, w), 1) == 0, 1.0, 0.0).astype(CD)


def _adaln_kernel(c_ref, w_ref, b_ref, o_ref):
    w = w_ref[0]
    for r in range(2):
        c = c_ref[:, r:r + 1]
        o_ref[0, r:r + 1, :] = jnp.sum((c * _sigmoid(c)) * w, axis=0, keepdims=True) + b_ref[0]


def adaln(cc, w_ada, b_ada):
    nl, d, n = w_ada.shape
    tn = 1024
    return pl.pallas_call(
        _adaln_kernel,
        grid=(nl, n // tn),
        in_specs=[pl.BlockSpec((d, 2), lambda l, j: (0, 0)),
                  pl.BlockSpec((1, d, tn), lambda l, j: (l, 0, j)),
                  pl.BlockSpec((1, 1, tn), lambda l, j: (l, 0, j))],
        out_specs=pl.BlockSpec((1, 2, tn), lambda l, j: (l, 0, j)),
        out_shape=jax.ShapeDtypeStruct((nl, 2, n), F32),
        compiler_params=_cparams("parallel", "parallel"),
        name="adaln",
    )(cc, w_ada, b_ada.reshape(nl, 1, n))


def _vec_spec(d, idx):
    return pl.BlockSpec((1, 1, d), lambda *_: (idx, 0, 0))


def _norm_mod(x, g, shift, scale):
    y = x * lax.rsqrt(jnp.mean(x * x, axis=-1, keepdims=True) + EPS)
    return (y * g) * (1.0 + scale) + shift


def _norm_mod_kernel(x_ref, g_ref, sh_ref, sc_ref, o_ref):
    o_ref[...] = _norm_mod(x_ref[...], g_ref[0], sh_ref[0], sc_ref[0]).astype(o_ref.dtype)


def norm_mod(x, vt, norm_idx, out_dtype):
    t, d = x.shape
    tm = _row_tile(t, 1024)
    return pl.pallas_call(
        _norm_mod_kernel,
        grid=(t // tm,),
        in_specs=[pl.BlockSpec((tm, d), lambda i: (i, 0))] + [_vec_spec(d, ix) for ix in norm_idx],
        out_specs=pl.BlockSpec((tm, d), lambda i: (i, 0)),
        out_shape=jax.ShapeDtypeStruct((t, d), out_dtype),
        compiler_params=_cparams("parallel"),
        name="norm_mod",
    )(x, vt, vt, vt)


def _mm_kernel(a_ref, b_ref, o_ref):
    o_ref[...] = _dot_nt(a_ref[...], b_ref[...]).astype(o_ref.dtype)


def matmul(a, bt, out_dtype, tn=None):
    t, k = a.shape
    n = bt.shape[0]
    tn = n if tn is None else tn
    tm = _row_tile(t, 1024)
    return pl.pallas_call(
        _mm_kernel,
        grid=(n // tn, t // tm),
        in_specs=[pl.BlockSpec((tm, k), lambda j, i: (i, 0)),
                  pl.BlockSpec((tn, k), lambda j, i: (j, 0))],
        out_specs=pl.BlockSpec((tm, tn), lambda j, i: (i, j)),
        out_shape=jax.ShapeDtypeStruct((t, n), out_dtype),
        compiler_params=_cparams("parallel", "parallel"),
        name="matmul",
    )(a, bt)


def _mm_res_norm_kernel(nj, tn, want_x, a_ref, b_ref, x_ref, gate_ref, g_ref, sh_ref, sc_ref, *rest):
    xbuf, ho_ref = (rest[0], rest[1]) if want_x else (rest[1], rest[0])
    j = pl.program_id(1)
    acc = _dot(a_ref[...], b_ref[0])
    for jj in range(nj):
        @pl.when(j == jj)
        def _(jj=jj):
            sl = slice(jj * tn, (jj + 1) * tn)
            xbuf[:, sl] = x_ref[...] + gate_ref[0][:, sl] * acc

    @pl.when(j == nj - 1)
    def _():
        ho_ref[...] = _norm_mod(xbuf[...], g_ref[0], sh_ref[0], sc_ref[0]).astype(ho_ref.dtype)


def matmul_residual_norm(a, b, l, x, vt, gate_idx, norm_idx, want_x, h_dtype, tn=None):
    t, kdim = a.shape
    n = b.shape[2]
    tn = n if tn is None else tn
    nj = n // tn
    tm = _row_tile(t)
    row = pl.BlockSpec((tm, n), lambda i, j: (i, 0))
    out_shape = [jax.ShapeDtypeStruct((t, n), h_dtype)]
    if want_x:
        out_shape = [jax.ShapeDtypeStruct((t, n), F32)] + out_shape
    out = pl.pallas_call(
        functools.partial(_mm_res_norm_kernel, nj, tn, want_x),
        grid=(t // tm, nj),
        in_specs=[pl.BlockSpec((tm, kdim), lambda i, j: (i, 0)),
                  pl.BlockSpec((1, kdim, tn), lambda i, j: (l, 0, j)),
                  pl.BlockSpec((tm, tn), lambda i, j: (i, j)),
                  _vec_spec(n, gate_idx)] + [_vec_spec(n, ix) for ix in norm_idx],
        out_specs=[row] * len(out_shape),
        out_shape=out_shape,
        scratch_shapes=[] if want_x else [pltpu.VMEM((tm, n), F32)],
        compiler_params=_cparams("parallel", "arbitrary"),
        name="matmul_residual_norm",
    )(a, b, x, vt, vt, vt, vt)
    return out if want_x else out[0]


def _rope_mm_kernel(n_rope, a_ref, w_ref, wr_ref, cos_ref, sin_ref, o_ref):
    a = a_ref[...]
    p = _dot_nt(a, w_ref[...])
    pr = _dot_nt(a, wr_ref[...])
    cos = cos_ref[...]
    sin = sin_ref[...]
    for j in range(p.shape[1] // LANE):
        sl = slice(j * LANE, (j + 1) * LANE)
        if j < n_rope:
            o_ref[:, sl] = (p[:, sl] * cos + pr[:, sl] * sin).astype(o_ref.dtype)
        else:
            o_ref[:, sl] = p[:, sl].astype(o_ref.dtype)


def rope_matmul(a, w, w_rot, cos, sin, n_rope):
    t, k = a.shape
    n = w.shape[0]
    nr = n_rope * LANE
    tm = _row_tile(t, 1024)
    return pl.pallas_call(
        functools.partial(_rope_mm_kernel, n_rope),
        grid=(t // tm,),
        in_specs=[pl.BlockSpec((tm, k), lambda i: (i, 0)),
                  pl.BlockSpec((n, k), lambda i: (0, 0)),
                  pl.BlockSpec((nr, k), lambda i: (0, 0)),
                  pl.BlockSpec((tm, LANE), lambda i: (i, 0)),
                  pl.BlockSpec((tm, LANE), lambda i: (i, 0))],
        out_specs=pl.BlockSpec((tm, n), lambda i: (i, 0)),
        out_shape=jax.ShapeDtypeStruct((t, n), CD),
        compiler_params=_cparams("parallel"),
        name="rope_matmul",
    )(a, w, w_rot, cos, sin)


def _mla_proj_kernel(h_ref, w1_ref, gq_ref, gkv_ref, wq_ref, wqr_ref, wkv_ref, cos_ref, sin_ref,
                     q_out, k_out, v_out):
    p = _dot_nt(h_ref[...], w1_ref[...])
    cq = p[:, :MLA_Q_RANK]
    ckv = p[:, MLA_Q_RANK:MLA_Q_RANK + MLA_KV_RANK]
    o = MLA_Q_RANK + MLA_KV_RANK
    kr = p[:, o:o + LANE]
    krr = p[:, o + LANE:o + 2 * LANE]
    cqn = (cq * lax.rsqrt(jnp.mean(cq * cq, axis=-1, keepdims=True) + EPS) * gq_ref[...]).astype(CD)
    ckvn = (ckv * lax.rsqrt(jnp.mean(ckv * ckv, axis=-1, keepdims=True) + EPS) * gkv_ref[...]).astype(CD)
    qa = _dot(cqn, wq_ref[...])
    qb = _dot(cqn, wqr_ref[...])
    kv = _dot(ckvn, wkv_ref[...])
    rope_lane = lax.broadcasted_iota(jnp.int32, cos_ref.shape, 1) < MLA_ROPE
    ck = jnp.where(rope_lane, cos_ref[...], 0.0)
    sk = jnp.where(rope_lane, sin_ref[...], 0.0)
    krp = (kr * ck + krr * sk).astype(CD)
    scale = (MLA_NOPE + MLA_ROPE) ** -0.5 * LOG2E
    cq_t = jnp.concatenate([jnp.full(ck.shape, scale, F32), ck * scale], axis=1)
    sq_t = jnp.concatenate([jnp.zeros(sk.shape, F32), sk * scale], axis=1)
    for h in range(MLA_HEADS):
        qs = slice(h * MLA_HW, (h + 1) * MLA_HW)
        q_out[:, qs] = (qa[:, qs] * cq_t + qb[:, qs] * sq_t).astype(CD)
        k_out[:, h * MLA_HW:h * MLA_HW + MLA_NOPE] = kv[:, h * 256:h * 256 + MLA_NOPE].astype(CD)
        k_out[:, h * MLA_HW + MLA_NOPE:(h + 1) * MLA_HW] = krp
        v_out[:, h * MLA_V:(h + 1) * MLA_V] = kv[:, h * 256 + MLA_NOPE:(h + 1) * 256].astype(CD)


def mla_proj(h, w1, gq, gkv, wq, wqr, wkv, cos, sin, total_rows, into=None):
    t, d = h.shape
    tm = _row_tile(t)
    full = lambda a: pl.BlockSpec(a.shape, lambda i: (0,) * a.ndim)
    rows = lambda w: pl.BlockSpec((tm, w), lambda i: (i, 0))
    hw = MLA_HEADS * MLA_HW
    vw = MLA_HEADS * MLA_V
    args = [h, w1, gq, gkv, wq, wqr, wkv, cos, sin]
    in_specs = [rows(d), full(w1), full(gq), full(gkv), full(wq), full(wqr), full(wkv),
                rows(LANE), rows(LANE)]
    if into is None:
        off, aliases, kern = 0, {}, _mla_proj_kernel
    else:
        assert (total_rows - t) % tm == 0
        off = (total_rows - t) // tm
        aliases = {len(args) + i: i for i in range(3)}
        in_specs = in_specs + [pl.BlockSpec(memory_space=pl.ANY)] * 3
        args = args + list(into)
        kern = lambda *refs: _mla_proj_kernel(*refs[:9], *refs[12:])
    orow = lambda w: pl.BlockSpec((tm, w), lambda i: (off + i, 0))
    return pl.pallas_call(
        kern,
        grid=(t // tm,),
        in_specs=in_specs,
        out_specs=[orow(hw), orow(hw), orow(vw)],
        out_shape=[jax.ShapeDtypeStruct((total_rows, hw), CD), jax.ShapeDtypeStruct((total_rows, hw), CD),
                   jax.ShapeDtypeStruct((total_rows, vw), CD)],
        input_output_aliases=aliases,
        compiler_params=_cparams("parallel"),
        name="mla_proj",
    )(*args)


def _flash_kernel(tq, tk, unroll, q_ref, k_ref, v_ref, o_ref, s_sc, m_sc, acc_sc):
    nkb = k_ref.shape[0] // tk
    nsub = q_ref.shape[0] // tq
    total = nsub * nkb
    m_sc[...] = jnp.full(m_sc.shape, NEG, F32)
    acc_sc[...] = jnp.zeros(acc_sc.shape, F32)
    ones_col = jnp.where(lax.broadcasted_iota(jnp.int32, (tk, LANE), 1) == 0, 1.0, 0.0).astype(CD)

    def split(b):
        sub = b // nkb
        return sub, b - sub * nkb

    def scores(b):
        sub, j = split(b)
        q = q_ref[pl.ds(pl.multiple_of(sub * tq, tq), tq), :]
        k = k_ref[pl.ds(pl.multiple_of(j * tk, tk), tk), :]
        return _dot_nt(q, k)

    def softmax_pv(s, b):
        sub, j = split(b)
        v = v_ref[pl.ds(pl.multiple_of(j * tk, tk), tk), :]
        m_prev = m_sc[sub]
        m_new = jnp.maximum(m_prev, jnp.max(s, axis=-1, keepdims=True))
        alpha = jnp.exp2(m_prev - m_new)
        p = jnp.exp2(s - m_new).astype(CD)
        v_aug = jnp.concatenate([v, ones_col], axis=1)
        acc_sc[sub] = alpha * acc_sc[sub] + _dot(p, v_aug)
        m_sc[sub] = m_new

    s_sc[0] = scores(0)

    def body(g, carry):
        b0 = g * unroll
        for u in range(unroll):
            s_sc[(u + 1) % 2] = scores(jnp.minimum(b0 + u + 1, total - 1))
            softmax_pv(s_sc[u % 2], b0 + u)
        return carry

    lax.fori_loop(0, total // unroll, body, 0)
    for sub in range(nsub):
        acc = acc_sc[sub]
        o_ref[sub * tq:(sub + 1) * tq, :] = (acc[:, :MLA_V] / acc[:, MLA_V:MLA_V + 1]).astype(o_ref.dtype)


def mla_flash(q, k, v, s):
    nk = k.shape[0]
    tq = _row_tile(s, 512)
    tstep = _row_tile(s, 16 * tq)
    tk = MLA_TK
    total = (tstep // tq) * (nk // tk)
    assert nk % tk == 0 and total % 2 == 0
    unroll = 16 if total % 16 == 0 else (4 if total % 4 == 0 else 2)
    return pl.pallas_call(
        functools.partial(_flash_kernel, tq, tk, unroll),
        grid=(MLA_HEADS, s // tstep),
        in_specs=[pl.BlockSpec((tstep, MLA_HW), lambda h, i: (i, h)),
                  pl.BlockSpec((nk, MLA_HW), lambda h, i: (0, h), pipeline_mode=pl.Buffered(1)),
                  pl.BlockSpec((nk, MLA_V), lambda h, i: (0, h), pipeline_mode=pl.Buffered(1))],
        out_specs=pl.BlockSpec((tstep, MLA_V), lambda h, i: (i, h)),
        out_shape=jax.ShapeDtypeStruct((s, MLA_HEADS * MLA_V), CD),
        scratch_shapes=[pltpu.VMEM((2, tq, tk), F32), pltpu.VMEM((tstep // tq, tq, 1), F32),
                        pltpu.VMEM((tstep // tq, tq, 2 * MLA_V), F32)],
        compiler_params=_cparams("parallel", "arbitrary"),
        name="mla_flash",
    )(q, k, v)


def _dense_attn_kernel(n_heads, group, dqk, dv, log2_scores, q_ref, k_ref, v_ref, sink_ref, o_ref):
    q = q_ref[...]
    k = k_ref[...]
    v = v_ref[...]
    ex = jnp.exp2 if log2_scores else jnp.exp
    for h in range(n_heads):
        g = h // group
        s = _dot_nt(q[:, h * dqk:(h + 1) * dqk], k[:, g * dqk:(g + 1) * dqk])
        m = jnp.max(s, axis=-1, keepdims=True)
        l = jnp.zeros_like(m)
        if sink_ref is not None:
            sk = sink_ref[0:1, h:h + 1] * (LOG2E if log2_scores else 1.0)
            m = jnp.maximum(m, sk)
            l = ex(sk - m)
        p = ex(s - m)
        l = l + jnp.sum(p, axis=-1, keepdims=True)
        o = _dot(p.astype(CD), v[:, g * dv:(g + 1) * dv]) / l
        o_ref[:, h * dv:(h + 1) * dv] = o.astype(o_ref.dtype)


def dense_attn(qkv_specs, n_heads, group, dqk, dv, c, sink=None, log2_scores=False, row_block=0):
    arrays = [a for a, _, _ in qkv_specs]
    specs = [pl.BlockSpec((c, w), functools.partial(lambda b, i: (row_block, b), b)) for _, w, b in qkv_specs]
    if sink is None:
        kern = lambda q, k, v, o: _dense_attn_kernel(n_heads, group, dqk, dv, log2_scores, q, k, v, None, o)
    else:
        kern = functools.partial(_dense_attn_kernel, n_heads, group, dqk, dv, log2_scores)
        arrays.append(sink)
        specs.append(pl.BlockSpec(sink.shape, lambda i: (0, 0)))
    return pl.pallas_call(
        kern,
        grid=(1,),
        in_specs=specs,
        out_specs=pl.BlockSpec((c, n_heads * dv), lambda i: (0, 0)),
        out_shape=jax.ShapeDtypeStruct((c, n_heads * dv), CD),
        compiler_params=_cparams("arbitrary"),
        name="dense_attn",
    )(*arrays)


def _na_kernel(q_ref, kp_ref, ko_ref, kn_ref, vp_ref, vo_ref, vn_ref, kc_ref, vc_ref, bias_ref, o_ref):
    q = q_ref[...]
    kcat = jnp.concatenate([kp_ref[...], ko_ref[...], kn_ref[...], kc_ref[...]], axis=0)
    vcat = jnp.concatenate([vp_ref[...], vo_ref[...], vn_ref[...], vc_ref[...]], axis=0)
    tb = q.shape[0]
    nloc = 3 * tb
    ones = _ones_col(kcat.shape[0], LANE)
    first_half = lax.broadcasted_iota(jnp.int32, (tb, LANE), 1) < NA_DIM
    zero = jnp.zeros((tb, LANE), CD)
    for j in range(NA_HEADS // 2):
        ps = slice(j * LANE, (j + 1) * LANE)
        q2 = q[:, ps]
        k2 = kcat[:, ps]
        v_aug = jnp.concatenate([vcat[:, ps], ones], axis=1)
        outs = []
        for e in range(2):
            qm = jnp.where(first_half, q2, zero) if e == 0 else jnp.where(first_half, zero, q2)
            s = _dot_nt(qm, k2)
            s = jnp.concatenate([s[:, :nloc] + bias_ref[0, 2 * j + e], s[:, nloc:]], axis=1)
            m = jnp.max(s, axis=-1, keepdims=True)
            p = jnp.exp2(s - _lanes(m, s.shape[1])).astype(CD)
            acc = _dot(p, v_aug)
            outs.append(acc[:, :LANE] / _lanes(acc[:, LANE:LANE + 1], LANE))
        o_ref[:, ps] = jnp.where(first_half, outs[0], outs[1]).astype(o_ref.dtype)


def _na_bias(rpb, rows):
    qr = NA_QROWS
    tb = qr * GRID_W
    nb = rows // qr
    nkr = 3 * qr
    rsel = np.zeros((3, qr, nkr, 2 * NA_WIN_R - 1), np.float32)
    for v, b in enumerate((0, 1, nb - 1)):
        for a in range(qr):
            r = qr * b + a
            rs = min(max(r - NA_WIN_R // 2, 0), rows - NA_WIN_R)
            for u in range(nkr):
                kr = qr * (b - 1) + u
                if 0 <= b - 1 + u // qr < nb and rs <= kr < rs + NA_WIN_R:
                    rsel[v, a, u, kr - r + NA_WIN_R - 1] = 1.0
    csel = np.zeros((GRID_W, GRID_W, 2 * NA_WIN_C - 1), np.float32)
    for qc in range(GRID_W):
        cs = min(max(qc - NA_WIN_C // 2, 0), GRID_W - NA_WIN_C)
        for kc in range(cs, cs + NA_WIN_C):
            csel[qc, kc, kc - qc + NA_WIN_C - 1] = 1.0
    valid = (np.einsum('vaud,qkj->vaquk', rsel, csel) > 0).reshape(3, tb, 3 * tb)
    assert 2 * qr == NA_WIN_R
    nd = 2 * NA_WIN_R - 1
    rp = rpb.astype(F32) * LOG2E
    gap = jnp.zeros((NA_HEADS, nd, GRID_W + 1 - (2 * NA_WIN_C - 1)), F32)
    ring = jnp.concatenate([rp[..., NA_WIN_C - 1:], gap, rp[..., :NA_WIN_C - 1]], axis=-1)
    cols = jnp.tile(ring, (1, 1, GRID_W))[..., :GRID_W * GRID_W].reshape(NA_HEADS, nd, GRID_W, GRID_W)
    off = NA_WIN_R - 1 - qr
    slabs = [cols[:, off - a:off - a + nkr].transpose(0, 2, 1, 3).reshape(NA_HEADS, GRID_W, 3 * tb)
             for a in range(qr)]
    band = jnp.stack(slabs, axis=1).reshape(NA_HEADS, tb, 3 * tb)
    return jnp.where(valid[:, None], band[None], NEG)


def na_attn(p_lat, p_ctx, rpb, rows):
    s = p_lat.shape[0]
    c = p_ctx.shape[0]
    w = NA_HEADS * NA_DIM
    tb = NA_QROWS * GRID_W
    nb = s // tb
    assert rows % NA_QROWS == 0 and nb >= 3 and rows >= 2 * NA_WIN_R
    bias = _na_bias(rpb, rows)

    def blk(col, off):
        return pl.BlockSpec((tb, w), lambda i: (jnp.clip(i + off, 0, nb - 1), col))

    return pl.pallas_call(
        _na_kernel,
        grid=(nb,),
        in_specs=[blk(0, 0), blk(1, -1), blk(1, 0), blk(1, 1), blk(2, -1), blk(2, 0), blk(2, 1),
                  pl.BlockSpec((c, w), lambda i: (0, 1)),
                  pl.BlockSpec((c, w), lambda i: (0, 2)),
                  pl.BlockSpec((1, NA_HEADS, tb, 3 * tb),
                               lambda i: (jnp.where(i == 0, 0, jnp.where(i == nb - 1, 2, 1)), 0, 0, 0))],
        out_specs=pl.BlockSpec((tb, w), lambda i: (i, 0)),
        out_shape=jax.ShapeDtypeStruct((s, w), CD),
        compiler_params=_cparams("parallel"),
        name="na_attn",
    )(p_lat, p_lat, p_lat, p_lat, p_lat, p_lat, p_lat, p_ctx, p_ctx, bias)


def _sw_kernel(seq, q_ref, kp_ref, ko_ref, kn_ref, vp_ref, vo_ref, vn_ref, kc_ref, vc_ref, sink_ref, o_ref):
    i = pl.program_id(0)
    q = q_ref[...]
    tq = q.shape[0]
    kcat = jnp.concatenate([kp_ref[...], ko_ref[...], kn_ref[...], kc_ref[...]], axis=0)
    vcat = jnp.concatenate([vp_ref[...], vo_ref[...], vn_ref[...], vc_ref[...]], axis=0)
    nkk = tq + 2 * SW_WINDOW
    r = lax.broadcasted_iota(jnp.int32, (tq, nkk), 0)
    cc = lax.broadcasted_iota(jnp.int32, (tq, nkk), 1)
    kpos = i * tq - SW_WINDOW + cc
    d = cc - r
    mask = (d >= 0) & (d <= 2 * SW_WINDOW) & (kpos >= 0) & (kpos < seq)
    assert SW_KV_HEADS == 2 and SW_DIM * 2 == LANE
    ones = _ones_col(kcat.shape[0], LANE)
    swap = lambda x: jnp.concatenate([x[:, SW_DIM:], x[:, :SW_DIM]], axis=1)
    k_by_order = (kcat, swap(kcat))
    v_by_order = (jnp.concatenate([vcat, ones], axis=1), jnp.concatenate([swap(vcat), ones], axis=1))
    first_half = lax.broadcasted_iota(jnp.int32, (tq, LANE), 1) < SW_DIM
    zero = jnp.zeros((tq, LANE), CD)
    grp = SW_HEADS // SW_KV_HEADS
    for j in range(SW_HEADS // 2):
        ps = slice(j * LANE, (j + 1) * LANE)
        q2 = q[:, ps]
        outs = []
        for e in range(2):
            h = 2 * j + e
            order = 0 if h // grp == e else 1
            qm = jnp.where(first_half, q2, zero) if e == 0 else jnp.where(first_half, zero, q2)
            s = _dot_nt(qm, k_by_order[order])
            s = jnp.concatenate([jnp.where(mask, s[:, :nkk], NEG), s[:, nkk:]], axis=1)
            sk = sink_ref[0:1, h:h + 1] * LOG2E
            m = jnp.maximum(jnp.max(s, axis=-1, keepdims=True), sk)
            p = jnp.exp2(s - _lanes(m, s.shape[1])).astype(CD)
            acc = _dot(p, v_by_order[order])
            outs.append(acc[:, :LANE] / _lanes(acc[:, LANE:LANE + 1] + jnp.exp2(sk - m), LANE))
        o_ref[:, ps] = jnp.where(first_half, outs[0], outs[1]).astype(o_ref.dtype)


def sw_attn(p_lat, p_ctx, sink):
    s = p_lat.shape[0]
    c = p_ctx.shape[0]
    tq = SW_TQ
    assert s % tq == 0 and tq % SW_WINDOW == 0
    nq = s // tq
    per = tq // SW_WINDOW
    nwb = s // SW_WINDOW
    kcol = SW_HEADS * SW_DIM // LANE
    prev = lambda col: pl.BlockSpec((SW_WINDOW, LANE), lambda i: (jnp.maximum(i * per - 1, 0), col))
    own = lambda col: pl.BlockSpec((tq, LANE), lambda i: (i, col))
    nxt = lambda col: pl.BlockSpec((SW_WINDOW, LANE), lambda i: (jnp.minimum((i + 1) * per, nwb - 1), col))
    return pl.pallas_call(
        functools.partial(_sw_kernel, s),
        grid=(nq,),
        in_specs=[pl.BlockSpec((tq, SW_HEADS * SW_DIM), lambda i: (i, 0)),
                  prev(kcol), own(kcol), nxt(kcol), prev(kcol + 1), own(kcol + 1), nxt(kcol + 1),
                  pl.BlockSpec((c, LANE), lambda i: (0, kcol)),
                  pl.BlockSpec((c, LANE), lambda i: (0, kcol + 1)),
                  pl.BlockSpec(sink.shape, lambda i: (0, 0))],
        out_specs=pl.BlockSpec((tq, SW_HEADS * SW_DIM), lambda i: (i, 0)),
        out_shape=jax.ShapeDtypeStruct((s, SW_HEADS * SW_DIM), CD),
        compiler_params=_cparams("parallel"),
        name="sw_attn",
    )(p_lat, p_lat, p_lat, p_lat, p_lat, p_lat, p_lat, p_ctx, p_ctx, sink)


def _ml_prep_kernel(nt, x_ref, xp_ref, xn_ref, g_ref, w_ref, b_ref, q_out, k_out, g_out):
    i = pl.program_id(0)
    x = x_ref[...]
    tm = x.shape[0]
    row = lax.broadcasted_iota(jnp.int32, x.shape, 0)
    has_prev = jnp.where(i > 0, 1.0, 0.0)
    has_next = jnp.where(i < nt - 1, 1.0, 0.0)
    prev_row = xp_ref[7:8, :] * has_prev
    next_row = xn_ref[0:1, :] * has_next
    xm = jnp.where(row == 0, prev_row, pltpu.roll(x, 1, axis=0))
    xq = jnp.where(row == tm - 1, next_row, pltpu.roll(x, tm - 1, axis=0))
    y = w_ref[0:1, :] * xm + w_ref[1:2, :] * x + w_ref[2:3, :] * xq
    y = y * _sigmoid(y)
    q_out[...] = y[:, :ML_QKW].astype(CD)
    k_out[...] = (y[:, ML_QKW:] * (ML_QK ** -0.5)).astype(CD)
    g = g_ref[...] + b_ref[...]
    lane = lax.broadcasted_iota(jnp.int32, g.shape, 1)
    logsig = jnp.minimum(g, 0.0) - jnp.log(1.0 + jnp.exp(-jnp.abs(g)))
    g_out[...] = jnp.where(lane < 2 * ML_HEADS, g, logsig)


def ml_prep(p_ml, conv_w, gate_b):
    t = p_ml.shape[0]
    tm = _row_tile(t, 1024)
    nt = t // tm
    wq = 2 * ML_QKW
    gcol = (2 * ML_QKW + 2 * ML_VW) // LANE
    w8 = jnp.zeros((8, wq), F32).at[:ML_CONV].set(conv_w.astype(F32))
    b = jnp.zeros((1, LANE), F32).at[0, :4 * ML_HEADS].set(gate_b.astype(F32).reshape(-1))
    h8 = tm // 8
    return pl.pallas_call(
        functools.partial(_ml_prep_kernel, nt),
        grid=(nt,),
        in_specs=[pl.BlockSpec((tm, wq), lambda i: (i, 0)),
                  pl.BlockSpec((8, wq), lambda i: (jnp.maximum(i * h8 - 1, 0), 0)),
                  pl.BlockSpec((8, wq), lambda i: (jnp.minimum((i + 1) * h8, t // 8 - 1), 0)),
                  pl.BlockSpec((tm, LANE), lambda i: (i, gcol)),
                  pl.BlockSpec((8, wq), lambda i: (0, 0)),
                  pl.BlockSpec((1, LANE), lambda i: (0, 0))],
        out_specs=[pl.BlockSpec((tm, ML_QKW), lambda i: (i, 0)),
                   pl.BlockSpec((tm, ML_QKW), lambda i: (i, 0)),
                   pl.BlockSpec((tm, LANE), lambda i: (i, 0))],
        out_shape=[jax.ShapeDtypeStruct((t, ML_QKW), CD), jax.ShapeDtypeStruct((t, ML_QKW), CD),
                   jax.ShapeDtypeStruct((t, LANE), F32)],
        compiler_params=_cparams("parallel"),
        name="ml_prep",
    )(p_ml, p_ml, p_ml, p_ml, w8, b)


def _ml_scan_kernel(qf_ref, ktf_ref, vf_ref, gcf_ref, grf_ref,
                    qb_ref, ktb_ref, vb_ref, gcb_ref, grb_ref,
                    c0_ref, m0_ref, hf_ref, hb_ref, c_ref, m_ref):
    @pl.when(pl.program_id(0) == 0)
    def _():
        c_ref[...] = c0_ref[...]
        m_ref[...] = m0_ref[...]

    ln = qf_ref.shape[0]
    r = lax.broadcasted_iota(jnp.int32, (ln, ln), 0)
    c = lax.broadcasted_iota(jnp.int32, (ln, ln), 1)
    lower = r >= c
    upper = c >= r
    lower_f = jnp.where(lower, 1.0, 0.0)
    upper_f = jnp.where(upper, 1.0, 0.0)
    ones_col = jnp.where(lax.broadcasted_iota(jnp.int32, (ln, LANE), 1) == 0, 1.0, 0.0).astype(CD)
    hi = lax.Precision.HIGHEST
    first_half = lax.broadcasted_iota(jnp.int32, (ln, LANE), 1) < ML_QK
    zero_q = jnp.zeros((ln, LANE), CD)
    new_states = []

    streams = ((qf_ref, ktf_ref, vf_ref, gcf_ref, grf_ref, hf_ref, lower, lower_f, upper_f, ln - 1),
               (qb_ref, ktb_ref, vb_ref, gcb_ref, grb_ref, hb_ref, upper, upper_f, lower_f, 0))
    for d, (q_ref, kt_ref, v_ref, gc_ref, gr_ref, h_ref, mask, tri_c, tri_r, last) in enumerate(streams):
        gcol = gc_ref[...]
        grow = gr_ref[...]
        bcol = jnp.dot(tri_c, gcol, precision=hi, preferred_element_type=F32)
        brow = jnp.dot(grow, tri_r, precision=hi, preferred_element_type=F32)
        q = q_ref[...]
        kt = kt_ref[...]
        v = v_ref[...]
        for hd in range(ML_HEADS):
            ch = d * ML_HEADS + hd
            ic = d * ML_HEADS + hd
            fc = 2 * ML_HEADS + ic
            pair, half = hd // 2, hd % 2
            b_c = bcol[:, fc:fc + 1]
            r_row = grow[ic:ic + 1, :] - brow[fc:fc + 1, :]
            m_prev = m_ref[ch, 0:1, 0:1]
            cst = c_ref[ch]
            rmat = jnp.where(mask, r_row, NEG)
            mx = jnp.maximum(jnp.max(rmat, axis=-1, keepdims=True), m_prev)
            q2 = q[:, pair * LANE:(pair + 1) * LANE]
            qm = jnp.where(first_half, q2, zero_q) if half == 0 else jnp.where(first_half, zero_q, q2)
            s = _dot(qm, kt[pair * LANE:(pair + 1) * LANE, :]) * jnp.exp(rmat - mx)
            w_inter = jnp.exp(m_prev - mx)
            v_aug = jnp.concatenate([v[:, hd * ML_V:(hd + 1) * ML_V].astype(CD), ones_col], axis=1)
            c_pair = jnp.concatenate([c_ref[ch - half], c_ref[ch - half + 1]], axis=0).astype(CD)
            num = _dot(s.astype(CD), v_aug) + w_inter * _dot(qm, c_pair)
            den = num[:, ML_V:ML_V + 1]
            h = num[:, :ML_V] / jnp.maximum(jnp.abs(den), jnp.exp(-(b_c + mx)))
            h_ref[:, hd * ML_V:(hd + 1) * ML_V] = h
            b_end = b_c[last:last + 1, :]
            m_new = b_end + jnp.maximum(m_prev, jnp.max(r_row, axis=-1, keepdims=True))
            w_s = jnp.exp(b_end + r_row - m_new)
            w_c = jnp.exp(b_end + m_prev - m_new)
            kth = kt[hd * ML_QK:(hd + 1) * ML_QK, :]
            ktw = (kth.astype(F32) * w_s).astype(CD)
            new_states.append((ch, w_c * cst + _dot(ktw, v_aug), m_new))
    for ch, c_new, m_new in new_states:
        c_ref[ch] = c_new
        m_ref[ch] = jnp.broadcast_to(m_new, m_ref.shape[1:])


def ml_scan(qc, kc, p_ml, gates, state):
    t = qc.shape[0]
    ln = min(ML_CHUNK, t)
    assert t % ln == 0
    nc = t // ln
    kt = kc.T
    gr = gates[:, :4 * ML_HEADS].T
    c0, m0 = state
    vcol = 2 * ML_QKW // ML_VW
    fwd = lambda j: j
    bwd = lambda j: nc - 1 - j

    def specs(ix):
        return [pl.BlockSpec((ln, ML_QKW), lambda j: (ix(j), 0)),
                pl.BlockSpec((ML_QKW, ln), lambda j: (0, ix(j))),
                pl.BlockSpec((ln, ML_VW), lambda j: (ix(j), vcol)),
                pl.BlockSpec((ln, LANE), lambda j: (ix(j), 0)),
                pl.BlockSpec((4 * ML_HEADS, ln), lambda j: (0, ix(j)))]

    st_specs = [pl.BlockSpec(c0.shape, lambda j: (0, 0, 0)), pl.BlockSpec(m0.shape, lambda j: (0, 0, 0))]
    return pl.pallas_call(
        _ml_scan_kernel,
        grid=(nc,),
        in_specs=specs(fwd) + specs(bwd) + st_specs,
        out_specs=[pl.BlockSpec((ln, ML_VW), lambda j: (fwd(j), 0)),
                   pl.BlockSpec((ln, ML_VW), lambda j: (bwd(j), 0))] + st_specs,
        out_shape=[jax.ShapeDtypeStruct((t, ML_VW), F32), jax.ShapeDtypeStruct((t, ML_VW), F32),
                   jax.ShapeDtypeStruct(c0.shape, F32), jax.ShapeDtypeStruct(m0.shape, F32)],
        compiler_params=_cparams("arbitrary"),
        name="ml_scan",
    )(qc, kt, p_ml, gates, gr, qc, kt, p_ml, gates, gr, c0, m0)


def _ml_finish_kernel(hf_ref, hb_ref, o_ref, g_ref, y_ref):
    h = hf_ref[...] + hb_ref[...]
    og = _sigmoid(o_ref[...])
    g = g_ref[...]
    for hd in range(ML_HEADS):
        sl = slice(hd * ML_V, (hd + 1) * ML_V)
        hh = h[:, sl]
        y = hh * lax.rsqrt(jnp.mean(hh * hh, axis=-1, keepdims=True) + EPS) * g[:, sl]
        y_ref[:, sl] = (y * og[:, sl]).astype(y_ref.dtype)


def ml_finish(hf, hb, p_ml, norm_g):
    t = hf.shape[0]
    tm = _row_tile(t, 1024)
    ocol = (2 * ML_QKW + ML_VW) // ML_VW
    blk = pl.BlockSpec((tm, ML_VW), lambda i: (i, 0))
    return pl.pallas_call(
        _ml_finish_kernel,
        grid=(t // tm,),
        in_specs=[blk, blk, pl.BlockSpec((tm, ML_VW), lambda i: (i, ocol)),
                  pl.BlockSpec((1, ML_VW), lambda i: (0, 0))],
        out_specs=blk,
        out_shape=jax.ShapeDtypeStruct((t, ML_VW), CD),
        compiler_params=_cparams("parallel"),
        name="ml_finish",
    )(hf, hb, p_ml, norm_g.reshape(1, ML_VW).astype(F32))


def _merge_kernel(h_ref, ya_ref, yb_ref, yc_ref, yd_ref, g0_ref, g1_ref, g2_ref, g3_ref, wb_ref, o_ref):
    h = h_ref[...]
    acc = None
    for i, (y_ref, g_ref) in enumerate(zip((ya_ref, yb_ref, yc_ref, yd_ref), (g0_ref, g1_ref, g2_ref, g3_ref))):
        term = _sigmoid(_dot_nt(h, g_ref[...])) * _dot(y_ref[...], wb_ref[0, i])
        acc = term if acc is None else acc + term
    o_ref[...] = acc.astype(o_ref.dtype)


def gated_merge(h, ys, w_gate, w_branch, l, tn=512):
    t, d = h.shape
    tm = _row_tile(t, 1024)
    nj = d // tn
    yspec = pl.BlockSpec((tm, BRANCH_W), lambda j, i: (i, 0))
    gspec = lambda b: pl.BlockSpec((tn, d), lambda j, i: (b * nj + j, 0))
    return pl.pallas_call(
        _merge_kernel,
        grid=(nj, t // tm),
        in_specs=[pl.BlockSpec((tm, d), lambda j, i: (i, 0)), yspec, yspec, yspec, yspec,
                  gspec(0), gspec(1), gspec(2), gspec(3),
                  pl.BlockSpec((1, N_BRANCH, BRANCH_W, tn), lambda j, i: (l, 0, 0, j))],
        out_specs=pl.BlockSpec((tm, tn), lambda j, i: (i, j)),
        out_shape=jax.ShapeDtypeStruct((t, d), CD),
        compiler_params=_cparams("parallel", "parallel"),
        name="gated_merge",
    )(h, *ys, w_gate, w_gate, w_gate, w_gate, w_branch)


def _ffn1_kernel(h_ref, wa_ref, wg_ref, o_ref, wa_sc, wg_sc):
    @pl.when(pl.program_id(1) == 0)
    def _():
        wa_sc[...] = wa_ref[0].astype(CD)
        wg_sc[...] = wg_ref[0].astype(CD)

    h = h_ref[...]
    a = _dot(h, wa_sc[...])
    g = _dot(h, wg_sc[...])
    o_ref[...] = (a * _sigmoid(a) * g).astype(o_ref.dtype)


def ffn_up(h, w, l, tn=512):
    t, d = h.shape
    dff = w.shape[2] // 2
    tm = _row_tile(t, 1024)
    nj = dff // tn
    return pl.pallas_call(
        _ffn1_kernel,
        grid=(nj, t // tm),
        in_specs=[pl.BlockSpec((tm, d), lambda j, i: (i, 0)),
                  pl.BlockSpec((1, d, tn), lambda j, i: (l, 0, j)),
                  pl.BlockSpec((1, d, tn), lambda j, i: (l, 0, nj + j))],
        out_specs=pl.BlockSpec((tm, tn), lambda j, i: (i, j)),
        out_shape=jax.ShapeDtypeStruct((t, dff), CD),
        scratch_shapes=[pltpu.VMEM((d, tn), CD), pltpu.VMEM((d, tn), CD)],
        compiler_params=_cparams("parallel", "arbitrary"),
        name="ffn_up",
    )(h, w, w)


def _rot_cols(w, dim):
    k, n = w.shape
    w4 = w.reshape(k, n // dim, 2, dim // 2)
    return jnp.stack([-w4[:, :, 1], w4[:, :, 0]], axis=2).reshape(k, n)


def _rope_tables(n_tokens, dim):
    a = dim // 2
    inv = 1.0 / (ROPE_THETA ** (jnp.arange(0, a, 2, dtype=F32) / a))
    rows = n_tokens // GRID_W
    assert rows * GRID_W == n_tokens
    ang_r = jnp.arange(rows, dtype=F32)[:, None] * inv
    ang_c = jnp.arange(GRID_W, dtype=F32)[:, None] * inv

    def expand(fr, fc):
        shape = (rows, GRID_W, a // 2)
        return jnp.concatenate([jnp.broadcast_to(fr[:, None, :], shape), jnp.broadcast_to(fc[None], shape)],
                               axis=-1).reshape(n_tokens, a)

    return expand(jnp.cos(ang_r), jnp.cos(ang_c)), expand(jnp.sin(ang_r), jnp.sin(ang_c))


_O_NA = ML_COLS
_O_SW = _O_NA + NA_COLS
_O_MLA = _O_SW + SW_COLS
_O_KR = _O_MLA + MLA_Q_RANK + MLA_KV_RANK
_O_GATE = _O_MLA + MLA_COLS


def _rot_half_rows(x, dim):
    half = dim // 2
    parts = []
    for b in range(x.shape[0] // dim):
        parts += [-x[b * dim + half:(b + 1) * dim], x[b * dim:b * dim + half]]
    return jnp.concatenate(parts, axis=0)


def _repack_kernel(w_ref, ml_ref, na_ref, sw_ref, swr_ref, mla_ref, gate_ref):
    w = w_ref[0]
    cols = w.shape[1]
    zeros = lambda n: jnp.zeros((n, cols), F32)
    ml_ref[...] = jnp.concatenate([w[:ML_COLS], zeros(ML_PAD - ML_COLS)], axis=0).astype(CD)
    nq = NA_HEADS * NA_DIM
    na_ref[...] = jnp.concatenate([w[_O_NA:_O_NA + nq] * (NA_DIM ** -0.5 * LOG2E), w[_O_NA + nq:_O_SW]],
                                  axis=0).astype(CD)
    sq = SW_HEADS * SW_DIM
    skv = SW_KV_HEADS * SW_DIM
    sw_q = w[_O_SW:_O_SW + sq] * (SW_DIM ** -0.5 * LOG2E)
    sw_k = w[_O_SW + sq:_O_SW + sq + skv]
    sw_ref[...] = jnp.concatenate([sw_q, w[_O_SW + sq:_O_MLA]], axis=0).astype(CD)
    swr_ref[...] = _rot_half_rows(jnp.concatenate([sw_q, sw_k], axis=0), SW_DIM).astype(CD)
    kr = w[_O_KR:_O_GATE]
    pad = zeros(LANE - MLA_ROPE)
    mla_ref[...] = jnp.concatenate([w[_O_MLA:_O_KR], kr, pad, _rot_half_rows(kr, MLA_ROPE), pad], axis=0).astype(CD)
    gate_ref[...] = w[_O_GATE:].astype(CD)


def _repack_w_in(w_in_t, l):
    _, n, d = w_in_t.shape
    tc = 256
    heights = (ML_PAD, NA_COLS, SW_COLS, (SW_HEADS + SW_KV_HEADS) * SW_DIM,
               MLA_Q_RANK + MLA_KV_RANK + 2 * LANE, n - _O_GATE)
    return pl.pallas_call(
        _repack_kernel,
        grid=(d // tc,),
        in_specs=[pl.BlockSpec((1, n, tc), lambda i: (l, 0, i))],
        out_specs=[pl.BlockSpec((ht, tc), lambda i: (0, i)) for ht in heights],
        out_shape=[jax.ShapeDtypeStruct((ht, d), CD) for ht in heights],
        compiler_params=_cparams("parallel"),
        name="repack_w_in",
    )(w_in_t)


def _layer_weights(w_in_t, l, mla_w_uq_l, mla_w_ukv_l):
    w = {}
    w["ml"], w["na"], w["sw"], w["sw_rot"], w["mla1"], w["gate"] = _repack_w_in(w_in_t, l)
    uq = mla_w_uq_l.reshape(MLA_Q_RANK, MLA_HEADS, MLA_NOPE + MLA_ROPE)
    zq = jnp.zeros((MLA_Q_RANK, MLA_HEADS, MLA_HW - MLA_NOPE - MLA_ROPE), F32)
    w["mla_q"] = jnp.concatenate([uq, zq], axis=2).reshape(MLA_Q_RANK, -1).astype(CD)
    uqr = _rot_cols(uq[:, :, MLA_NOPE:].reshape(MLA_Q_RANK, -1), MLA_ROPE).reshape(MLA_Q_RANK, MLA_HEADS, MLA_ROPE)
    w["mla_q_rot"] = jnp.concatenate([jnp.zeros_like(uq[:, :, :MLA_NOPE]), uqr, zq], axis=2
                                     ).reshape(MLA_Q_RANK, -1).astype(CD)
    w["mla_kv"] = mla_w_ukv_l.astype(CD)
    return w


def _tables(s, c):
    assert SW_DIM == MLA_ROPE
    cos, sin = _rope_tables(s, SW_DIM)
    tile4 = lambda a: jnp.concatenate([a, a, a, a], axis=1)
    return ((tile4(cos), tile4(sin)), (jnp.ones((c, LANE), F32), jnp.zeros((c, LANE), F32)))


def _project(h, w, tab, gq, gkv, total_rows, mla_into=None):
    p_ml = matmul(h, w["ml"], F32)
    p_na = matmul(h, w["na"], CD)
    n_rope = (SW_HEADS + SW_KV_HEADS) * SW_DIM // LANE
    p_sw = rope_matmul(h, w["sw"], w["sw_rot"], tab[0], tab[1], n_rope)
    mla = mla_proj(h, w["mla1"], gq, gkv, w["mla_q"], w["mla_q_rot"], w["mla_kv"],
                   tab[0], tab[1], total_rows, mla_into)
    return p_ml, p_na, p_sw, mla


def _dense_tail(x, h, ys, w, l, w_branch, w_out, w_ffn_in, w_ffn_out, vt, g1, norm2, g2, norm_next, last):
    merged = gated_merge(h, ys, w["gate"], w_branch, l)
    x, h2 = matmul_residual_norm(merged, w_out, l, x, vt, g1, norm2, True, CD)
    u = ffn_up(h2, w_ffn_in, l)
    if last:
        return None, matmul_residual_norm(u, w_ffn_out, l, x, vt, g2, norm_next, False, F32, tn=1024)
    return matmul_residual_norm(u, w_ffn_out, l, x, vt, g2, norm_next, True, CD, tn=1024)


def kernel(x, c, ctx, c_ctx, w_ada, b_ada, norm1_g, norm2_g, w_in, ml_conv_w, ml_gate_b, ml_norm_g, na_rpb,
           sw_sink, mla_q_norm_g, mla_kv_norm_g, mla_w_uq, mla_w_ukv, w_branch, w_out, w_ffn_in, w_ffn_out,
           final_norm_g):
    bsz, s, d = x.shape
    assert bsz == 1
    depth = w_in.shape[0]
    n_ctx = ctx.shape[1]
    rows = s // GRID_W
    xl = x[0].astype(F32)
    xc = ctx[0].astype(F32)
    mod = adaln(jnp.stack([c[0], c_ctx], axis=1).astype(F32), w_ada, b_ada)
    vt = jnp.concatenate([mod.reshape(depth * 12, d), norm1_g.astype(F32), norm2_g.astype(F32),
                          final_norm_g.reshape(1, d).astype(F32), jnp.zeros((1, d), F32)]).reshape(-1, 1, d)
    mod_row = lambda l, stream, part: (l * 2 + stream) * 6 + part
    n1_row = lambda l: depth * 12 + l
    n2_row = lambda l: depth * 13 + l
    fin_row, zero_row = depth * 14, depth * 14 + 1
    norm1 = lambda l, st: (n1_row(l), mod_row(l, st, 0), mod_row(l, st, 1))
    norm2 = lambda l, st: (n2_row(l), mod_row(l, st, 3), mod_row(l, st, 4))
    tab_l, tab_c = _tables(s, n_ctx)
    zero_state = (jnp.zeros((2 * ML_HEADS, ML_QK, 2 * ML_V), F32), jnp.zeros((2 * ML_HEADS, 8, LANE), F32))

    w_in_t = jnp.transpose(w_in, (0, 2, 1))
    wb = w_branch.astype(CD)
    wo = w_out.astype(CD)
    wf2 = w_ffn_out.astype(CD)

    hl = norm_mod(xl, vt, norm1(0, 0), CD)
    hc = norm_mod(xc, vt, norm1(0, 1), CD)
    for l in range(depth):
        last = l == depth - 1
        w = _layer_weights(w_in_t, l, mla_w_uq[l], mla_w_ukv[l])
        gq = mla_q_norm_g[l].reshape(1, -1).astype(F32)
        gkv = mla_kv_norm_g[l].reshape(1, -1).astype(F32)
        sink = sw_sink[l].reshape(1, -1).astype(F32)

        ml_l, na_l, sw_l, mla_lat = _project(hl, w, tab_l, gq, gkv, s + n_ctx)
        ml_c, na_c, sw_c, (mq, mk, mv) = _project(hc, w, tab_c, gq, gkv, s + n_ctx, mla_lat)

        qc_c, kc_c, gt_c = ml_prep(ml_c, ml_conv_w[l], ml_gate_b[l])
        qc_l, kc_l, gt_l = ml_prep(ml_l, ml_conv_w[l], ml_gate_b[l])
        hf_c, hb_c, cst, mst = ml_scan(qc_c, kc_c, ml_c, gt_c, zero_state)
        hf_l, hb_l, _, _ = ml_scan(qc_l, kc_l, ml_l, gt_l, (cst, mst))
        ya_l = ml_finish(hf_l, hb_l, ml_l, ml_norm_g[l])
        yb_l = na_attn(na_l, na_c, na_rpb[l], rows)
        yc_l = sw_attn(sw_l, sw_c, sink)
        yd_l = mla_flash(mq, mk, mv, s)

        nxt_l = (fin_row, zero_row, zero_row) if last else norm1(l + 1, 0)
        xl_new, hl_new = _dense_tail(xl, hl, (ya_l, yb_l, yc_l, yd_l), w, l, wb, wo, w_ffn_in, wf2, vt,
                                     mod_row(l, 0, 2), norm2(l, 0), mod_row(l, 0, 5), nxt_l, last)
        if not last:
            ya_c = ml_finish(hf_c, hb_c, ml_c, ml_norm_g[l])
            nw = NA_HEADS * NA_DIM
            yb_c = dense_attn([(na_c, nw, 0), (na_c, nw, 1), (na_c, nw, 2)], NA_HEADS, 1, NA_DIM, NA_DIM,
                              n_ctx, log2_scores=True)
            kcol = SW_HEADS * SW_DIM // LANE
            yc_c = dense_attn([(sw_c, SW_HEADS * SW_DIM, 0), (sw_c, LANE, kcol), (sw_c, LANE, kcol + 1)],
                              SW_HEADS, SW_HEADS // SW_KV_HEADS, SW_DIM, SW_DIM, n_ctx, sink=sink,
                              log2_scores=True)
            assert s % n_ctx == 0
            yd_c = dense_attn([(mq, MLA_HEADS * MLA_HW, 0), (mk, MLA_HEADS * MLA_HW, 0),
                               (mv, MLA_HEADS * MLA_V, 0)], MLA_HEADS, 1, MLA_HW, MLA_V, n_ctx,
                              log2_scores=True, row_block=s // n_ctx)
            xc, hc = _dense_tail(xc, hc, (ya_c, yb_c, yc_c, yd_c), w, l, wb, wo, w_ffn_in, wf2, vt,
                                 mod_row(l, 1, 2), norm2(l, 1), mod_row(l, 1, 5), norm1(l + 1, 1), False)
        xl, hl = xl_new, hl_new

    return hl[None].astype(x.dtype)
```

```python
import functools

import numpy as np
import jax
import jax.numpy as jnp
from jax import lax
from jax.experimental import pallas as pl
from jax.experimental.pallas import tpu as pltpu

F32 = jnp.float32
CD = jnp.bfloat16

GRID_W = 64
EPS = 1e-6
ROPE_THETA = 10000.0
ML_HEADS, ML_QK, ML_V, ML_CONV = 4, 64, 128, 3
NA_HEADS, NA_DIM, NA_WIN_R, NA_WIN_C = 8, 64, 8, 16
SW_HEADS, SW_KV_HEADS, SW_DIM, SW_WINDOW = 8, 2, 64, 128
MLA_HEADS, MLA_Q_RANK, MLA_KV_RANK, MLA_NOPE, MLA_ROPE, MLA_V = 4, 384, 256, 128, 64, 128
N_BRANCH, BRANCH_W = 4, 512

LANE = 128
ML_QKW = ML_HEADS * ML_QK
ML_VW = ML_HEADS * ML_V
ML_COLS = 2 * ML_QKW + 2 * ML_VW + 4 * ML_HEADS
ML_PAD = -(-ML_COLS // LANE) * LANE
NA_COLS = 3 * NA_HEADS * NA_DIM
SW_COLS = (SW_HEADS + 2 * SW_KV_HEADS) * SW_DIM
MLA_COLS = MLA_Q_RANK + MLA_KV_RANK + MLA_ROPE
MLA_HW = 256
NEG = -1e30
LOG2E = float(np.log2(np.e))
VMEM_LIMIT = 56 * 1024 * 1024

NA_QROWS = 4
SW_TQ = 256
ML_CHUNK = 256
MLA_TK = 1280


def _cparams(*sem):
    return pltpu.CompilerParams(dimension_semantics=sem, vmem_limit_bytes=VMEM_LIMIT)


def _row_tile(t, pref=512):
    tm = min(pref, t)
    assert t % tm == 0
    return tm


def _dot(a, b):
    return jnp.dot(a, b, preferred_element_type=F32)


def _dot_nt(a, b):
    return lax.dot_general(a, b, (((1,), (1,)), ((), ())), preferred_element_type=F32)


def _sigmoid(x):
    return 1.0 / (1.0 + jnp.exp(-x))


def _lanes(col, n):
    tile = jnp.broadcast_to(col, (col.shape[0], LANE))
    return tile if n == LANE else jnp.tile(tile, (1, n // LANE))


def _ones_col(n, w):
    return jnp.where(lax.broadcasted_iota(jnp.int32, (n, w), 1) == 0, 1.0, 0.0).astype(CD)


def _adaln_kernel(c_ref, w_ref, b_ref, o_ref):
    w = w_ref[0]
    for r in range(2):
        c = c_ref[:, r:r + 1]
        o_ref[0, r:r + 1, :] = jnp.sum((c * _sigmoid(c)) * w, axis=0, keepdims=True) + b_ref[0]


def adaln(cc, w_ada, b_ada):
    nl, d, n = w_ada.shape
    tn = 1024
    return pl.pallas_call(
        _adaln_kernel,
        grid=(nl, n // tn),
        in_specs=[pl.BlockSpec((d, 2), lambda l, j: (0, 0)),
                  pl.BlockSpec((1, d, tn), lambda l, j: (l, 0, j)),
                  pl.BlockSpec((1, 1, tn), lambda l, j: (l, 0, j))],
        out_specs=pl.BlockSpec((1, 2, tn), lambda l, j: (l, 0, j)),
        out_shape=jax.ShapeDtypeStruct((nl, 2, n), F32),
        compiler_params=_cparams("parallel", "parallel"),
        name="adaln",
    )(cc, w_ada, b_ada.reshape(nl, 1, n))


def _vec_spec(d, idx):
    return pl.BlockSpec((1, 1, d), lambda *_: (idx, 0, 0))


def _norm_mod(x, g, shift, scale):
    y = x * lax.rsqrt(jnp.mean(x * x, axis=-1, keepdims=True) + EPS)
    return (y * g) * (1.0 + scale) + shift


def _norm_mod_kernel(x_ref, g_ref, sh_ref, sc_ref, o_ref):
    o_ref[...] = _norm_mod(x_ref[...], g_ref[0], sh_ref[0], sc_ref[0]).astype(o_ref.dtype)


def norm_mod(x, vt, norm_idx, out_dtype):
    t, d = x.shape
    tm = _row_tile(t, 1024)
    return pl.pallas_call(
        _norm_mod_kernel,
        grid=(t // tm,),
        in_specs=[pl.BlockSpec((tm, d), lambda i: (i, 0))] + [_vec_spec(d, ix) for ix in norm_idx],
        out_specs=pl.BlockSpec((tm, d), lambda i: (i, 0)),
        out_shape=jax.ShapeDtypeStruct((t, d), out_dtype),
        compiler_params=_cparams("parallel"),
        name="norm_mod",
    )(x, vt, vt, vt)


def _mm_kernel(a_ref, b_ref, o_ref):
    o_ref[...] = _dot_nt(a_ref[...], b_ref[...]).astype(o_ref.dtype)


def matmul(a, bt, out_dtype, tn=None):
    t, k = a.shape
    n = bt.shape[0]
    tn = n if tn is None else tn
    tm = _row_tile(t, 1024)
    return pl.pallas_call(
        _mm_kernel,
        grid=(n // tn, t // tm),
        in_specs=[pl.BlockSpec((tm, k), lambda j, i: (i, 0)),
                  pl.BlockSpec((tn, k), lambda j, i: (j, 0))],
        out_specs=pl.BlockSpec((tm, tn), lambda j, i: (i, j)),
        out_shape=jax.ShapeDtypeStruct((t, n), out_dtype),
        compiler_params=_cparams("parallel", "parallel"),
        name="matmul",
    )(a, bt)


def _mm_res_norm_kernel(nj, tn, want_x, a_ref, b_ref, x_ref, gate_ref, g_ref, sh_ref, sc_ref, *rest):
    xbuf, ho_ref = (rest[0], rest[1]) if want_x else (rest[1], rest[0])
    j = pl.program_id(1)
    acc = _dot(a_ref[...], b_ref[0])
    for jj in range(nj):
        @pl.when(j == jj)
        def _(jj=jj):
            sl = slice(jj * tn, (jj + 1) * tn)
            xbuf[:, sl] = x_ref[...] + gate_ref[0][:, sl] * acc

    @pl.when(j == nj - 1)
    def _():
        ho_ref[...] = _norm_mod(xbuf[...], g_ref[0], sh_ref[0], sc_ref[0]).astype(ho_ref.dtype)


def matmul_residual_norm(a, b, l, x, vt, gate_idx, norm_idx, want_x, h_dtype, tn=None):
    t, kdim = a.shape
    n = b.shape[2]
    tn = n if tn is None else tn
    nj = n // tn
    tm = _row_tile(t)
    row = pl.BlockSpec((tm, n), lambda i, j: (i, 0))
    out_shape = [jax.ShapeDtypeStruct((t, n), h_dtype)]
    if want_x:
        out_shape = [jax.ShapeDtypeStruct((t, n), F32)] + out_shape
    out = pl.pallas_call(
        functools.partial(_mm_res_norm_kernel, nj, tn, want_x),
        grid=(t // tm, nj),
        in_specs=[pl.BlockSpec((tm, kdim), lambda i, j: (i, 0)),
                  pl.BlockSpec((1, kdim, tn), lambda i, j: (l, 0, j)),
                  pl.BlockSpec((tm, tn), lambda i, j: (i, j)),
                  _vec_spec(n, gate_idx)] + [_vec_spec(n, ix) for ix in norm_idx],
        out_specs=[row] * len(out_shape),
        out_shape=out_shape,
        scratch_shapes=[] if want_x else [pltpu.VMEM((tm, n), F32)],
        compiler_params=_cparams("parallel", "arbitrary"),
        name="matmul_residual_norm",
    )(a, b, x, vt, vt, vt, vt)
    return out if want_x else out[0]


def _rope_mm_kernel(n_rope, a_ref, w_ref, wr_ref, cos_ref, sin_ref, o_ref):
    a = a_ref[...]
    p = _dot_nt(a, w_ref[...])
    pr = _dot_nt(a, wr_ref[...])
    cos = cos_ref[...]
    sin = sin_ref[...]
    for j in range(p.shape[1] // LANE):
        sl = slice(j * LANE, (j + 1) * LANE)
        if j < n_rope:
            o_ref[:, sl] = (p[:, sl] * cos + pr[:, sl] * sin).astype(o_ref.dtype)
        else:
            o_ref[:, sl] = p[:, sl].astype(o_ref.dtype)


def rope_matmul(a, w, w_rot, cos, sin, n_rope):
    t, k = a.shape
    n = w.shape[0]
    nr = n_rope * LANE
    tm = _row_tile(t, 1024)
    return pl.pallas_call(
        functools.partial(_rope_mm_kernel, n_rope),
        grid=(t // tm,),
        in_specs=[pl.BlockSpec((tm, k), lambda i: (i, 0)),
                  pl.BlockSpec((n, k), lambda i: (0, 0)),
                  pl.BlockSpec((nr, k), lambda i: (0, 0)),
                  pl.BlockSpec((tm, LANE), lambda i: (i, 0)),
                  pl.BlockSpec((tm, LANE), lambda i: (i, 0))],
        out_specs=pl.BlockSpec((tm, n), lambda i: (i, 0)),
        out_shape=jax.ShapeDtypeStruct((t, n), CD),
        compiler_params=_cparams("parallel"),
        name="rope_matmul",
    )(a, w, w_rot, cos, sin)


def _mla_proj_kernel(h_ref, w1_ref, gq_ref, gkv_ref, wq_ref, wqr_ref, wkv_ref, cos_ref, sin_ref,
                     q_out, k_out, v_out):
    p = _dot_nt(h_ref[...], w1_ref[...])
    cq = p[:, :MLA_Q_RANK]
    ckv = p[:, MLA_Q_RANK:MLA_Q_RANK + MLA_KV_RANK]
    o = MLA_Q_RANK + MLA_KV_RANK
    kr = p[:, o:o + LANE]
    krr = p[:, o + LANE:o + 2 * LANE]
    cqn = (cq * lax.rsqrt(jnp.mean(cq * cq, axis=-1, keepdims=True) + EPS) * gq_ref[...]).astype(CD)
    ckvn = (ckv * lax.rsqrt(jnp.mean(ckv * ckv, axis=-1, keepdims=True) + EPS) * gkv_ref[...]).astype(CD)
    qa = _dot(cqn, wq_ref[...])
    qb = _dot(cqn, wqr_ref[...])
    kv = _dot(ckvn, wkv_ref[...])
    rope_lane = lax.broadcasted_iota(jnp.int32, cos_ref.shape, 1) < MLA_ROPE
    ck = jnp.where(rope_lane, cos_ref[...], 0.0)
    sk = jnp.where(rope_lane, sin_ref[...], 0.0)
    krp = (kr * ck + krr * sk).astype(CD)
    scale = (MLA_NOPE + MLA_ROPE) ** -0.5 * LOG2E
    cq_t = jnp.concatenate([jnp.full(ck.shape, scale, F32), ck * scale], axis=1)
    sq_t = jnp.concatenate([jnp.zeros(sk.shape, F32), sk * scale], axis=1)
    kvw = MLA_NOPE + MLA_V
    for h in range(MLA_HEADS):
        qs = slice(h * MLA_HW, (h + 1) * MLA_HW)
        q_out[:, qs] = (qa[:, qs] * cq_t + qb[:, qs] * sq_t).astype(CD)
        k_out[:, h * MLA_HW:h * MLA_HW + MLA_NOPE] = kv[:, h * kvw:h * kvw + MLA_NOPE].astype(CD)
        k_out[:, h * MLA_HW + MLA_NOPE:(h + 1) * MLA_HW] = krp
        v_out[:, h * MLA_V:(h + 1) * MLA_V] = kv[:, h * kvw + MLA_NOPE:(h + 1) * kvw].astype(CD)


def mla_proj(h, w1, gq, gkv, wq, wqr, wkv, cos, sin, total_rows, into=None):
    t, d = h.shape
    tm = _row_tile(t)
    nt = t // tm
    full = lambda a: pl.BlockSpec(a.shape, lambda i: (0,) * a.ndim)
    rows = lambda w: pl.BlockSpec((tm, w), lambda i: (jnp.minimum(i, nt - 1), 0))
    hw = MLA_HEADS * MLA_HW
    vw = MLA_HEADS * MLA_V
    args = [h, w1, gq, gkv, wq, wqr, wkv, cos, sin]
    in_specs = [rows(d), full(w1), full(gq), full(gkv), full(wq), full(wqr), full(wkv),
                rows(LANE), rows(LANE)]
    if into is None:
        off, aliases = 0, {}
        n_tail = -(-(total_rows - t) // tm)

        def kern(*refs):
            i = pl.program_id(0)

            @pl.when(i < nt)
            def _():
                _mla_proj_kernel(*refs)

            @pl.when(i >= nt)
            def _():
                for o_ref in refs[-3:]:
                    o_ref[...] = jnp.zeros(o_ref.shape, o_ref.dtype)
    else:
        assert (total_rows - t) % tm == 0
        off = (total_rows - t) // tm
        n_tail = 0
        aliases = {len(args) + i: i for i in range(3)}
        in_specs = in_specs + [pl.BlockSpec(memory_space=pl.ANY)] * 3
        args = args + list(into)
        kern = lambda *refs: _mla_proj_kernel(*refs[:9], *refs[12:])
    orow = lambda w: pl.BlockSpec((tm, w), lambda i: (off + i, 0))
    return pl.pallas_call(
        kern,
        grid=(nt + n_tail,),
        in_specs=in_specs,
        out_specs=[orow(hw), orow(hw), orow(vw)],
        out_shape=[jax.ShapeDtypeStruct((total_rows, hw), CD), jax.ShapeDtypeStruct((total_rows, hw), CD),
                   jax.ShapeDtypeStruct((total_rows, vw), CD)],
        input_output_aliases=aliases,
        compiler_params=_cparams("parallel"),
        name="mla_proj",
    )(*args)


def _flash_kernel(tq, tk, unroll, q_ref, k_ref, v_ref, o_ref, s_sc, m_sc, acc_sc):
    nkb = k_ref.shape[0] // tk
    nsub = q_ref.shape[0] // tq
    total = nsub * nkb
    m_sc[...] = jnp.full(m_sc.shape, NEG, F32)
    acc_sc[...] = jnp.zeros(acc_sc.shape, F32)
    ones_col = jnp.where(lax.broadcasted_iota(jnp.int32, (tk, LANE), 1) == 0, 1.0, 0.0).astype(CD)

    def split(b):
        sub = b // nkb
        return sub, b - sub * nkb

    def scores(b):
        sub, j = split(b)
        q = q_ref[pl.ds(pl.multiple_of(sub * tq, tq), tq), :]
        k = k_ref[pl.ds(pl.multiple_of(j * tk, tk), tk), :]
        return _dot_nt(q, k)

    def softmax_pv(s, b):
        sub, j = split(b)
        v = v_ref[pl.ds(pl.multiple_of(j * tk, tk), tk), :]
        m_prev = m_sc[sub]
        m_new = jnp.maximum(m_prev, jnp.max(s, axis=-1, keepdims=True))
        alpha = jnp.exp2(m_prev - m_new)
        p = jnp.exp2(s - m_new).astype(CD)
        v_aug = jnp.concatenate([v, ones_col], axis=1)
        acc_sc[sub] = alpha * acc_sc[sub] + _dot(p, v_aug)
        m_sc[sub] = m_new

    s_sc[0] = scores(0)

    def body(g, carry):
        b0 = g * unroll
        for u in range(unroll):
            s_sc[(u + 1) % 2] = scores(jnp.minimum(b0 + u + 1, total - 1))
            softmax_pv(s_sc[u % 2], b0 + u)
        return carry

    lax.fori_loop(0, total // unroll, body, 0)
    for sub in range(nsub):
        acc = acc_sc[sub]
        o_ref[sub * tq:(sub + 1) * tq, :] = (acc[:, :MLA_V] / acc[:, MLA_V:MLA_V + 1]).astype(o_ref.dtype)


def mla_flash(q, k, v, s):
    nk = k.shape[0]
    tq = _row_tile(s, 512)
    tstep = _row_tile(s, 16 * tq)
    tk = MLA_TK
    total = (tstep // tq) * (nk // tk)
    assert nk % tk == 0 and total % 2 == 0
    unroll = 16 if total % 16 == 0 else (4 if total % 4 == 0 else 2)
    return pl.pallas_call(
        functools.partial(_flash_kernel, tq, tk, unroll),
        grid=(MLA_HEADS, s // tstep),
        in_specs=[pl.BlockSpec((tstep, MLA_HW), lambda h, i: (i, h)),
                  pl.BlockSpec((nk, MLA_HW), lambda h, i: (0, h), pipeline_mode=pl.Buffered(1)),
                  pl.BlockSpec((nk, MLA_V), lambda h, i: (0, h), pipeline_mode=pl.Buffered(1))],
        out_specs=pl.BlockSpec((tstep, MLA_V), lambda h, i: (i, h)),
        out_shape=jax.ShapeDtypeStruct((s, MLA_HEADS * MLA_V), CD),
        scratch_shapes=[pltpu.VMEM((2, tq, tk), F32), pltpu.VMEM((tstep // tq, tq, 1), F32),
                        pltpu.VMEM((tstep // tq, tq, 2 * MLA_V), F32)],
        compiler_params=_cparams("parallel", "arbitrary"),
        name="mla_flash",
    )(q, k, v)


def _dense_attn_kernel(n_heads, group, dqk, dv, log2_scores, q_ref, k_ref, v_ref, sink_ref, o_ref):
    q = q_ref[...]
    k = k_ref[...]
    v = v_ref[...]
    ex = jnp.exp2 if log2_scores else jnp.exp
    for h in range(n_heads):
        g = h // group
        s = _dot_nt(q[:, h * dqk:(h + 1) * dqk], k[:, g * dqk:(g + 1) * dqk])
        m = jnp.max(s, axis=-1, keepdims=True)
        l = jnp.zeros_like(m)
        if sink_ref is not None:
            sk = sink_ref[0:1, h:h + 1] * (LOG2E if log2_scores else 1.0)
            m = jnp.maximum(m, sk)
            l = ex(sk - m)
        p = ex(s - m)
        l = l + jnp.sum(p, axis=-1, keepdims=True)
        o = _dot(p.astype(CD), v[:, g * dv:(g + 1) * dv]) / l
        o_ref[:, h * dv:(h + 1) * dv] = o.astype(o_ref.dtype)


def dense_attn(qkv_specs, n_heads, group, dqk, dv, c, sink=None, log2_scores=False, row_block=0):
    arrays = [a for a, _, _ in qkv_specs]
    specs = [pl.BlockSpec((c, w), functools.partial(lambda b, i: (row_block, b), b)) for _, w, b in qkv_specs]
    if sink is None:
        kern = lambda q, k, v, o: _dense_attn_kernel(n_heads, group, dqk, dv, log2_scores, q, k, v, None, o)
    else:
        kern = functools.partial(_dense_attn_kernel, n_heads, group, dqk, dv, log2_scores)
        arrays.append(sink)
        specs.append(pl.BlockSpec(sink.shape, lambda i: (0, 0)))
    return pl.pallas_call(
        kern,
        grid=(1,),
        in_specs=specs,
        out_specs=pl.BlockSpec((c, n_heads * dv), lambda i: (0, 0)),
        out_shape=jax.ShapeDtypeStruct((c, n_heads * dv), CD),
        compiler_params=_cparams("arbitrary"),
        name="dense_attn",
    )(*arrays)


def _na_kernel(q_ref, kp_ref, ko_ref, kn_ref, vp_ref, vo_ref, vn_ref, kc_ref, vc_ref, bias_ref, o_ref):
    q = q_ref[...]
    kcat = jnp.concatenate([kp_ref[...], ko_ref[...], kn_ref[...], kc_ref[...]], axis=0)
    vcat = jnp.concatenate([vp_ref[...], vo_ref[...], vn_ref[...], vc_ref[...]], axis=0)
    tb = q.shape[0]
    nloc = 3 * tb
    ones = _ones_col(kcat.shape[0], LANE)
    first_half = lax.broadcasted_iota(jnp.int32, (tb, LANE), 1) < NA_DIM
    zero = jnp.zeros((tb, LANE), CD)
    for j in range(NA_HEADS // 2):
        ps = slice(j * LANE, (j + 1) * LANE)
        q2 = q[:, ps]
        k2 = kcat[:, ps]
        v_aug = jnp.concatenate([vcat[:, ps], ones], axis=1)
        outs = []
        for e in range(2):
            qm = jnp.where(first_half, q2, zero) if e == 0 else jnp.where(first_half, zero, q2)
            s = _dot_nt(qm, k2)
            s = jnp.concatenate([s[:, :nloc] + bias_ref[0, 2 * j + e], s[:, nloc:]], axis=1)
            m = jnp.max(s, axis=-1, keepdims=True)
            p = jnp.exp2(s - _lanes(m, s.shape[1])).astype(CD)
            acc = _dot(p, v_aug)
            outs.append(acc[:, :LANE] / _lanes(acc[:, LANE:LANE + 1], LANE))
        o_ref[:, ps] = jnp.where(first_half, outs[0], outs[1]).astype(o_ref.dtype)


def _na_bias(rpb, rows):
    qr = NA_QROWS
    tb = qr * GRID_W
    nb = rows // qr
    nkr = 3 * qr
    rsel = np.zeros((3, qr, nkr, 2 * NA_WIN_R - 1), np.float32)
    for v, b in enumerate((0, 1, nb - 1)):
        for a in range(qr):
            r = qr * b + a
            rs = min(max(r - NA_WIN_R // 2, 0), rows - NA_WIN_R)
            for u in range(nkr):
                kr = qr * (b - 1) + u
                if 0 <= b - 1 + u // qr < nb and rs <= kr < rs + NA_WIN_R:
                    rsel[v, a, u, kr - r + NA_WIN_R - 1] = 1.0
    csel = np.zeros((GRID_W, GRID_W, 2 * NA_WIN_C - 1), np.float32)
    for qc in range(GRID_W):
        cs = min(max(qc - NA_WIN_C // 2, 0), GRID_W - NA_WIN_C)
        for kc in range(cs, cs + NA_WIN_C):
            csel[qc, kc, kc - qc + NA_WIN_C - 1] = 1.0
    valid = (np.einsum('vaud,qkj->vaquk', rsel, csel) > 0).reshape(3, tb, 3 * tb)
    assert 2 * qr == NA_WIN_R
    nd = 2 * NA_WIN_R - 1
    rp = rpb.astype(F32) * LOG2E
    gap = jnp.zeros((NA_HEADS, nd, GRID_W + 1 - (2 * NA_WIN_C - 1)), F32)
    ring = jnp.concatenate([rp[..., NA_WIN_C - 1:], gap, rp[..., :NA_WIN_C - 1]], axis=-1)
    cols = jnp.tile(ring, (1, 1, GRID_W))[..., :GRID_W * GRID_W].reshape(NA_HEADS, nd, GRID_W, GRID_W)
    off = NA_WIN_R - 1 - qr
    slabs = [cols[:, off - a:off - a + nkr].transpose(0, 2, 1, 3).reshape(NA_HEADS, GRID_W, 3 * tb)
             for a in range(qr)]
    band = jnp.stack(slabs, axis=1).reshape(NA_HEADS, tb, 3 * tb)
    return jnp.where(valid[:, None], band[None], NEG)


def na_attn(p_lat, p_ctx, rpb, rows):
    s = p_lat.shape[0]
    c = p_ctx.shape[0]
    w = NA_HEADS * NA_DIM
    tb = NA_QROWS * GRID_W
    nb = s // tb
    assert rows % NA_QROWS == 0 and nb >= 3 and rows >= 2 * NA_WIN_R
    bias = _na_bias(rpb, rows)

    def blk(col, off):
        return pl.BlockSpec((tb, w), lambda i: (jnp.clip(i + off, 0, nb - 1), col))

    return pl.pallas_call(
        _na_kernel,
        grid=(nb,),
        in_specs=[blk(0, 0), blk(1, -1), blk(1, 0), blk(1, 1), blk(2, -1), blk(2, 0), blk(2, 1),
                  pl.BlockSpec((c, w), lambda i: (0, 1)),
                  pl.BlockSpec((c, w), lambda i: (0, 2)),
                  pl.BlockSpec((1, NA_HEADS, tb, 3 * tb),
                               lambda i: (jnp.where(i == 0, 0, jnp.where(i == nb - 1, 2, 1)), 0, 0, 0))],
        out_specs=pl.BlockSpec((tb, w), lambda i: (i, 0)),
        out_shape=jax.ShapeDtypeStruct((s, w), CD),
        compiler_params=_cparams("parallel"),
        name="na_attn",
    )(p_lat, p_lat, p_lat, p_lat, p_lat, p_lat, p_lat, p_ctx, p_ctx, bias)


def _sw_kernel(seq, q_ref, kp_ref, ko_ref, kn_ref, vp_ref, vo_ref, vn_ref, kc_ref, vc_ref, sink_ref, o_ref):
    i = pl.program_id(0)
    q = q_ref[...]
    tq = q.shape[0]
    kcat = jnp.concatenate([kp_ref[...], ko_ref[...], kn_ref[...], kc_ref[...]], axis=0)
    vcat = jnp.concatenate([vp_ref[...], vo_ref[...], vn_ref[...], vc_ref[...]], axis=0)
    nkk = tq + 2 * SW_WINDOW
    r = lax.broadcasted_iota(jnp.int32, (tq, nkk), 0)
    cc = lax.broadcasted_iota(jnp.int32, (tq, nkk), 1)
    kpos = i * tq - SW_WINDOW + cc
    d = cc - r
    mask = (d >= 0) & (d <= 2 * SW_WINDOW) & (kpos >= 0) & (kpos < seq)
    assert SW_KV_HEADS == 2 and SW_DIM * 2 == LANE
    ones = _ones_col(kcat.shape[0], LANE)
    swap = lambda x: jnp.concatenate([x[:, SW_DIM:], x[:, :SW_DIM]], axis=1)
    k_by_order = (kcat, swap(kcat))
    v_by_order = (jnp.concatenate([vcat, ones], axis=1), jnp.concatenate([swap(vcat), ones], axis=1))
    first_half = lax.broadcasted_iota(jnp.int32, (tq, LANE), 1) < SW_DIM
    zero = jnp.zeros((tq, LANE), CD)
    grp = SW_HEADS // SW_KV_HEADS
    for j in range(SW_HEADS // 2):
        ps = slice(j * LANE, (j + 1) * LANE)
        q2 = q[:, ps]
        outs = []
        for e in range(2):
            h = 2 * j + e
            order = 0 if h // grp == e else 1
            qm = jnp.where(first_half, q2, zero) if e == 0 else jnp.where(first_half, zero, q2)
            s = _dot_nt(qm, k_by_order[order])
            s = jnp.concatenate([jnp.where(mask, s[:, :nkk], NEG), s[:, nkk:]], axis=1)
            sk = sink_ref[0:1, h:h + 1] * LOG2E
            m = jnp.maximum(jnp.max(s, axis=-1, keepdims=True), sk)
            p = jnp.exp2(s - _lanes(m, s.shape[1])).astype(CD)
            acc = _dot(p, v_by_order[order])
            outs.append(acc[:, :LANE] / _lanes(acc[:, LANE:LANE + 1] + jnp.exp2(sk - m), LANE))
        o_ref[:, ps] = jnp.where(first_half, outs[0], outs[1]).astype(o_ref.dtype)


def sw_attn(p_lat, p_ctx, sink):
    s = p_lat.shape[0]
    c = p_ctx.shape[0]
    tq = SW_TQ
    assert s % tq == 0 and tq % SW_WINDOW == 0
    nq = s // tq
    per = tq // SW_WINDOW
    nwb = s // SW_WINDOW
    kcol = SW_HEADS * SW_DIM // LANE
    prev = lambda col: pl.BlockSpec((SW_WINDOW, LANE), lambda i: (jnp.maximum(i * per - 1, 0), col))
    own = lambda col: pl.BlockSpec((tq, LANE), lambda i: (i, col))
    nxt = lambda col: pl.BlockSpec((SW_WINDOW, LANE), lambda i: (jnp.minimum((i + 1) * per, nwb - 1), col))
    return pl.pallas_call(
        functools.partial(_sw_kernel, s),
        grid=(nq,),
        in_specs=[pl.BlockSpec((tq, SW_HEADS * SW_DIM), lambda i: (i, 0)),
                  prev(kcol), own(kcol), nxt(kcol), prev(kcol + 1), own(kcol + 1), nxt(kcol + 1),
                  pl.BlockSpec((c, LANE), lambda i: (0, kcol)),
                  pl.BlockSpec((c, LANE), lambda i: (0, kcol + 1)),
                  pl.BlockSpec(sink.shape, lambda i: (0, 0))],
        out_specs=pl.BlockSpec((tq, SW_HEADS * SW_DIM), lambda i: (i, 0)),
        out_shape=jax.ShapeDtypeStruct((s, SW_HEADS * SW_DIM), CD),
        compiler_params=_cparams("parallel"),
        name="sw_attn",
    )(p_lat, p_lat, p_lat, p_lat, p_lat, p_lat, p_lat, p_ctx, p_ctx, sink)


def _ml_prep_kernel(nt, x_ref, xp_ref, xn_ref, g_ref, w_ref, b_ref, q_out, k_out, g_out):
    i = pl.program_id(0)
    x = x_ref[...]
    tm = x.shape[0]
    row = lax.broadcasted_iota(jnp.int32, x.shape, 0)
    has_prev = jnp.where(i > 0, 1.0, 0.0)
    has_next = jnp.where(i < nt - 1, 1.0, 0.0)
    prev_row = xp_ref[7:8, :] * has_prev
    next_row = xn_ref[0:1, :] * has_next
    xm = jnp.where(row == 0, prev_row, pltpu.roll(x, 1, axis=0))
    xq = jnp.where(row == tm - 1, next_row, pltpu.roll(x, tm - 1, axis=0))
    y = w_ref[0:1, :] * xm + w_ref[1:2, :] * x + w_ref[2:3, :] * xq
    y = y * _sigmoid(y)
    q_out[...] = y[:, :ML_QKW].astype(CD)
    k_out[...] = (y[:, ML_QKW:] * (ML_QK ** -0.5)).astype(CD)
    g = g_ref[...] + b_ref[...]
    lane = lax.broadcasted_iota(jnp.int32, g.shape, 1)
    logsig = jnp.minimum(g, 0.0) - jnp.log(1.0 + jnp.exp(-jnp.abs(g)))
    g_out[...] = jnp.where(lane < 2 * ML_HEADS, g, logsig)


def ml_prep(p_ml, conv_w, gate_b):
    t = p_ml.shape[0]
    tm = _row_tile(t, 1024)
    nt = t // tm
    wq = 2 * ML_QKW
    gcol = (2 * ML_QKW + 2 * ML_VW) // LANE
    w8 = jnp.zeros((8, wq), F32).at[:ML_CONV].set(conv_w.astype(F32))
    b = jnp.zeros((1, LANE), F32).at[0, :4 * ML_HEADS].set(gate_b.astype(F32).reshape(-1))
    h8 = tm // 8
    return pl.pallas_call(
        functools.partial(_ml_prep_kernel, nt),
        grid=(nt,),
        in_specs=[pl.BlockSpec((tm, wq), lambda i: (i, 0)),
                  pl.BlockSpec((8, wq), lambda i: (jnp.maximum(i * h8 - 1, 0), 0)),
                  pl.BlockSpec((8, wq), lambda i: (jnp.minimum((i + 1) * h8, t // 8 - 1), 0)),
                  pl.BlockSpec((tm, LANE), lambda i: (i, gcol)),
                  pl.BlockSpec((8, wq), lambda i: (0, 0)),
                  pl.BlockSpec((1, LANE), lambda i: (0, 0))],
        out_specs=[pl.BlockSpec((tm, ML_QKW), lambda i: (i, 0)),
                   pl.BlockSpec((tm, ML_QKW), lambda i: (i, 0)),
                   pl.BlockSpec((tm, LANE), lambda i: (i, 0))],
        out_shape=[jax.ShapeDtypeStruct((t, ML_QKW), CD), jax.ShapeDtypeStruct((t, ML_QKW), CD),
                   jax.ShapeDtypeStruct((t, LANE), F32)],
        compiler_params=_cparams("parallel"),
        name="ml_prep",
    )(p_ml, p_ml, p_ml, p_ml, w8, b)


def _ml_scan_kernel(qf_ref, ktf_ref, vf_ref, gcf_ref, grf_ref,
                    qb_ref, ktb_ref, vb_ref, gcb_ref, grb_ref,
                    c0_ref, m0_ref, hf_ref, hb_ref, c_ref, m_ref):
    @pl.when(pl.program_id(0) == 0)
    def _():
        c_ref[...] = c0_ref[...]
        m_ref[...] = m0_ref[...]

    ln = qf_ref.shape[0]
    r = lax.broadcasted_iota(jnp.int32, (ln, ln), 0)
    c = lax.broadcasted_iota(jnp.int32, (ln, ln), 1)
    lower = r >= c
    upper = c >= r
    lower_f = jnp.where(lower, 1.0, 0.0)
    upper_f = jnp.where(upper, 1.0, 0.0)
    ones_col = jnp.where(lax.broadcasted_iota(jnp.int32, (ln, LANE), 1) == 0, 1.0, 0.0).astype(CD)
    hi = lax.Precision.HIGHEST
    first_half = lax.broadcasted_iota(jnp.int32, (ln, LANE), 1) < ML_QK
    zero_q = jnp.zeros((ln, LANE), CD)
    new_states = []

    streams = ((qf_ref, ktf_ref, vf_ref, gcf_ref, grf_ref, hf_ref, lower, lower_f, upper_f, ln - 1),
               (qb_ref, ktb_ref, vb_ref, gcb_ref, grb_ref, hb_ref, upper, upper_f, lower_f, 0))
    for d, (q_ref, kt_ref, v_ref, gc_ref, gr_ref, h_ref, mask, tri_c, tri_r, last) in enumerate(streams):
        gcol = gc_ref[...]
        grow = gr_ref[...]
        bcol = jnp.dot(tri_c, gcol, precision=hi, preferred_element_type=F32)
        brow = jnp.dot(grow, tri_r, precision=hi, preferred_element_type=F32)
        q = q_ref[...]
        kt = kt_ref[...]
        v = v_ref[...]
        for hd in range(ML_HEADS):
            ch = d * ML_HEADS + hd
            ic = d * ML_HEADS + hd
            fc = 2 * ML_HEADS + ic
            pair, half = hd // 2, hd % 2
            b_c = bcol[:, fc:fc + 1]
            r_row = grow[ic:ic + 1, :] - brow[fc:fc + 1, :]
            m_prev = m_ref[ch, 0:1, 0:1]
            cst = c_ref[ch]
            rmat = jnp.where(mask, r_row, NEG)
            mx = jnp.maximum(jnp.max(rmat, axis=-1, keepdims=True), m_prev)
            q2 = q[:, pair * LANE:(pair + 1) * LANE]
            qm = jnp.where(first_half, q2, zero_q) if half == 0 else jnp.where(first_half, zero_q, q2)
            s = _dot(qm, kt[pair * LANE:(pair + 1) * LANE, :]) * jnp.exp(rmat - mx)
            w_inter = jnp.exp(m_prev - mx)
            v_aug = jnp.concatenate([v[:, hd * ML_V:(hd + 1) * ML_V].astype(CD), ones_col], axis=1)
            c_pair = jnp.concatenate([c_ref[ch - half], c_ref[ch - half + 1]], axis=0).astype(CD)
            num = _dot(s.astype(CD), v_aug) + w_inter * _dot(qm, c_pair)
            den = num[:, ML_V:ML_V + 1]
            h = num[:, :ML_V] / jnp.maximum(jnp.abs(den), jnp.exp(-(b_c + mx)))
            h_ref[:, hd * ML_V:(hd + 1) * ML_V] = h
            b_end = b_c[last:last + 1, :]
            m_new = b_end + jnp.maximum(m_prev, jnp.max(r_row, axis=-1, keepdims=True))
            w_s = jnp.exp(b_end + r_row - m_new)
            w_c = jnp.exp(b_end + m_prev - m_new)
            kth = kt[hd * ML_QK:(hd + 1) * ML_QK, :]
            ktw = (kth.astype(F32) * w_s).astype(CD)
            new_states.append((ch, w_c * cst + _dot(ktw, v_aug), m_new))
    for ch, c_new, m_new in new_states:
        c_ref[ch] = c_new
        m_ref[ch] = jnp.broadcast_to(m_new, m_ref.shape[1:])


def ml_scan(qc, kc, p_ml, gates, state):
    t = qc.shape[0]
    ln = min(ML_CHUNK, t)
    assert t % ln == 0
    nc = t // ln
    kt = kc.T
    gr = gates[:, :4 * ML_HEADS].T
    c0, m0 = state
    vcol = 2 * ML_QKW // ML_VW
    fwd = lambda j: j
    bwd = lambda j: nc - 1 - j

    def specs(ix):
        return [pl.BlockSpec((ln, ML_QKW), lambda j: (ix(j), 0)),
                pl.BlockSpec((ML_QKW, ln), lambda j: (0, ix(j))),
                pl.BlockSpec((ln, ML_VW), lambda j: (ix(j), vcol)),
                pl.BlockSpec((ln, LANE), lambda j: (ix(j), 0)),
                pl.BlockSpec((4 * ML_HEADS, ln), lambda j: (0, ix(j)))]

    st_specs = [pl.BlockSpec(c0.shape, lambda j: (0, 0, 0)), pl.BlockSpec(m0.shape, lambda j: (0, 0, 0))]
    return pl.pallas_call(
        _ml_scan_kernel,
        grid=(nc,),
        in_specs=specs(fwd) + specs(bwd) + st_specs,
        out_specs=[pl.BlockSpec((ln, ML_VW), lambda j: (fwd(j), 0)),
                   pl.BlockSpec((ln, ML_VW), lambda j: (bwd(j), 0))] + st_specs,
        out_shape=[jax.ShapeDtypeStruct((t, ML_VW), F32), jax.ShapeDtypeStruct((t, ML_VW), F32),
                   jax.ShapeDtypeStruct(c0.shape, F32), jax.ShapeDtypeStruct(m0.shape, F32)],
        compiler_params=_cparams("arbitrary"),
        name="ml_scan",
    )(qc, kt, p_ml, gates, gr, qc, kt, p_ml, gates, gr, c0, m0)


def _ml_finish_kernel(hf_ref, hb_ref, o_ref, g_ref, y_ref):
    h = hf_ref[...] + hb_ref[...]
    og = _sigmoid(o_ref[...])
    g = g_ref[...]
    for hd in range(ML_HEADS):
        sl = slice(hd * ML_V, (hd + 1) * ML_V)
        hh = h[:, sl]
        y = hh * lax.rsqrt(jnp.mean(hh * hh, axis=-1, keepdims=True) + EPS) * g[:, sl]
        y_ref[:, sl] = (y * og[:, sl]).astype(y_ref.dtype)


def ml_finish(hf, hb, p_ml, norm_g):
    t = hf.shape[0]
    tm = _row_tile(t, 1024)
    ocol = (2 * ML_QKW + ML_VW) // ML_VW
    blk = pl.BlockSpec((tm, ML_VW), lambda i: (i, 0))
    return pl.pallas_call(
        _ml_finish_kernel,
        grid=(t // tm,),
        in_specs=[blk, blk, pl.BlockSpec((tm, ML_VW), lambda i: (i, ocol)),
                  pl.BlockSpec((1, ML_VW), lambda i: (0, 0))],
        out_specs=blk,
        out_shape=jax.ShapeDtypeStruct((t, ML_VW), CD),
        compiler_params=_cparams("parallel"),
        name="ml_finish",
    )(hf, hb, p_ml, norm_g.reshape(1, ML_VW).astype(F32))


def _merge_kernel(h_ref, ya_ref, yb_ref, yc_ref, yd_ref, g0_ref, g1_ref, g2_ref, g3_ref, wb_ref, o_ref):
    h = h_ref[...]
    acc = None
    for i, (y_ref, g_ref) in enumerate(zip((ya_ref, yb_ref, yc_ref, yd_ref), (g0_ref, g1_ref, g2_ref, g3_ref))):
        term = _sigmoid(_dot_nt(h, g_ref[...])) * _dot(y_ref[...], wb_ref[0, i])
        acc = term if acc is None else acc + term
    o_ref[...] = acc.astype(o_ref.dtype)


def gated_merge(h, ys, w_gate, w_branch, l, tn=512):
    t, d = h.shape
    tm = _row_tile(t, 1024)
    nj = d // tn
    yspec = pl.BlockSpec((tm, BRANCH_W), lambda j, i: (i, 0))
    gspec = lambda b: pl.BlockSpec((tn, d), lambda j, i: (b * nj + j, 0))
    return pl.pallas_call(
        _merge_kernel,
        grid=(nj, t // tm),
        in_specs=[pl.BlockSpec((tm, d), lambda j, i: (i, 0)), yspec, yspec, yspec, yspec,
                  gspec(0), gspec(1), gspec(2), gspec(3),
                  pl.BlockSpec((1, N_BRANCH, BRANCH_W, tn), lambda j, i: (l, 0, 0, j))],
        out_specs=pl.BlockSpec((tm, tn), lambda j, i: (i, j)),
        out_shape=jax.ShapeDtypeStruct((t, d), CD),
        compiler_params=_cparams("parallel", "parallel"),
        name="gated_merge",
    )(h, *ys, w_gate, w_gate, w_gate, w_gate, w_branch)


def _ffn1_kernel(h_ref, wa_ref, wg_ref, o_ref, wa_sc, wg_sc):
    @pl.when(pl.program_id(1) == 0)
    def _():
        wa_sc[...] = wa_ref[0].astype(CD)
        wg_sc[...] = wg_ref[0].astype(CD)

    h = h_ref[...]
    a = _dot(h, wa_sc[...])
    g = _dot(h, wg_sc[...])
    o_ref[...] = (a * _sigmoid(a) * g).astype(o_ref.dtype)


def ffn_up(h, w, l, tn=512):
    t, d = h.shape
    dff = w.shape[2] // 2
    tm = _row_tile(t, 1024)
    nj = dff // tn
    return pl.pallas_call(
        _ffn1_kernel,
        grid=(nj, t // tm),
        in_specs=[pl.BlockSpec((tm, d), lambda j, i: (i, 0)),
                  pl.BlockSpec((1, d, tn), lambda j, i: (l, 0, j)),
                  pl.BlockSpec((1, d, tn), lambda j, i: (l, 0, nj + j))],
        out_specs=pl.BlockSpec((tm, tn), lambda j, i: (i, j)),
        out_shape=jax.ShapeDtypeStruct((t, dff), CD),
        scratch_shapes=[pltpu.VMEM((d, tn), CD), pltpu.VMEM((d, tn), CD)],
        compiler_params=_cparams("parallel", "arbitrary"),
        name="ffn_up",
    )(h, w, w)


def _rot_cols(w, dim):
    k, n = w.shape
    w4 = w.reshape(k, n // dim, 2, dim // 2)
    return jnp.stack([-w4[:, :, 1], w4[:, :, 0]], axis=2).reshape(k, n)


def _rope_tables(n_tokens, dim):
    a = dim // 2
    inv = 1.0 / (ROPE_THETA ** (jnp.arange(0, a, 2, dtype=F32) / a))
    rows = n_tokens // GRID_W
    assert rows * GRID_W == n_tokens
    ang_r = jnp.arange(rows, dtype=F32)[:, None] * inv
    ang_c = jnp.arange(GRID_W, dtype=F32)[:, None] * inv

    def expand(fr, fc):
        shape = (rows, GRID_W, a // 2)
        return jnp.concatenate([jnp.broadcast_to(fr[:, None, :], shape), jnp.broadcast_to(fc[None], shape)],
                               axis=-1).reshape(n_tokens, a)

    return expand(jnp.cos(ang_r), jnp.cos(ang_c)), expand(jnp.sin(ang_r), jnp.sin(ang_c))


_O_NA = ML_COLS
_O_SW = _O_NA + NA_COLS
_O_MLA = _O_SW + SW_COLS
_O_KR = _O_MLA + MLA_Q_RANK + MLA_KV_RANK
_O_GATE = _O_MLA + MLA_COLS


def _rot_half_rows(x, dim):
    half = dim // 2
    parts = []
    for b in range(x.shape[0] // dim):
        parts += [-x[b * dim + half:(b + 1) * dim], x[b * dim:b * dim + half]]
    return jnp.concatenate(parts, axis=0)


def _repack_kernel(w_ref, ml_ref, na_ref, sw_ref, swr_ref, mla_ref, gate_ref):
    w = w_ref[0]
    cols = w.shape[1]
    zeros = lambda n: jnp.zeros((n, cols), F32)
    ml_ref[...] = jnp.concatenate([w[:ML_COLS], zeros(ML_PAD - ML_COLS)], axis=0).astype(CD)
    nq = NA_HEADS * NA_DIM
    na_ref[...] = jnp.concatenate([w[_O_NA:_O_NA + nq] * (NA_DIM ** -0.5 * LOG2E), w[_O_NA + nq:_O_SW]],
                                  axis=0).astype(CD)
    sq = SW_HEADS * SW_DIM
    skv = SW_KV_HEADS * SW_DIM
    sw_q = w[_O_SW:_O_SW + sq] * (SW_DIM ** -0.5 * LOG2E)
    sw_k = w[_O_SW + sq:_O_SW + sq + skv]
    sw_ref[...] = jnp.concatenate([sw_q, w[_O_SW + sq:_O_MLA]], axis=0).astype(CD)
    swr_ref[...] = _rot_half_rows(jnp.concatenate([sw_q, sw_k], axis=0), SW_DIM).astype(CD)
    kr = w[_O_KR:_O_GATE]
    pad = zeros(LANE - MLA_ROPE)
    mla_ref[...] = jnp.concatenate([w[_O_MLA:_O_KR], kr, pad, _rot_half_rows(kr, MLA_ROPE), pad], axis=0).astype(CD)
    gate_ref[...] = w[_O_GATE:].astype(CD)


def _repack_w_in(w_in_t, l):
    _, n, d = w_in_t.shape
    tc = 256
    heights = (ML_PAD, NA_COLS, SW_COLS, (SW_HEADS + SW_KV_HEADS) * SW_DIM,
               MLA_Q_RANK + MLA_KV_RANK + 2 * LANE, n - _O_GATE)
    return pl.pallas_call(
        _repack_kernel,
        grid=(d // tc,),
        in_specs=[pl.BlockSpec((1, n, tc), lambda i: (l, 0, i))],
        out_specs=[pl.BlockSpec((ht, tc), lambda i: (0, i)) for ht in heights],
        out_shape=[jax.ShapeDtypeStruct((ht, d), CD) for ht in heights],
        compiler_params=_cparams("parallel"),
        name="repack_w_in",
    )(w_in_t)


def _layer_weights(w_in_t, l, mla_w_uq_l, mla_w_ukv_l):
    w = {}
    w["ml"], w["na"], w["sw"], w["sw_rot"], w["mla1"], w["gate"] = _repack_w_in(w_in_t, l)
    uq = mla_w_uq_l.reshape(MLA_Q_RANK, MLA_HEADS, MLA_NOPE + MLA_ROPE)
    zq = jnp.zeros((MLA_Q_RANK, MLA_HEADS, MLA_HW - MLA_NOPE - MLA_ROPE), F32)
    w["mla_q"] = jnp.concatenate([uq, zq], axis=2).reshape(MLA_Q_RANK, -1).astype(CD)
    uqr = _rot_cols(uq[:, :, MLA_NOPE:].reshape(MLA_Q_RANK, -1), MLA_ROPE).reshape(MLA_Q_RANK, MLA_HEADS, MLA_ROPE)
    w["mla_q_rot"] = jnp.concatenate([jnp.zeros_like(uq[:, :, :MLA_NOPE]), uqr, zq], axis=2
                                     ).reshape(MLA_Q_RANK, -1).astype(CD)
    w["mla_kv"] = mla_w_ukv_l.astype(CD)
    return w


def _tables(s, c):
    assert SW_DIM == MLA_ROPE
    cos, sin = _rope_tables(s, SW_DIM)
    tile4 = lambda a: jnp.concatenate([a, a, a, a], axis=1)
    return ((tile4(cos), tile4(sin)), (jnp.ones((c, LANE), F32), jnp.zeros((c, LANE), F32)))


def _project(h, w, tab, gq, gkv, total_rows, mla_into=None):
    p_ml = matmul(h, w["ml"], F32)
    p_na = matmul(h, w["na"], CD)
    n_rope = (SW_HEADS + SW_KV_HEADS) * SW_DIM // LANE
    p_sw = rope_matmul(h, w["sw"], w["sw_rot"], tab[0], tab[1], n_rope)
    mla = mla_proj(h, w["mla1"], gq, gkv, w["mla_q"], w["mla_q_rot"], w["mla_kv"],
                   tab[0], tab[1], total_rows, mla_into)
    return p_ml, p_na, p_sw, mla


def _dense_tail(x, h, ys, w, l, w_branch, w_out, w_ffn_in, w_ffn_out, vt, g1, norm2, g2, norm_next, last):
    merged = gated_merge(h, ys, w["gate"], w_branch, l)
    x, h2 = matmul_residual_norm(merged, w_out, l, x, vt, g1, norm2, True, CD)
    u = ffn_up(h2, w_ffn_in, l)
    if last:
        return None, matmul_residual_norm(u, w_ffn_out, l, x, vt, g2, norm_next, False, F32, tn=1024)
    return matmul_residual_norm(u, w_ffn_out, l, x, vt, g2, norm_next, True, CD, tn=1024)


def kernel(x, c, ctx, c_ctx, w_ada, b_ada, norm1_g, norm2_g, w_in, ml_conv_w, ml_gate_b, ml_norm_g, na_rpb,
           sw_sink, mla_q_norm_g, mla_kv_norm_g, mla_w_uq, mla_w_ukv, w_branch, w_out, w_ffn_in, w_ffn_out,
           final_norm_g):
    bsz, s, d = x.shape
    assert bsz == 1
    depth = w_in.shape[0]
    n_ctx = ctx.shape[1]
    rows = s // GRID_W
    xl = x[0].astype(F32)
    xc = ctx[0].astype(F32)
    mod = adaln(jnp.stack([c[0], c_ctx], axis=1).astype(F32), w_ada, b_ada)
    vt = jnp.concatenate([mod.reshape(depth * 12, d), norm1_g.astype(F32), norm2_g.astype(F32),
                          final_norm_g.reshape(1, d).astype(F32), jnp.zeros((1, d), F32)]).reshape(-1, 1, d)
    mod_row = lambda l, stream, part: (l * 2 + stream) * 6 + part
    n1_row = lambda l: depth * 12 + l
    n2_row = lambda l: depth * 13 + l
    fin_row, zero_row = depth * 14, depth * 14 + 1
    norm1 = lambda l, st: (n1_row(l), mod_row(l, st, 0), mod_row(l, st, 1))
    norm2 = lambda l, st: (n2_row(l), mod_row(l, st, 3), mod_row(l, st, 4))
    tab_l, tab_c = _tables(s, n_ctx)
    zero_state = (jnp.zeros((2 * ML_HEADS, ML_QK, 2 * ML_V), F32), jnp.zeros((2 * ML_HEADS, 8, LANE), F32))

    w_in_t = jnp.transpose(w_in, (0, 2, 1))
    wb = w_branch.astype(CD)
    wo = w_out.astype(CD)
    wf2 = w_ffn_out.astype(CD)

    hl = norm_mod(xl, vt, norm1(0, 0), CD)
    hc = norm_mod(xc, vt, norm1(0, 1), CD)
    for l in range(depth):
        last = l == depth - 1
        w = _layer_weights(w_in_t, l, mla_w_uq[l], mla_w_ukv[l])
        gq = mla_q_norm_g[l].reshape(1, -1).astype(F32)
        gkv = mla_kv_norm_g[l].reshape(1, -1).astype(F32)
        sink = sw_sink[l].reshape(1, -1).astype(F32)

        ml_l, na_l, sw_l, mla_lat = _project(hl, w, tab_l, gq, gkv, s + n_ctx)
        ml_c, na_c, sw_c, (mq, mk, mv) = _project(hc, w, tab_c, gq, gkv, s + n_ctx, mla_lat)

        qc_c, kc_c, gt_c = ml_prep(ml_c, ml_conv_w[l], ml_gate_b[l])
        qc_l, kc_l, gt_l = ml_prep(ml_l, ml_conv_w[l], ml_gate_b[l])
        hf_c, hb_c, cst, mst = ml_scan(qc_c, kc_c, ml_c, gt_c, zero_state)
        hf_l, hb_l, _, _ = ml_scan(qc_l, kc_l, ml_l, gt_l, (cst, mst))
        ya_l = ml_finish(hf_l, hb_l, ml_l, ml_norm_g[l])
        yb_l = na_attn(na_l, na_c, na_rpb[l], rows)
        yc_l = sw_attn(sw_l, sw_c, sink)
        yd_l = mla_flash(mq, mk, mv, s)

        nxt_l = (fin_row, zero_row, zero_row) if last else norm1(l + 1, 0)
        xl_new, hl_new = _dense_tail(xl, hl, (ya_l, yb_l, yc_l, yd_l), w, l, wb, wo, w_ffn_in, wf2, vt,
                                     mod_row(l, 0, 2), norm2(l, 0), mod_row(l, 0, 5), nxt_l, last)
        if not last:
            ya_c = ml_finish(hf_c, hb_c, ml_c, ml_norm_g[l])
            nw = NA_HEADS * NA_DIM
            yb_c = dense_attn([(na_c, nw, 0), (na_c, nw, 1), (na_c, nw, 2)], NA_HEADS, 1, NA_DIM, NA_DIM,
                              n_ctx, log2_scores=True)
            kcol = SW_HEADS * SW_DIM // LANE
            yc_c = dense_attn([(sw_c, SW_HEADS * SW_DIM, 0), (sw_c, LANE, kcol), (sw_c, LANE, kcol + 1)],
                              SW_HEADS, SW_HEADS // SW_KV_HEADS, SW_DIM, SW_DIM, n_ctx, sink=sink,
                              log2_scores=True)
            assert s % n_ctx == 0
            yd_c = dense_attn([(mq, MLA_HEADS * MLA_HW, 0), (mk, MLA_HEADS * MLA_HW, 0),
                               (mv, MLA_HEADS * MLA_V, 0)], MLA_HEADS, 1, MLA_HW, MLA_V, n_ctx,
                              log2_scores=True, row_block=s // n_ctx)
            xc, hc = _dense_tail(xc, hc, (ya_c, yb_c, yc_c, yd_c), w, l, wb, wo, w_ffn_in, wf2, vt,
                                 mod_row(l, 1, 2), norm2(l, 1), mod_row(l, 1, 5), norm1(l + 1, 1), False)
        xl, hl = xl_new, hl_new

    return hl[None].astype(x.dtype)
```

```python
import functools

import numpy as np
import jax
import jax.numpy as jnp
from jax import lax
from jax.experimental import pallas as pl
from jax.experimental.pallas import tpu as pltpu

F32 = jnp.float32
CD = jnp.bfloat16

GRID_W = 64
EPS = 1e-6
ROPE_THETA = 10000.0
ML_HEADS, ML_QK, ML_V, ML_CONV = 4, 64, 128, 3
NA_HEADS, NA_DIM, NA_WIN_R, NA_WIN_C = 8, 64, 8, 16
SW_HEADS, SW_KV_HEADS, SW_DIM, SW_WINDOW = 8, 2, 64, 128
MLA_HEADS, MLA_Q_RANK, MLA_KV_RANK, MLA_NOPE, MLA_ROPE, MLA_V = 4, 384, 256, 128, 64, 128
N_BRANCH, BRANCH_W = 4, 512

LANE = 128
ML_QKW = ML_HEADS * ML_QK
ML_VW = ML_HEADS * ML_V
ML_COLS = 2 * ML_QKW + 2 * ML_VW + 4 * ML_HEADS
ML_PAD = -(-ML_COLS // LANE) * LANE
NA_COLS = 3 * NA_HEADS * NA_DIM
SW_COLS = (SW_HEADS + 2 * SW_KV_HEADS) * SW_DIM
MLA_COLS = MLA_Q_RANK + MLA_KV_RANK + MLA_ROPE
MLA_HW = 256
NEG = -1e30
LOG2E = float(np.log2(np.e))
VMEM_LIMIT = 56 * 1024 * 1024

NA_QROWS = 4
SW_TQ = 256
ML_CHUNK = 256
MLA_TK = 1280


def _cparams(*sem):
    return pltpu.CompilerParams(dimension_semantics=sem, vmem_limit_bytes=VMEM_LIMIT)


def _row_tile(t, pref=512):
    tm = min(pref, t)
    assert t % tm == 0
    return tm


def _dot(a, b):
    return jnp.dot(a, b, preferred_element_type=F32)


def _dot_nt(a, b):
    return lax.dot_general(a, b, (((1,), (1,)), ((), ())), preferred_element_type=F32)


def _sigmoid(x):
    return 1.0 / (1.0 + jnp.exp(-x))


def _lanes(col, n):
    tile = jnp.broadcast_to(col, (col.shape[0], LANE))
    return tile if n == LANE else jnp.tile(tile, (1, n // LANE))


def _ones_col(n, w):
    return jnp.where(lax.broadcasted_iota(jnp.int32, (n, w), 1) == 0, 1.0, 0.0).astype(CD)


def _adaln_kernel(c_ref, w_ref, b_ref, o_ref):
    w = w_ref[0]
    for r in range(2):
        c = c_ref[:, r:r + 1]
        o_ref[0, r:r + 1, :] = jnp.sum((c * _sigmoid(c)) * w, axis=0, keepdims=True) + b_ref[0]


def adaln(cc, w_ada, b_ada):
    nl, d, n = w_ada.shape
    tn = 1024
    return pl.pallas_call(
        _adaln_kernel,
        grid=(nl, n // tn),
        in_specs=[pl.BlockSpec((d, 2), lambda l, j: (0, 0)),
                  pl.BlockSpec((1, d, tn), lambda l, j: (l, 0, j)),
                  pl.BlockSpec((1, 1, tn), lambda l, j: (l, 0, j))],
        out_specs=pl.BlockSpec((1, 2, tn), lambda l, j: (l, 0, j)),
        out_shape=jax.ShapeDtypeStruct((nl, 2, n), F32),
        compiler_params=_cparams("parallel", "parallel"),
        name="adaln",
    )(cc, w_ada, b_ada.reshape(nl, 1, n))


def _vec_spec(d, idx):
    return pl.BlockSpec((1, 1, d), lambda *_: (idx, 0, 0))


def _norm_mod(x, g, shift, scale):
    y = x * lax.rsqrt(jnp.mean(x * x, axis=-1, keepdims=True) + EPS)
    return (y * g) * (1.0 + scale) + shift


def _norm_mod_kernel(x_ref, g_ref, sh_ref, sc_ref, o_ref):
    o_ref[...] = _norm_mod(x_ref[...], g_ref[0], sh_ref[0], sc_ref[0]).astype(o_ref.dtype)


def norm_mod(x, vt, norm_idx, out_dtype):
    t, d = x.shape
    tm = _row_tile(t, 1024)
    return pl.pallas_call(
        _norm_mod_kernel,
        grid=(t // tm,),
        in_specs=[pl.BlockSpec((tm, d), lambda i: (i, 0))] + [_vec_spec(d, ix) for ix in norm_idx],
        out_specs=pl.BlockSpec((tm, d), lambda i: (i, 0)),
        out_shape=jax.ShapeDtypeStruct((t, d), out_dtype),
        compiler_params=_cparams("parallel"),
        name="norm_mod",
    )(x, vt, vt, vt)


def _mm_kernel(a_ref, b_ref, o_ref):
    o_ref[...] = _dot_nt(a_ref[...], b_ref[...]).astype(o_ref.dtype)


def matmul(a, bt, out_dtype, tn=None):
    t, k = a.shape
    n = bt.shape[0]
    tn = n if tn is None else tn
    tm = _row_tile(t, 1024)
    return pl.pallas_call(
        _mm_kernel,
        grid=(n // tn, t // tm),
        in_specs=[pl.BlockSpec((tm, k), lambda j, i: (i, 0)),
                  pl.BlockSpec((tn, k), lambda j, i: (j, 0))],
        out_specs=pl.BlockSpec((tm, tn), lambda j, i: (i, j)),
        out_shape=jax.ShapeDtypeStruct((t, n), out_dtype),
        compiler_params=_cparams("parallel", "parallel"),
        name="matmul",
    )(a, bt)


def _mm_res_norm_kernel(nj, tn, want_x, a_ref, b_ref, x_ref, gate_ref, g_ref, sh_ref, sc_ref, *rest):
    xbuf, ho_ref = (rest[0], rest[1]) if want_x else (rest[1], rest[0])
    j = pl.program_id(1)
    acc = _dot(a_ref[...], b_ref[0])
    for jj in range(nj):
        @pl.when(j == jj)
        def _(jj=jj):
            sl = slice(jj * tn, (jj + 1) * tn)
            xbuf[:, sl] = x_ref[...] + gate_ref[0][:, sl] * acc

    @pl.when(j == nj - 1)
    def _():
        ho_ref[...] = _norm_mod(xbuf[...], g_ref[0], sh_ref[0], sc_ref[0]).astype(ho_ref.dtype)


def matmul_residual_norm(a, b, l, x, vt, gate_idx, norm_idx, want_x, h_dtype, tn=None):
    t, kdim = a.shape
    n = b.shape[2]
    tn = n if tn is None else tn
    nj = n // tn
    tm = _row_tile(t)
    row = pl.BlockSpec((tm, n), lambda i, j: (i, 0))
    out_shape = [jax.ShapeDtypeStruct((t, n), h_dtype)]
    if want_x:
        out_shape = [jax.ShapeDtypeStruct((t, n), F32)] + out_shape
    out = pl.pallas_call(
        functools.partial(_mm_res_norm_kernel, nj, tn, want_x),
        grid=(t // tm, nj),
        in_specs=[pl.BlockSpec((tm, kdim), lambda i, j: (i, 0)),
                  pl.BlockSpec((1, kdim, tn), lambda i, j: (l, 0, j)),
                  pl.BlockSpec((tm, tn), lambda i, j: (i, j)),
                  _vec_spec(n, gate_idx)] + [_vec_spec(n, ix) for ix in norm_idx],
        out_specs=[row] * len(out_shape),
        out_shape=out_shape,
        scratch_shapes=[] if want_x else [pltpu.VMEM((tm, n), F32)],
        compiler_params=_cparams("parallel", "arbitrary"),
        name="matmul_residual_norm",
    )(a, b, x, vt, vt, vt, vt)
    return out if want_x else out[0]


def _rope_mm_kernel(n_rope, a_ref, w_ref, wr_ref, cos_ref, sin_ref, o_ref):
    a = a_ref[...]
    p = _dot_nt(a, w_ref[...])
    pr = _dot_nt(a, wr_ref[...])
    cos = cos_ref[...]
    sin = sin_ref[...]
    for j in range(p.shape[1] // LANE):
        sl = slice(j * LANE, (j + 1) * LANE)
        if j < n_rope:
            o_ref[:, sl] = (p[:, sl] * cos + pr[:, sl] * sin).astype(o_ref.dtype)
        else:
            o_ref[:, sl] = p[:, sl].astype(o_ref.dtype)


def rope_matmul(a, w, w_rot, cos, sin, n_rope):
    t, k = a.shape
    n = w.shape[0]
    nr = n_rope * LANE
    tm = _row_tile(t, 1024)
    return pl.pallas_call(
        functools.partial(_rope_mm_kernel, n_rope),
        grid=(t // tm,),
        in_specs=[pl.BlockSpec((tm, k), lambda i: (i, 0)),
                  pl.BlockSpec((n, k), lambda i: (0, 0)),
                  pl.BlockSpec((nr, k), lambda i: (0, 0)),
                  pl.BlockSpec((tm, LANE), lambda i: (i, 0)),
                  pl.BlockSpec((tm, LANE), lambda i: (i, 0))],
        out_specs=pl.BlockSpec((tm, n), lambda i: (i, 0)),
        out_shape=jax.ShapeDtypeStruct((t, n), CD),
        compiler_params=_cparams("parallel"),
        name="rope_matmul",
    )(a, w, w_rot, cos, sin)


def _mla_proj_kernel(h_ref, w1_ref, gq_ref, gkv_ref, wq_ref, wqr_ref, wkv_ref, cos_ref, sin_ref,
                     q_out, k_out, v_out):
    p = _dot_nt(h_ref[...], w1_ref[...])
    cq = p[:, :MLA_Q_RANK]
    ckv = p[:, MLA_Q_RANK:MLA_Q_RANK + MLA_KV_RANK]
    o = MLA_Q_RANK + MLA_KV_RANK
    kr = p[:, o:o + LANE]
    krr = p[:, o + LANE:o + 2 * LANE]
    cqn = (cq * lax.rsqrt(jnp.mean(cq * cq, axis=-1, keepdims=True) + EPS) * gq_ref[...]).astype(CD)
    ckvn = (ckv * lax.rsqrt(jnp.mean(ckv * ckv, axis=-1, keepdims=True) + EPS) * gkv_ref[...]).astype(CD)
    qa = _dot(cqn, wq_ref[...])
    qb = _dot(cqn, wqr_ref[...])
    kv = _dot(ckvn, wkv_ref[...])
    rope_lane = lax.broadcasted_iota(jnp.int32, cos_ref.shape, 1) < MLA_ROPE
    ck = jnp.where(rope_lane, cos_ref[...], 0.0)
    sk = jnp.where(rope_lane, sin_ref[...], 0.0)
    krp = (kr * ck + krr * sk).astype(CD)
    scale = (MLA_NOPE + MLA_ROPE) ** -0.5 * LOG2E
    cq_t = jnp.concatenate([jnp.full(ck.shape, scale, F32), ck * scale], axis=1)
    sq_t = jnp.concatenate([jnp.zeros(sk.shape, F32), sk * scale], axis=1)
    kvw = MLA_NOPE + MLA_V
    for h in range(MLA_HEADS):
        qs = slice(h * MLA_HW, (h + 1) * MLA_HW)
        q_out[:, qs] = (qa[:, qs] * cq_t + qb[:, qs] * sq_t).astype(CD)
        k_out[:, h * MLA_HW:h * MLA_HW + MLA_NOPE] = kv[:, h * kvw:h * kvw + MLA_NOPE].astype(CD)
        k_out[:, h * MLA_HW + MLA_NOPE:(h + 1) * MLA_HW] = krp
        v_out[:, h * MLA_V:(h + 1) * MLA_V] = kv[:, h * kvw + MLA_NOPE:(h + 1) * kvw].astype(CD)


def mla_proj(h, w1, gq, gkv, wq, wqr, wkv, cos, sin, total_rows, into=None):
    t, d = h.shape
    tm = _row_tile(t)
    nt = t // tm
    full = lambda a: pl.BlockSpec(a.shape, lambda i: (0,) * a.ndim)
    rows = lambda w: pl.BlockSpec((tm, w), lambda i: (jnp.minimum(i, nt - 1), 0))
    hw = MLA_HEADS * MLA_HW
    vw = MLA_HEADS * MLA_V
    args = [h, w1, gq, gkv, wq, wqr, wkv, cos, sin]
    in_specs = [rows(d), full(w1), full(gq), full(gkv), full(wq), full(wqr), full(wkv),
                rows(LANE), rows(LANE)]
    if into is None:
        off, aliases = 0, {}
        n_tail = -(-(total_rows - t) // tm)

        def kern(*refs):
            i = pl.program_id(0)

            @pl.when(i < nt)
            def _():
                _mla_proj_kernel(*refs)

            @pl.when(i >= nt)
            def _():
                for o_ref in refs[-3:]:
                    o_ref[...] = jnp.zeros(o_ref.shape, o_ref.dtype)
    else:
        assert (total_rows - t) % tm == 0
        off = (total_rows - t) // tm
        n_tail = 0
        aliases = {len(args) + i: i for i in range(3)}
        in_specs = in_specs + [pl.BlockSpec(memory_space=pl.ANY)] * 3
        args = args + list(into)
        kern = lambda *refs: _mla_proj_kernel(*refs[:9], *refs[12:])
    orow = lambda w: pl.BlockSpec((tm, w), lambda i: (off + i, 0))
    return pl.pallas_call(
        kern,
        grid=(nt + n_tail,),
        in_specs=in_specs,
        out_specs=[orow(hw), orow(hw), orow(vw)],
        out_shape=[jax.ShapeDtypeStruct((total_rows, hw), CD), jax.ShapeDtypeStruct((total_rows, hw), CD),
                   jax.ShapeDtypeStruct((total_rows, vw), CD)],
        input_output_aliases=aliases,
        compiler_params=_cparams("parallel"),
        name="mla_proj",
    )(*args)


def _flash_kernel(tq, tk, unroll, q_ref, k_ref, v_ref, o_ref, s_sc, m_sc, acc_sc):
    nkb = k_ref.shape[0] // tk
    nsub = q_ref.shape[0] // tq
    total = nsub * nkb
    m_sc[...] = jnp.full(m_sc.shape, NEG, F32)
    acc_sc[...] = jnp.zeros(acc_sc.shape, F32)
    ones_col = jnp.where(lax.broadcasted_iota(jnp.int32, (tk, LANE), 1) == 0, 1.0, 0.0).astype(CD)

    def split(b):
        sub = b // nkb
        return sub, b - sub * nkb

    def scores(b):
        sub, j = split(b)
        q = q_ref[pl.ds(pl.multiple_of(sub * tq, tq), tq), :]
        k = k_ref[pl.ds(pl.multiple_of(j * tk, tk), tk), :]
        return _dot_nt(q, k)

    def softmax_pv(s, b):
        sub, j = split(b)
        v = v_ref[pl.ds(pl.multiple_of(j * tk, tk), tk), :]
        m_prev = m_sc[sub]
        m_new = jnp.maximum(m_prev, jnp.max(s, axis=-1, keepdims=True))
        alpha = jnp.exp2(m_prev - m_new)
        p = jnp.exp2(s - m_new).astype(CD)
        v_aug = jnp.concatenate([v, ones_col], axis=1)
        acc_sc[sub] = alpha * acc_sc[sub] + _dot(p, v_aug)
        m_sc[sub] = m_new

    s_sc[0] = scores(0)

    def body(g, carry):
        b0 = g * unroll
        for u in range(unroll):
            s_sc[(u + 1) % 2] = scores(jnp.minimum(b0 + u + 1, total - 1))
            softmax_pv(s_sc[u % 2], b0 + u)
        return carry

    lax.fori_loop(0, total // unroll, body, 0)
    for sub in range(nsub):
        acc = acc_sc[sub]
        o_ref[sub * tq:(sub + 1) * tq, :] = (acc[:, :MLA_V] / acc[:, MLA_V:MLA_V + 1]).astype(o_ref.dtype)


def mla_flash(q, k, v, s):
    nk = k.shape[0]
    tq = _row_tile(s, 512)
    tstep = _row_tile(s, 16 * tq)
    tk = MLA_TK
    total = (tstep // tq) * (nk // tk)
    assert nk % tk == 0 and total % 2 == 0
    unroll = 16 if total % 16 == 0 else (4 if total % 4 == 0 else 2)
    return pl.pallas_call(
        functools.partial(_flash_kernel, tq, tk, unroll),
        grid=(MLA_HEADS, s // tstep),
        in_specs=[pl.BlockSpec((tstep, MLA_HW), lambda h, i: (i, h)),
                  pl.BlockSpec((nk, MLA_HW), lambda h, i: (0, h), pipeline_mode=pl.Buffered(1)),
                  pl.BlockSpec((nk, MLA_V), lambda h, i: (0, h), pipeline_mode=pl.Buffered(1))],
        out_specs=pl.BlockSpec((tstep, MLA_V), lambda h, i: (i, h)),
        out_shape=jax.ShapeDtypeStruct((s, MLA_HEADS * MLA_V), CD),
        scratch_shapes=[pltpu.VMEM((2, tq, tk), F32), pltpu.VMEM((tstep // tq, tq, 1), F32),
                        pltpu.VMEM((tstep // tq, tq, 2 * MLA_V), F32)],
        compiler_params=_cparams("parallel", "arbitrary"),
        name="mla_flash",
    )(q, k, v)


def _dense_attn_kernel(n_heads, group, dqk, dv, log2_scores, q_ref, k_ref, v_ref, sink_ref, o_ref):
    q = q_ref[...]
    k = k_ref[...]
    v = v_ref[...]
    ex = jnp.exp2 if log2_scores else jnp.exp
    for h in range(n_heads):
        g = h // group
        s = _dot_nt(q[:, h * dqk:(h + 1) * dqk], k[:, g * dqk:(g + 1) * dqk])
        m = jnp.max(s, axis=-1, keepdims=True)
        l = jnp.zeros_like(m)
        if sink_ref is not None:
            sk = sink_ref[0:1, h:h + 1] * (LOG2E if log2_scores else 1.0)
            m = jnp.maximum(m, sk)
            l = ex(sk - m)
        p = ex(s - m)
        l = l + jnp.sum(p, axis=-1, keepdims=True)
        o = _dot(p.astype(CD), v[:, g * dv:(g + 1) * dv]) / l
        o_ref[:, h * dv:(h + 1) * dv] = o.astype(o_ref.dtype)


def dense_attn(qkv_specs, n_heads, group, dqk, dv, c, sink=None, log2_scores=False, row_block=0):
    arrays = [a for a, _, _ in qkv_specs]
    specs = [pl.BlockSpec((c, w), functools.partial(lambda b, i: (row_block, b), b)) for _, w, b in qkv_specs]
    if sink is None:
        kern = lambda q, k, v, o: _dense_attn_kernel(n_heads, group, dqk, dv, log2_scores, q, k, v, None, o)
    else:
        kern = functools.partial(_dense_attn_kernel, n_heads, group, dqk, dv, log2_scores)
        arrays.append(sink)
        specs.append(pl.BlockSpec(sink.shape, lambda i: (0, 0)))
    return pl.pallas_call(
        kern,
        grid=(1,),
        in_specs=specs,
        out_specs=pl.BlockSpec((c, n_heads * dv), lambda i: (0, 0)),
        out_shape=jax.ShapeDtypeStruct((c, n_heads * dv), CD),
        compiler_params=_cparams("arbitrary"),
        name="dense_attn",
    )(*arrays)


def _na_kernel(q_ref, kp_ref, ko_ref, kn_ref, vp_ref, vo_ref, vn_ref, kc_ref, vc_ref, bias_ref, o_ref):
    q = q_ref[...]
    kcat = jnp.concatenate([kp_ref[...], ko_ref[...], kn_ref[...], kc_ref[...]], axis=0)
    vcat = jnp.concatenate([vp_ref[...], vo_ref[...], vn_ref[...], vc_ref[...]], axis=0)
    tb = q.shape[0]
    nloc = 3 * tb
    ones = _ones_col(kcat.shape[0], LANE)
    first_half = lax.broadcasted_iota(jnp.int32, (tb, LANE), 1) < NA_DIM
    zero = jnp.zeros((tb, LANE), CD)

    def scores(h):
        ps = slice(h // 2 * LANE, (h // 2 + 1) * LANE)
        qm = jnp.where(first_half, q[:, ps], zero) if h % 2 == 0 else jnp.where(first_half, zero, q[:, ps])
        return _dot_nt(qm, kcat[:, ps])

    s_next = scores(0)
    outs = []
    for h in range(NA_HEADS):
        ps = slice(h // 2 * LANE, (h // 2 + 1) * LANE)
        s = s_next
        if h + 1 < NA_HEADS:
            s_next = scores(h + 1)
        s = jnp.concatenate([s[:, :nloc] + bias_ref[0, h], s[:, nloc:]], axis=1)
        m = jnp.max(s, axis=-1, keepdims=True)
        p = jnp.exp2(s - _lanes(m, s.shape[1])).astype(CD)
        acc = _dot(p, jnp.concatenate([vcat[:, ps], ones], axis=1))
        outs.append(acc[:, :LANE] / _lanes(acc[:, LANE:LANE + 1], LANE))
        if h % 2 == 1:
            o_ref[:, ps] = jnp.where(first_half, outs[-2], outs[-1]).astype(o_ref.dtype)


def _na_bias(rpb, rows):
    qr = NA_QROWS
    tb = qr * GRID_W
    nb = rows // qr
    nkr = 3 * qr
    rsel = np.zeros((3, qr, nkr, 2 * NA_WIN_R - 1), np.float32)
    for v, b in enumerate((0, 1, nb - 1)):
        for a in range(qr):
            r = qr * b + a
            rs = min(max(r - NA_WIN_R // 2, 0), rows - NA_WIN_R)
            for u in range(nkr):
                kr = qr * (b - 1) + u
                if 0 <= b - 1 + u // qr < nb and rs <= kr < rs + NA_WIN_R:
                    rsel[v, a, u, kr - r + NA_WIN_R - 1] = 1.0
    csel = np.zeros((GRID_W, GRID_W, 2 * NA_WIN_C - 1), np.float32)
    for qc in range(GRID_W):
        cs = min(max(qc - NA_WIN_C // 2, 0), GRID_W - NA_WIN_C)
        for kc in range(cs, cs + NA_WIN_C):
            csel[qc, kc, kc - qc + NA_WIN_C - 1] = 1.0
    valid = (np.einsum('vaud,qkj->vaquk', rsel, csel) > 0).reshape(3, tb, 3 * tb)
    assert 2 * qr == NA_WIN_R
    nd = 2 * NA_WIN_R - 1
    rp = rpb.astype(F32) * LOG2E
    gap = jnp.zeros((NA_HEADS, nd, GRID_W + 1 - (2 * NA_WIN_C - 1)), F32)
    ring = jnp.concatenate([rp[..., NA_WIN_C - 1:], gap, rp[..., :NA_WIN_C - 1]], axis=-1)
    cols = jnp.tile(ring, (1, 1, GRID_W))[..., :GRID_W * GRID_W].reshape(NA_HEADS, nd, GRID_W, GRID_W)
    off = NA_WIN_R - 1 - qr
    slabs = [cols[:, off - a:off - a + nkr].transpose(0, 2, 1, 3).reshape(NA_HEADS, GRID_W, 3 * tb)
             for a in range(qr)]
    band = jnp.stack(slabs, axis=1).reshape(NA_HEADS, tb, 3 * tb)
    return jnp.where(valid[:, None], band[None], NEG)


def na_attn(p_lat, p_ctx, rpb, rows):
    s = p_lat.shape[0]
    c = p_ctx.shape[0]
    w = NA_HEADS * NA_DIM
    tb = NA_QROWS * GRID_W
    nb = s // tb
    assert rows % NA_QROWS == 0 and nb >= 3 and rows >= 2 * NA_WIN_R
    bias = _na_bias(rpb, rows)

    def blk(col, off):
        return pl.BlockSpec((tb, w), lambda i: (jnp.clip(i + off, 0, nb - 1), col))

    return pl.pallas_call(
        _na_kernel,
        grid=(nb,),
        in_specs=[blk(0, 0), blk(1, -1), blk(1, 0), blk(1, 1), blk(2, -1), blk(2, 0), blk(2, 1),
                  pl.BlockSpec((c, w), lambda i: (0, 1)),
                  pl.BlockSpec((c, w), lambda i: (0, 2)),
                  pl.BlockSpec((1, NA_HEADS, tb, 3 * tb),
                               lambda i: (jnp.where(i == 0, 0, jnp.where(i == nb - 1, 2, 1)), 0, 0, 0))],
        out_specs=pl.BlockSpec((tb, w), lambda i: (i, 0)),
        out_shape=jax.ShapeDtypeStruct((s, w), CD),
        compiler_params=_cparams("parallel"),
        name="na_attn",
    )(p_lat, p_lat, p_lat, p_lat, p_lat, p_lat, p_lat, p_ctx, p_ctx, bias)


def _sw_kernel(seq, q_ref, kp_ref, ko_ref, kn_ref, vp_ref, vo_ref, vn_ref, kc_ref, vc_ref, sink_ref, o_ref):
    i = pl.program_id(0)
    q = q_ref[...]
    tq = q.shape[0]
    kcat = jnp.concatenate([kp_ref[...], ko_ref[...], kn_ref[...], kc_ref[...]], axis=0)
    vcat = jnp.concatenate([vp_ref[...], vo_ref[...], vn_ref[...], vc_ref[...]], axis=0)
    nkk = tq + 2 * SW_WINDOW
    r = lax.broadcasted_iota(jnp.int32, (tq, nkk), 0)
    cc = lax.broadcasted_iota(jnp.int32, (tq, nkk), 1)
    kpos = i * tq - SW_WINDOW + cc
    d = cc - r
    mask = (d >= 0) & (d <= 2 * SW_WINDOW) & (kpos >= 0) & (kpos < seq)
    assert SW_KV_HEADS == 2 and SW_DIM * 2 == LANE
    ones = _ones_col(kcat.shape[0], LANE)
    swap = lambda x: jnp.concatenate([x[:, SW_DIM:], x[:, :SW_DIM]], axis=1)
    k_by_order = (kcat, swap(kcat))
    v_by_order = (jnp.concatenate([vcat, ones], axis=1), jnp.concatenate([swap(vcat), ones], axis=1))
    first_half = lax.broadcasted_iota(jnp.int32, (tq, LANE), 1) < SW_DIM
    zero = jnp.zeros((tq, LANE), CD)
    grp = SW_HEADS // SW_KV_HEADS
    order = lambda h: 0 if h // grp == h % 2 else 1

    def scores(h):
        ps = slice(h // 2 * LANE, (h // 2 + 1) * LANE)
        qm = jnp.where(first_half, q[:, ps], zero) if h % 2 == 0 else jnp.where(first_half, zero, q[:, ps])
        return _dot_nt(qm, k_by_order[order(h)])

    s_next = scores(0)
    outs = []
    for h in range(SW_HEADS):
        s = s_next
        if h + 1 < SW_HEADS:
            s_next = scores(h + 1)
        s = jnp.concatenate([jnp.where(mask, s[:, :nkk], NEG), s[:, nkk:]], axis=1)
        sk = sink_ref[0:1, h:h + 1] * LOG2E
        m = jnp.maximum(jnp.max(s, axis=-1, keepdims=True), sk)
        p = jnp.exp2(s - _lanes(m, s.shape[1])).astype(CD)
        acc = _dot(p, v_by_order[order(h)])
        outs.append(acc[:, :LANE] / _lanes(acc[:, LANE:LANE + 1] + jnp.exp2(sk - m), LANE))
        if h % 2 == 1:
            ps = slice(h // 2 * LANE, (h // 2 + 1) * LANE)
            o_ref[:, ps] = jnp.where(first_half, outs[-2], outs[-1]).astype(o_ref.dtype)


def sw_attn(p_lat, p_ctx, sink):
    s = p_lat.shape[0]
    c = p_ctx.shape[0]
    tq = SW_TQ
    assert s % tq == 0 and tq % SW_WINDOW == 0
    nq = s // tq
    per = tq // SW_WINDOW
    nwb = s // SW_WINDOW
    kcol = SW_HEADS * SW_DIM // LANE
    prev = lambda col: pl.BlockSpec((SW_WINDOW, LANE), lambda i: (jnp.maximum(i * per - 1, 0), col))
    own = lambda col: pl.BlockSpec((tq, LANE), lambda i: (i, col))
    nxt = lambda col: pl.BlockSpec((SW_WINDOW, LANE), lambda i: (jnp.minimum((i + 1) * per, nwb - 1), col))
    return pl.pallas_call(
        functools.partial(_sw_kernel, s),
        grid=(nq,),
        in_specs=[pl.BlockSpec((tq, SW_HEADS * SW_DIM), lambda i: (i, 0)),
                  prev(kcol), own(kcol), nxt(kcol), prev(kcol + 1), own(kcol + 1), nxt(kcol + 1),
                  pl.BlockSpec((c, LANE), lambda i: (0, kcol)),
                  pl.BlockSpec((c, LANE), lambda i: (0, kcol + 1)),
                  pl.BlockSpec(sink.shape, lambda i: (0, 0))],
        out_specs=pl.BlockSpec((tq, SW_HEADS * SW_DIM), lambda i: (i, 0)),
        out_shape=jax.ShapeDtypeStruct((s, SW_HEADS * SW_DIM), CD),
        compiler_params=_cparams("parallel"),
        name="sw_attn",
    )(p_lat, p_lat, p_lat, p_lat, p_lat, p_lat, p_lat, p_ctx, p_ctx, sink)


def _ml_prep_kernel(nt, x_ref, xp_ref, xn_ref, g_ref, w_ref, b_ref, q_out, k_out, g_out):
    i = pl.program_id(0)
    x = x_ref[...]
    tm = x.shape[0]
    row = lax.broadcasted_iota(jnp.int32, x.shape, 0)
    has_prev = jnp.where(i > 0, 1.0, 0.0)
    has_next = jnp.where(i < nt - 1, 1.0, 0.0)
    prev_row = xp_ref[7:8, :] * has_prev
    next_row = xn_ref[0:1, :] * has_next
    xm = jnp.where(row == 0, prev_row, pltpu.roll(x, 1, axis=0))
    xq = jnp.where(row == tm - 1, next_row, pltpu.roll(x, tm - 1, axis=0))
    y = w_ref[0:1, :] * xm + w_ref[1:2, :] * x + w_ref[2:3, :] * xq
    y = y * _sigmoid(y)
    q_out[...] = y[:, :ML_QKW].astype(CD)
    k_out[...] = (y[:, ML_QKW:] * (ML_QK ** -0.5)).astype(CD)
    g = g_ref[...] + b_ref[...]
    lane = lax.broadcasted_iota(jnp.int32, g.shape, 1)
    logsig = jnp.minimum(g, 0.0) - jnp.log(1.0 + jnp.exp(-jnp.abs(g)))
    g_out[...] = jnp.where(lane < 2 * ML_HEADS, g, logsig)


def ml_prep(p_ml, conv_w, gate_b):
    t = p_ml.shape[0]
    tm = _row_tile(t, 1024)
    nt = t // tm
    wq = 2 * ML_QKW
    gcol = (2 * ML_QKW + 2 * ML_VW) // LANE
    w8 = jnp.zeros((8, wq), F32).at[:ML_CONV].set(conv_w.astype(F32))
    b = jnp.zeros((1, LANE), F32).at[0, :4 * ML_HEADS].set(gate_b.astype(F32).reshape(-1))
    h8 = tm // 8
    return pl.pallas_call(
        functools.partial(_ml_prep_kernel, nt),
        grid=(nt,),
        in_specs=[pl.BlockSpec((tm, wq), lambda i: (i, 0)),
                  pl.BlockSpec((8, wq), lambda i: (jnp.maximum(i * h8 - 1, 0), 0)),
                  pl.BlockSpec((8, wq), lambda i: (jnp.minimum((i + 1) * h8, t // 8 - 1), 0)),
                  pl.BlockSpec((tm, LANE), lambda i: (i, gcol)),
                  pl.BlockSpec((8, wq), lambda i: (0, 0)),
                  pl.BlockSpec((1, LANE), lambda i: (0, 0))],
        out_specs=[pl.BlockSpec((tm, ML_QKW), lambda i: (i, 0)),
                   pl.BlockSpec((tm, ML_QKW), lambda i: (i, 0)),
                   pl.BlockSpec((tm, LANE), lambda i: (i, 0))],
        out_shape=[jax.ShapeDtypeStruct((t, ML_QKW), CD), jax.ShapeDtypeStruct((t, ML_QKW), CD),
                   jax.ShapeDtypeStruct((t, LANE), F32)],
        compiler_params=_cparams("parallel"),
        name="ml_prep",
    )(p_ml, p_ml, p_ml, p_ml, w8, b)


def _ml_scan_kernel(qf_ref, ktf_ref, vf_ref, gcf_ref, grf_ref,
                    qb_ref, ktb_ref, vb_ref, gcb_ref, grb_ref,
                    c0_ref, m0_ref, hf_ref, hb_ref, c_ref, m_ref):
    @pl.when(pl.program_id(0) == 0)
    def _():
        c_ref[...] = c0_ref[...]
        m_ref[...] = m0_ref[...]

    ln = qf_ref.shape[0]
    r = lax.broadcasted_iota(jnp.int32, (ln, ln), 0)
    c = lax.broadcasted_iota(jnp.int32, (ln, ln), 1)
    lower = r >= c
    upper = c >= r
    lower_f = jnp.where(lower, 1.0, 0.0)
    upper_f = jnp.where(upper, 1.0, 0.0)
    ones_col = jnp.where(lax.broadcasted_iota(jnp.int32, (ln, LANE), 1) == 0, 1.0, 0.0).astype(CD)
    hi = lax.Precision.HIGHEST
    first_half = lax.broadcasted_iota(jnp.int32, (ln, LANE), 1) < ML_QK
    zero_q = jnp.zeros((ln, LANE), CD)
    new_states = []

    streams = ((qf_ref, ktf_ref, vf_ref, gcf_ref, grf_ref, hf_ref, lower, lower_f, upper_f, ln - 1),
               (qb_ref, ktb_ref, vb_ref, gcb_ref, grb_ref, hb_ref, upper, upper_f, lower_f, 0))
    for d, (q_ref, kt_ref, v_ref, gc_ref, gr_ref, h_ref, mask, tri_c, tri_r, last) in enumerate(streams):
        gcol = gc_ref[...]
        grow = gr_ref[...]
        bcol = jnp.dot(tri_c, gcol, precision=hi, preferred_element_type=F32)
        brow = jnp.dot(grow, tri_r, precision=hi, preferred_element_type=F32)
        q = q_ref[...]
        kt = kt_ref[...]
        v = v_ref[...]
        for hd in range(ML_HEADS):
            ch = d * ML_HEADS + hd
            ic = d * ML_HEADS + hd
            fc = 2 * ML_HEADS + ic
            pair, half = hd // 2, hd % 2
            b_c = bcol[:, fc:fc + 1]
            r_row = grow[ic:ic + 1, :] - brow[fc:fc + 1, :]
            m_prev = m_ref[ch, 0:1, 0:1]
            cst = c_ref[ch]
            rmat = jnp.where(mask, r_row, NEG)
            mx = jnp.maximum(jnp.max(rmat, axis=-1, keepdims=True), m_prev)
            q2 = q[:, pair * LANE:(pair + 1) * LANE]
            qm = jnp.where(first_half, q2, zero_q) if half == 0 else jnp.where(first_half, zero_q, q2)
            s = _dot(qm, kt[pair * LANE:(pair + 1) * LANE, :]) * jnp.exp(rmat - mx)
            w_inter = jnp.exp(m_prev - mx)
            v_aug = jnp.concatenate([v[:, hd * ML_V:(hd + 1) * ML_V].astype(CD), ones_col], axis=1)
            c_pair = jnp.concatenate([c_ref[ch - half], c_ref[ch - half + 1]], axis=0).astype(CD)
            num = _dot(s.astype(CD), v_aug) + w_inter * _dot(qm, c_pair)
            den = num[:, ML_V:ML_V + 1]
            h = num[:, :ML_V] / jnp.maximum(jnp.abs(den), jnp.exp(-(b_c + mx)))
            h_ref[:, hd * ML_V:(hd + 1) * ML_V] = h
            b_end = b_c[last:last + 1, :]
            m_new = b_end + jnp.maximum(m_prev, jnp.max(r_row, axis=-1, keepdims=True))
            w_s = jnp.exp(b_end + r_row - m_new)
            w_c = jnp.exp(b_end + m_prev - m_new)
            kth = kt[hd * ML_QK:(hd + 1) * ML_QK, :]
            ktw = (kth.astype(F32) * w_s).astype(CD)
            new_states.append((ch, w_c * cst + _dot(ktw, v_aug), m_new))
    for ch, c_new, m_new in new_states:
        c_ref[ch] = c_new
        m_ref[ch] = jnp.broadcast_to(m_new, m_ref.shape[1:])


def ml_scan(qc, kc, p_ml, gates, state):
    t = qc.shape[0]
    ln = min(ML_CHUNK, t)
    assert t % ln == 0
    nc = t // ln
    kt = kc.T
    gr = gates[:, :4 * ML_HEADS].T
    c0, m0 = state
    vcol = 2 * ML_QKW // ML_VW
    fwd = lambda j: j
    bwd = lambda j: nc - 1 - j

    def specs(ix):
        return [pl.BlockSpec((ln, ML_QKW), lambda j: (ix(j), 0)),
                pl.BlockSpec((ML_QKW, ln), lambda j: (0, ix(j))),
                pl.BlockSpec((ln, ML_VW), lambda j: (ix(j), vcol)),
                pl.BlockSpec((ln, LANE), lambda j: (ix(j), 0)),
                pl.BlockSpec((4 * ML_HEADS, ln), lambda j: (0, ix(j)))]

    st_specs = [pl.BlockSpec(c0.shape, lambda j: (0, 0, 0)), pl.BlockSpec(m0.shape, lambda j: (0, 0, 0))]
    return pl.pallas_call(
        _ml_scan_kernel,
        grid=(nc,),
        in_specs=specs(fwd) + specs(bwd) + st_specs,
        out_specs=[pl.BlockSpec((ln, ML_VW), lambda j: (fwd(j), 0)),
                   pl.BlockSpec((ln, ML_VW), lambda j: (bwd(j), 0))] + st_specs,
        out_shape=[jax.ShapeDtypeStruct((t, ML_VW), F32), jax.ShapeDtypeStruct((t, ML_VW), F32),
                   jax.ShapeDtypeStruct(c0.shape, F32), jax.ShapeDtypeStruct(m0.shape, F32)],
        compiler_params=_cparams("arbitrary"),
        name="ml_scan",
    )(qc, kt, p_ml, gates, gr, qc, kt, p_ml, gates, gr, c0, m0)


def _ml_finish_kernel(hf_ref, hb_ref, o_ref, g_ref, y_ref):
    h = hf_ref[...] + hb_ref[...]
    og = _sigmoid(o_ref[...])
    g = g_ref[...]
    for hd in range(ML_HEADS):
        sl = slice(hd * ML_V, (hd + 1) * ML_V)
        hh = h[:, sl]
        y = hh * lax.rsqrt(jnp.mean(hh * hh, axis=-1, keepdims=True) + EPS) * g[:, sl]
        y_ref[:, sl] = (y * og[:, sl]).astype(y_ref.dtype)


def ml_finish(hf, hb, p_ml, norm_g):
    t = hf.shape[0]
    tm = _row_tile(t, 1024)
    ocol = (2 * ML_QKW + ML_VW) // ML_VW
    blk = pl.BlockSpec((tm, ML_VW), lambda i: (i, 0))
    return pl.pallas_call(
        _ml_finish_kernel,
        grid=(t // tm,),
        in_specs=[blk, blk, pl.BlockSpec((tm, ML_VW), lambda i: (i, ocol)),
                  pl.BlockSpec((1, ML_VW), lambda i: (0, 0))],
        out_specs=blk,
        out_shape=jax.ShapeDtypeStruct((t, ML_VW), CD),
        compiler_params=_cparams("parallel"),
        name="ml_finish",
    )(hf, hb, p_ml, norm_g.reshape(1, ML_VW).astype(F32))


def _merge_kernel(h_ref, ya_ref, yb_ref, yc_ref, yd_ref, g0_ref, g1_ref, g2_ref, g3_ref, wb_ref, o_ref):
    h = h_ref[...]
    acc = None
    for i, (y_ref, g_ref) in enumerate(zip((ya_ref, yb_ref, yc_ref, yd_ref), (g0_ref, g1_ref, g2_ref, g3_ref))):
        term = _sigmoid(_dot_nt(h, g_ref[...])) * _dot(y_ref[...], wb_ref[0, i])
        acc = term if acc is None else acc + term
    o_ref[...] = acc.astype(o_ref.dtype)


def gated_merge(h, ys, w_gate, w_branch, l, tn=512):
    t, d = h.shape
    tm = _row_tile(t, 1024)
    nj = d // tn
    yspec = pl.BlockSpec((tm, BRANCH_W), lambda j, i: (i, 0))
    gspec = lambda b: pl.BlockSpec((tn, d), lambda j, i: (b * nj + j, 0))
    return pl.pallas_call(
        _merge_kernel,
        grid=(nj, t // tm),
        in_specs=[pl.BlockSpec((tm, d), lambda j, i: (i, 0)), yspec, yspec, yspec, yspec,
                  gspec(0), gspec(1), gspec(2), gspec(3),
                  pl.BlockSpec((1, N_BRANCH, BRANCH_W, tn), lambda j, i: (l, 0, 0, j))],
        out_specs=pl.BlockSpec((tm, tn), lambda j, i: (i, j)),
        out_shape=jax.ShapeDtypeStruct((t, d), CD),
        compiler_params=_cparams("parallel", "parallel"),
        name="gated_merge",
    )(h, *ys, w_gate, w_gate, w_gate, w_gate, w_branch)


def _ffn1_kernel(h_ref, wa_ref, wg_ref, o_ref, wa_sc, wg_sc):
    @pl.when(pl.program_id(1) == 0)
    def _():
        wa_sc[...] = wa_ref[0].astype(CD)
        wg_sc[...] = wg_ref[0].astype(CD)

    h = h_ref[...]
    a = _dot(h, wa_sc[...])
    g = _dot(h, wg_sc[...])
    o_ref[...] = (a * _sigmoid(a) * g).astype(o_ref.dtype)


def ffn_up(h, w, l, tn=512):
    t, d = h.shape
    dff = w.shape[2] // 2
    tm = _row_tile(t, 1024)
    nj = dff // tn
    return pl.pallas_call(
        _ffn1_kernel,
        grid=(nj, t // tm),
        in_specs=[pl.BlockSpec((tm, d), lambda j, i: (i, 0)),
                  pl.BlockSpec((1, d, tn), lambda j, i: (l, 0, j)),
                  pl.BlockSpec((1, d, tn), lambda j, i: (l, 0, nj + j))],
        out_specs=pl.BlockSpec((tm, tn), lambda j, i: (i, j)),
        out_shape=jax.ShapeDtypeStruct((t, dff), CD),
        scratch_shapes=[pltpu.VMEM((d, tn), CD), pltpu.VMEM((d, tn), CD)],
        compiler_params=_cparams("parallel", "arbitrary"),
        name="ffn_up",
    )(h, w, w)


def _rot_cols(w, dim):
    k, n = w.shape
    w4 = w.reshape(k, n // dim, 2, dim // 2)
    return jnp.stack([-w4[:, :, 1], w4[:, :, 0]], axis=2).reshape(k, n)


def _rope_tables(n_tokens, dim):
    a = dim // 2
    inv = 1.0 / (ROPE_THETA ** (jnp.arange(0, a, 2, dtype=F32) / a))
    rows = n_tokens // GRID_W
    assert rows * GRID_W == n_tokens
    ang_r = jnp.arange(rows, dtype=F32)[:, None] * inv
    ang_c = jnp.arange(GRID_W, dtype=F32)[:, None] * inv

    def expand(fr, fc):
        shape = (rows, GRID_W, a // 2)
        return jnp.concatenate([jnp.broadcast_to(fr[:, None, :], shape), jnp.broadcast_to(fc[None], shape)],
                               axis=-1).reshape(n_tokens, a)

    return expand(jnp.cos(ang_r), jnp.cos(ang_c)), expand(jnp.sin(ang_r), jnp.sin(ang_c))


_O_NA = ML_COLS
_O_SW = _O_NA + NA_COLS
_O_MLA = _O_SW + SW_COLS
_O_KR = _O_MLA + MLA_Q_RANK + MLA_KV_RANK
_O_GATE = _O_MLA + MLA_COLS


def _rot_half_rows(x, dim):
    half = dim // 2
    parts = []
    for b in range(x.shape[0] // dim):
        parts += [-x[b * dim + half:(b + 1) * dim], x[b * dim:b * dim + half]]
    return jnp.concatenate(parts, axis=0)


def _repack_kernel(w_ref, ml_ref, na_ref, sw_ref, swr_ref, mla_ref, gate_ref):
    w = w_ref[0]
    cols = w.shape[1]
    zeros = lambda n: jnp.zeros((n, cols), F32)
    ml_ref[...] = jnp.concatenate([w[:ML_COLS], zeros(ML_PAD - ML_COLS)], axis=0).astype(CD)
    nq = NA_HEADS * NA_DIM
    na_ref[...] = jnp.concatenate([w[_O_NA:_O_NA + nq] * (NA_DIM ** -0.5 * LOG2E), w[_O_NA + nq:_O_SW]],
                                  axis=0).astype(CD)
    sq = SW_HEADS * SW_DIM
    skv = SW_KV_HEADS * SW_DIM
    sw_q = w[_O_SW:_O_SW + sq] * (SW_DIM ** -0.5 * LOG2E)
    sw_k = w[_O_SW + sq:_O_SW + sq + skv]
    sw_ref[...] = jnp.concatenate([sw_q, w[_O_SW + sq:_O_MLA]], axis=0).astype(CD)
    swr_ref[...] = _rot_half_rows(jnp.concatenate([sw_q, sw_k], axis=0), SW_DIM).astype(CD)
    kr = w[_O_KR:_O_GATE]
    pad = zeros(LANE - MLA_ROPE)
    mla_ref[...] = jnp.concatenate([w[_O_MLA:_O_KR], kr, pad, _rot_half_rows(kr, MLA_ROPE), pad], axis=0).astype(CD)
    gate_ref[...] = w[_O_GATE:].astype(CD)


def _repack_w_in(w_in_t, l):
    _, n, d = w_in_t.shape
    tc = 256
    heights = (ML_PAD, NA_COLS, SW_COLS, (SW_HEADS + SW_KV_HEADS) * SW_DIM,
               MLA_Q_RANK + MLA_KV_RANK + 2 * LANE, n - _O_GATE)
    return pl.pallas_call(
        _repack_kernel,
        grid=(d // tc,),
        in_specs=[pl.BlockSpec((1, n, tc), lambda i: (l, 0, i))],
        out_specs=[pl.BlockSpec((ht, tc), lambda i: (0, i)) for ht in heights],
        out_shape=[jax.ShapeDtypeStruct((ht, d), CD) for ht in heights],
        compiler_params=_cparams("parallel"),
        name="repack_w_in",
    )(w_in_t)


def _layer_weights(w_in_t, l, mla_w_uq_l, mla_w_ukv_l):
    w = {}
    w["ml"], w["na"], w["sw"], w["sw_rot"], w["mla1"], w["gate"] = _repack_w_in(w_in_t, l)
    uq = mla_w_uq_l.reshape(MLA_Q_RANK, MLA_HEADS, MLA_NOPE + MLA_ROPE)
    zq = jnp.zeros((MLA_Q_RANK, MLA_HEADS, MLA_HW - MLA_NOPE - MLA_ROPE), F32)
    w["mla_q"] = jnp.concatenate([uq, zq], axis=2).reshape(MLA_Q_RANK, -1).astype(CD)
    uqr = _rot_cols(uq[:, :, MLA_NOPE:].reshape(MLA_Q_RANK, -1), MLA_ROPE).reshape(MLA_Q_RANK, MLA_HEADS, MLA_ROPE)
    w["mla_q_rot"] = jnp.concatenate([jnp.zeros_like(uq[:, :, :MLA_NOPE]), uqr, zq], axis=2
                                     ).reshape(MLA_Q_RANK, -1).astype(CD)
    w["mla_kv"] = mla_w_ukv_l.astype(CD)
    return w


def _tables(s, c):
    assert SW_DIM == MLA_ROPE
    cos, sin = _rope_tables(s, SW_DIM)
    tile4 = lambda a: jnp.concatenate([a, a, a, a], axis=1)
    return ((tile4(cos), tile4(sin)), (jnp.ones((c, LANE), F32), jnp.zeros((c, LANE), F32)))


def _project(h, w, tab, gq, gkv, total_rows, mla_into=None):
    p_ml = matmul(h, w["ml"], F32)
    p_na = matmul(h, w["na"], CD)
    n_rope = (SW_HEADS + SW_KV_HEADS) * SW_DIM // LANE
    p_sw = rope_matmul(h, w["sw"], w["sw_rot"], tab[0], tab[1], n_rope)
    mla = mla_proj(h, w["mla1"], gq, gkv, w["mla_q"], w["mla_q_rot"], w["mla_kv"],
                   tab[0], tab[1], total_rows, mla_into)
    return p_ml, p_na, p_sw, mla


def _dense_tail(x, h, ys, w, l, w_branch, w_out, w_ffn_in, w_ffn_out, vt, g1, norm2, g2, norm_next, last):
    merged = gated_merge(h, ys, w["gate"], w_branch, l)
    x, h2 = matmul_residual_norm(merged, w_out, l, x, vt, g1, norm2, True, CD)
    u = ffn_up(h2, w_ffn_in, l)
    if last:
        return None, matmul_residual_norm(u, w_ffn_out, l, x, vt, g2, norm_next, False, F32, tn=1024)
    return matmul_residual_norm(u, w_ffn_out, l, x, vt, g2, norm_next, True, CD, tn=1024)


def kernel(x, c, ctx, c_ctx, w_ada, b_ada, norm1_g, norm2_g, w_in, ml_conv_w, ml_gate_b, ml_norm_g, na_rpb,
           sw_sink, mla_q_norm_g, mla_kv_norm_g, mla_w_uq, mla_w_ukv, w_branch, w_out, w_ffn_in, w_ffn_out,
           final_norm_g):
    bsz, s, d = x.shape
    assert bsz == 1
    depth = w_in.shape[0]
    n_ctx = ctx.shape[1]
    rows = s // GRID_W
    xl = x[0].astype(F32)
    xc = ctx[0].astype(F32)
    mod = adaln(jnp.stack([c[0], c_ctx], axis=1).astype(F32), w_ada, b_ada)
    vt = jnp.concatenate([mod.reshape(depth * 12, d), norm1_g.astype(F32), norm2_g.astype(F32),
                          final_norm_g.reshape(1, d).astype(F32), jnp.zeros((1, d), F32)]).reshape(-1, 1, d)
    mod_row = lambda l, stream, part: (l * 2 + stream) * 6 + part
    n1_row = lambda l: depth * 12 + l
    n2_row = lambda l: depth * 13 + l
    fin_row, zero_row = depth * 14, depth * 14 + 1
    norm1 = lambda l, st: (n1_row(l), mod_row(l, st, 0), mod_row(l, st, 1))
    norm2 = lambda l, st: (n2_row(l), mod_row(l, st, 3), mod_row(l, st, 4))
    tab_l, tab_c = _tables(s, n_ctx)
    zero_state = (jnp.zeros((2 * ML_HEADS, ML_QK, 2 * ML_V), F32), jnp.zeros((2 * ML_HEADS, 8, LANE), F32))

    w_in_t = jnp.transpose(w_in, (0, 2, 1))
    wb = w_branch.astype(CD)
    wo = w_out.astype(CD)
    wf2 = w_ffn_out.astype(CD)

    hl = norm_mod(xl, vt, norm1(0, 0), CD)
    hc = norm_mod(xc, vt, norm1(0, 1), CD)
    for l in range(depth):
        last = l == depth - 1
        w = _layer_weights(w_in_t, l, mla_w_uq[l], mla_w_ukv[l])
        gq = mla_q_norm_g[l].reshape(1, -1).astype(F32)
        gkv = mla_kv_norm_g[l].reshape(1, -1).astype(F32)
        sink = sw_sink[l].reshape(1, -1).astype(F32)

        ml_l, na_l, sw_l, mla_lat = _project(hl, w, tab_l, gq, gkv, s + n_ctx)
        ml_c, na_c, sw_c, (mq, mk, mv) = _project(hc, w, tab_c, gq, gkv, s + n_ctx, mla_lat)

        qc_c, kc_c, gt_c = ml_prep(ml_c, ml_conv_w[l], ml_gate_b[l])
        qc_l, kc_l, gt_l = ml_prep(ml_l, ml_conv_w[l], ml_gate_b[l])
        hf_c, hb_c, cst, mst = ml_scan(qc_c, kc_c, ml_c, gt_c, zero_state)
        hf_l, hb_l, _, _ = ml_scan(qc_l, kc_l, ml_l, gt_l, (cst, mst))
        ya_l = ml_finish(hf_l, hb_l, ml_l, ml_norm_g[l])
        yb_l = na_attn(na_l, na_c, na_rpb[l], rows)
        yc_l = sw_attn(sw_l, sw_c, sink)
        yd_l = mla_flash(mq, mk, mv, s)

        nxt_l = (fin_row, zero_row, zero_row) if last else norm1(l + 1, 0)
        xl_new, hl_new = _dense_tail(xl, hl, (ya_l, yb_l, yc_l, yd_l), w, l, wb, wo, w_ffn_in, wf2, vt,
                                     mod_row(l, 0, 2), norm2(l, 0), mod_row(l, 0, 5), nxt_l, last)
        if not last:
            ya_c = ml_finish(hf_c, hb_c, ml_c, ml_norm_g[l])
            nw = NA_HEADS * NA_DIM
            yb_c = dense_attn([(na_c, nw, 0), (na_c, nw, 1), (na_c, nw, 2)], NA_HEADS, 1, NA_DIM, NA_DIM,
                              n_ctx, log2_scores=True)
            kcol = SW_HEADS * SW_DIM // LANE
            yc_c = dense_attn([(sw_c, SW_HEADS * SW_DIM, 0), (sw_c, LANE, kcol), (sw_c, LANE, kcol + 1)],
                              SW_HEADS, SW_HEADS // SW_KV_HEADS, SW_DIM, SW_DIM, n_ctx, sink=sink,
                              log2_scores=True)
            assert s % n_ctx == 0
            yd_c = dense_attn([(mq, MLA_HEADS * MLA_HW, 0), (mk, MLA_HEADS * MLA_HW, 0),
                               (mv, MLA_HEADS * MLA_V, 0)], MLA_HEADS, 1, MLA_HW, MLA_V, n_ctx,
                              log2_scores=True, row_block=s // n_ctx)
            xc, hc = _dense_tail(xc, hc, (ya_c, yb_c, yc_c, yd_c), w, l, wb, wo, w_ffn_in, wf2, vt,
                                 mod_row(l, 1, 2), norm2(l, 1), mod_row(l, 1, 5), norm1(l + 1, 1), False)
        xl, hl = xl_new, hl_new

    return hl[None].astype(x.dtype)
```
